```python
import math
import jax, jax.numpy as jnp
from jax import lax
import numpy as np

D_MODEL = 1024
BATCH = 8
SEQ = 16384
DEPTH = 4

N_MIXERS = 2
ROPE_THETA = 500000.0
LN_EPS = 1e-5
RMS_EPS = 1e-6

MLA_HEADS = 8
MLA_NOPE = 128
MLA_ROPE = 64
MLA_V = 128
MLA_Q_RANK = 384
MLA_KV_RANK = 256
MLA_QBLOCK = 128

SWA_HEADS = 16
SWA_KV_HEADS = 4
SWA_HEAD_DIM = 64
SWA_WINDOW = 128
SWA_ROT = SWA_HEAD_DIM // 4

D_FF = ((8 * D_MODEL + 3 * 256 - 1) // (3 * 256)) * 256

DEEPNORM_ALPHA = (2 * DEPTH) ** 0.25
DEEPNORM_BETA = (8 * DEPTH) ** -0.25

N_MLA = (DEPTH + 1) // 2
N_SWA = DEPTH // 2

kernel_name = "hybrid_mla_swa_sink_deepnorm_adaln"


def layer_norm(x, g, b):
    xf = x.astype(jnp.float32)
    mu = jnp.mean(xf, -1, keepdims=True)
    var = jnp.mean(jnp.square(xf - mu), -1, keepdims=True)
    return ((xf - mu) * lax.rsqrt(var + LN_EPS) * g + b).astype(x.dtype)


def rms_norm(x, g):
    xf = x.astype(jnp.float32)
    return (xf * lax.rsqrt(jnp.mean(jnp.square(xf), -1, keepdims=True) + RMS_EPS) * g).astype(x.dtype)


def rope_cos_sin(positions, rot_dim):
    inv = ROPE_THETA ** (-jnp.arange(0, rot_dim, 2, dtype=jnp.float32) / rot_dim)
    ang = positions.astype(jnp.float32)[..., None] * inv
    return jnp.cos(ang), jnp.sin(ang)


def apply_rope(x, cos, sin):
    half = x.shape[-1] // 2
    x1, x2 = x[..., :half], x[..., half:]
    c = cos[:, :, None, :]
    s = sin[:, :, None, :]
    return jnp.concatenate([x1 * c - x2 * s, x2 * c + x1 * s], -1).astype(x.dtype)


def mla_mixer(h, cos, sin, w_in, q_norm, w_q_b, kv_norm, w_kv_b, w_o):
    B, S, _ = h.shape
    H = MLA_HEADS
    lat = h @ w_in
    q_lat = lat[..., :MLA_Q_RANK]
    kv_lat = lat[..., MLA_Q_RANK:MLA_Q_RANK + MLA_KV_RANK]
    k_rope = lat[..., MLA_Q_RANK + MLA_KV_RANK:]
    q = (rms_norm(q_lat, q_norm) @ w_q_b).reshape(B, S, H, MLA_NOPE + MLA_ROPE)
    q_nope = q[..., :MLA_NOPE]
    q_rope = apply_rope(q[..., MLA_NOPE:], cos, sin)
    k_rope = apply_rope(k_rope[:, :, None, :], cos, sin)[:, :, 0, :]
    kv = (rms_norm(kv_lat, kv_norm) @ w_kv_b).reshape(B, S, H, MLA_NOPE + MLA_V)
    k_nope = kv[..., :MLA_NOPE]
    v = kv[..., MLA_NOPE:]
    scale = (MLA_NOPE + MLA_ROPE) ** -0.5
    nb = S // MLA_QBLOCK
    qn_b = q_nope.reshape(B, nb, MLA_QBLOCK, H, MLA_NOPE).transpose(1, 0, 2, 3, 4)
    qr_b = q_rope.reshape(B, nb, MLA_QBLOCK, H, MLA_ROPE).transpose(1, 0, 2, 3, 4)
    starts = jnp.arange(nb, dtype=jnp.int32) * MLA_QBLOCK
    k_idx = jnp.arange(S, dtype=jnp.int32)

    def q_block(args):
        qn, qr, start = args
        s = (jnp.einsum('bqhd,bkhd->bhqk', qn, k_nope, preferred_element_type=jnp.float32)
             + jnp.einsum('bqhr,bkr->bhqk', qr, k_rope, preferred_element_type=jnp.float32)) * scale
        q_idx = start + jnp.arange(MLA_QBLOCK, dtype=jnp.int32)
        causal = k_idx[None, :] <= q_idx[:, None]
        s = jnp.where(causal[None, None], s, -jnp.inf)
        p = jax.nn.softmax(s, axis=-1).astype(v.dtype)
        return jnp.einsum('bhqk,bkhd->bqhd', p, v)

    o = lax.map(q_block, (qn_b, qr_b, starts))
    o = o.transpose(1, 0, 2, 3, 4).reshape(B, S, H * MLA_V)
    return o @ w_o


def partial_rope(x, cos, sin):
    return jnp.concatenate([apply_rope(x[..., :SWA_ROT], cos, sin), x[..., SWA_ROT:]], -1)


def swa_mixer(h, cos, sin, w_qkv, b_qkv, sinks, w_o, b_o):
    B, S, _ = h.shape
    HQ, HKV, HD, W = SWA_HEADS, SWA_KV_HEADS, SWA_HEAD_DIM, SWA_WINDOW
    G = HQ // HKV
    qkv = h @ w_qkv + b_qkv
    q = qkv[..., :HQ * HD].reshape(B, S, HQ, HD)
    k = qkv[..., HQ * HD:(HQ + HKV) * HD].reshape(B, S, HKV, HD)
    v = qkv[..., (HQ + HKV) * HD:].reshape(B, S, HKV, HD)
    q = partial_rope(q, cos, sin)
    k = partial_rope(k, cos, sin)
    nb = S // W
    qb = q.reshape(B, nb, W, HKV, G, HD)
    kb = k.reshape(B, nb, W, HKV, HD)
    vb = v.reshape(B, nb, W, HKV, HD)
    kpad = jnp.zeros_like(kb[:, :1])
    vpad = jnp.zeros_like(vb[:, :1])
    k2 = jnp.concatenate([jnp.concatenate([kpad, kb[:, :-1]], 1), kb], axis=2)
    v2 = jnp.concatenate([jnp.concatenate([vpad, vb[:, :-1]], 1), vb], axis=2)
    s = jnp.einsum('bnqhgd,bnkhd->bnhgqk', qb, k2, preferred_element_type=jnp.float32) * (HD ** -0.5)
    q_pos = jnp.arange(W, dtype=jnp.int32)[:, None] + W
    k_pos = jnp.arange(2 * W, dtype=jnp.int32)[None, :]
    rel = q_pos - k_pos
    band = (rel >= 0) & (rel < W)
    has_prev = (jnp.arange(nb)[:, None, None] > 0) | (k_pos[None] >= W)
    mask = band[None] & has_prev
    s = jnp.where(mask[None, :, None, None], s, -jnp.inf)
    sink = jnp.broadcast_to(sinks.astype(jnp.float32).reshape(HKV, G)[None, None, :, :, None, None],
                            s.shape[:-1] + (1,))
    p = jax.nn.softmax(jnp.concatenate([s, sink], -1), axis=-1)[..., :-1]
    o = jnp.einsum('bnhgqk,bnkhd->bnqhgd', p.astype(v.dtype), v2).reshape(B, S, HQ * HD)
    return o @ w_o + b_o


def swiglu(h, w_gate, w_up, w_down):
    return (jax.nn.silu(h @ w_gate) * (h @ w_up)) @ w_down


def _fwd_setup_inputs(seed: int = 0) -> dict:
    key = jax.random.key(seed)
    ks = iter(jax.random.split(key, 32))
    D, F = D_MODEL, D_FF
    nrm = lambda shape, std: jax.random.normal(next(ks), shape, jnp.float32) * std
    offs = jax.random.randint(next(ks), (BATCH, 1), 0, 4096, dtype=jnp.int32)
    positions = offs + jnp.arange(SEQ, dtype=jnp.int32)[None, :]
    mla_in_w = MLA_Q_RANK + MLA_KV_RANK + MLA_ROPE
    swa_qkv_w = (SWA_HEADS + 2 * SWA_KV_HEADS) * SWA_HEAD_DIM
    return {
        "x": nrm((BATCH, SEQ, D), 1.0),
        "c": nrm((BATCH, D), 1.0),
        "positions": positions,
        "ada_w": nrm((DEPTH, D, 6 * D), 0.5 * D ** -0.5),
        "ada_b": nrm((DEPTH, 6 * D), 0.02),
        "ln_mix_g": 1.0 + nrm((DEPTH, D), 0.02),
        "ln_mix_b": nrm((DEPTH, D), 0.02),
        "ln_ffn_g": 1.0 + nrm((DEPTH, D), 0.02),
        "ln_ffn_b": nrm((DEPTH, D), 0.02),
        "ffn_w_gate": nrm((DEPTH, D, F), D ** -0.5),
        "ffn_w_up": nrm((DEPTH, D, F), D ** -0.5),
        "ffn_w_down": nrm((DEPTH, F, D), F ** -0.5 * DEEPNORM_BETA),
        "mla_w_in": nrm((N_MLA, D, mla_in_w), D ** -0.5),
        "mla_q_norm": 1.0 + nrm((N_MLA, MLA_Q_RANK), 0.02),
        "mla_w_q_b": nrm((N_MLA, MLA_Q_RANK, MLA_HEADS * (MLA_NOPE + MLA_ROPE)), MLA_Q_RANK ** -0.5),
        "mla_kv_norm": 1.0 + nrm((N_MLA, MLA_KV_RANK), 0.02),
        "mla_w_kv_b": nrm((N_MLA, MLA_KV_RANK, MLA_HEADS * (MLA_NOPE + MLA_V)), MLA_KV_RANK ** -0.5),
        "mla_w_o": nrm((N_MLA, MLA_HEADS * MLA_V, D), (MLA_HEADS * MLA_V) ** -0.5 * DEEPNORM_BETA),
        "swa_w_qkv": nrm((N_SWA, D, swa_qkv_w), D ** -0.5),
        "swa_b_qkv": nrm((N_SWA, swa_qkv_w), 0.02),
        "swa_sinks": nrm((N_SWA, SWA_HEADS), 1.0),
        "swa_w_o": nrm((N_SWA, SWA_HEADS * SWA_HEAD_DIM, D), (SWA_HEADS * SWA_HEAD_DIM) ** -0.5 * DEEPNORM_BETA),
        "swa_b_o": nrm((N_SWA, D), 0.02),
    }


def _fwd_reference(x, c, positions, ada_w, ada_b, ln_mix_g, ln_mix_b, ln_ffn_g, ln_ffn_b,
              ffn_w_gate, ffn_w_up, ffn_w_down, mla_w_in, mla_q_norm, mla_w_q_b, mla_kv_norm,
              mla_w_kv_b, mla_w_o, swa_w_qkv, swa_b_qkv, swa_sinks, swa_w_o, swa_b_o):
    cos_a, sin_a = rope_cos_sin(positions, MLA_ROPE)
    cos_b, sin_b = rope_cos_sin(positions, SWA_ROT)
    cond = jax.nn.silu(c)
    for i in range(DEPTH):
        mod = (cond @ ada_w[i] + ada_b[i])[:, None, :]
        sh_m, sc_m, g_m, sh_f, sc_f, g_f = jnp.split(mod, 6, axis=-1)
        h = x * (1.0 + sc_m) + sh_m
        j = i // N_MIXERS
        if i % N_MIXERS == 0:
            y = mla_mixer(h, cos_a, sin_a, mla_w_in[j], mla_q_norm[j], mla_w_q_b[j],
                          mla_kv_norm[j], mla_w_kv_b[j], mla_w_o[j])
        else:
            y = swa_mixer(h, cos_b, sin_b, swa_w_qkv[j], swa_b_qkv[j], swa_sinks[j],
                          swa_w_o[j], swa_b_o[j])
        x = layer_norm(DEEPNORM_ALPHA * x + g_m * y, ln_mix_g[i], ln_mix_b[i])
        h = x * (1.0 + sc_f) + sh_f
        y = swiglu(h, ffn_w_gate[i], ffn_w_up[i], ffn_w_down[i])
        x = layer_norm(DEEPNORM_ALPHA * x + g_f * y, ln_ffn_g[i], ln_ffn_b[i])
    return x


import jax as _jax
import jax.numpy as _jnp

TWIN_FORMAT = 'train_step'
FWD_PARAMS = ['x', 'c', 'positions', 'ada_w', 'ada_b', 'ln_mix_g', 'ln_mix_b', 'ln_ffn_g', 'ln_ffn_b', 'ffn_w_gate', 'ffn_w_up', 'ffn_w_down', 'mla_w_in', 'mla_q_norm', 'mla_w_q_b', 'mla_kv_norm', 'mla_w_kv_b', 'mla_w_o', 'swa_w_qkv', 'swa_b_qkv', 'swa_sinks', 'swa_w_o', 'swa_b_o']
TWIN_WEIGHTS = ['ada_w', 'ada_b', 'ln_mix_g', 'ln_mix_b', 'ln_ffn_g', 'ln_ffn_b', 'ffn_w_gate', 'ffn_w_up', 'ffn_w_down', 'mla_w_in', 'mla_q_norm', 'mla_w_q_b', 'mla_kv_norm', 'mla_w_kv_b', 'mla_w_o', 'swa_w_qkv', 'swa_b_qkv', 'swa_sinks', 'swa_w_o', 'swa_b_o']
TWIN_DIFF_INPUT = 'x'
TWIN_INPUTS = ['x', 'c', 'positions', 'ada_w', 'ada_b', 'ln_mix_g', 'ln_mix_b', 'ln_ffn_g', 'ln_ffn_b', 'ffn_w_gate', 'ffn_w_up', 'ffn_w_down', 'mla_w_in', 'mla_q_norm', 'mla_w_q_b', 'mla_kv_norm', 'mla_w_kv_b', 'mla_w_o', 'swa_w_qkv', 'swa_b_qkv', 'swa_sinks', 'swa_w_o', 'swa_b_o', 'loss_target', 'm_ada_w', 'm_ada_b', 'm_ln_mix_g', 'm_ln_mix_b', 'm_ln_ffn_g', 'm_ln_ffn_b', 'm_ffn_w_gate', 'm_ffn_w_up', 'm_ffn_w_down', 'm_mla_w_in', 'm_mla_q_norm', 'm_mla_w_q_b', 'm_mla_kv_norm', 'm_mla_w_kv_b', 'm_mla_w_o', 'm_swa_w_qkv', 'm_swa_b_qkv', 'm_swa_sinks', 'm_swa_w_o', 'm_swa_b_o', 'v_ada_w', 'v_ada_b', 'v_ln_mix_g', 'v_ln_mix_b', 'v_ln_ffn_g', 'v_ln_ffn_b', 'v_ffn_w_gate', 'v_ffn_w_up', 'v_ffn_w_down', 'v_mla_w_in', 'v_mla_q_norm', 'v_mla_w_q_b', 'v_mla_kv_norm', 'v_mla_w_kv_b', 'v_mla_w_o', 'v_swa_w_qkv', 'v_swa_b_qkv', 'v_swa_sinks', 'v_swa_w_o', 'v_swa_b_o']
TWIN_OUTPUTS = ['loss', 'grad_x', 'grad_ada_w', 'grad_ada_b', 'grad_ln_mix_g', 'grad_ln_mix_b', 'grad_ln_ffn_g', 'grad_ln_ffn_b', 'grad_ffn_w_gate', 'grad_ffn_w_up', 'grad_ffn_w_down', 'grad_mla_w_in', 'grad_mla_q_norm', 'grad_mla_w_q_b', 'grad_mla_kv_norm', 'grad_mla_w_kv_b', 'grad_mla_w_o', 'grad_swa_w_qkv', 'grad_swa_b_qkv', 'grad_swa_sinks', 'grad_swa_w_o', 'grad_swa_b_o', 'delta_ada_w', 'delta_ada_b', 'delta_ln_mix_g', 'delta_ln_mix_b', 'delta_ln_ffn_g', 'delta_ln_ffn_b', 'delta_ffn_w_gate', 'delta_ffn_w_up', 'delta_ffn_w_down', 'delta_mla_w_in', 'delta_mla_q_norm', 'delta_mla_w_q_b', 'delta_mla_kv_norm', 'delta_mla_w_kv_b', 'delta_mla_w_o', 'delta_swa_w_qkv', 'delta_swa_b_qkv', 'delta_swa_sinks', 'delta_swa_w_o', 'delta_swa_b_o', 'new_m_ada_w', 'new_m_ada_b', 'new_m_ln_mix_g', 'new_m_ln_mix_b', 'new_m_ln_ffn_g', 'new_m_ln_ffn_b', 'new_m_ffn_w_gate', 'new_m_ffn_w_up', 'new_m_ffn_w_down', 'new_m_mla_w_in', 'new_m_mla_q_norm', 'new_m_mla_w_q_b', 'new_m_mla_kv_norm', 'new_m_mla_w_kv_b', 'new_m_mla_w_o', 'new_m_swa_w_qkv', 'new_m_swa_b_qkv', 'new_m_swa_sinks', 'new_m_swa_w_o', 'new_m_swa_b_o', 'new_v_ada_w', 'new_v_ada_b', 'new_v_ln_mix_g', 'new_v_ln_mix_b', 'new_v_ln_ffn_g', 'new_v_ln_ffn_b', 'new_v_ffn_w_gate', 'new_v_ffn_w_up', 'new_v_ffn_w_down', 'new_v_mla_w_in', 'new_v_mla_q_norm', 'new_v_mla_w_q_b', 'new_v_mla_kv_norm', 'new_v_mla_w_kv_b', 'new_v_mla_w_o', 'new_v_swa_w_qkv', 'new_v_swa_b_qkv', 'new_v_swa_sinks', 'new_v_swa_w_o', 'new_v_swa_b_o']
TWIN_LEAF_KINDS = {'loss': 'loss', 'grad_x': 'grad_x', 'grad_ada_w': 'grad_w', 'grad_ada_b': 'grad_w', 'grad_ln_mix_g': 'grad_w', 'grad_ln_mix_b': 'grad_w', 'grad_ln_ffn_g': 'grad_w', 'grad_ln_ffn_b': 'grad_w', 'grad_ffn_w_gate': 'grad_w', 'grad_ffn_w_up': 'grad_w', 'grad_ffn_w_down': 'grad_w', 'grad_mla_w_in': 'grad_w', 'grad_mla_q_norm': 'grad_w', 'grad_mla_w_q_b': 'grad_w', 'grad_mla_kv_norm': 'grad_w', 'grad_mla_w_kv_b': 'grad_w', 'grad_mla_w_o': 'grad_w', 'grad_swa_w_qkv': 'grad_w', 'grad_swa_b_qkv': 'grad_w', 'grad_swa_sinks': 'grad_w', 'grad_swa_w_o': 'grad_w', 'grad_swa_b_o': 'grad_w', 'delta_ada_w': 'delta_w', 'delta_ada_b': 'delta_w', 'delta_ln_mix_g': 'delta_w', 'delta_ln_mix_b': 'delta_w', 'delta_ln_ffn_g': 'delta_w', 'delta_ln_ffn_b': 'delta_w', 'delta_ffn_w_gate': 'delta_w', 'delta_ffn_w_up': 'delta_w', 'delta_ffn_w_down': 'delta_w', 'delta_mla_w_in': 'delta_w', 'delta_mla_q_norm': 'delta_w', 'delta_mla_w_q_b': 'delta_w', 'delta_mla_kv_norm': 'delta_w', 'delta_mla_w_kv_b': 'delta_w', 'delta_mla_w_o': 'delta_w', 'delta_swa_w_qkv': 'delta_w', 'delta_swa_b_qkv': 'delta_w', 'delta_swa_sinks': 'delta_w', 'delta_swa_w_o': 'delta_w', 'delta_swa_b_o': 'delta_w', 'new_m_ada_w': 'new_m', 'new_m_ada_b': 'new_m', 'new_m_ln_mix_g': 'new_m', 'new_m_ln_mix_b': 'new_m', 'new_m_ln_ffn_g': 'new_m', 'new_m_ln_ffn_b': 'new_m', 'new_m_ffn_w_gate': 'new_m', 'new_m_ffn_w_up': 'new_m', 'new_m_ffn_w_down': 'new_m', 'new_m_mla_w_in': 'new_m', 'new_m_mla_q_norm': 'new_m', 'new_m_mla_w_q_b': 'new_m', 'new_m_mla_kv_norm': 'new_m', 'new_m_mla_w_kv_b': 'new_m', 'new_m_mla_w_o': 'new_m', 'new_m_swa_w_qkv': 'new_m', 'new_m_swa_b_qkv': 'new_m', 'new_m_swa_sinks': 'new_m', 'new_m_swa_w_o': 'new_m', 'new_m_swa_b_o': 'new_m', 'new_v_ada_w': 'new_v', 'new_v_ada_b': 'new_v', 'new_v_ln_mix_g': 'new_v', 'new_v_ln_mix_b': 'new_v', 'new_v_ln_ffn_g': 'new_v', 'new_v_ln_ffn_b': 'new_v', 'new_v_ffn_w_gate': 'new_v', 'new_v_ffn_w_up': 'new_v', 'new_v_ffn_w_down': 'new_v', 'new_v_mla_w_in': 'new_v', 'new_v_mla_q_norm': 'new_v', 'new_v_mla_w_q_b': 'new_v', 'new_v_mla_kv_norm': 'new_v', 'new_v_mla_w_kv_b': 'new_v', 'new_v_mla_w_o': 'new_v', 'new_v_swa_w_qkv': 'new_v', 'new_v_swa_b_qkv': 'new_v', 'new_v_swa_sinks': 'new_v', 'new_v_swa_w_o': 'new_v', 'new_v_swa_b_o': 'new_v'}


def _forward(args):
    return _fwd_reference(*[args[k] for k in FWD_PARAMS])


def _output_shape():
    def fwd():
        inp = _fwd_setup_inputs(0)
        return _fwd_reference(*[inp[k] for k in FWD_PARAMS])
    out = _jax.eval_shape(fwd)
    return out.shape, out.dtype

N_MICROBATCH = 1
ADAM_LR = 0.001
ADAM_B1 = 0.9
ADAM_B2 = 0.999
ADAM_EPS = 1e-08
ADAM_WD = 0.01
ADAM_STEP = 10
PER_EXAMPLE_BATCH_AXIS = {'x': 0, 'c': 0, 'positions': 0, 'loss_target': 0}
SHARED_INPUTS = []
_WEIGHT_DTYPES = {'ada_w': _jnp.float32, 'ada_b': _jnp.float32, 'ln_mix_g': _jnp.float32, 'ln_mix_b': _jnp.float32, 'ln_ffn_g': _jnp.float32, 'ln_ffn_b': _jnp.float32, 'ffn_w_gate': _jnp.float32, 'ffn_w_up': _jnp.float32, 'ffn_w_down': _jnp.float32, 'mla_w_in': _jnp.float32, 'mla_q_norm': _jnp.float32, 'mla_w_q_b': _jnp.float32, 'mla_kv_norm': _jnp.float32, 'mla_w_kv_b': _jnp.float32, 'mla_w_o': _jnp.float32, 'swa_w_qkv': _jnp.float32, 'swa_b_qkv': _jnp.float32, 'swa_sinks': _jnp.float32, 'swa_w_o': _jnp.float32, 'swa_b_o': _jnp.float32}
MOMENT_SCALE = {'ada_w': 2.208780e-02, 'ada_b': 4.025992e-02, 'ln_mix_g': 3.965982e+00, 'ln_mix_b': 1.882180e+00, 'ln_ffn_g': 6.444938e+01, 'ln_ffn_b': 3.631201e+00, 'ffn_w_gate': 1.178453e-02, 'ffn_w_up': 1.143943e-02, 'ffn_w_down': 4.525232e-02, 'mla_w_in': 1.370677e-02, 'mla_q_norm': 7.594490e-03, 'mla_w_q_b': 3.730403e-03, 'mla_kv_norm': 2.077817e-02, 'mla_w_kv_b': 7.203214e-03, 'mla_w_o': 2.293118e-02, 'swa_w_qkv': 1.086955e-02, 'swa_b_qkv': 4.138626e-02, 'swa_sinks': 6.100980e-03, 'swa_w_o': 2.893101e-02, 'swa_b_o': 1.249616e-01}


def _to_microbatches(a, axis):
    t = _jnp.moveaxis(a, axis, 0)
    t = t.reshape((N_MICROBATCH, t.shape[0] // N_MICROBATCH) + t.shape[1:])
    return _jnp.moveaxis(t, 1, axis + 1)


def setup_inputs(seed: int = 0) -> dict:
    inp = _fwd_setup_inputs(seed)
    key = _jax.random.fold_in(_jax.random.key(seed), 7919)
    shape, _ = _output_shape()
    out = dict(inp)
    out["loss_target"] = _jax.random.normal(_jax.random.fold_in(key, 0), shape, _jnp.float32)
    for i, name in enumerate(TWIN_WEIGHTS):
        w = inp[name].astype(_jnp.float32)
        if MOMENT_SCALE is None:
            s = _jnp.sqrt(_jnp.mean(_jnp.square(w)) + 1e-30)
        else:
            s = MOMENT_SCALE[name]
        km, kv = _jax.random.split(_jax.random.fold_in(key, i + 1))
        out[name] = w
        out["m_" + name] = s * _jax.random.normal(km, w.shape, _jnp.float32)
        out["v_" + name] = (s * s) * _jax.random.uniform(kv, w.shape, _jnp.float32, 0.5, 1.5)
    if N_MICROBATCH > 1:
        for name, axis in PER_EXAMPLE_BATCH_AXIS.items():
            out[name] = _to_microbatches(out[name], axis)
    return {'x': out['x'], 'c': out['c'], 'positions': out['positions'], 'ada_w': out['ada_w'], 'ada_b': out['ada_b'], 'ln_mix_g': out['ln_mix_g'], 'ln_mix_b': out['ln_mix_b'], 'ln_ffn_g': out['ln_ffn_g'], 'ln_ffn_b': out['ln_ffn_b'], 'ffn_w_gate': out['ffn_w_gate'], 'ffn_w_up': out['ffn_w_up'], 'ffn_w_down': out['ffn_w_down'], 'mla_w_in': out['mla_w_in'], 'mla_q_norm': out['mla_q_norm'], 'mla_w_q_b': out['mla_w_q_b'], 'mla_kv_norm': out['mla_kv_norm'], 'mla_w_kv_b': out['mla_w_kv_b'], 'mla_w_o': out['mla_w_o'], 'swa_w_qkv': out['swa_w_qkv'], 'swa_b_qkv': out['swa_b_qkv'], 'swa_sinks': out['swa_sinks'], 'swa_w_o': out['swa_w_o'], 'swa_b_o': out['swa_b_o'], 'loss_target': out['loss_target'], 'm_ada_w': out['m_ada_w'], 'm_ada_b': out['m_ada_b'], 'm_ln_mix_g': out['m_ln_mix_g'], 'm_ln_mix_b': out['m_ln_mix_b'], 'm_ln_ffn_g': out['m_ln_ffn_g'], 'm_ln_ffn_b': out['m_ln_ffn_b'], 'm_ffn_w_gate': out['m_ffn_w_gate'], 'm_ffn_w_up': out['m_ffn_w_up'], 'm_ffn_w_down': out['m_ffn_w_down'], 'm_mla_w_in': out['m_mla_w_in'], 'm_mla_q_norm': out['m_mla_q_norm'], 'm_mla_w_q_b': out['m_mla_w_q_b'], 'm_mla_kv_norm': out['m_mla_kv_norm'], 'm_mla_w_kv_b': out['m_mla_w_kv_b'], 'm_mla_w_o': out['m_mla_w_o'], 'm_swa_w_qkv': out['m_swa_w_qkv'], 'm_swa_b_qkv': out['m_swa_b_qkv'], 'm_swa_sinks': out['m_swa_sinks'], 'm_swa_w_o': out['m_swa_w_o'], 'm_swa_b_o': out['m_swa_b_o'], 'v_ada_w': out['v_ada_w'], 'v_ada_b': out['v_ada_b'], 'v_ln_mix_g': out['v_ln_mix_g'], 'v_ln_mix_b': out['v_ln_mix_b'], 'v_ln_ffn_g': out['v_ln_ffn_g'], 'v_ln_ffn_b': out['v_ln_ffn_b'], 'v_ffn_w_gate': out['v_ffn_w_gate'], 'v_ffn_w_up': out['v_ffn_w_up'], 'v_ffn_w_down': out['v_ffn_w_down'], 'v_mla_w_in': out['v_mla_w_in'], 'v_mla_q_norm': out['v_mla_q_norm'], 'v_mla_w_q_b': out['v_mla_w_q_b'], 'v_mla_kv_norm': out['v_mla_kv_norm'], 'v_mla_w_kv_b': out['v_mla_w_kv_b'], 'v_mla_w_o': out['v_mla_w_o'], 'v_swa_w_qkv': out['v_swa_w_qkv'], 'v_swa_b_qkv': out['v_swa_b_qkv'], 'v_swa_sinks': out['v_swa_sinks'], 'v_swa_w_o': out['v_swa_w_o'], 'v_swa_b_o': out['v_swa_b_o']}


def _loss(weights, diff, rest, loss_target):
    with _jax.named_scope("forward"):
        args = {**rest, TWIN_DIFF_INPUT: diff, **{k: w.astype(_WEIGHT_DTYPES[k]) for k, w in weights.items()}}
        y = _forward(args)
    with _jax.named_scope("loss_head"):
        err = _jnp.square(y.astype(_jnp.float32) - loss_target)
        return 0.5 * _jnp.sum(_jnp.mean(err, axis=-1)) if err.ndim else 0.5 * err


def _adamw(w, g, m, v):
    m = ADAM_B1 * m + (1.0 - ADAM_B1) * g
    v = ADAM_B2 * v + (1.0 - ADAM_B2) * _jnp.square(g)
    m_hat = m / (1.0 - ADAM_B1 ** ADAM_STEP)
    v_hat = v / (1.0 - ADAM_B2 ** ADAM_STEP)
    delta = -ADAM_LR * (m_hat / (_jnp.sqrt(v_hat) + ADAM_EPS) + ADAM_WD * w)
    return delta, m, v


def reference(x, c, positions, ada_w, ada_b, ln_mix_g, ln_mix_b, ln_ffn_g, ln_ffn_b, ffn_w_gate, ffn_w_up, ffn_w_down, mla_w_in, mla_q_norm, mla_w_q_b, mla_kv_norm, mla_w_kv_b, mla_w_o, swa_w_qkv, swa_b_qkv, swa_sinks, swa_w_o, swa_b_o, loss_target, m_ada_w, m_ada_b, m_ln_mix_g, m_ln_mix_b, m_ln_ffn_g, m_ln_ffn_b, m_ffn_w_gate, m_ffn_w_up, m_ffn_w_down, m_mla_w_in, m_mla_q_norm, m_mla_w_q_b, m_mla_kv_norm, m_mla_w_kv_b, m_mla_w_o, m_swa_w_qkv, m_swa_b_qkv, m_swa_sinks, m_swa_w_o, m_swa_b_o, v_ada_w, v_ada_b, v_ln_mix_g, v_ln_mix_b, v_ln_ffn_g, v_ln_ffn_b, v_ffn_w_gate, v_ffn_w_up, v_ffn_w_down, v_mla_w_in, v_mla_q_norm, v_mla_w_q_b, v_mla_kv_norm, v_mla_w_kv_b, v_mla_w_o, v_swa_w_qkv, v_swa_b_qkv, v_swa_sinks, v_swa_w_o, v_swa_b_o):
    given = dict(x=x, c=c, positions=positions, ada_w=ada_w, ada_b=ada_b, ln_mix_g=ln_mix_g, ln_mix_b=ln_mix_b, ln_ffn_g=ln_ffn_g, ln_ffn_b=ln_ffn_b, ffn_w_gate=ffn_w_gate, ffn_w_up=ffn_w_up, ffn_w_down=ffn_w_down, mla_w_in=mla_w_in, mla_q_norm=mla_q_norm, mla_w_q_b=mla_w_q_b, mla_kv_norm=mla_kv_norm, mla_w_kv_b=mla_w_kv_b, mla_w_o=mla_w_o, swa_w_qkv=swa_w_qkv, swa_b_qkv=swa_b_qkv, swa_sinks=swa_sinks, swa_w_o=swa_w_o, swa_b_o=swa_b_o, loss_target=loss_target, m_ada_w=m_ada_w, m_ada_b=m_ada_b, m_ln_mix_g=m_ln_mix_g, m_ln_mix_b=m_ln_mix_b, m_ln_ffn_g=m_ln_ffn_g, m_ln_ffn_b=m_ln_ffn_b, m_ffn_w_gate=m_ffn_w_gate, m_ffn_w_up=m_ffn_w_up, m_ffn_w_down=m_ffn_w_down, m_mla_w_in=m_mla_w_in, m_mla_q_norm=m_mla_q_norm, m_mla_w_q_b=m_mla_w_q_b, m_mla_kv_norm=m_mla_kv_norm, m_mla_w_kv_b=m_mla_w_kv_b, m_mla_w_o=m_mla_w_o, m_swa_w_qkv=m_swa_w_qkv, m_swa_b_qkv=m_swa_b_qkv, m_swa_sinks=m_swa_sinks, m_swa_w_o=m_swa_w_o, m_swa_b_o=m_swa_b_o, v_ada_w=v_ada_w, v_ada_b=v_ada_b, v_ln_mix_g=v_ln_mix_g, v_ln_mix_b=v_ln_mix_b, v_ln_ffn_g=v_ln_ffn_g, v_ln_ffn_b=v_ln_ffn_b, v_ffn_w_gate=v_ffn_w_gate, v_ffn_w_up=v_ffn_w_up, v_ffn_w_down=v_ffn_w_down, v_mla_w_in=v_mla_w_in, v_mla_q_norm=v_mla_q_norm, v_mla_w_q_b=v_mla_w_q_b, v_mla_kv_norm=v_mla_kv_norm, v_mla_w_kv_b=v_mla_w_kv_b, v_mla_w_o=v_mla_w_o, v_swa_w_qkv=v_swa_w_qkv, v_swa_b_qkv=v_swa_b_qkv, v_swa_sinks=v_swa_sinks, v_swa_w_o=v_swa_w_o, v_swa_b_o=v_swa_b_o)
    weights = {n: given[n] for n in TWIN_WEIGHTS}
    shared = {n: given[n] for n in SHARED_INPUTS}
    per_example = {n: given[n] for n in ['x', 'c', 'positions']}
    grad_fn = _jax.value_and_grad(_loss, argnums=(0, 1))

    def one_microbatch(ex, loss_target):
        ex = dict(ex)
        diff = ex.pop(TWIN_DIFF_INPUT)
        return grad_fn(weights, diff, {**shared, **ex}, loss_target)

    if N_MICROBATCH == 1:
        loss, (grad_w, grad_x) = one_microbatch(per_example, given["loss_target"])
    else:
        def body(carry, xs):
            loss_sum, grad_sum = carry
            l_k, (gw_k, gx_k) = one_microbatch(xs[0], xs[1])
            with _jax.named_scope("update"):
                return (loss_sum + l_k, _jax.tree.map(_jnp.add, grad_sum, gw_k)), gx_k

        init = (_jnp.zeros((), _jnp.float32), _jax.tree.map(_jnp.zeros_like, weights))
        (loss, grad_w), grad_x = _jax.lax.scan(body, init, (per_example, given["loss_target"]))
    with _jax.named_scope("update"):
        delta_w, new_m, new_v = {}, {}, {}
        for n in TWIN_WEIGHTS:
            delta_w[n], new_m[n], new_v[n] = _adamw(weights[n], grad_w[n], given["m_" + n], given["v_" + n])
    return (loss, grad_x, *[grad_w[n] for n in TWIN_WEIGHTS], *[delta_w[n] for n in TWIN_WEIGHTS],
            *[new_m[n] for n in TWIN_WEIGHTS], *[new_v[n] for n in TWIN_WEIGHTS])
```

```python
import jax
import jax.numpy as jnp
from jax import lax
from jax.experimental import pallas as pl
from jax.experimental.pallas import tpu as pltpu

F32 = jnp.float32
BF16 = jnp.bfloat16

D = 1024
DEPTH = 4
ROPE_THETA = 500000.0
LN_EPS = 1e-5
RMS_EPS = 1e-6
MLA_H = 8
MLA_NOPE = 128
MLA_ROPE = 64
MLA_V = 128
MLA_QR = 384
MLA_KVR = 256
MLA_LAT = 768
SWA_HQ = 16
SWA_HKV = 4
SWA_HD = 64
SWA_W = 128
SWA_ROT = 16
FF = 2816
ALPHA = (2 * DEPTH) ** 0.25
ADAM_LR = 0.001
ADAM_B1 = 0.9
ADAM_B2 = 0.999
ADAM_EPS = 1e-08
ADAM_WD = 0.01
ADAM_STEP = 10
NEG = -1e30
LANES = 128
N_CHIPS = 4
PACK_COLS = 1024
PACK_ROW_ALIGN = 1024
SMALL_ROWS = 512
ROW_TILE = 512

SHARDED = (
    ("ffn_w_gate", 2), ("ffn_w_up", 2), ("ffn_w_down", 1), ("mla_w_in", 1), ("mla_w_q_b", 2),
    ("mla_w_kv_b", 2), ("mla_w_o", 1), ("swa_w_qkv", 2), ("swa_b_qkv", 1), ("swa_w_o", 1), ("swa_b_o", 1),
)


def _cparams(*sem):
    return pltpu.CompilerParams(dimension_semantics=sem)


def _row_spec(tr, cols):
    return pl.BlockSpec((tr, cols), lambda i: (i, 0))


def _vec_spec(cols):
    return pl.BlockSpec((1, cols), lambda i: (0, 0))


def _rope(x, c, sa, sb, sh):
    n = x.shape[1]
    return x * c + pltpu.roll(x, n - sh, 1) * sa + pltpu.roll(x, sh, 1) * sb


def _rope_t(d, c, sa, sb, sh):
    n = d.shape[1]
    return d * c + pltpu.roll(d * sa, sh, 1) + pltpu.roll(d * sb, n - sh, 1)


def _rowsum8(t):
    r, n = t.shape
    return jnp.sum(t.reshape(r // 8, 8, n), axis=0)


def _exchange(name, src, axes, mode):
    g = 2 ** len(axes)
    blk = src.shape if mode == "gather" else src.shape[1:]

    def body(src_ref, out_ref, send_sems, recv_sems, loc_sem):
        pos = {a: lax.axis_index(a) for a in ("x", "y", "c")}

        def gidx(p):
            idx = 0
            for a in axes:
                idx = idx * 2 + p[a]
            return idx

        def view(i):
            return src_ref if mode == "gather" else src_ref.at[i]

        me = gidx(pos)
        loc = pltpu.make_async_copy(view(me), out_ref.at[me], loc_sem)
        loc.start()
        copies = []
        for j in range(1, g):
            peer = dict(pos)
            for bit, a in enumerate(reversed(axes)):
                if (j >> bit) & 1:
                    peer[a] = 1 - pos[a]
            cp = pltpu.make_async_remote_copy(
                src_ref=view(gidx(peer)), dst_ref=out_ref.at[me],
                send_sem=send_sems.at[j - 1], recv_sem=recv_sems.at[j - 1],
                device_id=(peer["x"], peer["y"], peer["c"]), device_id_type=pl.DeviceIdType.MESH)
            cp.start()
            copies.append(cp)
        for cp in copies:
            cp.wait()
        loc.wait()

    return pl.pallas_call(
        body, name=name,
        out_shape=jax.ShapeDtypeStruct((g,) + tuple(blk), src.dtype),
        in_specs=[pl.BlockSpec(memory_space=pl.ANY)],
        out_specs=pl.BlockSpec(memory_space=pl.ANY),
        scratch_shapes=[pltpu.SemaphoreType.DMA((g - 1,)), pltpu.SemaphoreType.DMA((g - 1,)),
                        pltpu.SemaphoreType.DMA(())],
    )(src)


def _sum_groups(name, a):
    g, r, c = a.shape
    tr = min(ROW_TILE, r)

    def body(a_ref, o_ref):
        acc = a_ref[0]
        for i in range(1, g):
            acc = acc + a_ref[i]
        o_ref[...] = acc

    return pl.pallas_call(
        body, name=name, grid=(r // tr,),
        in_specs=[pl.BlockSpec((g, tr, c), lambda i: (0, i, 0))],
        out_specs=pl.BlockSpec((tr, c), lambda i: (i, 0)),
        out_shape=jax.ShapeDtypeStruct((r, c), F32),
        compiler_params=_cparams("parallel"),
    )(a)


def _mm(name, a, b, *, tm, tn=None, out_dtype=F32, epilogue=None, extras=(), extra_specs=()):
    m, k = a.shape
    n = b.shape[1]
    tn = tn or n
    tm = min(tm, m)

    def body(a_ref, b_ref, *rest):
        o_ref = rest[-1]
        acc = jnp.dot(a_ref[...], b_ref[...], preferred_element_type=F32)
        if epilogue is None:
            o_ref[...] = acc.astype(o_ref.dtype)
        else:
            epilogue(acc, o_ref, *rest[:-1])

    return pl.pallas_call(
        body, name=name, grid=(m // tm, n // tn),
        in_specs=[pl.BlockSpec((tm, k), lambda i, j: (i, 0)), pl.BlockSpec((k, tn), lambda i, j: (0, j)),
                  *extra_specs],
        out_specs=pl.BlockSpec((tm, tn), lambda i, j: (i, j)),
        out_shape=jax.ShapeDtypeStruct((m, n), out_dtype),
        compiler_params=_cparams("parallel", "parallel"),
    )(a, b, *extras)


def _mm_tn(name, a, b, *, tn=None, tk=1024):
    s, m = a.shape
    n = b.shape[1]
    tn = tn or n
    tk = min(tk, s)

    def body(a_ref, b_ref, o_ref):
        part = lax.dot_general(a_ref[...], b_ref[...], (((0,), (0,)), ((), ())), preferred_element_type=F32)

        @pl.when(pl.program_id(1) == 0)
        def _():
            o_ref[...] = part

        @pl.when(pl.program_id(1) > 0)
        def _():
            o_ref[...] += part

    return pl.pallas_call(
        body, name=name, grid=(n // tn, s // tk),
        in_specs=[pl.BlockSpec((tk, m), lambda j, k: (k, 0)), pl.BlockSpec((tk, tn), lambda j, k: (k, j))],
        out_specs=pl.BlockSpec((m, tn), lambda j, k: (0, j)),
        out_shape=jax.ShapeDtypeStruct((m, n), F32),
        compiler_params=_cparams("parallel", "arbitrary"),
    )(a, b)


def _modulate(name, x, sc, sh):
    s = x.shape[0]
    tr = min(ROW_TILE, s)

    def body(x_ref, sc_ref, sh_ref, h_ref):
        h_ref[...] = (x_ref[...] * (1.0 + sc_ref[...]) + sh_ref[...]).astype(BF16)

    return pl.pallas_call(
        body, name=name, grid=(s // tr,),
        in_specs=[_row_spec(tr, D), _vec_spec(D), _vec_spec(D)],
        out_specs=_row_spec(tr, D),
        out_shape=jax.ShapeDtypeStruct((s, D), BF16),
        compiler_params=_cparams("parallel"),
    )(x, sc, sh)


def _ln_stats(z):
    mu = jnp.mean(z, axis=1, keepdims=True)
    zc = z - mu
    var = jnp.mean(zc * zc, axis=1, keepdims=True)
    r = lax.rsqrt(var + LN_EPS)
    return zc * r, r


def _post_mod(name, x, y, g, gamma, beta, sc, sh):
    s = x.shape[0]
    tr = min(ROW_TILE, s)

    def body(x_ref, y_ref, g_ref, ga_ref, be_ref, sc_ref, sh_ref, xn_ref, h_ref):
        zh, _ = _ln_stats(ALPHA * x_ref[...] + g_ref[...] * y_ref[...])
        xn = zh * ga_ref[...] + be_ref[...]
        xn_ref[...] = xn
        h_ref[...] = (xn * (1.0 + sc_ref[...]) + sh_ref[...]).astype(BF16)

    return pl.pallas_call(
        body, name=name, grid=(s // tr,),
        in_specs=[_row_spec(tr, D), _row_spec(tr, D)] + [_vec_spec(D)] * 5,
        out_specs=[_row_spec(tr, D), _row_spec(tr, D)],
        out_shape=[jax.ShapeDtypeStruct((s, D), F32), jax.ShapeDtypeStruct((s, D), BF16)],
        compiler_params=_cparams("parallel"),
    )(x, y, g, gamma, beta, sc, sh)


def _post_loss(name, x, y, g, gamma, beta, target):
    s = x.shape[0]
    tr = min(ROW_TILE, s)
    nt = s // tr

    def body(x_ref, y_ref, g_ref, ga_ref, be_ref, t_ref, dx_ref, loss_ref, acc):
        i = pl.program_id(0)
        zh, _ = _ln_stats(ALPHA * x_ref[...] + g_ref[...] * y_ref[...])
        e = zh * ga_ref[...] + be_ref[...] - t_ref[...]
        dx_ref[...] = e * (1.0 / D)

        @pl.when(i == 0)
        def _():
            acc[...] = jnp.zeros_like(acc)

        acc[...] += _rowsum8(e * e)

        @pl.when(i == nt - 1)
        def _():
            loss_ref[...] = jnp.full(loss_ref.shape, jnp.sum(acc[...]) * (0.5 / D), F32)

    return pl.pallas_call(
        body, name=name, grid=(nt,),
        in_specs=[_row_spec(tr, D), _row_spec(tr, D)] + [_vec_spec(D)] * 3 + [_row_spec(tr, D)],
        out_specs=[_row_spec(tr, D), pl.BlockSpec((8, LANES), lambda i: (0, 0))],
        out_shape=[jax.ShapeDtypeStruct((s, D), F32), jax.ShapeDtypeStruct((8, LANES), F32)],
        scratch_shapes=[pltpu.VMEM((8, D), F32)],
        compiler_params=_cparams("arbitrary"),
    )(x, y, g, gamma, beta, target)


def _post_bwd(name, dxn, x, y, g, gamma):
    s = x.shape[0]
    tr = min(ROW_TILE, s)
    nt = s // tr

    def body(d_ref, x_ref, y_ref, g_ref, ga_ref, dxp_ref, dy_ref, sums_ref, a0, a1, a2, a3):
        i = pl.program_id(0)
        yv = y_ref[...]
        gv = g_ref[...]
        zh, r = _ln_stats(ALPHA * x_ref[...] + gv * yv)
        dxn_v = d_ref[...]
        dzh = dxn_v * ga_ref[...]
        dz = r * (dzh - jnp.mean(dzh, axis=1, keepdims=True) - zh * jnp.mean(dzh * zh, axis=1, keepdims=True))
        dxp_ref[...] = ALPHA * dz
        dyv = gv * dz
        dy_ref[...] = dyv.astype(BF16)

        @pl.when(i == 0)
        def _():
            for a in (a0, a1, a2, a3):
                a[...] = jnp.zeros_like(a)

        a0[...] += _rowsum8(dxn_v * zh)
        a1[...] += _rowsum8(dxn_v)
        a2[...] += _rowsum8(dz * yv)
        a3[...] += _rowsum8(dyv)

        @pl.when(i == nt - 1)
        def _():
            for k, a in enumerate((a0, a1, a2, a3)):
                sums_ref[k:k + 1, :] = jnp.sum(a[...], axis=0, keepdims=True)

    return pl.pallas_call(
        body, name=name, grid=(nt,),
        in_specs=[_row_spec(tr, D)] * 3 + [_vec_spec(D)] * 2,
        out_specs=[_row_spec(tr, D), _row_spec(tr, D), pl.BlockSpec((4, D), lambda i: (0, 0))],
        out_shape=[jax.ShapeDtypeStruct((s, D), F32), jax.ShapeDtypeStruct((s, D), BF16),
                   jax.ShapeDtypeStruct((4, D), F32)],
        scratch_shapes=[pltpu.VMEM((8, D), F32)] * 4,
        compiler_params=_cparams("arbitrary"),
    )(dxn, x, y, g, gamma)


def _mod_bwd(name, dh, x, dxp, sc):
    s = x.shape[0]
    tr = min(ROW_TILE, s)
    nt = s // tr

    def body(dh_ref, x_ref, dxp_ref, sc_ref, dx_ref, sums_ref, a0, a1):
        i = pl.program_id(0)
        dhv = dh_ref[...]
        dx_ref[...] = dxp_ref[...] + dhv * (1.0 + sc_ref[...])

        @pl.when(i == 0)
        def _():
            a0[...] = jnp.zeros_like(a0)
            a1[...] = jnp.zeros_like(a1)

        a0[...] += _rowsum8(dhv * x_ref[...])
        a1[...] += _rowsum8(dhv)

        @pl.when(i == nt - 1)
        def _():
            sums_ref[0:1, :] = jnp.sum(a0[...], axis=0, keepdims=True)
            sums_ref[1:2, :] = jnp.sum(a1[...], axis=0, keepdims=True)

    return pl.pallas_call(
        body, name=name, grid=(nt,),
        in_specs=[_row_spec(tr, D)] * 3 + [_vec_spec(D)],
        out_specs=[_row_spec(tr, D), pl.BlockSpec((2, D), lambda i: (0, 0))],
        out_shape=[jax.ShapeDtypeStruct((s, D), F32), jax.ShapeDtypeStruct((2, D), F32)],
        scratch_shapes=[pltpu.VMEM((8, D), F32)] * 2,
        compiler_params=_cparams("arbitrary"),
    )(dh, x, dxp, sc)


def _ffn_tiles(s):
    return min(ROW_TILE, s), FF // 2


def _ffn_fwd(name, h, wg, wu, wd):
    s = h.shape[0]
    tm, tf = _ffn_tiles(s)

    def body(h_ref, wg_ref, wu_ref, wd_ref, gate_ref, up_ref, y_ref):
        hv = h_ref[...]
        gt = jnp.dot(hv, wg_ref[...], preferred_element_type=F32)
        up = jnp.dot(hv, wu_ref[...], preferred_element_type=F32)
        gate_ref[...] = gt
        up_ref[...] = up
        act = (gt * jax.nn.sigmoid(gt) * up).astype(BF16)
        part = jnp.dot(act, wd_ref[...], preferred_element_type=F32)

        @pl.when(pl.program_id(1) == 0)
        def _():
            y_ref[...] = part

        @pl.when(pl.program_id(1) > 0)
        def _():
            y_ref[...] += part

    return pl.pallas_call(
        body, name=name, grid=(s // tm, FF // tf),
        in_specs=[pl.BlockSpec((tm, D), lambda i, f: (i, 0)), pl.BlockSpec((D, tf), lambda i, f: (0, f)),
                  pl.BlockSpec((D, tf), lambda i, f: (0, f)), pl.BlockSpec((tf, D), lambda i, f: (f, 0))],
        out_specs=[pl.BlockSpec((tm, tf), lambda i, f: (i, f)), pl.BlockSpec((tm, tf), lambda i, f: (i, f)),
                   pl.BlockSpec((tm, D), lambda i, f: (i, 0))],
        out_shape=[jax.ShapeDtypeStruct((s, FF), F32), jax.ShapeDtypeStruct((s, FF), F32),
                   jax.ShapeDtypeStruct((s, D), F32)],
        compiler_params=_cparams("parallel", "arbitrary"),
    )(h, wg, wu, wd)


def _ffn_bwd(name, dy, gate, up, wd_t, wg_t, wu_t):
    s = dy.shape[0]
    tm, tf = _ffn_tiles(s)

    def body(dy_ref, gate_ref, up_ref, wdt_ref, wgt_ref, wut_ref, dg_ref, du_ref, act_ref, dh_ref):
        dact = jnp.dot(dy_ref[...], wdt_ref[...], preferred_element_type=F32)
        gt = gate_ref[...]
        up = up_ref[...]
        sig = jax.nn.sigmoid(gt)
        silu = gt * sig
        dgt = (dact * up * (sig * (1.0 + gt * (1.0 - sig)))).astype(BF16)
        dup = (dact * silu).astype(BF16)
        dg_ref[...] = dgt
        du_ref[...] = dup
        act_ref[...] = (silu * up).astype(BF16)
        part = (jnp.dot(dgt, wgt_ref[...], preferred_element_type=F32)
                + jnp.dot(dup, wut_ref[...], preferred_element_type=F32))

        @pl.when(pl.program_id(1) == 0)
        def _():
            dh_ref[...] = part

        @pl.when(pl.program_id(1) > 0)
        def _():
            dh_ref[...] += part

    tile = pl.BlockSpec((tm, tf), lambda i, f: (i, f))
    return pl.pallas_call(
        body, name=name, grid=(s // tm, FF // tf),
        in_specs=[pl.BlockSpec((tm, D), lambda i, f: (i, 0)), tile, tile,
                  pl.BlockSpec((D, tf), lambda i, f: (0, f)), pl.BlockSpec((tf, D), lambda i, f: (f, 0)),
                  pl.BlockSpec((tf, D), lambda i, f: (f, 0))],
        out_specs=[tile, tile, tile, pl.BlockSpec((tm, D), lambda i, f: (i, 0))],
        out_shape=[jax.ShapeDtypeStruct((s, FF), BF16)] * 3 + [jax.ShapeDtypeStruct((s, D), F32)],
        compiler_params=_cparams("parallel", "arbitrary"),
    )(dy, gate, up, wd_t, wg_t, wu_t)


def _mla_lat_post(name, lat, qw, kvw, rc, rsa, rsb):
    s = lat.shape[0]
    tr = min(ROW_TILE, s)

    def body(lat_ref, qw_ref, kvw_ref, c_ref, sa_ref, sb_ref, qn_ref, kvn_ref, kr_ref):
        ql = lat_ref[:, 0:MLA_QR]
        kl = lat_ref[:, MLA_QR:MLA_QR + MLA_KVR]
        qn_ref[...] = (ql * lax.rsqrt(jnp.mean(ql * ql, axis=1, keepdims=True) + RMS_EPS) * qw_ref[...]).astype(BF16)
        kvn_ref[...] = (kl * lax.rsqrt(jnp.mean(kl * kl, axis=1, keepdims=True) + RMS_EPS) * kvw_ref[...]).astype(BF16)
        kr_ref[...] = _rope(lat_ref[:, MLA_QR + MLA_KVR:MLA_LAT], c_ref[...], sa_ref[...], sb_ref[...],
                            MLA_ROPE // 2).astype(BF16)

    return pl.pallas_call(
        body, name=name, grid=(s // tr,),
        in_specs=[_row_spec(tr, MLA_LAT), _vec_spec(MLA_QR), _vec_spec(MLA_KVR)] + [_row_spec(tr, LANES)] * 3,
        out_specs=[_row_spec(tr, MLA_QR), _row_spec(tr, MLA_KVR), _row_spec(tr, LANES)],
        out_shape=[jax.ShapeDtypeStruct((s, MLA_QR), BF16), jax.ShapeDtypeStruct((s, MLA_KVR), BF16),
                   jax.ShapeDtypeStruct((s, LANES), BF16)],
        compiler_params=_cparams("parallel"),
    )(lat, qw, kvw, rc, rsa, rsb)


def _mla_lat_bwd(name, lat, dqn, dkvn, dkr_heads, qw, kvw, rc, rsa, rsb):
    s = lat.shape[0]
    tr = min(ROW_TILE, s)
    nt = s // tr

    def rms_bwd(x, w, dy):
        r = lax.rsqrt(jnp.mean(x * x, axis=1, keepdims=True) + RMS_EPS)
        xh = x * r
        gdy = dy * w
        return r * (gdy - xh * jnp.mean(gdy * xh, axis=1, keepdims=True)), dy * xh

    def body(lat_ref, dqn_ref, dkvn_ref, dkr_ref, qw_ref, kvw_ref, c_ref, sa_ref, sb_ref,
             dlat_ref, dqw_ref, dkvw_ref, aq, akv):
        i = pl.program_id(0)
        dq, dqw = rms_bwd(lat_ref[:, 0:MLA_QR], qw_ref[...], dqn_ref[...])
        dk, dkw = rms_bwd(lat_ref[:, MLA_QR:MLA_QR + MLA_KVR], kvw_ref[...], dkvn_ref[...])
        dkr = dkr_ref[0]
        for hh in range(1, MLA_H):
            dkr = dkr + dkr_ref[hh]
        dkr = _rope_t(dkr, c_ref[...], sa_ref[...], sb_ref[...], MLA_ROPE // 2)
        dlat_ref[:, 0:MLA_QR] = dq.astype(BF16)
        dlat_ref[:, MLA_QR:MLA_QR + MLA_KVR] = dk.astype(BF16)
        dlat_ref[:, MLA_QR + MLA_KVR:MLA_LAT] = dkr.astype(BF16)

        @pl.when(i == 0)
        def _():
            aq[...] = jnp.zeros_like(aq)
            akv[...] = jnp.zeros_like(akv)

        aq[...] += _rowsum8(dqw)
        akv[...] += _rowsum8(dkw)

        @pl.when(i == nt - 1)
        def _():
            dqw_ref[...] = jnp.sum(aq[...], axis=0, keepdims=True)
            dkvw_ref[...] = jnp.sum(akv[...], axis=0, keepdims=True)

    return pl.pallas_call(
        body, name=name, grid=(nt,),
        in_specs=[_row_spec(tr, MLA_LAT), _row_spec(tr, MLA_QR), _row_spec(tr, MLA_KVR),
                  pl.BlockSpec((MLA_H, tr, LANES), lambda i: (0, i, 0)), _vec_spec(MLA_QR), _vec_spec(MLA_KVR)]
        + [_row_spec(tr, LANES)] * 3,
        out_specs=[_row_spec(tr, MLA_LAT), _vec_spec(MLA_QR), _vec_spec(MLA_KVR)],
        out_shape=[jax.ShapeDtypeStruct((s, MLA_LAT), BF16), jax.ShapeDtypeStruct((1, MLA_QR), F32),
                   jax.ShapeDtypeStruct((1, MLA_KVR), F32)],
        scratch_shapes=[pltpu.VMEM((8, MLA_QR), F32), pltpu.VMEM((8, MLA_KVR), F32)],
        compiler_params=_cparams("arbitrary"),
    )(lat, dqn, dkvn, dkr_heads, qw, kvw, rc, rsa, rsb)


def _attn_tile(s):
    return min(1024, max(LANES, s // 2))


MLA_SCALE = (MLA_NOPE + MLA_ROPE) ** -0.5
NT_DIMS = (((1,), (1,)), ((), ()))
TN_DIMS = (((0,), (0,)), ((), ()))


def _causal(sc):
    row = lax.broadcasted_iota(jnp.int32, sc.shape, 0)
    col = lax.broadcasted_iota(jnp.int32, sc.shape, 1)
    return jnp.where(col <= row, sc, NEG)


def _mla_attn_fwd(name, qq, kv, kr):
    s = qq.shape[0]
    t = _attn_tile(s)
    n = s // t

    def body(q_ref, kv_ref, kr_ref, o_ref, lse_ref, m_scr, l_scr, acc_scr):
        qi = pl.program_id(1)
        ki = pl.program_id(2)

        @pl.when(ki == 0)
        def _():
            m_scr[...] = jnp.full(m_scr.shape, NEG, F32)
            l_scr[...] = jnp.zeros_like(l_scr)
            acc_scr[...] = jnp.zeros_like(acc_scr)

        def step(diag):
            k = jnp.concatenate([kv_ref[:, 0:LANES], kr_ref[...]], axis=1)
            sc = lax.dot_general(q_ref[...], k, NT_DIMS, preferred_element_type=F32)
            if diag:
                sc = _causal(sc)
            m_prev = m_scr[...]
            m_next = jnp.maximum(m_prev, jnp.max(sc, axis=1, keepdims=True))
            a = jnp.exp(MLA_SCALE * (m_prev - m_next))
            p = jnp.exp(MLA_SCALE * (sc - m_next[:, 0:1]))
            l_scr[...] = a * l_scr[...] + jnp.sum(p, axis=1, keepdims=True)
            acc_scr[...] = a * acc_scr[...] + jnp.dot(p.astype(BF16), kv_ref[:, LANES:2 * LANES],
                                                      preferred_element_type=F32)
            m_scr[...] = m_next

        @pl.when(ki < qi)
        def _():
            step(False)

        @pl.when(ki == qi)
        def _():
            step(True)
            l = l_scr[...]
            o_ref[...] = (acc_scr[...] / l).astype(BF16)
            lse_ref[...] = MLA_SCALE * m_scr[...] + jnp.log(l)

    qblk = lambda w: pl.BlockSpec((t, w), lambda h, qi, ki: (qi, h))
    return pl.pallas_call(
        body, name=name, grid=(MLA_H, n, n),
        in_specs=[qblk(2 * LANES), pl.BlockSpec((t, 2 * LANES), lambda h, qi, ki: (jnp.minimum(ki, qi), h)),
                  pl.BlockSpec((t, LANES), lambda h, qi, ki: (jnp.minimum(ki, qi), 0))],
        out_specs=[qblk(LANES), qblk(LANES)],
        out_shape=[jax.ShapeDtypeStruct((s, MLA_H * MLA_V), BF16), jax.ShapeDtypeStruct((s, MLA_H * LANES), F32)],
        scratch_shapes=[pltpu.VMEM((t, LANES), F32)] * 3,
        compiler_params=_cparams("parallel", "parallel", "arbitrary"),
    )(qq, kv, kr)


def _mla_attn_dq(name, qq, kv, kr, do, o, lse, rc, rsa, rsb):
    s = qq.shape[0]
    t = _attn_tile(s)
    n = s // t

    def body(q_ref, kv_ref, kr_ref, do_ref, o_ref, lse_ref, c_ref, sa_ref, sb_ref,
             dq_ref, delta_ref, acc_scr, delta_scr):
        qi = pl.program_id(1)
        ki = pl.program_id(2)

        @pl.when(ki == 0)
        def _():
            acc_scr[...] = jnp.zeros_like(acc_scr)
            dl = jnp.sum(do_ref[...].astype(F32) * o_ref[...].astype(F32), axis=1, keepdims=True)
            delta_scr[...] = jnp.broadcast_to(dl, delta_scr.shape)

        def step(diag):
            k = jnp.concatenate([kv_ref[:, 0:LANES], kr_ref[...]], axis=1)
            sc = lax.dot_general(q_ref[...], k, NT_DIMS, preferred_element_type=F32)
            if diag:
                sc = _causal(sc)
            p = jnp.exp(MLA_SCALE * sc - lse_ref[:, 0:1])
            dp = lax.dot_general(do_ref[...], kv_ref[:, LANES:2 * LANES], NT_DIMS, preferred_element_type=F32)
            ds = (p * (dp - delta_scr[:, 0:1]) * MLA_SCALE).astype(BF16)
            acc_scr[...] += jnp.dot(ds, k, preferred_element_type=F32)

        @pl.when(ki < qi)
        def _():
            step(False)

        @pl.when(ki == qi)
        def _():
            step(True)
            dq_ref[:, 0:LANES] = acc_scr[:, 0:LANES].astype(BF16)
            dq_ref[:, LANES:2 * LANES] = _rope_t(acc_scr[:, LANES:2 * LANES], c_ref[...], sa_ref[...], sb_ref[...],
                                                 MLA_ROPE // 2).astype(BF16)
            delta_ref[...] = delta_scr[...]

    qblk = lambda w: pl.BlockSpec((t, w), lambda h, qi, ki: (qi, h))
    tab = pl.BlockSpec((t, LANES), lambda h, qi, ki: (qi, 0))
    return pl.pallas_call(
        body, name=name, grid=(MLA_H, n, n),
        in_specs=[qblk(2 * LANES), pl.BlockSpec((t, 2 * LANES), lambda h, qi, ki: (jnp.minimum(ki, qi), h)),
                  pl.BlockSpec((t, LANES), lambda h, qi, ki: (jnp.minimum(ki, qi), 0)),
                  qblk(LANES), qblk(LANES), qblk(LANES), tab, tab, tab],
        out_specs=[qblk(2 * LANES), qblk(LANES)],
        out_shape=[jax.ShapeDtypeStruct((s, 2 * MLA_H * LANES), BF16),
                   jax.ShapeDtypeStruct((s, MLA_H * LANES), F32)],
        scratch_shapes=[pltpu.VMEM((t, 2 * LANES), F32), pltpu.VMEM((t, LANES), F32)],
        compiler_params=_cparams("parallel", "parallel", "arbitrary"),
    )(qq, kv, kr, do, o, lse, rc, rsa, rsb)


def _mla_attn_dkv(name, qq, kv, kr, do, lse, delta):
    s = qq.shape[0]
    t = _attn_tile(s)
    n = s // t

    def body(q_ref, kv_ref, kr_ref, do_ref, lse_ref, delta_ref, dkv_ref, dkr_ref, dk_scr, dv_scr):
        ki = pl.program_id(1)
        qi = pl.program_id(2)

        @pl.when(qi == 0)
        def _():
            dk_scr[...] = jnp.zeros_like(dk_scr)
            dv_scr[...] = jnp.zeros_like(dv_scr)

        def step(diag):
            q = q_ref[...]
            k = jnp.concatenate([kv_ref[:, 0:LANES], kr_ref[...]], axis=1)
            sc = lax.dot_general(q, k, NT_DIMS, preferred_element_type=F32)
            if diag:
                sc = _causal(sc)
            p = jnp.exp(MLA_SCALE * sc - lse_ref[:, 0:1])
            dov = do_ref[...]
            dp = lax.dot_general(dov, kv_ref[:, LANES:2 * LANES], NT_DIMS, preferred_element_type=F32)
            ds = (p * (dp - delta_ref[:, 0:1]) * MLA_SCALE).astype(BF16)
            dv_scr[...] += lax.dot_general(p.astype(BF16), dov, TN_DIMS, preferred_element_type=F32)
            dk_scr[...] += lax.dot_general(ds, q, TN_DIMS, preferred_element_type=F32)

        @pl.when(qi == ki)
        def _():
            step(True)

        @pl.when(qi > ki)
        def _():
            step(False)

        @pl.when(qi == n - 1)
        def _():
            dkv_ref[:, 0:LANES] = dk_scr[:, 0:LANES].astype(BF16)
            dkv_ref[:, LANES:2 * LANES] = dv_scr[...].astype(BF16)
            dkr_ref[0] = dk_scr[:, LANES:2 * LANES]

    qblk = lambda w: pl.BlockSpec((t, w), lambda h, ki, qi: (jnp.maximum(qi, ki), h))
    kblk = pl.BlockSpec((t, 2 * LANES), lambda h, ki, qi: (ki, h))
    return pl.pallas_call(
        body, name=name, grid=(MLA_H, n, n),
        in_specs=[qblk(2 * LANES), kblk, pl.BlockSpec((t, LANES), lambda h, ki, qi: (ki, 0)),
                  qblk(LANES), qblk(LANES), qblk(LANES)],
        out_specs=[kblk, pl.BlockSpec((1, t, LANES), lambda h, ki, qi: (h, ki, 0))],
        out_shape=[jax.ShapeDtypeStruct((s, 2 * MLA_H * LANES), BF16),
                   jax.ShapeDtypeStruct((MLA_H, s, LANES), F32)],
        scratch_shapes=[pltpu.VMEM((t, 2 * LANES), F32), pltpu.VMEM((t, LANES), F32)],
        compiler_params=_cparams("parallel", "parallel", "arbitrary"),
    )(qq, kv, kr, do, lse, delta)


def _rope_groups(acc, o_ref, c, sa, sb, sh, groups):
    for gi in range(acc.shape[1] // LANES):
        blk = acc[:, gi * LANES:(gi + 1) * LANES]
        if gi in groups:
            blk = _rope(blk, c, sa, sb, sh)
        o_ref[:, gi * LANES:(gi + 1) * LANES] = blk.astype(o_ref.dtype)


def _mla_fwd(tag, h, w, tabs):
    s = h.shape[0]
    rc, rsa, rsb = tabs
    lat = _mm(f"{tag}_lat", h, w["w_in"], tm=512)
    qn, kvn, kr = _mla_lat_post(f"{tag}_latpost", lat, w["q_norm"], w["kv_norm"], rc, rsa, rsb)
    tm = min(512, s)

    def q_epi(acc, o_ref, c_ref, sa_ref, sb_ref):
        _rope_groups(acc, o_ref, c_ref[...], sa_ref[...], sb_ref[...], MLA_ROPE // 2, range(1, MLA_H, 2))

    tab = pl.BlockSpec((tm, LANES), lambda i, j: (i, 0))
    qq = _mm(f"{tag}_q", qn, w["w_q"], tm=512, tn=MLA_H * LANES, out_dtype=BF16, epilogue=q_epi,
             extras=(rc, rsa, rsb), extra_specs=(tab, tab, tab))
    kv = _mm(f"{tag}_kv", kvn, w["w_kv"], tm=512, out_dtype=BF16)
    o, lse = _mla_attn_fwd(f"{tag}_attn", qq, kv, kr)
    y = _mm(f"{tag}_o", o, w["w_o"], tm=512)
    return y, dict(h=h, lat=lat, qn=qn, kvn=kvn, kr=kr, qq=qq, kv=kv, o=o, lse=lse)


def _mla_bwd(tag, dy, res, w, tabs):
    rc, rsa, rsb = tabs
    do = _mm(f"{tag}_do", dy, w["w_o_t"], tm=512, out_dtype=BF16)
    g_wo = _mm_tn(f"{tag}_gwo", res["o"], dy)
    dqq, delta = _mla_attn_dq(f"{tag}_dq", res["qq"], res["kv"], res["kr"], do, res["o"], res["lse"], rc, rsa, rsb)
    dkv, dkr = _mla_attn_dkv(f"{tag}_dkv", res["qq"], res["kv"], res["kr"], do, res["lse"], delta)
    dqn = _mm(f"{tag}_dqn", dqq, w["w_q_t"], tm=512)
    g_wq = _mm_tn(f"{tag}_gwq", res["qn"], dqq, tn=1024)
    dkvn = _mm(f"{tag}_dkvn", dkv, w["w_kv_t"], tm=512)
    g_wkv = _mm_tn(f"{tag}_gwkv", res["kvn"], dkv, tn=1024)
    dlat, g_qn, g_kvn = _mla_lat_bwd(f"{tag}_latbwd", res["lat"], dqn, dkvn, dkr, w["q_norm"], w["kv_norm"],
                                     rc, rsa, rsb)
    dh = _mm(f"{tag}_dh", dlat, w["w_in_t"], tm=512)
    g_win = _mm_tn(f"{tag}_gwin", res["h"], dlat)
    return dh, dict(w_in=g_win, q_norm=g_qn, w_q=g_wq, kv_norm=g_kvn, w_kv=g_wkv, w_o=g_wo)


SWA_QW = SWA_HQ * SWA_HD
SWA_KW = SWA_HKV * LANES
SWA_NQKV = SWA_QW + 2 * SWA_KW
SWA_SCALE = SWA_HD ** -0.5
SWA_GROUP_ROWS = 4 * SWA_W


def _swa_tile(s):
    return min(512, max(SWA_W, s // 2))


def _swa_masks():
    lane = lax.broadcasted_iota(jnp.int32, (SWA_W, LANES), 1)
    return lane < SWA_HD


def _swa_q4(qa, qb, lo):
    z = jnp.zeros_like(qa)
    return jnp.concatenate([jnp.where(lo, qa, z), jnp.where(lo, z, qa), jnp.where(lo, qb, z), jnp.where(lo, z, qb)],
                           axis=0)


def _swa_probs(q4, kwin, sink_col, first_block):
    sc = lax.dot_general(q4, kwin, NT_DIMS, preferred_element_type=F32) * SWA_SCALE
    row = lax.broadcasted_iota(jnp.int32, sc.shape, 0) % SWA_W
    col = lax.broadcasted_iota(jnp.int32, sc.shape, 1)
    rel = row + SWA_W - col
    ok = (rel >= 0) & (rel < SWA_W) & ((col >= SWA_W) | jnp.logical_not(first_block))
    sc = jnp.where(ok, sc, NEG)
    m = jnp.maximum(jnp.max(sc, axis=1, keepdims=True), sink_col)
    e = jnp.exp(sc - m)
    es = jnp.exp(sink_col - m)
    inv = 1.0 / (jnp.sum(e, axis=1, keepdims=True) + es)
    return e * inv, es * inv


def _sink_col(sinks_ref, grp):
    seg = lax.broadcasted_iota(jnp.int32, (SWA_GROUP_ROWS, 1), 0) // SWA_W
    col = jnp.zeros((SWA_GROUP_ROWS, 1), F32)
    for j in range(4):
        col = jnp.where(seg == j, sinks_ref[0, 4 * grp + j], col)
    return col


def _swa_attn_fwd(name, qkv, sinks):
    s = qkv.shape[0]
    t = _swa_tile(s)
    nb = t // SWA_W

    def body(sinks_ref, q_ref, kv_ref, kvp_ref, o_ref):
        i = pl.program_id(0)
        lo = _swa_masks()
        for grp in range(SWA_HKV):
            sink_col = _sink_col(sinks_ref, grp)
            kcat = jnp.concatenate([kvp_ref[:, grp * LANES:(grp + 1) * LANES],
                                    kv_ref[:, grp * LANES:(grp + 1) * LANES]], axis=0)
            vcat = jnp.concatenate([kvp_ref[:, SWA_KW + grp * LANES:SWA_KW + (grp + 1) * LANES],
                                    kv_ref[:, SWA_KW + grp * LANES:SWA_KW + (grp + 1) * LANES]], axis=0)
            for b in range(nb):
                r0 = b * SWA_W
                qa = q_ref[r0:r0 + SWA_W, grp * 2 * LANES:grp * 2 * LANES + LANES]
                qb = q_ref[r0:r0 + SWA_W, grp * 2 * LANES + LANES:(grp + 1) * 2 * LANES]
                first = jnp.logical_and(i == 0, b == 0)
                p, _ = _swa_probs(_swa_q4(qa, qb, lo), kcat[r0:r0 + 2 * SWA_W], sink_col, first)
                o4 = jnp.dot(p.astype(BF16), vcat[r0:r0 + 2 * SWA_W], preferred_element_type=F32)
                oa = jnp.where(lo, o4[0:SWA_W], o4[SWA_W:2 * SWA_W])
                ob = jnp.where(lo, o4[2 * SWA_W:3 * SWA_W], o4[3 * SWA_W:4 * SWA_W])
                o_ref[r0:r0 + SWA_W, grp * 2 * LANES:grp * 2 * LANES + LANES] = oa.astype(BF16)
                o_ref[r0:r0 + SWA_W, grp * 2 * LANES + LANES:(grp + 1) * 2 * LANES] = ob.astype(BF16)

    return pl.pallas_call(
        body, name=name, grid=(s // t,),
        in_specs=[pl.BlockSpec(memory_space=pltpu.SMEM),
                  pl.BlockSpec((t, SWA_QW), lambda i: (i, 0)),
                  pl.BlockSpec((t, 2 * SWA_KW), lambda i: (i, 1)),
                  pl.BlockSpec((SWA_W, 2 * SWA_KW), lambda i: (jnp.maximum(i * nb - 1, 0), 1))],
        out_specs=pl.BlockSpec((t, SWA_QW), lambda i: (i, 0)),
        out_shape=jax.ShapeDtypeStruct((s, SWA_QW), BF16),
        compiler_params=_cparams("parallel"),
    )(sinks, qkv, qkv, qkv)


def _swa_attn_bwd(name, qkv, sinks, do):
    s = qkv.shape[0]
    t = _swa_tile(s)
    nb = t // SWA_W
    nt = s // t

    def body(sinks_ref, q_ref, kv_ref, kvp_ref, do_ref, dq_ref, dkv_ref, dkvp_ref, dsink_ref, dcat, sink_acc):
        i = pl.program_id(0)
        lo = _swa_masks()

        @pl.when(i == 0)
        def _():
            sink_acc[...] = jnp.zeros_like(sink_acc)

        dcat[...] = jnp.zeros_like(dcat)
        for grp in range(SWA_HKV):
            sink_col = _sink_col(sinks_ref, grp)
            kcat = jnp.concatenate([kvp_ref[:, grp * LANES:(grp + 1) * LANES],
                                    kv_ref[:, grp * LANES:(grp + 1) * LANES]], axis=0)
            vcat = jnp.concatenate([kvp_ref[:, SWA_KW + grp * LANES:SWA_KW + (grp + 1) * LANES],
                                    kv_ref[:, SWA_KW + grp * LANES:SWA_KW + (grp + 1) * LANES]], axis=0)
            for b in range(nb):
                r0 = b * SWA_W
                ca = slice(grp * 2 * LANES, grp * 2 * LANES + LANES)
                cb = slice(grp * 2 * LANES + LANES, (grp + 1) * 2 * LANES)
                q4 = _swa_q4(q_ref[r0:r0 + SWA_W, ca], q_ref[r0:r0 + SWA_W, cb], lo)
                do4 = _swa_q4(do_ref[r0:r0 + SWA_W, ca], do_ref[r0:r0 + SWA_W, cb], lo)
                first = jnp.logical_and(i == 0, b == 0)
                kwin = kcat[r0:r0 + 2 * SWA_W]
                vwin = vcat[r0:r0 + 2 * SWA_W]
                p, ps = _swa_probs(q4, kwin, sink_col, first)
                dp = lax.dot_general(do4, vwin, NT_DIMS, preferred_element_type=F32)
                rowdot = jnp.sum(p * dp, axis=1, keepdims=True)
                ds = (p * (dp - rowdot) * SWA_SCALE).astype(BF16)
                sink_acc[grp] += jnp.broadcast_to(-ps * rowdot, (SWA_GROUP_ROWS, LANES))
                dq4 = jnp.dot(ds, kwin, preferred_element_type=F32)
                dq_ref[r0:r0 + SWA_W, ca] = jnp.where(lo, dq4[0:SWA_W], dq4[SWA_W:2 * SWA_W])
                dq_ref[r0:r0 + SWA_W, cb] = jnp.where(lo, dq4[2 * SWA_W:3 * SWA_W], dq4[3 * SWA_W:4 * SWA_W])
                dk = lax.dot_general(ds, q4, TN_DIMS, preferred_element_type=F32)
                dv = lax.dot_general(p.astype(BF16), do4, TN_DIMS, preferred_element_type=F32)
                dcat[r0:r0 + 2 * SWA_W, grp * LANES:(grp + 1) * LANES] += dk
                dcat[r0:r0 + 2 * SWA_W, SWA_KW + grp * LANES:SWA_KW + (grp + 1) * LANES] += dv
        dkvp_ref[0] = dcat[0:SWA_W]
        dkv_ref[...] = dcat[SWA_W:SWA_W + t]

        @pl.when(i == nt - 1)
        def _():
            for grp in range(SWA_HKV):
                for j in range(4):
                    tot = jnp.sum(sink_acc[grp, j * SWA_W:(j + 1) * SWA_W, 0:1])
                    dsink_ref[4 * grp + j:4 * grp + j + 1, :] = jnp.full((1, LANES), tot, F32)

    return pl.pallas_call(
        body, name=name, grid=(nt,),
        in_specs=[pl.BlockSpec(memory_space=pltpu.SMEM),
                  pl.BlockSpec((t, SWA_QW), lambda i: (i, 0)),
                  pl.BlockSpec((t, 2 * SWA_KW), lambda i: (i, 1)),
                  pl.BlockSpec((SWA_W, 2 * SWA_KW), lambda i: (jnp.maximum(i * nb - 1, 0), 1)),
                  pl.BlockSpec((t, SWA_QW), lambda i: (i, 0))],
        out_specs=[pl.BlockSpec((t, SWA_QW), lambda i: (i, 0)), pl.BlockSpec((t, 2 * SWA_KW), lambda i: (i, 0)),
                   pl.BlockSpec((1, SWA_W, 2 * SWA_KW), lambda i: (i, 0, 0)),
                   pl.BlockSpec((SWA_HQ, LANES), lambda i: (0, 0))],
        out_shape=[jax.ShapeDtypeStruct((s, SWA_QW), F32), jax.ShapeDtypeStruct((s, 2 * SWA_KW), F32),
                   jax.ShapeDtypeStruct((nt, SWA_W, 2 * SWA_KW), F32), jax.ShapeDtypeStruct((SWA_HQ, LANES), F32)],
        scratch_shapes=[pltpu.VMEM((SWA_W + t, 2 * SWA_KW), F32), pltpu.VMEM((SWA_HKV, SWA_GROUP_ROWS, LANES), F32)],
        compiler_params=_cparams("arbitrary"),
    )(sinks, qkv, qkv, qkv, do)


def _swa_dqkv(name, dq, dkv, dkvp, rc, rsa, rsb):
    s = dq.shape[0]
    t = _swa_tile(s)
    nt = s // t
    sh = SWA_ROT // 2

    def body(dq_ref, dkv_ref, dkvn_ref, c_ref, sa_ref, sb_ref, out_ref, bsum_ref, acc):
        i = pl.program_id(0)
        c, sa, sb = c_ref[...], sa_ref[...], sb_ref[...]
        lo = lax.broadcasted_iota(jnp.int32, (t, LANES), 1) < SWA_HD
        rows = lax.broadcasted_iota(jnp.int32, (t, LANES), 0)
        tail = jnp.logical_and(rows >= t - SWA_W, i < nt - 1)

        @pl.when(i == 0)
        def _():
            acc[...] = jnp.zeros_like(acc)

        for gi in range(SWA_QW // LANES):
            blk = _rope_t(dq_ref[:, gi * LANES:(gi + 1) * LANES], c, sa, sb, sh)
            out_ref[:, gi * LANES:(gi + 1) * LANES] = blk.astype(BF16)
            acc[:, gi * LANES:(gi + 1) * LANES] += _rowsum8(blk)
        for gi in range(2 * SWA_KW // LANES):
            cols = slice(gi * LANES, (gi + 1) * LANES)
            nxt = jnp.concatenate([jnp.zeros((t - SWA_W, LANES), F32), dkvn_ref[0, :, cols]], axis=0)
            blk = dkv_ref[:, cols] + jnp.where(tail, nxt, 0.0)
            blk = jnp.where(lo, blk + pltpu.roll(blk, SWA_HD, 1), 0.0)
            if gi < SWA_HKV:
                blk = _rope_t(blk, c, sa, sb, sh)
            out_ref[:, SWA_QW + gi * LANES:SWA_QW + (gi + 1) * LANES] = blk.astype(BF16)
            acc[:, SWA_QW + gi * LANES:SWA_QW + (gi + 1) * LANES] += _rowsum8(blk)

        @pl.when(i == nt - 1)
        def _():
            bsum_ref[...] = jnp.sum(acc[...], axis=0, keepdims=True)

    return pl.pallas_call(
        body, name=name, grid=(nt,),
        in_specs=[pl.BlockSpec((t, SWA_QW), lambda i: (i, 0)), pl.BlockSpec((t, 2 * SWA_KW), lambda i: (i, 0)),
                  pl.BlockSpec((1, SWA_W, 2 * SWA_KW), lambda i: (jnp.minimum(i + 1, nt - 1), 0, 0))]
        + [_row_spec(t, LANES)] * 3,
        out_specs=[pl.BlockSpec((t, SWA_NQKV), lambda i: (i, 0)), pl.BlockSpec((1, SWA_NQKV), lambda i: (0, 0))],
        out_shape=[jax.ShapeDtypeStruct((s, SWA_NQKV), BF16), jax.ShapeDtypeStruct((1, SWA_NQKV), F32)],
        scratch_shapes=[pltpu.VMEM((8, SWA_NQKV), F32)],
        compiler_params=_cparams("arbitrary"),
    )(dq, dkv, dkvp, rc, rsa, rsb)


def _swa_fwd(tag, h, w, tabs):
    s = h.shape[0]
    rc, rsa, rsb = tabs
    tm = min(512, s)
    sh = SWA_ROT // 2

    def qkv_epi(acc, o_ref, b_ref, c_ref, sa_ref, sb_ref):
        acc = acc + b_ref[...]

        @pl.when(pl.program_id(1) == 0)
        def _():
            _rope_groups(acc, o_ref, c_ref[...], sa_ref[...], sb_ref[...], sh, range(SWA_QW // LANES))

        @pl.when(pl.program_id(1) == 1)
        def _():
            _rope_groups(acc, o_ref, c_ref[...], sa_ref[...], sb_ref[...], sh, range(SWA_HKV))

    tab = pl.BlockSpec((tm, LANES), lambda i, j: (i, 0))
    qkv = _mm(f"{tag}_qkv", h, w["w_qkv"], tm=512, tn=SWA_QW, out_dtype=BF16, epilogue=qkv_epi,
              extras=(w["b_qkv"], rc, rsa, rsb),
              extra_specs=(pl.BlockSpec((1, SWA_QW), lambda i, j: (0, j)), tab, tab, tab))
    o = _swa_attn_fwd(f"{tag}_attn", qkv, w["sinks"])

    def o_epi(acc, o_ref, b_ref):
        o_ref[...] = acc + b_ref[...]

    y = _mm(f"{tag}_o", o, w["w_o"], tm=512, epilogue=o_epi, extras=(w["b_o"],),
            extra_specs=(pl.BlockSpec((1, D), lambda i, j: (0, 0)),))
    return y, dict(h=h, qkv=qkv, o=o)


def _swa_bwd(tag, dy, res, w, tabs):
    rc, rsa, rsb = tabs
    do = _mm(f"{tag}_do", dy, w["w_o_t"], tm=512, out_dtype=BF16)
    g_wo = _mm_tn(f"{tag}_gwo", res["o"], dy)
    dq, dkv, dkvp, dsink = _swa_attn_bwd(f"{tag}_attnbwd", res["qkv"], w["sinks"], do)
    dqkv, g_b = _swa_dqkv(f"{tag}_dqkv", dq, dkv, dkvp, rc, rsa, rsb)
    dh = _mm(f"{tag}_dh", dqkv, w["w_qkv_t"], tm=512)
    g_wqkv = _mm_tn(f"{tag}_gwqkv", res["h"], dqkv, tn=1024)
    return dh, dict(w_qkv=g_wqkv, b_qkv=g_b, sinks=dsink, w_o=g_wo)


def _ada_fwd(name, c_all, w_sh, b_sh):
    cols = w_sh.shape[2]
    tn = cols // 3

    def body(c_ref, w_ref, b_ref, o_ref, cond_ref):
        cv = c_ref[...]
        cond = cv * jax.nn.sigmoid(cv)
        cond_ref[...] = cond
        o_ref[0] = jnp.dot(cond, w_ref[0], preferred_element_type=F32, precision=lax.Precision.HIGHEST) + b_ref[0]

    return pl.pallas_call(
        body, name=name, grid=(DEPTH, cols // tn),
        in_specs=[pl.BlockSpec((8, D), lambda l, j: (0, 0)), pl.BlockSpec((1, D, tn), lambda l, j: (l, 0, j)),
                  pl.BlockSpec((1, 1, tn), lambda l, j: (l, 0, j))],
        out_specs=[pl.BlockSpec((1, 8, tn), lambda l, j: (l, 0, j)), pl.BlockSpec((8, D), lambda l, j: (0, 0))],
        out_shape=[jax.ShapeDtypeStruct((DEPTH, 8, cols), F32), jax.ShapeDtypeStruct((8, D), F32)],
        compiler_params=_cparams("arbitrary", "arbitrary"),
    )(c_all, w_sh, b_sh)


def _ada_grad(name, cond_t, dmod_sh):
    cols = dmod_sh.shape[2]
    tn = cols // 3

    def body(ct_ref, dm_ref, o_ref):
        acc = ct_ref[:, 0:1] * dm_ref[0, 0:1, :]
        for b in range(1, 8):
            acc = acc + ct_ref[:, b:b + 1] * dm_ref[0, b:b + 1, :]
        o_ref[0] = acc

    return pl.pallas_call(
        body, name=name, grid=(DEPTH, cols // tn),
        in_specs=[pl.BlockSpec((D, 8), lambda l, j: (0, 0)), pl.BlockSpec((1, 8, tn), lambda l, j: (l, 0, j))],
        out_specs=pl.BlockSpec((1, D, tn), lambda l, j: (l, 0, j)),
        out_shape=jax.ShapeDtypeStruct((DEPTH, D, cols), F32),
        compiler_params=_cparams("parallel", "parallel"),
    )(cond_t, dmod_sh)


def _adamw(name, g, w, m, v):
    r = g.shape[0]
    tr = min(ROW_TILE, r)

    def body(g_ref, w_ref, m_ref, v_ref, d_ref, nm_ref, nv_ref):
        gv = g_ref[...]
        mn = ADAM_B1 * m_ref[...] + (1.0 - ADAM_B1) * gv
        vn = ADAM_B2 * v_ref[...] + (1.0 - ADAM_B2) * (gv * gv)
        m_hat = mn / (1.0 - ADAM_B1 ** ADAM_STEP)
        v_hat = vn / (1.0 - ADAM_B2 ** ADAM_STEP)
        d_ref[...] = -ADAM_LR * (m_hat / (jnp.sqrt(v_hat) + ADAM_EPS) + ADAM_WD * w_ref[...])
        nm_ref[...] = mn
        nv_ref[...] = vn

    spec = _row_spec(tr, PACK_COLS)
    return pl.pallas_call(
        body, name=name, grid=(r // tr,),
        in_specs=[spec] * 4, out_specs=[spec] * 3,
        out_shape=[jax.ShapeDtypeStruct(g.shape, F32)] * 3,
        compiler_params=_cparams("parallel"),
    )(g, w, m, v)


def _to_chips(full, axis):
    shp = full.shape
    a = full.reshape(shp[:axis] + (N_CHIPS, shp[axis] // N_CHIPS) + shp[axis + 1:])
    return jnp.moveaxis(a, axis, 0).reshape(N_CHIPS, -1)


def _from_chips(stacked, shard_shape, axis):
    a = jnp.moveaxis(stacked.reshape((N_CHIPS,) + tuple(shard_shape)), 0, axis)
    shp = a.shape
    return a.reshape(shp[:axis] + (shp[axis] * shp[axis + 1],) + shp[axis + 2:])


def _pack_rows(total):
    rows = -(-total // PACK_COLS)
    return -(-rows // PACK_ROW_ALIGN) * PACK_ROW_ALIGN


def _pack(parts, rows):
    cat = jnp.concatenate(parts, axis=-1)
    pad = rows * PACK_COLS - cat.shape[-1]
    cat = jnp.pad(cat, [(0, 0)] * (cat.ndim - 1) + [(0, pad)])
    return cat.reshape(cat.shape[:-1] + (rows, PACK_COLS))


def _unpack(flat, shapes):
    out, off = [], 0
    for shp in shapes:
        n = 1
        for d in shp:
            n *= d
        out.append(flat[..., off:off + n].reshape(flat.shape[:-1] + tuple(shp)))
        off += n
    return out


def _rope_tables(positions, rot, lanes_per_head):
    half = rot // 2
    inv = ROPE_THETA ** (-jnp.arange(0, rot, 2, dtype=F32) / rot)
    ang = positions.astype(F32)[:, None] * inv
    cos, sin = jnp.cos(ang), jnp.sin(ang)
    s = positions.shape[0]
    rest = lanes_per_head - rot
    fill = 1.0 if lanes_per_head == SWA_HD else 0.0
    c = jnp.concatenate([cos, cos, jnp.full((s, rest), fill, F32)], axis=1)
    sa = jnp.concatenate([-sin, jnp.zeros((s, half + rest), F32)], axis=1)
    sb = jnp.concatenate([jnp.zeros((s, half), F32), sin, jnp.zeros((s, rest), F32)], axis=1)
    reps = LANES // lanes_per_head
    return tuple(jnp.tile(t, (1, reps)) for t in (c, sa, sb))


def _mla_weights(w_in, q_norm, w_q_b, kv_norm, w_kv_b, w_o):
    w_in_p = jnp.pad(w_in, ((0, 0), (0, MLA_LAT - w_in.shape[1])))
    wq = w_q_b.reshape(MLA_QR, MLA_H, MLA_NOPE + MLA_ROPE)
    wq_p = jnp.pad(wq, ((0, 0), (0, 0), (0, 2 * LANES - MLA_NOPE - MLA_ROPE))).reshape(MLA_QR, MLA_H * 2 * LANES)
    return dict(w_in=w_in_p, w_in_t=w_in_p.T, q_norm=q_norm.reshape(1, -1), kv_norm=kv_norm.reshape(1, -1),
                w_q=wq_p, w_q_t=wq_p.T, w_kv=w_kv_b, w_kv_t=w_kv_b.T, w_o=w_o, w_o_t=w_o.T)


def _mla_grads_unpermute(g):
    gq = g["w_q"].reshape(MLA_QR, MLA_H, 2 * LANES)[:, :, :MLA_NOPE + MLA_ROPE]
    return dict(mla_w_in=g["w_in"][:, :MLA_QR + MLA_KVR + MLA_ROPE], mla_q_norm=g["q_norm"][0],
                mla_w_q_b=gq.reshape(MLA_QR, -1), mla_kv_norm=g["kv_norm"][0], mla_w_kv_b=g["w_kv"],
                mla_w_o=g["w_o"])


def _swa_dup(a):
    lead = a.shape[:-1]
    a = a.reshape(lead + (SWA_HKV, SWA_HD))
    return jnp.concatenate([a, a], axis=-1).reshape(lead + (SWA_KW,))


def _swa_undup(a):
    lead = a.shape[:-1]
    return a.reshape(lead + (SWA_HKV, LANES))[..., :SWA_HD].reshape(lead + (SWA_HKV * SWA_HD,))


def _swa_weights(w_qkv, b_qkv, sinks, w_o, b_o):
    nk = SWA_HKV * SWA_HD
    perm = lambda a: jnp.concatenate([a[..., :SWA_QW], _swa_dup(a[..., SWA_QW:SWA_QW + nk]),
                                      _swa_dup(a[..., SWA_QW + nk:])], axis=-1)
    w_p = perm(w_qkv)
    return dict(w_qkv=w_p, w_qkv_t=w_p.T, b_qkv=perm(b_qkv.astype(F32)).reshape(1, -1),
                sinks=sinks.reshape(1, -1), w_o=w_o, w_o_t=w_o.T, b_o=b_o.astype(F32).reshape(1, -1))


def _swa_grads_unpermute(g):
    unperm = lambda a: jnp.concatenate([a[..., :SWA_QW], _swa_undup(a[..., SWA_QW:SWA_QW + SWA_KW]),
                                        _swa_undup(a[..., SWA_QW + SWA_KW:])], axis=-1)
    return dict(swa_w_qkv=unperm(g["w_qkv"]), swa_b_qkv=unperm(g["b_qkv"])[0], swa_sinks=g["sinks"][:, 0],
                swa_w_o=g["w_o"], swa_b_o=g["b_o"])


SMALL_LAYOUT = (("ada_b", 24), ("ln_mix_g", 4), ("ln_mix_b", 4), ("ln_ffn_g", 4), ("ln_ffn_b", 4),
                ("mla_q_norm", 2), ("mla_kv_norm", 2), ("swa_sinks", 1), ("loss", 1))


def _small_pack(vals):
    rows = []
    for name, nrows in SMALL_LAYOUT:
        a = vals[name].reshape(nrows, -1).astype(F32)
        rows.append(jnp.pad(a, ((0, 0), (0, PACK_COLS - a.shape[1]))))
    cat = jnp.concatenate(rows, axis=0)
    return jnp.pad(cat, ((0, SMALL_ROWS - cat.shape[0]), (0, 0)))


def _small_unpack(packed, shapes):
    out, r = {}, 0
    for name, nrows in SMALL_LAYOUT:
        shp = shapes[name]
        n = 1
        for d in shp:
            n *= d
        out[name] = packed[r:r + nrows, :n // nrows].reshape(shp)
        r += nrows
    return out


def kernel(x, c, positions, ada_w, ada_b, ln_mix_g, ln_mix_b, ln_ffn_g, ln_ffn_b, ffn_w_gate, ffn_w_up, ffn_w_down, mla_w_in, mla_q_norm, mla_w_q_b, mla_kv_norm, mla_w_kv_b, mla_w_o, swa_w_qkv, swa_b_qkv, swa_sinks, swa_w_o, swa_b_o, loss_target, m_ada_w, m_ada_b, m_ln_mix_g, m_ln_mix_b, m_ln_ffn_g, m_ln_ffn_b, m_ffn_w_gate, m_ffn_w_up, m_ffn_w_down, m_mla_w_in, m_mla_q_norm, m_mla_w_q_b, m_mla_kv_norm, m_mla_w_kv_b, m_mla_w_o, m_swa_w_qkv, m_swa_b_qkv, m_swa_sinks, m_swa_w_o, m_swa_b_o, v_ada_w, v_ada_b, v_ln_mix_g, v_ln_mix_b, v_ln_ffn_g, v_ln_ffn_b, v_ffn_w_gate, v_ffn_w_up, v_ffn_w_down, v_mla_w_in, v_mla_q_norm, v_mla_w_q_b, v_mla_kv_norm, v_mla_w_kv_b, v_mla_w_o, v_swa_w_qkv, v_swa_b_qkv, v_swa_sinks, v_swa_w_o, v_swa_b_o):
    weights = dict(ada_w=ada_w, ada_b=ada_b, ln_mix_g=ln_mix_g, ln_mix_b=ln_mix_b, ln_ffn_g=ln_ffn_g,
                   ln_ffn_b=ln_ffn_b, ffn_w_gate=ffn_w_gate, ffn_w_up=ffn_w_up, ffn_w_down=ffn_w_down,
                   mla_w_in=mla_w_in, mla_q_norm=mla_q_norm, mla_w_q_b=mla_w_q_b, mla_kv_norm=mla_kv_norm,
                   mla_w_kv_b=mla_w_kv_b, mla_w_o=mla_w_o, swa_w_qkv=swa_w_qkv, swa_b_qkv=swa_b_qkv,
                   swa_sinks=swa_sinks, swa_w_o=swa_w_o, swa_b_o=swa_b_o)
    mom_m = dict(ada_w=m_ada_w, ada_b=m_ada_b, ln_mix_g=m_ln_mix_g, ln_mix_b=m_ln_mix_b, ln_ffn_g=m_ln_ffn_g,
                 ln_ffn_b=m_ln_ffn_b, ffn_w_gate=m_ffn_w_gate, ffn_w_up=m_ffn_w_up, ffn_w_down=m_ffn_w_down,
                 mla_w_in=m_mla_w_in, mla_q_norm=m_mla_q_norm, mla_w_q_b=m_mla_w_q_b, mla_kv_norm=m_mla_kv_norm,
                 mla_w_kv_b=m_mla_w_kv_b, mla_w_o=m_mla_w_o, swa_w_qkv=m_swa_w_qkv, swa_b_qkv=m_swa_b_qkv,
                 swa_sinks=m_swa_sinks, swa_w_o=m_swa_w_o, swa_b_o=m_swa_b_o)
    mom_v = dict(ada_w=v_ada_w, ada_b=v_ada_b, ln_mix_g=v_ln_mix_g, ln_mix_b=v_ln_mix_b, ln_ffn_g=v_ln_ffn_g,
                 ln_ffn_b=v_ln_ffn_b, ffn_w_gate=v_ffn_w_gate, ffn_w_up=v_ffn_w_up, ffn_w_down=v_ffn_w_down,
                 mla_w_in=v_mla_w_in, mla_q_norm=v_mla_q_norm, mla_w_q_b=v_mla_w_q_b, mla_kv_norm=v_mla_kv_norm,
                 mla_w_kv_b=v_mla_w_kv_b, mla_w_o=v_mla_w_o, swa_w_qkv=v_swa_w_qkv, swa_b_qkv=v_swa_b_qkv,
                 swa_sinks=v_swa_sinks, swa_w_o=v_swa_w_o, swa_b_o=v_swa_b_o)
    names = list(weights)
    my_x, my_y, my_c = lax.axis_index("x"), lax.axis_index("y"), lax.axis_index("c")
    chip = 2 * my_x + my_y
    batch_row = 2 * chip + my_c
    xs = x[0]
    target = loss_target[0]
    pos = positions[0]
    s = xs.shape[0]

    shard_shapes = [weights[n].shape for n, _ in SHARDED]
    total = sum(int(weights[n].size) for n, _ in SHARDED)
    rows = _pack_rows(total)
    half = rows // 2
    wpack = _pack([weights[n].reshape(-1).astype(BF16) for n, _ in SHARDED], rows)
    my_half = lax.dynamic_slice_in_dim(wpack, my_c * half, half, axis=0)
    by_chip = _exchange("ag_w_chips", my_half, ("x", "y"), "gather")
    by_core = _exchange("ag_w_cores", by_chip.reshape(N_CHIPS * half, PACK_COLS), ("c",), "gather")
    gathered = jnp.moveaxis(by_core.reshape(2, N_CHIPS, half, PACK_COLS), 0, 1).reshape(N_CHIPS, rows * PACK_COLS)
    full = {}
    for (n, axis), part in zip(SHARDED, _unpack(gathered, shard_shapes)):
        full[n] = _from_chips(part, weights[n].shape, axis)

    c_rows = jnp.pad(c, ((0, 7), (0, 0)))
    c_all = _exchange("ag_c", c_rows, ("x", "y", "c"), "gather")[:, 0, :]
    ada_cols = ada_w.shape[2]
    ada_b_sh = lax.dynamic_slice_in_dim(ada_b, chip * ada_cols, ada_cols, axis=1).reshape(DEPTH, 1, ada_cols)
    mod_sh, cond_all = _ada_fwd("ada_fwd", c_all, ada_w, ada_b_sh)
    mod_all = _exchange("ag_mod", mod_sh.reshape(DEPTH * 8, ada_cols), ("x", "y"), "gather")
    mod_all = mod_all.reshape(N_CHIPS, DEPTH, 8, ada_cols)
    mod_mine = lax.dynamic_index_in_dim(mod_all, batch_row, axis=2, keepdims=False)
    mod = jnp.moveaxis(mod_mine, 0, 1).reshape(DEPTH, 6, 1, D)

    tabs_a = _rope_tables(pos, MLA_ROPE, LANES)
    tabs_b = _rope_tables(pos, SWA_ROT, SWA_HD)
    vec = lambda a, l: a[l].reshape(1, D)

    mix_w, ffn_w = [], []
    for l in range(DEPTH):
        j = l // 2
        if l % 2 == 0:
            mix_w.append(_mla_weights(full["mla_w_in"][j], mla_q_norm[j], full["mla_w_q_b"][j], mla_kv_norm[j],
                                      full["mla_w_kv_b"][j], full["mla_w_o"][j]))
        else:
            mix_w.append(_swa_weights(full["swa_w_qkv"][j], full["swa_b_qkv"][j], swa_sinks[j], full["swa_w_o"][j],
                                      full["swa_b_o"][j]))
        ffn_w.append(dict(wg=full["ffn_w_gate"][l], wu=full["ffn_w_up"][l], wd=full["ffn_w_down"][l],
                          wg_t=full["ffn_w_gate"][l].T, wu_t=full["ffn_w_up"][l].T, wd_t=full["ffn_w_down"][l].T))

    saved = []
    x_cur = xs
    h = _modulate("mod0", x_cur, mod[0, 1], mod[0, 0])
    for l in range(DEPTH):
        if l % 2 == 0:
            y_mix, res = _mla_fwd(f"mla{l}", h, mix_w[l], tabs_a)
        else:
            y_mix, res = _swa_fwd(f"swa{l}", h, mix_w[l], tabs_b)
        x_mid, h2 = _post_mod(f"post_mix{l}", x_cur, y_mix, mod[l, 2], vec(ln_mix_g, l), vec(ln_mix_b, l),
                              mod[l, 4], mod[l, 3])
        gate, up, y_ffn = _ffn_fwd(f"ffn{l}", h2, ffn_w[l]["wg"], ffn_w[l]["wu"], ffn_w[l]["wd"])
        saved.append(dict(x_in=x_cur, y_mix=y_mix, res=res, x_mid=x_mid, h2=h2, gate=gate, up=up, y_ffn=y_ffn))
        if l < DEPTH - 1:
            x_cur, h = _post_mod(f"post_ffn{l}", x_mid, y_ffn, mod[l, 5], vec(ln_ffn_g, l), vec(ln_ffn_b, l),
                                 mod[l + 1, 1], mod[l + 1, 0])
        else:
            dxn, loss_part = _post_loss("post_loss", x_mid, y_ffn, mod[l, 5], vec(ln_ffn_g, l), vec(ln_ffn_b, l),
                                        target)

    gfull = {n: [None] * weights[n].shape[0] for n, _ in SHARDED}
    gsmall = {n: [None] * weights[n].shape[0] for n in ("ln_mix_g", "ln_mix_b", "ln_ffn_g", "ln_ffn_b",
                                                         "mla_q_norm", "mla_kv_norm", "swa_sinks")}
    dmod = [None] * DEPTH
    for l in reversed(range(DEPTH)):
        sv = saved[l]
        j = l // 2
        dxp, dy, sums_f = _post_bwd(f"post_ffn_bwd{l}", dxn, sv["x_mid"], sv["y_ffn"], mod[l, 5], vec(ln_ffn_g, l))
        dgt, dup, act, dh2 = _ffn_bwd(f"ffn_bwd{l}", dy, sv["gate"], sv["up"], ffn_w[l]["wd_t"], ffn_w[l]["wg_t"],
                                      ffn_w[l]["wu_t"])
        gfull["ffn_w_gate"][l] = _mm_tn(f"ffn_gwg{l}", sv["h2"], dgt, tn=FF // 2)
        gfull["ffn_w_up"][l] = _mm_tn(f"ffn_gwu{l}", sv["h2"], dup, tn=FF // 2)
        gfull["ffn_w_down"][l] = _mm_tn(f"ffn_gwd{l}", act, dy)
        dx_mid, sums_fm = _mod_bwd(f"mod_ffn_bwd{l}", dh2, sv["x_mid"], dxp, mod[l, 4])
        dxp, dy, sums_m = _post_bwd(f"post_mix_bwd{l}", dx_mid, sv["x_in"], sv["y_mix"], mod[l, 2], vec(ln_mix_g, l))
        if l % 2 == 0:
            dh, g = _mla_bwd(f"mla{l}", dy, sv["res"], mix_w[l], tabs_a)
            g = _mla_grads_unpermute(g)
        else:
            dh, g = _swa_bwd(f"swa{l}", dy, sv["res"], mix_w[l], tabs_b)
            g["b_o"] = sums_m[3]
            g = _swa_grads_unpermute(g)
        for n, val in g.items():
            (gfull if n in gfull else gsmall)[n][j] = val
        dxn, sums_mm = _mod_bwd(f"mod_mix_bwd{l}", dh, sv["x_in"], dxp, mod[l, 1])
        gsmall["ln_ffn_g"][l], gsmall["ln_ffn_b"][l] = sums_f[0], sums_f[1]
        gsmall["ln_mix_g"][l], gsmall["ln_mix_b"][l] = sums_m[0], sums_m[1]
        dmod[l] = jnp.stack([sums_mm[1], sums_mm[0], sums_m[2], sums_fm[1], sums_fm[0], sums_f[2]])
    grad_x = dxn[None]

    small_vals = {n: jnp.stack(v) for n, v in gsmall.items()}
    small_vals["ada_b"] = jnp.stack(dmod)
    small_vals["loss"] = loss_part[0, 0:1]
    small_all = _exchange("ag_small", _small_pack(small_vals), ("x", "y", "c"), "gather")
    small_sum = _sum_groups("sum_small", small_all)
    dmod_all = small_all[:, :DEPTH * 6, :].reshape(8, DEPTH, 6 * D)
    dmod_sh = jnp.moveaxis(lax.dynamic_slice_in_dim(dmod_all, chip * ada_cols, ada_cols, axis=2), 0, 1)
    g_ada_w = _ada_grad("ada_grad", cond_all.T, dmod_sh)

    gsend = _pack([_to_chips(jnp.stack(gfull[n]), axis) for n, axis in SHARDED], rows)
    by_half = jnp.moveaxis(gsend.reshape(N_CHIPS, 2, half, PACK_COLS), 1, 0).reshape(2, N_CHIPS * half, PACK_COLS)
    core_parts = _exchange("rs_cores", by_half, ("c",), "a2a")
    core_sum = _sum_groups("rs_sum_cores", core_parts).reshape(N_CHIPS, half, PACK_COLS)
    chip_parts = _exchange("rs_chips", core_sum, ("x", "y"), "a2a")
    chip_sum = _sum_groups("rs_sum_chips", chip_parts)
    g_pack = _exchange("rs_share", chip_sum, ("c",), "gather").reshape(rows, PACK_COLS)

    ada_rows = int(ada_w.size) // PACK_COLS
    small_shapes = {n: weights[n].shape for n, _ in SMALL_LAYOUT if n != "loss"}
    small_shapes["loss"] = (1,)

    def flat_all(src):
        sharded = _pack([src[n].reshape(-1) for n, _ in SHARDED], rows)
        small = _small_pack({**{n: src[n] for n in small_shapes if n != "loss"}, "loss": jnp.zeros((1,), F32)})
        return jnp.concatenate([sharded, src["ada_w"].reshape(ada_rows, PACK_COLS), small], axis=0)

    g_flat = jnp.concatenate([g_pack, g_ada_w.reshape(ada_rows, PACK_COLS), small_sum], axis=0)
    delta, new_m, new_v = _adamw("adamw", g_flat, flat_all(weights), flat_all(mom_m), flat_all(mom_v))

    def split_all(flat):
        out = dict(zip([n for n, _ in SHARDED], _unpack(flat[:rows].reshape(-1), shard_shapes)))
        out["ada_w"] = flat[rows:rows + ada_rows].reshape(ada_w.shape)
        out.update(_small_unpack(flat[rows + ada_rows:], small_shapes))
        return out

    outs = [split_all(a) for a in (g_flat, delta, new_m, new_v)]
    loss = outs[0]["loss"][0]
    return (loss, grad_x, *[o[n] for o in outs for n in names])
```

```python
import jax
import jax.numpy as jnp
from jax import lax
from jax.experimental import pallas as pl
from jax.experimental.pallas import tpu as pltpu

F32 = jnp.float32
BF16 = jnp.bfloat16

D = 1024
DEPTH = 4
ROPE_THETA = 500000.0
LN_EPS = 1e-5
RMS_EPS = 1e-6
MLA_H = 8
MLA_NOPE = 128
MLA_ROPE = 64
MLA_V = 128
MLA_QR = 384
MLA_KVR = 256
MLA_LAT = 768
SWA_HQ = 16
SWA_HKV = 4
SWA_HD = 64
SWA_W = 128
SWA_ROT = 16
FF = 2816
ALPHA = (2 * DEPTH) ** 0.25
ADAM_LR = 0.001
ADAM_B1 = 0.9
ADAM_B2 = 0.999
ADAM_EPS = 1e-08
ADAM_WD = 0.01
ADAM_STEP = 10
NEG = -1e30
LANES = 128
N_CHIPS = 4
PACK_COLS = 1024
PACK_ROW_ALIGN = 1024
SMALL_ROWS = 512
ROW_TILE = 512

SHARDED = (
    ("ffn_w_gate", 2), ("ffn_w_up", 2), ("ffn_w_down", 1), ("mla_w_in", 1), ("mla_w_q_b", 2),
    ("mla_w_kv_b", 2), ("mla_w_o", 1), ("swa_w_qkv", 2), ("swa_b_qkv", 1), ("swa_w_o", 1), ("swa_b_o", 1),
)


def _cparams(*sem):
    return pltpu.CompilerParams(dimension_semantics=sem)


def _row_spec(tr, cols):
    return pl.BlockSpec((tr, cols), lambda i: (i, 0))


def _vec_spec(cols):
    return pl.BlockSpec((1, cols), lambda i: (0, 0))


def _rope(x, c, sa, sb, sh):
    n = x.shape[1]
    return x * c + pltpu.roll(x, n - sh, 1) * sa + pltpu.roll(x, sh, 1) * sb


def _rope_t(d, c, sa, sb, sh):
    n = d.shape[1]
    return d * c + pltpu.roll(d * sa, sh, 1) + pltpu.roll(d * sb, n - sh, 1)


def _rowsum8(t):
    r, n = t.shape
    return jnp.sum(t.reshape(r // 8, 8, n), axis=0)


def _exchange(name, src, axes, mode, chunks=1):
    g = 2 ** len(axes)
    blk = src.shape if mode == "gather" else src.shape[1:]
    rows = blk[0] // chunks

    def body(src_ref, out_ref, send_sems, recv_sems, loc_sem):
        pos = {a: lax.axis_index(a) for a in ("x", "y", "c")}

        def gidx(p):
            idx = 0
            for a in axes:
                idx = idx * 2 + p[a]
            return idx

        def view(i):
            return src_ref if mode == "gather" else src_ref.at[i]

        me = gidx(pos)
        loc = pltpu.make_async_copy(view(me), out_ref.at[me], loc_sem)
        loc.start()
        copies = []
        for k in range(chunks):
            piece = pl.ds(k * rows, rows)
            for j in range(1, g):
                peer = dict(pos)
                for bit, a in enumerate(reversed(axes)):
                    if (j >> bit) & 1:
                        peer[a] = 1 - pos[a]
                sem = (j - 1) * chunks + k
                cp = pltpu.make_async_remote_copy(
                    src_ref=view(gidx(peer)).at[piece], dst_ref=out_ref.at[me, piece],
                    send_sem=send_sems.at[sem], recv_sem=recv_sems.at[sem],
                    device_id=(peer["x"], peer["y"], peer["c"]), device_id_type=pl.DeviceIdType.MESH)
                cp.start()
                copies.append(cp)
        for cp in copies:
            cp.wait()
        loc.wait()

    nsem = (g - 1) * chunks
    return pl.pallas_call(
        body, name=name,
        out_shape=jax.ShapeDtypeStruct((g,) + tuple(blk), src.dtype),
        in_specs=[pl.BlockSpec(memory_space=pl.ANY)],
        out_specs=pl.BlockSpec(memory_space=pl.ANY),
        scratch_shapes=[pltpu.SemaphoreType.DMA((nsem,)), pltpu.SemaphoreType.DMA((nsem,)),
                        pltpu.SemaphoreType.DMA(())],
    )(src)


def _sum_groups(name, a):
    g, r, c = a.shape
    tr = min(ROW_TILE, r)

    def body(a_ref, o_ref):
        acc = a_ref[0]
        for i in range(1, g):
            acc = acc + a_ref[i]
        o_ref[...] = acc

    return pl.pallas_call(
        body, name=name, grid=(r // tr,),
        in_specs=[pl.BlockSpec((g, tr, c), lambda i: (0, i, 0))],
        out_specs=pl.BlockSpec((tr, c), lambda i: (i, 0)),
        out_shape=jax.ShapeDtypeStruct((r, c), F32),
        compiler_params=_cparams("parallel"),
    )(a)


def _mm(name, a, b, *, tm, tn=None, out_dtype=F32, epilogue=None, extras=(), extra_specs=()):
    m, k = a.shape
    n = b.shape[1]
    tn = tn or n
    tm = min(tm, m)

    def body(a_ref, b_ref, *rest):
        o_ref = rest[-1]
        acc = jnp.dot(a_ref[...], b_ref[...], preferred_element_type=F32)
        if epilogue is None:
            o_ref[...] = acc.astype(o_ref.dtype)
        else:
            epilogue(acc, o_ref, *rest[:-1])

    return pl.pallas_call(
        body, name=name, grid=(m // tm, n // tn),
        in_specs=[pl.BlockSpec((tm, k), lambda i, j: (i, 0)), pl.BlockSpec((k, tn), lambda i, j: (0, j)),
                  *extra_specs],
        out_specs=pl.BlockSpec((tm, tn), lambda i, j: (i, j)),
        out_shape=jax.ShapeDtypeStruct((m, n), out_dtype),
        compiler_params=_cparams("parallel", "parallel"),
    )(a, b, *extras)


def _mm_tn(name, a, b, *, tn=None, tk=1024):
    s, m = a.shape
    n = b.shape[1]
    tn = tn or n
    tk = min(tk, s)

    def body(a_ref, b_ref, o_ref):
        part = lax.dot_general(a_ref[...], b_ref[...], (((0,), (0,)), ((), ())), preferred_element_type=F32)

        @pl.when(pl.program_id(1) == 0)
        def _():
            o_ref[...] = part

        @pl.when(pl.program_id(1) > 0)
        def _():
            o_ref[...] += part

    return pl.pallas_call(
        body, name=name, grid=(n // tn, s // tk),
        in_specs=[pl.BlockSpec((tk, m), lambda j, k: (k, 0)), pl.BlockSpec((tk, tn), lambda j, k: (k, j))],
        out_specs=pl.BlockSpec((m, tn), lambda j, k: (0, j)),
        out_shape=jax.ShapeDtypeStruct((m, n), F32),
        compiler_params=_cparams("parallel", "arbitrary"),
    )(a, b)


def _modulate(name, x, sc, sh):
    s = x.shape[0]
    tr = min(ROW_TILE, s)

    def body(x_ref, sc_ref, sh_ref, h_ref):
        h_ref[...] = (x_ref[...] * (1.0 + sc_ref[...]) + sh_ref[...]).astype(BF16)

    return pl.pallas_call(
        body, name=name, grid=(s // tr,),
        in_specs=[_row_spec(tr, D), _vec_spec(D), _vec_spec(D)],
        out_specs=_row_spec(tr, D),
        out_shape=jax.ShapeDtypeStruct((s, D), BF16),
        compiler_params=_cparams("parallel"),
    )(x, sc, sh)


def _ln_stats(z):
    mu = jnp.mean(z, axis=1, keepdims=True)
    zc = z - mu
    var = jnp.mean(zc * zc, axis=1, keepdims=True)
    r = lax.rsqrt(var + LN_EPS)
    return zc * r, r


def _post_mod(name, x, y, g, gamma, beta, sc, sh):
    s = x.shape[0]
    tr = min(ROW_TILE, s)

    def body(x_ref, y_ref, g_ref, ga_ref, be_ref, sc_ref, sh_ref, xn_ref, h_ref):
        zh, _ = _ln_stats(ALPHA * x_ref[...] + g_ref[...] * y_ref[...])
        xn = zh * ga_ref[...] + be_ref[...]
        xn_ref[...] = xn
        h_ref[...] = (xn * (1.0 + sc_ref[...]) + sh_ref[...]).astype(BF16)

    return pl.pallas_call(
        body, name=name, grid=(s // tr,),
        in_specs=[_row_spec(tr, D), _row_spec(tr, D)] + [_vec_spec(D)] * 5,
        out_specs=[_row_spec(tr, D), _row_spec(tr, D)],
        out_shape=[jax.ShapeDtypeStruct((s, D), F32), jax.ShapeDtypeStruct((s, D), BF16)],
        compiler_params=_cparams("parallel"),
    )(x, y, g, gamma, beta, sc, sh)


def _post_loss(name, x, y, g, gamma, beta, target):
    s = x.shape[0]
    tr = min(ROW_TILE, s)
    nt = s // tr

    def body(x_ref, y_ref, g_ref, ga_ref, be_ref, t_ref, dx_ref, loss_ref, acc):
        i = pl.program_id(0)
        zh, _ = _ln_stats(ALPHA * x_ref[...] + g_ref[...] * y_ref[...])
        e = zh * ga_ref[...] + be_ref[...] - t_ref[...]
        dx_ref[...] = e * (1.0 / D)

        @pl.when(i == 0)
        def _():
            acc[...] = jnp.zeros_like(acc)

        acc[...] += _rowsum8(e * e)

        @pl.when(i == nt - 1)
        def _():
            loss_ref[...] = jnp.full(loss_ref.shape, jnp.sum(acc[...]) * (0.5 / D), F32)

    return pl.pallas_call(
        body, name=name, grid=(nt,),
        in_specs=[_row_spec(tr, D), _row_spec(tr, D)] + [_vec_spec(D)] * 3 + [_row_spec(tr, D)],
        out_specs=[_row_spec(tr, D), pl.BlockSpec((8, LANES), lambda i: (0, 0))],
        out_shape=[jax.ShapeDtypeStruct((s, D), F32), jax.ShapeDtypeStruct((8, LANES), F32)],
        scratch_shapes=[pltpu.VMEM((8, D), F32)],
        compiler_params=_cparams("arbitrary"),
    )(x, y, g, gamma, beta, target)


def _post_bwd(name, dxn, x, y, g, gamma):
    s = x.shape[0]
    tr = min(ROW_TILE, s)
    nt = s // tr

    def body(d_ref, x_ref, y_ref, g_ref, ga_ref, dxp_ref, dy_ref, sums_ref, a0, a1, a2, a3):
        i = pl.program_id(0)
        yv = y_ref[...]
        gv = g_ref[...]
        zh, r = _ln_stats(ALPHA * x_ref[...] + gv * yv)
        dxn_v = d_ref[...]
        dzh = dxn_v * ga_ref[...]
        dz = r * (dzh - jnp.mean(dzh, axis=1, keepdims=True) - zh * jnp.mean(dzh * zh, axis=1, keepdims=True))
        dxp_ref[...] = ALPHA * dz
        dyv = gv * dz
        dy_ref[...] = dyv.astype(BF16)

        @pl.when(i == 0)
        def _():
            for a in (a0, a1, a2, a3):
                a[...] = jnp.zeros_like(a)

        a0[...] += _rowsum8(dxn_v * zh)
        a1[...] += _rowsum8(dxn_v)
        a2[...] += _rowsum8(dz * yv)
        a3[...] += _rowsum8(dyv)

        @pl.when(i == nt - 1)
        def _():
            for k, a in enumerate((a0, a1, a2, a3)):
                sums_ref[k:k + 1, :] = jnp.sum(a[...], axis=0, keepdims=True)

    return pl.pallas_call(
        body, name=name, grid=(nt,),
        in_specs=[_row_spec(tr, D)] * 3 + [_vec_spec(D)] * 2,
        out_specs=[_row_spec(tr, D), _row_spec(tr, D), pl.BlockSpec((4, D), lambda i: (0, 0))],
        out_shape=[jax.ShapeDtypeStruct((s, D), F32), jax.ShapeDtypeStruct((s, D), BF16),
                   jax.ShapeDtypeStruct((4, D), F32)],
        scratch_shapes=[pltpu.VMEM((8, D), F32)] * 4,
        compiler_params=_cparams("arbitrary"),
    )(dxn, x, y, g, gamma)


def _mod_bwd(name, dh, x, dxp, sc):
    s = x.shape[0]
    tr = min(ROW_TILE, s)
    nt = s // tr

    def body(dh_ref, x_ref, dxp_ref, sc_ref, dx_ref, sums_ref, a0, a1):
        i = pl.program_id(0)
        dhv = dh_ref[...]
        dx_ref[...] = dxp_ref[...] + dhv * (1.0 + sc_ref[...])

        @pl.when(i == 0)
        def _():
            a0[...] = jnp.zeros_like(a0)
            a1[...] = jnp.zeros_like(a1)

        a0[...] += _rowsum8(dhv * x_ref[...])
        a1[...] += _rowsum8(dhv)

        @pl.when(i == nt - 1)
        def _():
            sums_ref[0:1, :] = jnp.sum(a0[...], axis=0, keepdims=True)
            sums_ref[1:2, :] = jnp.sum(a1[...], axis=0, keepdims=True)

    return pl.pallas_call(
        body, name=name, grid=(nt,),
        in_specs=[_row_spec(tr, D)] * 3 + [_vec_spec(D)],
        out_specs=[_row_spec(tr, D), pl.BlockSpec((2, D), lambda i: (0, 0))],
        out_shape=[jax.ShapeDtypeStruct((s, D), F32), jax.ShapeDtypeStruct((2, D), F32)],
        scratch_shapes=[pltpu.VMEM((8, D), F32)] * 2,
        compiler_params=_cparams("arbitrary"),
    )(dh, x, dxp, sc)


def _ffn_tiles(s):
    return min(ROW_TILE, s), FF // 2


def _ffn_fwd(name, h, wg, wu, wd):
    s = h.shape[0]
    tm, tf = _ffn_tiles(s)

    def body(h_ref, wg_ref, wu_ref, wd_ref, gate_ref, up_ref, y_ref):
        hv = h_ref[...]
        gt = jnp.dot(hv, wg_ref[...], preferred_element_type=F32)
        up = jnp.dot(hv, wu_ref[...], preferred_element_type=F32)
        gate_ref[...] = gt
        up_ref[...] = up
        act = (gt * jax.nn.sigmoid(gt) * up).astype(BF16)
        part = jnp.dot(act, wd_ref[...], preferred_element_type=F32)

        @pl.when(pl.program_id(1) == 0)
        def _():
            y_ref[...] = part

        @pl.when(pl.program_id(1) > 0)
        def _():
            y_ref[...] += part

    return pl.pallas_call(
        body, name=name, grid=(s // tm, FF // tf),
        in_specs=[pl.BlockSpec((tm, D), lambda i, f: (i, 0)), pl.BlockSpec((D, tf), lambda i, f: (0, f)),
                  pl.BlockSpec((D, tf), lambda i, f: (0, f)), pl.BlockSpec((tf, D), lambda i, f: (f, 0))],
        out_specs=[pl.BlockSpec((tm, tf), lambda i, f: (i, f)), pl.BlockSpec((tm, tf), lambda i, f: (i, f)),
                   pl.BlockSpec((tm, D), lambda i, f: (i, 0))],
        out_shape=[jax.ShapeDtypeStruct((s, FF), F32), jax.ShapeDtypeStruct((s, FF), F32),
                   jax.ShapeDtypeStruct((s, D), F32)],
        compiler_params=_cparams("parallel", "arbitrary"),
    )(h, wg, wu, wd)


def _ffn_bwd(name, dy, gate, up, wd_t, wg_t, wu_t):
    s = dy.shape[0]
    tm, tf = _ffn_tiles(s)

    def body(dy_ref, gate_ref, up_ref, wdt_ref, wgt_ref, wut_ref, dg_ref, du_ref, act_ref, dh_ref):
        dact = jnp.dot(dy_ref[...], wdt_ref[...], preferred_element_type=F32)
        gt = gate_ref[...]
        up = up_ref[...]
        sig = jax.nn.sigmoid(gt)
        silu = gt * sig
        dgt = (dact * up * (sig * (1.0 + gt * (1.0 - sig)))).astype(BF16)
        dup = (dact * silu).astype(BF16)
        dg_ref[...] = dgt
        du_ref[...] = dup
        act_ref[...] = (silu * up).astype(BF16)
        part = (jnp.dot(dgt, wgt_ref[...], preferred_element_type=F32)
                + jnp.dot(dup, wut_ref[...], preferred_element_type=F32))

        @pl.when(pl.program_id(1) == 0)
        def _():
            dh_ref[...] = part

        @pl.when(pl.program_id(1) > 0)
        def _():
            dh_ref[...] += part

    tile = pl.BlockSpec((tm, tf), lambda i, f: (i, f))
    return pl.pallas_call(
        body, name=name, grid=(s // tm, FF // tf),
        in_specs=[pl.BlockSpec((tm, D), lambda i, f: (i, 0)), tile, tile,
                  pl.BlockSpec((D, tf), lambda i, f: (0, f)), pl.BlockSpec((tf, D), lambda i, f: (f, 0)),
                  pl.BlockSpec((tf, D), lambda i, f: (f, 0))],
        out_specs=[tile, tile, tile, pl.BlockSpec((tm, D), lambda i, f: (i, 0))],
        out_shape=[jax.ShapeDtypeStruct((s, FF), BF16)] * 3 + [jax.ShapeDtypeStruct((s, D), F32)],
        compiler_params=_cparams("parallel", "arbitrary"),
    )(dy, gate, up, wd_t, wg_t, wu_t)


def _mla_lat_post(name, lat, qw, kvw, rc, rsa, rsb):
    s = lat.shape[0]
    tr = min(ROW_TILE, s)

    def body(lat_ref, qw_ref, kvw_ref, c_ref, sa_ref, sb_ref, qn_ref, kvn_ref, kr_ref):
        ql = lat_ref[:, 0:MLA_QR]
        kl = lat_ref[:, MLA_QR:MLA_QR + MLA_KVR]
        qn_ref[...] = (ql * lax.rsqrt(jnp.mean(ql * ql, axis=1, keepdims=True) + RMS_EPS) * qw_ref[...]).astype(BF16)
        kvn_ref[...] = (kl * lax.rsqrt(jnp.mean(kl * kl, axis=1, keepdims=True) + RMS_EPS) * kvw_ref[...]).astype(BF16)
        kr_ref[...] = _rope(lat_ref[:, MLA_QR + MLA_KVR:MLA_LAT], c_ref[...], sa_ref[...], sb_ref[...],
                            MLA_ROPE // 2).astype(BF16)

    return pl.pallas_call(
        body, name=name, grid=(s // tr,),
        in_specs=[_row_spec(tr, MLA_LAT), _vec_spec(MLA_QR), _vec_spec(MLA_KVR)] + [_row_spec(tr, LANES)] * 3,
        out_specs=[_row_spec(tr, MLA_QR), _row_spec(tr, MLA_KVR), _row_spec(tr, LANES)],
        out_shape=[jax.ShapeDtypeStruct((s, MLA_QR), BF16), jax.ShapeDtypeStruct((s, MLA_KVR), BF16),
                   jax.ShapeDtypeStruct((s, LANES), BF16)],
        compiler_params=_cparams("parallel"),
    )(lat, qw, kvw, rc, rsa, rsb)


def _mla_lat_bwd(name, lat, dqn, dkvn, dkr_heads, qw, kvw, rc, rsa, rsb):
    s = lat.shape[0]
    tr = min(ROW_TILE, s)
    nt = s // tr

    def rms_bwd(x, w, dy):
        r = lax.rsqrt(jnp.mean(x * x, axis=1, keepdims=True) + RMS_EPS)
        xh = x * r
        gdy = dy * w
        return r * (gdy - xh * jnp.mean(gdy * xh, axis=1, keepdims=True)), dy * xh

    def body(lat_ref, dqn_ref, dkvn_ref, dkr_ref, qw_ref, kvw_ref, c_ref, sa_ref, sb_ref,
             dlat_ref, dqw_ref, dkvw_ref, aq, akv):
        i = pl.program_id(0)
        dq, dqw = rms_bwd(lat_ref[:, 0:MLA_QR], qw_ref[...], dqn_ref[...])
        dk, dkw = rms_bwd(lat_ref[:, MLA_QR:MLA_QR + MLA_KVR], kvw_ref[...], dkvn_ref[...])
        dkr = dkr_ref[0]
        for hh in range(1, MLA_H):
            dkr = dkr + dkr_ref[hh]
        dkr = _rope_t(dkr, c_ref[...], sa_ref[...], sb_ref[...], MLA_ROPE // 2)
        dlat_ref[:, 0:MLA_QR] = dq.astype(BF16)
        dlat_ref[:, MLA_QR:MLA_QR + MLA_KVR] = dk.astype(BF16)
        dlat_ref[:, MLA_QR + MLA_KVR:MLA_LAT] = dkr.astype(BF16)

        @pl.when(i == 0)
        def _():
            aq[...] = jnp.zeros_like(aq)
            akv[...] = jnp.zeros_like(akv)

        aq[...] += _rowsum8(dqw)
        akv[...] += _rowsum8(dkw)

        @pl.when(i == nt - 1)
        def _():
            dqw_ref[...] = jnp.sum(aq[...], axis=0, keepdims=True)
            dkvw_ref[...] = jnp.sum(akv[...], axis=0, keepdims=True)

    return pl.pallas_call(
        body, name=name, grid=(nt,),
        in_specs=[_row_spec(tr, MLA_LAT), _row_spec(tr, MLA_QR), _row_spec(tr, MLA_KVR),
                  pl.BlockSpec((MLA_H, tr, LANES), lambda i: (0, i, 0)), _vec_spec(MLA_QR), _vec_spec(MLA_KVR)]
        + [_row_spec(tr, LANES)] * 3,
        out_specs=[_row_spec(tr, MLA_LAT), _vec_spec(MLA_QR), _vec_spec(MLA_KVR)],
        out_shape=[jax.ShapeDtypeStruct((s, MLA_LAT), BF16), jax.ShapeDtypeStruct((1, MLA_QR), F32),
                   jax.ShapeDtypeStruct((1, MLA_KVR), F32)],
        scratch_shapes=[pltpu.VMEM((8, MLA_QR), F32), pltpu.VMEM((8, MLA_KVR), F32)],
        compiler_params=_cparams("arbitrary"),
    )(lat, dqn, dkvn, dkr_heads, qw, kvw, rc, rsa, rsb)


def _attn_tile(s):
    return min(1024, max(LANES, s // 2))


MLA_SCALE = (MLA_NOPE + MLA_ROPE) ** -0.5
LOG2E = 1.4426950408889634
MLA_C2 = MLA_SCALE * LOG2E
NT_DIMS = (((1,), (1,)), ((), ()))
TN_DIMS = (((0,), (0,)), ((), ()))


def _causal(sc, transposed=False):
    row = lax.broadcasted_iota(jnp.int32, sc.shape, 0)
    col = lax.broadcasted_iota(jnp.int32, sc.shape, 1)
    return jnp.where(row <= col if transposed else col <= row, sc, NEG)


def _mla_attn_fwd(name, qq, kv, kr):
    s = qq.shape[0]
    t = _attn_tile(s)
    n = s // t

    def body(q_ref, kv_ref, kr_ref, o_ref, lse_ref, m_scr, acc_scr):
        qi = pl.program_id(1)
        ki = pl.program_id(2)

        @pl.when(ki == 0)
        def _():
            m_scr[...] = jnp.full(m_scr.shape, NEG, F32)
            acc_scr[...] = jnp.zeros_like(acc_scr)

        def step(diag):
            k = jnp.concatenate([kv_ref[:, 0:LANES], kr_ref[...]], axis=1)
            sc = lax.dot_general(q_ref[...], k, NT_DIMS, preferred_element_type=F32)
            if diag:
                sc = _causal(sc)
            m_prev = m_scr[...]
            m_next = jnp.maximum(m_prev, jnp.max(sc, axis=1, keepdims=True))
            a = jnp.exp2(MLA_C2 * (m_prev - m_next))
            p = jnp.exp2(MLA_C2 * sc - MLA_C2 * m_next[:, 0:1]).astype(BF16)
            v1 = jnp.concatenate([kv_ref[:, LANES:2 * LANES], jnp.ones((t, LANES), BF16)], axis=1)
            pv = jnp.dot(p, v1, preferred_element_type=F32)
            acc_scr[:, 0:LANES] = a * acc_scr[:, 0:LANES] + pv[:, 0:LANES]
            acc_scr[:, LANES:2 * LANES] = a * acc_scr[:, LANES:2 * LANES] + pv[:, LANES:2 * LANES]
            m_scr[...] = m_next

        @pl.when(ki < qi)
        def _():
            step(False)

        @pl.when(ki == qi)
        def _():
            step(True)
            l = acc_scr[:, LANES:2 * LANES]
            o_ref[...] = (acc_scr[:, 0:LANES] / l).astype(BF16)
            lse_ref[...] = MLA_SCALE * m_scr[...] + jnp.log(l)

    qblk = lambda w: pl.BlockSpec((t, w), lambda h, qi, ki: (qi, h))
    return pl.pallas_call(
        body, name=name, grid=(MLA_H, n, n),
        in_specs=[qblk(2 * LANES), pl.BlockSpec((t, 2 * LANES), lambda h, qi, ki: (jnp.minimum(ki, qi), h)),
                  pl.BlockSpec((t, LANES), lambda h, qi, ki: (jnp.minimum(ki, qi), 0))],
        out_specs=[qblk(LANES), qblk(LANES)],
        out_shape=[jax.ShapeDtypeStruct((s, MLA_H * MLA_V), BF16), jax.ShapeDtypeStruct((s, MLA_H * LANES), F32)],
        scratch_shapes=[pltpu.VMEM((t, LANES), F32), pltpu.VMEM((t, 2 * LANES), F32)],
        compiler_params=_cparams("parallel", "parallel", "arbitrary"),
    )(qq, kv, kr)


def _mla_attn_dq(name, qq, kv, kr, do, o, lse, rc, rsa, rsb):
    s = qq.shape[0]
    t = _attn_tile(s)
    n = s // t

    def body(q_ref, kv_ref, kr_ref, do_ref, o_ref, lse_ref, c_ref, sa_ref, sb_ref,
             dq_ref, stat_ref, acc_scr, lse2_scr, delta_scr):
        qi = pl.program_id(1)
        ki = pl.program_id(2)

        @pl.when(ki == 0)
        def _():
            acc_scr[...] = jnp.zeros_like(acc_scr)
            dl = jnp.sum(do_ref[...].astype(F32) * o_ref[...].astype(F32), axis=1, keepdims=True)
            delta_scr[...] = jnp.broadcast_to(dl, delta_scr.shape)
            lse2_scr[...] = lse_ref[...] * LOG2E

        def step(diag):
            k = jnp.concatenate([kv_ref[:, 0:LANES], kr_ref[...]], axis=1)
            sc = lax.dot_general(q_ref[...], k, NT_DIMS, preferred_element_type=F32)
            if diag:
                sc = _causal(sc)
            p = jnp.exp2(MLA_C2 * sc - lse2_scr[:, 0:1])
            dp = lax.dot_general(do_ref[...], kv_ref[:, LANES:2 * LANES], NT_DIMS, preferred_element_type=F32)
            ds = (p * (dp - delta_scr[:, 0:1])).astype(BF16)
            acc_scr[...] += jnp.dot(ds, k, preferred_element_type=F32)

        @pl.when(ki < qi)
        def _():
            step(False)

        @pl.when(ki == qi)
        def _():
            step(True)
            dq_ref[:, 0:LANES] = (MLA_SCALE * acc_scr[:, 0:LANES]).astype(BF16)
            dq_ref[:, LANES:2 * LANES] = _rope_t(MLA_SCALE * acc_scr[:, LANES:2 * LANES], c_ref[...], sa_ref[...],
                                                 sb_ref[...], MLA_ROPE // 2).astype(BF16)
            rows = lax.broadcasted_iota(jnp.int32, (8, t), 0)
            lse_t = jnp.transpose(lse2_scr[...])[0:8]
            delta_t = jnp.transpose(delta_scr[...])[0:8]
            stat_ref[0] = jnp.where(rows == 0, lse_t, jnp.where(rows == 1, delta_t, 0.0))

    qblk = lambda w: pl.BlockSpec((t, w), lambda h, qi, ki: (qi, h))
    tab = pl.BlockSpec((t, LANES), lambda h, qi, ki: (qi, 0))
    return pl.pallas_call(
        body, name=name, grid=(MLA_H, n, n),
        in_specs=[qblk(2 * LANES), pl.BlockSpec((t, 2 * LANES), lambda h, qi, ki: (jnp.minimum(ki, qi), h)),
                  pl.BlockSpec((t, LANES), lambda h, qi, ki: (jnp.minimum(ki, qi), 0)),
                  qblk(LANES), qblk(LANES), qblk(LANES), tab, tab, tab],
        out_specs=[qblk(2 * LANES), pl.BlockSpec((1, 8, t), lambda h, qi, ki: (h, 0, qi))],
        out_shape=[jax.ShapeDtypeStruct((s, 2 * MLA_H * LANES), BF16),
                   jax.ShapeDtypeStruct((MLA_H, 8, s), F32)],
        scratch_shapes=[pltpu.VMEM((t, 2 * LANES), F32), pltpu.VMEM((t, LANES), F32), pltpu.VMEM((t, LANES), F32)],
        compiler_params=_cparams("parallel", "parallel", "arbitrary"),
    )(qq, kv, kr, do, o, lse, rc, rsa, rsb)


def _mla_attn_dkv(name, qq, kv, kr, do, stats):
    s = qq.shape[0]
    t = _attn_tile(s)
    n = s // t

    def body(q_ref, kv_ref, kr_ref, do_ref, stat_ref, dkv_ref, dkr_ref, dk_scr, dv_scr):
        ki = pl.program_id(1)
        qi = pl.program_id(2)

        @pl.when(qi == 0)
        def _():
            dk_scr[...] = jnp.zeros_like(dk_scr)
            dv_scr[...] = jnp.zeros_like(dv_scr)

        def step(diag):
            q = q_ref[...]
            k = jnp.concatenate([kv_ref[:, 0:LANES], kr_ref[...]], axis=1)
            sc = lax.dot_general(k, q, NT_DIMS, preferred_element_type=F32)
            if diag:
                sc = _causal(sc, transposed=True)
            p = jnp.exp2(MLA_C2 * sc - stat_ref[0, 0:1, :])
            dov = do_ref[...]
            dp = lax.dot_general(kv_ref[:, LANES:2 * LANES], dov, NT_DIMS, preferred_element_type=F32)
            ds = (p * (dp - stat_ref[0, 1:2, :])).astype(BF16)
            dv_scr[...] += jnp.dot(p.astype(BF16), dov, preferred_element_type=F32)
            dk_scr[...] += jnp.dot(ds, q, preferred_element_type=F32)

        @pl.when(qi == ki)
        def _():
            step(True)

        @pl.when(qi > ki)
        def _():
            step(False)

        @pl.when(qi == n - 1)
        def _():
            dkv_ref[:, 0:LANES] = (MLA_SCALE * dk_scr[:, 0:LANES]).astype(BF16)
            dkv_ref[:, LANES:2 * LANES] = dv_scr[...].astype(BF16)
            dkr_ref[0] = MLA_SCALE * dk_scr[:, LANES:2 * LANES]

    qblk = lambda w: pl.BlockSpec((t, w), lambda h, ki, qi: (jnp.maximum(qi, ki), h))
    kblk = pl.BlockSpec((t, 2 * LANES), lambda h, ki, qi: (ki, h))
    return pl.pallas_call(
        body, name=name, grid=(MLA_H, n, n),
        in_specs=[qblk(2 * LANES), kblk, pl.BlockSpec((t, LANES), lambda h, ki, qi: (ki, 0)), qblk(LANES),
                  pl.BlockSpec((1, 8, t), lambda h, ki, qi: (h, 0, jnp.maximum(qi, ki)))],
        out_specs=[kblk, pl.BlockSpec((1, t, LANES), lambda h, ki, qi: (h, ki, 0))],
        out_shape=[jax.ShapeDtypeStruct((s, 2 * MLA_H * LANES), BF16),
                   jax.ShapeDtypeStruct((MLA_H, s, LANES), F32)],
        scratch_shapes=[pltpu.VMEM((t, 2 * LANES), F32), pltpu.VMEM((t, LANES), F32)],
        compiler_params=_cparams("parallel", "parallel", "arbitrary"),
    )(qq, kv, kr, do, stats)


def _rope_groups(acc, o_ref, c, sa, sb, sh, groups):
    for gi in range(acc.shape[1] // LANES):
        blk = acc[:, gi * LANES:(gi + 1) * LANES]
        if gi in groups:
            blk = _rope(blk, c, sa, sb, sh)
        o_ref[:, gi * LANES:(gi + 1) * LANES] = blk.astype(o_ref.dtype)


def _mla_fwd(tag, h, w, tabs):
    s = h.shape[0]
    rc, rsa, rsb = tabs
    lat = _mm(f"{tag}_lat", h, w["w_in"], tm=512)
    qn, kvn, kr = _mla_lat_post(f"{tag}_latpost", lat, w["q_norm"], w["kv_norm"], rc, rsa, rsb)
    tm = min(512, s)

    def q_epi(acc, o_ref, c_ref, sa_ref, sb_ref):
        _rope_groups(acc, o_ref, c_ref[...], sa_ref[...], sb_ref[...], MLA_ROPE // 2, range(1, MLA_H, 2))

    tab = pl.BlockSpec((tm, LANES), lambda i, j: (i, 0))
    qq = _mm(f"{tag}_q", qn, w["w_q"], tm=512, tn=MLA_H * LANES, out_dtype=BF16, epilogue=q_epi,
             extras=(rc, rsa, rsb), extra_specs=(tab, tab, tab))
    kv = _mm(f"{tag}_kv", kvn, w["w_kv"], tm=512, out_dtype=BF16)
    o, lse = _mla_attn_fwd(f"{tag}_attn", qq, kv, kr)
    y = _mm(f"{tag}_o", o, w["w_o"], tm=512)
    return y, dict(h=h, lat=lat, qn=qn, kvn=kvn, kr=kr, qq=qq, kv=kv, o=o, lse=lse)


def _mla_bwd(tag, dy, res, w, tabs):
    rc, rsa, rsb = tabs
    do = _mm(f"{tag}_do", dy, w["w_o_t"], tm=512, out_dtype=BF16)
    g_wo = _mm_tn(f"{tag}_gwo", res["o"], dy)
    dqq, stats = _mla_attn_dq(f"{tag}_dq", res["qq"], res["kv"], res["kr"], do, res["o"], res["lse"], rc, rsa, rsb)
    dkv, dkr = _mla_attn_dkv(f"{tag}_dkv", res["qq"], res["kv"], res["kr"], do, stats)
    dqn = _mm(f"{tag}_dqn", dqq, w["w_q_t"], tm=512)
    g_wq = _mm_tn(f"{tag}_gwq", res["qn"], dqq, tn=1024)
    dkvn = _mm(f"{tag}_dkvn", dkv, w["w_kv_t"], tm=512)
    g_wkv = _mm_tn(f"{tag}_gwkv", res["kvn"], dkv, tn=1024)
    dlat, g_qn, g_kvn = _mla_lat_bwd(f"{tag}_latbwd", res["lat"], dqn, dkvn, dkr, w["q_norm"], w["kv_norm"],
                                     rc, rsa, rsb)
    dh = _mm(f"{tag}_dh", dlat, w["w_in_t"], tm=512)
    g_win = _mm_tn(f"{tag}_gwin", res["h"], dlat)
    return dh, dict(w_in=g_win, q_norm=g_qn, w_q=g_wq, kv_norm=g_kvn, w_kv=g_wkv, w_o=g_wo)


SWA_QW = SWA_HQ * SWA_HD
SWA_KW = SWA_HKV * LANES
SWA_NQKV = SWA_QW + 2 * SWA_KW
SWA_SCALE = SWA_HD ** -0.5
SWA_GROUP_ROWS = 4 * SWA_W


def _swa_tile(s):
    return min(512, max(SWA_W, s // 2))


def _swa_masks():
    lane = lax.broadcasted_iota(jnp.int32, (SWA_W, LANES), 1)
    return lane < SWA_HD


def _swa_q4(qa, qb, lo):
    z = jnp.zeros_like(qa)
    return jnp.concatenate([jnp.where(lo, qa, z), jnp.where(lo, z, qa), jnp.where(lo, qb, z), jnp.where(lo, z, qb)],
                           axis=0)


def _swa_probs(q4, kwin, sink_col, first_block):
    sc = lax.dot_general(q4, kwin, NT_DIMS, preferred_element_type=F32) * SWA_SCALE
    row = lax.broadcasted_iota(jnp.int32, sc.shape, 0) % SWA_W
    col = lax.broadcasted_iota(jnp.int32, sc.shape, 1)
    rel = row + SWA_W - col
    ok = (rel >= 0) & (rel < SWA_W) & ((col >= SWA_W) | jnp.logical_not(first_block))
    sc = jnp.where(ok, sc, NEG)
    m = jnp.maximum(jnp.max(sc, axis=1, keepdims=True), sink_col)
    e = jnp.exp(sc - m)
    es = jnp.exp(sink_col - m)
    inv = 1.0 / (jnp.sum(e, axis=1, keepdims=True) + es)
    return e * inv, es * inv


def _sink_col(sinks_ref, grp):
    seg = lax.broadcasted_iota(jnp.int32, (SWA_GROUP_ROWS, 1), 0) // SWA_W
    col = jnp.zeros((SWA_GROUP_ROWS, 1), F32)
    for j in range(4):
        col = jnp.where(seg == j, sinks_ref[0, 4 * grp + j], col)
    return col


def _swa_attn_fwd(name, qkv, sinks):
    s = qkv.shape[0]
    t = _swa_tile(s)
    nb = t // SWA_W

    def body(sinks_ref, q_ref, kv_ref, kvp_ref, o_ref):
        i = pl.program_id(0)
        lo = _swa_masks()
        for grp in range(SWA_HKV):
            sink_col = _sink_col(sinks_ref, grp)
            kcat = jnp.concatenate([kvp_ref[:, grp * LANES:(grp + 1) * LANES],
                                    kv_ref[:, grp * LANES:(grp + 1) * LANES]], axis=0)
            vcat = jnp.concatenate([kvp_ref[:, SWA_KW + grp * LANES:SWA_KW + (grp + 1) * LANES],
                                    kv_ref[:, SWA_KW + grp * LANES:SWA_KW + (grp + 1) * LANES]], axis=0)
            for b in range(nb):
                r0 = b * SWA_W
                qa = q_ref[r0:r0 + SWA_W, grp * 2 * LANES:grp * 2 * LANES + LANES]
                qb = q_ref[r0:r0 + SWA_W, grp * 2 * LANES + LANES:(grp + 1) * 2 * LANES]
                first = jnp.logical_and(i == 0, b == 0)
                p, _ = _swa_probs(_swa_q4(qa, qb, lo), kcat[r0:r0 + 2 * SWA_W], sink_col, first)
                o4 = jnp.dot(p.astype(BF16), vcat[r0:r0 + 2 * SWA_W], preferred_element_type=F32)
                oa = jnp.where(lo, o4[0:SWA_W], o4[SWA_W:2 * SWA_W])
                ob = jnp.where(lo, o4[2 * SWA_W:3 * SWA_W], o4[3 * SWA_W:4 * SWA_W])
                o_ref[r0:r0 + SWA_W, grp * 2 * LANES:grp * 2 * LANES + LANES] = oa.astype(BF16)
                o_ref[r0:r0 + SWA_W, grp * 2 * LANES + LANES:(grp + 1) * 2 * LANES] = ob.astype(BF16)

    return pl.pallas_call(
        body, name=name, grid=(s // t,),
        in_specs=[pl.BlockSpec(memory_space=pltpu.SMEM),
                  pl.BlockSpec((t, SWA_QW), lambda i: (i, 0)),
                  pl.BlockSpec((t, 2 * SWA_KW), lambda i: (i, 1)),
                  pl.BlockSpec((SWA_W, 2 * SWA_KW), lambda i: (jnp.maximum(i * nb - 1, 0), 1))],
        out_specs=pl.BlockSpec((t, SWA_QW), lambda i: (i, 0)),
        out_shape=jax.ShapeDtypeStruct((s, SWA_QW), BF16),
        compiler_params=_cparams("parallel"),
    )(sinks, qkv, qkv, qkv)


def _swa_attn_bwd(name, qkv, sinks, do):
    s = qkv.shape[0]
    t = _swa_tile(s)
    nb = t // SWA_W
    nt = s // t

    def body(sinks_ref, q_ref, kv_ref, kvp_ref, do_ref, dq_ref, dkv_ref, dkvp_ref, dsink_ref, dcat, sink_acc):
        i = pl.program_id(0)
        lo = _swa_masks()

        @pl.when(i == 0)
        def _():
            sink_acc[...] = jnp.zeros_like(sink_acc)

        dcat[...] = jnp.zeros_like(dcat)
        for grp in range(SWA_HKV):
            sink_col = _sink_col(sinks_ref, grp)
            kcat = jnp.concatenate([kvp_ref[:, grp * LANES:(grp + 1) * LANES],
                                    kv_ref[:, grp * LANES:(grp + 1) * LANES]], axis=0)
            vcat = jnp.concatenate([kvp_ref[:, SWA_KW + grp * LANES:SWA_KW + (grp + 1) * LANES],
                                    kv_ref[:, SWA_KW + grp * LANES:SWA_KW + (grp + 1) * LANES]], axis=0)
            for b in range(nb):
                r0 = b * SWA_W
                ca = slice(grp * 2 * LANES, grp * 2 * LANES + LANES)
                cb = slice(grp * 2 * LANES + LANES, (grp + 1) * 2 * LANES)
                q4 = _swa_q4(q_ref[r0:r0 + SWA_W, ca], q_ref[r0:r0 + SWA_W, cb], lo)
                do4 = _swa_q4(do_ref[r0:r0 + SWA_W, ca], do_ref[r0:r0 + SWA_W, cb], lo)
                first = jnp.logical_and(i == 0, b == 0)
                kwin = kcat[r0:r0 + 2 * SWA_W]
                vwin = vcat[r0:r0 + 2 * SWA_W]
                p, ps = _swa_probs(q4, kwin, sink_col, first)
                dp = lax.dot_general(do4, vwin, NT_DIMS, preferred_element_type=F32)
                rowdot = jnp.sum(p * dp, axis=1, keepdims=True)
                ds = (p * (dp - rowdot) * SWA_SCALE).astype(BF16)
                sink_acc[grp] += jnp.broadcast_to(-ps * rowdot, (SWA_GROUP_ROWS, LANES))
                dq4 = jnp.dot(ds, kwin, preferred_element_type=F32)
                dq_ref[r0:r0 + SWA_W, ca] = jnp.where(lo, dq4[0:SWA_W], dq4[SWA_W:2 * SWA_W])
                dq_ref[r0:r0 + SWA_W, cb] = jnp.where(lo, dq4[2 * SWA_W:3 * SWA_W], dq4[3 * SWA_W:4 * SWA_W])
                dk = lax.dot_general(ds, q4, TN_DIMS, preferred_element_type=F32)
                dv = lax.dot_general(p.astype(BF16), do4, TN_DIMS, preferred_element_type=F32)
                dcat[r0:r0 + 2 * SWA_W, grp * LANES:(grp + 1) * LANES] += dk
                dcat[r0:r0 + 2 * SWA_W, SWA_KW + grp * LANES:SWA_KW + (grp + 1) * LANES] += dv
        dkvp_ref[0] = dcat[0:SWA_W]
        dkv_ref[...] = dcat[SWA_W:SWA_W + t]

        @pl.when(i == nt - 1)
        def _():
            for grp in range(SWA_HKV):
                for j in range(4):
                    tot = jnp.sum(sink_acc[grp, j * SWA_W:(j + 1) * SWA_W, 0:1])
                    dsink_ref[4 * grp + j:4 * grp + j + 1, :] = jnp.full((1, LANES), tot, F32)

    return pl.pallas_call(
        body, name=name, grid=(nt,),
        in_specs=[pl.BlockSpec(memory_space=pltpu.SMEM),
                  pl.BlockSpec((t, SWA_QW), lambda i: (i, 0)),
                  pl.BlockSpec((t, 2 * SWA_KW), lambda i: (i, 1)),
                  pl.BlockSpec((SWA_W, 2 * SWA_KW), lambda i: (jnp.maximum(i * nb - 1, 0), 1)),
                  pl.BlockSpec((t, SWA_QW), lambda i: (i, 0))],
        out_specs=[pl.BlockSpec((t, SWA_QW), lambda i: (i, 0)), pl.BlockSpec((t, 2 * SWA_KW), lambda i: (i, 0)),
                   pl.BlockSpec((1, SWA_W, 2 * SWA_KW), lambda i: (i, 0, 0)),
                   pl.BlockSpec((SWA_HQ, LANES), lambda i: (0, 0))],
        out_shape=[jax.ShapeDtypeStruct((s, SWA_QW), F32), jax.ShapeDtypeStruct((s, 2 * SWA_KW), F32),
                   jax.ShapeDtypeStruct((nt, SWA_W, 2 * SWA_KW), F32), jax.ShapeDtypeStruct((SWA_HQ, LANES), F32)],
        scratch_shapes=[pltpu.VMEM((SWA_W + t, 2 * SWA_KW), F32), pltpu.VMEM((SWA_HKV, SWA_GROUP_ROWS, LANES), F32)],
        compiler_params=_cparams("arbitrary"),
    )(sinks, qkv, qkv, qkv, do)


def _swa_dqkv(name, dq, dkv, dkvp, rc, rsa, rsb):
    s = dq.shape[0]
    t = _swa_tile(s)
    nt = s // t
    sh = SWA_ROT // 2

    def body(dq_ref, dkv_ref, dkvn_ref, c_ref, sa_ref, sb_ref, out_ref, bsum_ref, acc):
        i = pl.program_id(0)
        c, sa, sb = c_ref[...], sa_ref[...], sb_ref[...]
        lo = lax.broadcasted_iota(jnp.int32, (t, LANES), 1) < SWA_HD
        rows = lax.broadcasted_iota(jnp.int32, (t, LANES), 0)
        tail = jnp.logical_and(rows >= t - SWA_W, i < nt - 1)

        @pl.when(i == 0)
        def _():
            acc[...] = jnp.zeros_like(acc)

        for gi in range(SWA_QW // LANES):
            blk = _rope_t(dq_ref[:, gi * LANES:(gi + 1) * LANES], c, sa, sb, sh)
            out_ref[:, gi * LANES:(gi + 1) * LANES] = blk.astype(BF16)
            acc[:, gi * LANES:(gi + 1) * LANES] += _rowsum8(blk)
        for gi in range(2 * SWA_KW // LANES):
            cols = slice(gi * LANES, (gi + 1) * LANES)
            nxt = jnp.concatenate([jnp.zeros((t - SWA_W, LANES), F32), dkvn_ref[0, :, cols]], axis=0)
            blk = dkv_ref[:, cols] + jnp.where(tail, nxt, 0.0)
            blk = jnp.where(lo, blk + pltpu.roll(blk, SWA_HD, 1), 0.0)
            if gi < SWA_HKV:
                blk = _rope_t(blk, c, sa, sb, sh)
            out_ref[:, SWA_QW + gi * LANES:SWA_QW + (gi + 1) * LANES] = blk.astype(BF16)
            acc[:, SWA_QW + gi * LANES:SWA_QW + (gi + 1) * LANES] += _rowsum8(blk)

        @pl.when(i == nt - 1)
        def _():
            bsum_ref[...] = jnp.sum(acc[...], axis=0, keepdims=True)

    return pl.pallas_call(
        body, name=name, grid=(nt,),
        in_specs=[pl.BlockSpec((t, SWA_QW), lambda i: (i, 0)), pl.BlockSpec((t, 2 * SWA_KW), lambda i: (i, 0)),
                  pl.BlockSpec((1, SWA_W, 2 * SWA_KW), lambda i: (jnp.minimum(i + 1, nt - 1), 0, 0))]
        + [_row_spec(t, LANES)] * 3,
        out_specs=[pl.BlockSpec((t, SWA_NQKV), lambda i: (i, 0)), pl.BlockSpec((1, SWA_NQKV), lambda i: (0, 0))],
        out_shape=[jax.ShapeDtypeStruct((s, SWA_NQKV), BF16), jax.ShapeDtypeStruct((1, SWA_NQKV), F32)],
        scratch_shapes=[pltpu.VMEM((8, SWA_NQKV), F32)],
        compiler_params=_cparams("arbitrary"),
    )(dq, dkv, dkvp, rc, rsa, rsb)


def _swa_fwd(tag, h, w, tabs):
    s = h.shape[0]
    rc, rsa, rsb = tabs
    tm = min(512, s)
    sh = SWA_ROT // 2

    def qkv_epi(acc, o_ref, b_ref, c_ref, sa_ref, sb_ref):
        acc = acc + b_ref[...]

        @pl.when(pl.program_id(1) == 0)
        def _():
            _rope_groups(acc, o_ref, c_ref[...], sa_ref[...], sb_ref[...], sh, range(SWA_QW // LANES))

        @pl.when(pl.program_id(1) == 1)
        def _():
            _rope_groups(acc, o_ref, c_ref[...], sa_ref[...], sb_ref[...], sh, range(SWA_HKV))

    tab = pl.BlockSpec((tm, LANES), lambda i, j: (i, 0))
    qkv = _mm(f"{tag}_qkv", h, w["w_qkv"], tm=512, tn=SWA_QW, out_dtype=BF16, epilogue=qkv_epi,
              extras=(w["b_qkv"], rc, rsa, rsb),
              extra_specs=(pl.BlockSpec((1, SWA_QW), lambda i, j: (0, j)), tab, tab, tab))
    o = _swa_attn_fwd(f"{tag}_attn", qkv, w["sinks"])

    def o_epi(acc, o_ref, b_ref):
        o_ref[...] = acc + b_ref[...]

    y = _mm(f"{tag}_o", o, w["w_o"], tm=512, epilogue=o_epi, extras=(w["b_o"],),
            extra_specs=(pl.BlockSpec((1, D), lambda i, j: (0, 0)),))
    return y, dict(h=h, qkv=qkv, o=o)


def _swa_bwd(tag, dy, res, w, tabs):
    rc, rsa, rsb = tabs
    do = _mm(f"{tag}_do", dy, w["w_o_t"], tm=512, out_dtype=BF16)
    g_wo = _mm_tn(f"{tag}_gwo", res["o"], dy)
    dq, dkv, dkvp, dsink = _swa_attn_bwd(f"{tag}_attnbwd", res["qkv"], w["sinks"], do)
    dqkv, g_b = _swa_dqkv(f"{tag}_dqkv", dq, dkv, dkvp, rc, rsa, rsb)
    dh = _mm(f"{tag}_dh", dqkv, w["w_qkv_t"], tm=512)
    g_wqkv = _mm_tn(f"{tag}_gwqkv", res["h"], dqkv, tn=1024)
    return dh, dict(w_qkv=g_wqkv, b_qkv=g_b, sinks=dsink, w_o=g_wo)


def _ada_fwd(name, c_all, w_sh, b_sh):
    cols = w_sh.shape[2]
    tn = cols // 3

    def body(c_ref, w_ref, b_ref, o_ref, cond_ref):
        cv = c_ref[...]
        cond = cv * jax.nn.sigmoid(cv)
        cond_ref[...] = cond
        o_ref[0] = jnp.dot(cond, w_ref[0], preferred_element_type=F32, precision=lax.Precision.HIGHEST) + b_ref[0]

    return pl.pallas_call(
        body, name=name, grid=(DEPTH, cols // tn),
        in_specs=[pl.BlockSpec((8, D), lambda l, j: (0, 0)), pl.BlockSpec((1, D, tn), lambda l, j: (l, 0, j)),
                  pl.BlockSpec((1, 1, tn), lambda l, j: (l, 0, j))],
        out_specs=[pl.BlockSpec((1, 8, tn), lambda l, j: (l, 0, j)), pl.BlockSpec((8, D), lambda l, j: (0, 0))],
        out_shape=[jax.ShapeDtypeStruct((DEPTH, 8, cols), F32), jax.ShapeDtypeStruct((8, D), F32)],
        compiler_params=_cparams("arbitrary", "arbitrary"),
    )(c_all, w_sh, b_sh)


def _ada_grad(name, cond_t, dmod_sh):
    cols = dmod_sh.shape[2]
    tn = cols // 3

    def body(ct_ref, dm_ref, o_ref):
        acc = ct_ref[:, 0:1] * dm_ref[0, 0:1, :]
        for b in range(1, 8):
            acc = acc + ct_ref[:, b:b + 1] * dm_ref[0, b:b + 1, :]
        o_ref[0] = acc

    return pl.pallas_call(
        body, name=name, grid=(DEPTH, cols // tn),
        in_specs=[pl.BlockSpec((D, 8), lambda l, j: (0, 0)), pl.BlockSpec((1, 8, tn), lambda l, j: (l, 0, j))],
        out_specs=pl.BlockSpec((1, D, tn), lambda l, j: (l, 0, j)),
        out_shape=jax.ShapeDtypeStruct((DEPTH, D, cols), F32),
        compiler_params=_cparams("parallel", "parallel"),
    )(cond_t, dmod_sh)


def _adamw(name, g, w, m, v):
    r = g.shape[0]
    tr = min(ROW_TILE, r)

    def body(g_ref, w_ref, m_ref, v_ref, d_ref, nm_ref, nv_ref):
        gv = g_ref[...]
        mn = ADAM_B1 * m_ref[...] + (1.0 - ADAM_B1) * gv
        vn = ADAM_B2 * v_ref[...] + (1.0 - ADAM_B2) * (gv * gv)
        m_hat = mn / (1.0 - ADAM_B1 ** ADAM_STEP)
        v_hat = vn / (1.0 - ADAM_B2 ** ADAM_STEP)
        d_ref[...] = -ADAM_LR * (m_hat / (jnp.sqrt(v_hat) + ADAM_EPS) + ADAM_WD * w_ref[...])
        nm_ref[...] = mn
        nv_ref[...] = vn

    spec = _row_spec(tr, PACK_COLS)
    return pl.pallas_call(
        body, name=name, grid=(r // tr,),
        in_specs=[spec] * 4, out_specs=[spec] * 3,
        out_shape=[jax.ShapeDtypeStruct(g.shape, F32)] * 3,
        compiler_params=_cparams("parallel"),
    )(g, w, m, v)


def _to_chips(full, axis):
    shp = full.shape
    a = full.reshape(shp[:axis] + (N_CHIPS, shp[axis] // N_CHIPS) + shp[axis + 1:])
    return jnp.moveaxis(a, axis, 0)


def _from_chips(stacked, axis):
    a = jnp.moveaxis(stacked, 0, axis)
    shp = a.shape
    return a.reshape(shp[:axis] + (shp[axis] * shp[axis + 1],) + shp[axis + 2:])


PIECE_ROW_ALIGN = 16


def _piece_rows(shape):
    n = 1
    for d in shape:
        n *= d
    rows = -(-n // PACK_COLS)
    return -(-rows // PIECE_ROW_ALIGN) * PIECE_ROW_ALIGN


def _as_rows(a, lead):
    head = a.shape[:lead]
    rows = _piece_rows(a.shape[lead:])
    n = 1
    for d in a.shape[lead:]:
        n *= d
    if n == rows * PACK_COLS:
        return a.reshape(head + (rows, PACK_COLS))
    flat = jnp.pad(a.reshape(head + (n,)), [(0, 0)] * lead + [(0, rows * PACK_COLS - n)])
    return flat.reshape(head + (rows, PACK_COLS))


def _pack(parts, lead, rows):
    pieces = [_as_rows(p, lead) for p in parts]
    used = sum(p.shape[lead] for p in pieces)
    head = pieces[0].shape[:lead]
    pieces.append(jnp.zeros(head + (rows - used, PACK_COLS), pieces[0].dtype))
    return jnp.concatenate(pieces, axis=lead)


def _unpack(packed, lead, shapes):
    out, off = [], 0
    head = packed.shape[:lead]
    for shp in shapes:
        rows = _piece_rows(shp)
        n = 1
        for d in shp:
            n *= d
        piece = lax.slice_in_dim(packed, off, off + rows, axis=lead)
        if n != rows * PACK_COLS:
            piece = piece.reshape(head + (rows * PACK_COLS,))[..., :n]
        out.append(piece.reshape(head + tuple(shp)))
        off += rows
    return out


def _pack_rows(shapes):
    rows = sum(_piece_rows(s) for s in shapes)
    return -(-rows // PACK_ROW_ALIGN) * PACK_ROW_ALIGN


def _rope_tables(positions, rot, lanes_per_head):
    half = rot // 2
    inv = ROPE_THETA ** (-jnp.arange(0, rot, 2, dtype=F32) / rot)
    ang = positions.astype(F32)[:, None] * inv
    cos, sin = jnp.cos(ang), jnp.sin(ang)
    s = positions.shape[0]
    rest = lanes_per_head - rot
    fill = 1.0 if lanes_per_head == SWA_HD else 0.0
    c = jnp.concatenate([cos, cos, jnp.full((s, rest), fill, F32)], axis=1)
    sa = jnp.concatenate([-sin, jnp.zeros((s, half + rest), F32)], axis=1)
    sb = jnp.concatenate([jnp.zeros((s, half), F32), sin, jnp.zeros((s, rest), F32)], axis=1)
    reps = LANES // lanes_per_head
    return tuple(jnp.tile(t, (1, reps)) for t in (c, sa, sb))


def _mla_weights(w_in, q_norm, w_q_b, kv_norm, w_kv_b, w_o):
    w_in_p = jnp.pad(w_in, ((0, 0), (0, MLA_LAT - w_in.shape[1])))
    wq = w_q_b.reshape(MLA_QR, MLA_H, MLA_NOPE + MLA_ROPE)
    wq_p = jnp.pad(wq, ((0, 0), (0, 0), (0, 2 * LANES - MLA_NOPE - MLA_ROPE))).reshape(MLA_QR, MLA_H * 2 * LANES)
    return dict(w_in=w_in_p, w_in_t=w_in_p.T, q_norm=q_norm.reshape(1, -1), kv_norm=kv_norm.reshape(1, -1),
                w_q=wq_p, w_q_t=wq_p.T, w_kv=w_kv_b, w_kv_t=w_kv_b.T, w_o=w_o, w_o_t=w_o.T)


def _mla_grads_unpermute(g):
    gq = g["w_q"].reshape(MLA_QR, MLA_H, 2 * LANES)[:, :, :MLA_NOPE + MLA_ROPE]
    return dict(mla_w_in=g["w_in"][:, :MLA_QR + MLA_KVR + MLA_ROPE], mla_q_norm=g["q_norm"][0],
                mla_w_q_b=gq.reshape(MLA_QR, -1), mla_kv_norm=g["kv_norm"][0], mla_w_kv_b=g["w_kv"],
                mla_w_o=g["w_o"])


def _swa_dup(a):
    lead = a.shape[:-1]
    a = a.reshape(lead + (SWA_HKV, SWA_HD))
    return jnp.concatenate([a, a], axis=-1).reshape(lead + (SWA_KW,))


def _swa_undup(a):
    lead = a.shape[:-1]
    return a.reshape(lead + (SWA_HKV, LANES))[..., :SWA_HD].reshape(lead + (SWA_HKV * SWA_HD,))


def _swa_weights(w_qkv, b_qkv, sinks, w_o, b_o):
    nk = SWA_HKV * SWA_HD
    perm = lambda a: jnp.concatenate([a[..., :SWA_QW], _swa_dup(a[..., SWA_QW:SWA_QW + nk]),
                                      _swa_dup(a[..., SWA_QW + nk:])], axis=-1)
    w_p = perm(w_qkv)
    return dict(w_qkv=w_p, w_qkv_t=w_p.T, b_qkv=perm(b_qkv.astype(F32)).reshape(1, -1),
                sinks=sinks.reshape(1, -1), w_o=w_o, w_o_t=w_o.T, b_o=b_o.astype(F32).reshape(1, -1))


def _swa_grads_unpermute(g):
    unperm = lambda a: jnp.concatenate([a[..., :SWA_QW], _swa_undup(a[..., SWA_QW:SWA_QW + SWA_KW]),
                                        _swa_undup(a[..., SWA_QW + SWA_KW:])], axis=-1)
    return dict(swa_w_qkv=unperm(g["w_qkv"]), swa_b_qkv=unperm(g["b_qkv"])[0], swa_sinks=g["sinks"][:, 0],
                swa_w_o=g["w_o"], swa_b_o=g["b_o"])


SMALL_LAYOUT = (("ada_b", 24), ("ln_mix_g", 4), ("ln_mix_b", 4), ("ln_ffn_g", 4), ("ln_ffn_b", 4),
                ("mla_q_norm", 2), ("mla_kv_norm", 2), ("swa_sinks", 1), ("loss", 1))


def _small_pack(vals):
    rows = []
    for name, nrows in SMALL_LAYOUT:
        a = vals[name].reshape(nrows, -1).astype(F32)
        rows.append(jnp.pad(a, ((0, 0), (0, PACK_COLS - a.shape[1]))))
    cat = jnp.concatenate(rows, axis=0)
    return jnp.pad(cat, ((0, SMALL_ROWS - cat.shape[0]), (0, 0)))


def _small_unpack(packed, shapes):
    out, r = {}, 0
    for name, nrows in SMALL_LAYOUT:
        shp = shapes[name]
        n = 1
        for d in shp:
            n *= d
        out[name] = packed[r:r + nrows, :n // nrows].reshape(shp)
        r += nrows
    return out


def kernel(x, c, positions, ada_w, ada_b, ln_mix_g, ln_mix_b, ln_ffn_g, ln_ffn_b, ffn_w_gate, ffn_w_up, ffn_w_down, mla_w_in, mla_q_norm, mla_w_q_b, mla_kv_norm, mla_w_kv_b, mla_w_o, swa_w_qkv, swa_b_qkv, swa_sinks, swa_w_o, swa_b_o, loss_target, m_ada_w, m_ada_b, m_ln_mix_g, m_ln_mix_b, m_ln_ffn_g, m_ln_ffn_b, m_ffn_w_gate, m_ffn_w_up, m_ffn_w_down, m_mla_w_in, m_mla_q_norm, m_mla_w_q_b, m_mla_kv_norm, m_mla_w_kv_b, m_mla_w_o, m_swa_w_qkv, m_swa_b_qkv, m_swa_sinks, m_swa_w_o, m_swa_b_o, v_ada_w, v_ada_b, v_ln_mix_g, v_ln_mix_b, v_ln_ffn_g, v_ln_ffn_b, v_ffn_w_gate, v_ffn_w_up, v_ffn_w_down, v_mla_w_in, v_mla_q_norm, v_mla_w_q_b, v_mla_kv_norm, v_mla_w_kv_b, v_mla_w_o, v_swa_w_qkv, v_swa_b_qkv, v_swa_sinks, v_swa_w_o, v_swa_b_o):
    weights = dict(ada_w=ada_w, ada_b=ada_b, ln_mix_g=ln_mix_g, ln_mix_b=ln_mix_b, ln_ffn_g=ln_ffn_g,
                   ln_ffn_b=ln_ffn_b, ffn_w_gate=ffn_w_gate, ffn_w_up=ffn_w_up, ffn_w_down=ffn_w_down,
                   mla_w_in=mla_w_in, mla_q_norm=mla_q_norm, mla_w_q_b=mla_w_q_b, mla_kv_norm=mla_kv_norm,
                   mla_w_kv_b=mla_w_kv_b, mla_w_o=mla_w_o, swa_w_qkv=swa_w_qkv, swa_b_qkv=swa_b_qkv,
                   swa_sinks=swa_sinks, swa_w_o=swa_w_o, swa_b_o=swa_b_o)
    mom_m = dict(ada_w=m_ada_w, ada_b=m_ada_b, ln_mix_g=m_ln_mix_g, ln_mix_b=m_ln_mix_b, ln_ffn_g=m_ln_ffn_g,
                 ln_ffn_b=m_ln_ffn_b, ffn_w_gate=m_ffn_w_gate, ffn_w_up=m_ffn_w_up, ffn_w_down=m_ffn_w_down,
                 mla_w_in=m_mla_w_in, mla_q_norm=m_mla_q_norm, mla_w_q_b=m_mla_w_q_b, mla_kv_norm=m_mla_kv_norm,
                 mla_w_kv_b=m_mla_w_kv_b, mla_w_o=m_mla_w_o, swa_w_qkv=m_swa_w_qkv, swa_b_qkv=m_swa_b_qkv,
                 swa_sinks=m_swa_sinks, swa_w_o=m_swa_w_o, swa_b_o=m_swa_b_o)
    mom_v = dict(ada_w=v_ada_w, ada_b=v_ada_b, ln_mix_g=v_ln_mix_g, ln_mix_b=v_ln_mix_b, ln_ffn_g=v_ln_ffn_g,
                 ln_ffn_b=v_ln_ffn_b, ffn_w_gate=v_ffn_w_gate, ffn_w_up=v_ffn_w_up, ffn_w_down=v_ffn_w_down,
                 mla_w_in=v_mla_w_in, mla_q_norm=v_mla_q_norm, mla_w_q_b=v_mla_w_q_b, mla_kv_norm=v_mla_kv_norm,
                 mla_w_kv_b=v_mla_w_kv_b, mla_w_o=v_mla_w_o, swa_w_qkv=v_swa_w_qkv, swa_b_qkv=v_swa_b_qkv,
                 swa_sinks=v_swa_sinks, swa_w_o=v_swa_w_o, swa_b_o=v_swa_b_o)
    names = list(weights)
    my_x, my_y, my_c = lax.axis_index("x"), lax.axis_index("y"), lax.axis_index("c")
    chip = 2 * my_x + my_y
    batch_row = 2 * chip + my_c
    xs = x[0]
    target = loss_target[0]
    pos = positions[0]
    s = xs.shape[0]

    shard_shapes = [weights[n].shape for n, _ in SHARDED]
    rows = _pack_rows(shard_shapes)
    half = rows // 2
    wpack = _pack([weights[n].astype(BF16) for n, _ in SHARDED], 0, rows)
    my_half = lax.dynamic_slice_in_dim(wpack, my_c * half, half, axis=0)
    by_chip = _exchange("ag_w_chips", my_half, ("x", "y"), "gather", chunks=8)
    by_core = _exchange("ag_w_cores", by_chip.reshape(N_CHIPS * half, PACK_COLS), ("c",), "gather", chunks=32)
    gathered = jnp.moveaxis(by_core.reshape(2, N_CHIPS, half, PACK_COLS), 0, 1).reshape(N_CHIPS, rows, PACK_COLS)
    full = {}
    for (n, axis), part in zip(SHARDED, _unpack(gathered, 1, shard_shapes)):
        full[n] = _from_chips(part, axis)

    c_rows = jnp.pad(c, ((0, 7), (0, 0)))
    c_all = _exchange("ag_c", c_rows, ("x", "y", "c"), "gather")[:, 0, :]
    ada_cols = ada_w.shape[2]
    ada_b_sh = lax.dynamic_slice_in_dim(ada_b, chip * ada_cols, ada_cols, axis=1).reshape(DEPTH, 1, ada_cols)
    mod_sh, cond_all = _ada_fwd("ada_fwd", c_all, ada_w, ada_b_sh)
    mod_all = _exchange("ag_mod", mod_sh.reshape(DEPTH * 8, ada_cols), ("x", "y"), "gather")
    mod_all = mod_all.reshape(N_CHIPS, DEPTH, 8, ada_cols)
    mod_mine = lax.dynamic_index_in_dim(mod_all, batch_row, axis=2, keepdims=False)
    mod = jnp.moveaxis(mod_mine, 0, 1).reshape(DEPTH, 6, 1, D)

    tabs_a = _rope_tables(pos, MLA_ROPE, LANES)
    tabs_b = _rope_tables(pos, SWA_ROT, SWA_HD)
    vec = lambda a, l: a[l].reshape(1, D)

    mix_w, ffn_w = [], []
    for l in range(DEPTH):
        j = l // 2
        if l % 2 == 0:
            mix_w.append(_mla_weights(full["mla_w_in"][j], mla_q_norm[j], full["mla_w_q_b"][j], mla_kv_norm[j],
                                      full["mla_w_kv_b"][j], full["mla_w_o"][j]))
        else:
            mix_w.append(_swa_weights(full["swa_w_qkv"][j], full["swa_b_qkv"][j], swa_sinks[j], full["swa_w_o"][j],
                                      full["swa_b_o"][j]))
        ffn_w.append(dict(wg=full["ffn_w_gate"][l], wu=full["ffn_w_up"][l], wd=full["ffn_w_down"][l],
                          wg_t=full["ffn_w_gate"][l].T, wu_t=full["ffn_w_up"][l].T, wd_t=full["ffn_w_down"][l].T))

    saved = []
    x_cur = xs
    h = _modulate("mod0", x_cur, mod[0, 1], mod[0, 0])
    for l in range(DEPTH):
        if l % 2 == 0:
            y_mix, res = _mla_fwd(f"mla{l}", h, mix_w[l], tabs_a)
        else:
            y_mix, res = _swa_fwd(f"swa{l}", h, mix_w[l], tabs_b)
        x_mid, h2 = _post_mod(f"post_mix{l}", x_cur, y_mix, mod[l, 2], vec(ln_mix_g, l), vec(ln_mix_b, l),
                              mod[l, 4], mod[l, 3])
        gate, up, y_ffn = _ffn_fwd(f"ffn{l}", h2, ffn_w[l]["wg"], ffn_w[l]["wu"], ffn_w[l]["wd"])
        saved.append(dict(x_in=x_cur, y_mix=y_mix, res=res, x_mid=x_mid, h2=h2, gate=gate, up=up, y_ffn=y_ffn))
        if l < DEPTH - 1:
            x_cur, h = _post_mod(f"post_ffn{l}", x_mid, y_ffn, mod[l, 5], vec(ln_ffn_g, l), vec(ln_ffn_b, l),
                                 mod[l + 1, 1], mod[l + 1, 0])
        else:
            dxn, loss_part = _post_loss("post_loss", x_mid, y_ffn, mod[l, 5], vec(ln_ffn_g, l), vec(ln_ffn_b, l),
                                        target)

    gfull = {n: [None] * weights[n].shape[0] for n, _ in SHARDED}
    gsmall = {n: [None] * weights[n].shape[0] for n in ("ln_mix_g", "ln_mix_b", "ln_ffn_g", "ln_ffn_b",
                                                         "mla_q_norm", "mla_kv_norm", "swa_sinks")}
    dmod = [None] * DEPTH
    for l in reversed(range(DEPTH)):
        sv = saved[l]
        j = l // 2
        dxp, dy, sums_f = _post_bwd(f"post_ffn_bwd{l}", dxn, sv["x_mid"], sv["y_ffn"], mod[l, 5], vec(ln_ffn_g, l))
        dgt, dup, act, dh2 = _ffn_bwd(f"ffn_bwd{l}", dy, sv["gate"], sv["up"], ffn_w[l]["wd_t"], ffn_w[l]["wg_t"],
                                      ffn_w[l]["wu_t"])
        gfull["ffn_w_gate"][l] = _mm_tn(f"ffn_gwg{l}", sv["h2"], dgt, tn=FF // 2)
        gfull["ffn_w_up"][l] = _mm_tn(f"ffn_gwu{l}", sv["h2"], dup, tn=FF // 2)
        gfull["ffn_w_down"][l] = _mm_tn(f"ffn_gwd{l}", act, dy)
        dx_mid, sums_fm = _mod_bwd(f"mod_ffn_bwd{l}", dh2, sv["x_mid"], dxp, mod[l, 4])
        dxp, dy, sums_m = _post_bwd(f"post_mix_bwd{l}", dx_mid, sv["x_in"], sv["y_mix"], mod[l, 2], vec(ln_mix_g, l))
        if l % 2 == 0:
            dh, g = _mla_bwd(f"mla{l}", dy, sv["res"], mix_w[l], tabs_a)
            g = _mla_grads_unpermute(g)
        else:
            dh, g = _swa_bwd(f"swa{l}", dy, sv["res"], mix_w[l], tabs_b)
            g["b_o"] = sums_m[3]
            g = _swa_grads_unpermute(g)
        for n, val in g.items():
            (gfull if n in gfull else gsmall)[n][j] = val
        dxn, sums_mm = _mod_bwd(f"mod_mix_bwd{l}", dh, sv["x_in"], dxp, mod[l, 1])
        gsmall["ln_ffn_g"][l], gsmall["ln_ffn_b"][l] = sums_f[0], sums_f[1]
        gsmall["ln_mix_g"][l], gsmall["ln_mix_b"][l] = sums_m[0], sums_m[1]
        dmod[l] = jnp.stack([sums_mm[1], sums_mm[0], sums_m[2], sums_fm[1], sums_fm[0], sums_f[2]])
    grad_x = dxn[None]

    small_vals = {n: jnp.stack(v) for n, v in gsmall.items()}
    small_vals["ada_b"] = jnp.stack(dmod)
    small_vals["loss"] = loss_part[0, 0:1]
    small_all = _exchange("ag_small", _small_pack(small_vals), ("x", "y", "c"), "gather")
    small_sum = _sum_groups("sum_small", small_all)
    dmod_all = small_all[:, :DEPTH * 6, :].reshape(8, DEPTH, 6 * D)
    dmod_sh = jnp.moveaxis(lax.dynamic_slice_in_dim(dmod_all, chip * ada_cols, ada_cols, axis=2), 0, 1)
    g_ada_w = _ada_grad("ada_grad", cond_all.T, dmod_sh)

    gsend = _pack([_to_chips(jnp.stack(gfull[n]), axis) for n, axis in SHARDED], 1, rows)
    by_half = jnp.moveaxis(gsend.reshape(N_CHIPS, 2, half, PACK_COLS), 1, 0).reshape(2, N_CHIPS * half, PACK_COLS)
    core_parts = _exchange("rs_cores", by_half, ("c",), "a2a", chunks=32)
    core_sum = _sum_groups("rs_sum_cores", core_parts).reshape(N_CHIPS, half, PACK_COLS)
    chip_parts = _exchange("rs_chips", core_sum, ("x", "y"), "a2a", chunks=8)
    chip_sum = _sum_groups("rs_sum_chips", chip_parts)
    g_pack = _exchange("rs_share", chip_sum, ("c",), "gather", chunks=32).reshape(rows, PACK_COLS)

    ada_rows = int(ada_w.size) // PACK_COLS
    small_shapes = {n: weights[n].shape for n, _ in SMALL_LAYOUT if n != "loss"}
    small_shapes["loss"] = (1,)

    def flat_all(src):
        sharded = _pack([src[n] for n, _ in SHARDED], 0, rows)
        small = _small_pack({**{n: src[n] for n in small_shapes if n != "loss"}, "loss": jnp.zeros((1,), F32)})
        return jnp.concatenate([sharded, src["ada_w"].reshape(ada_rows, PACK_COLS), small], axis=0)

    g_flat = jnp.concatenate([g_pack, g_ada_w.reshape(ada_rows, PACK_COLS), small_sum], axis=0)
    delta, new_m, new_v = _adamw("adamw", g_flat, flat_all(weights), flat_all(mom_m), flat_all(mom_v))

    def split_all(flat):
        out = dict(zip([n for n, _ in SHARDED], _unpack(flat[:rows], 0, shard_shapes)))
        out["ada_w"] = flat[rows:rows + ada_rows].reshape(ada_w.shape)
        out.update(_small_unpack(flat[rows + ada_rows:], small_shapes))
        return out

    outs = [split_all(a) for a in (g_flat, delta, new_m, new_v)]
    loss = outs[0]["loss"][0]
    return (loss, grad_x, *[o[n] for o in outs for n in names])
```

```python
import jax
import jax.numpy as jnp
from jax import lax
from jax.experimental import pallas as pl
from jax.experimental.pallas import tpu as pltpu

F32 = jnp.float32
BF16 = jnp.bfloat16

D = 1024
DEPTH = 4
ROPE_THETA = 500000.0
LN_EPS = 1e-5
RMS_EPS = 1e-6
MLA_H = 8
MLA_NOPE = 128
MLA_ROPE = 64
MLA_V = 128
MLA_QR = 384
MLA_KVR = 256
MLA_LAT = 768
SWA_HQ = 16
SWA_HKV = 4
SWA_HD = 64
SWA_W = 128
SWA_ROT = 16
FF = 2816
ALPHA = (2 * DEPTH) ** 0.25
ADAM_LR = 0.001
ADAM_B1 = 0.9
ADAM_B2 = 0.999
ADAM_EPS = 1e-08
ADAM_WD = 0.01
ADAM_STEP = 10
NEG = -1e30
LANES = 128
N_CHIPS = 4
PACK_COLS = 1024
PACK_ROW_ALIGN = 1024
SMALL_ROWS = 512
ROW_TILE = 512

SHARDED = (
    ("ffn_w_gate", 2), ("ffn_w_up", 2), ("ffn_w_down", 1), ("mla_w_in", 1), ("mla_w_q_b", 2),
    ("mla_w_kv_b", 2), ("mla_w_o", 1), ("swa_w_qkv", 2), ("swa_b_qkv", 1), ("swa_w_o", 1), ("swa_b_o", 1),
)


def _cparams(*sem):
    return pltpu.CompilerParams(dimension_semantics=sem)


def _row_spec(tr, cols):
    return pl.BlockSpec((tr, cols), lambda i: (i, 0))


def _vec_spec(cols):
    return pl.BlockSpec((1, cols), lambda i: (0, 0))


def _rope(x, c, sa, sb, sh):
    n = x.shape[1]
    return x * c + pltpu.roll(x, n - sh, 1) * sa + pltpu.roll(x, sh, 1) * sb


def _rope_t(d, c, sa, sb, sh):
    n = d.shape[1]
    return d * c + pltpu.roll(d * sa, sh, 1) + pltpu.roll(d * sb, n - sh, 1)


def _rowsum8(t):
    r, n = t.shape
    return jnp.sum(t.reshape(r // 8, 8, n), axis=0)


def _exchange(name, src, axes, mode, chunks=1):
    g = 2 ** len(axes)
    blk = src.shape if mode == "gather" else src.shape[1:]
    rows = blk[0] // chunks

    def body(src_ref, out_ref, send_sems, recv_sems, loc_sem):
        pos = {a: lax.axis_index(a) for a in ("x", "y", "c")}

        def gidx(p):
            idx = 0
            for a in axes:
                idx = idx * 2 + p[a]
            return idx

        def view(i):
            return src_ref if mode == "gather" else src_ref.at[i]

        me = gidx(pos)
        loc = pltpu.make_async_copy(view(me), out_ref.at[me], loc_sem)
        loc.start()
        copies = []
        for k in range(chunks):
            piece = pl.ds(k * rows, rows)
            for j in range(1, g):
                peer = dict(pos)
                for bit, a in enumerate(reversed(axes)):
                    if (j >> bit) & 1:
                        peer[a] = 1 - pos[a]
                sem = (j - 1) * chunks + k
                cp = pltpu.make_async_remote_copy(
                    src_ref=view(gidx(peer)).at[piece], dst_ref=out_ref.at[me, piece],
                    send_sem=send_sems.at[sem], recv_sem=recv_sems.at[sem],
                    device_id=(peer["x"], peer["y"], peer["c"]), device_id_type=pl.DeviceIdType.MESH)
                cp.start()
                copies.append(cp)
        for cp in copies:
            cp.wait()
        loc.wait()

    nsem = (g - 1) * chunks
    return pl.pallas_call(
        body, name=name,
        out_shape=jax.ShapeDtypeStruct((g,) + tuple(blk), src.dtype),
        in_specs=[pl.BlockSpec(memory_space=pl.ANY)],
        out_specs=pl.BlockSpec(memory_space=pl.ANY),
        scratch_shapes=[pltpu.SemaphoreType.DMA((nsem,)), pltpu.SemaphoreType.DMA((nsem,)),
                        pltpu.SemaphoreType.DMA(())],
    )(src)


def _sum_groups(name, a):
    g, r, c = a.shape
    tr = min(ROW_TILE, r)

    def body(a_ref, o_ref):
        acc = a_ref[0]
        for i in range(1, g):
            acc = acc + a_ref[i]
        o_ref[...] = acc

    return pl.pallas_call(
        body, name=name, grid=(r // tr,),
        in_specs=[pl.BlockSpec((g, tr, c), lambda i: (0, i, 0))],
        out_specs=pl.BlockSpec((tr, c), lambda i: (i, 0)),
        out_shape=jax.ShapeDtypeStruct((r, c), F32),
        compiler_params=_cparams("parallel"),
    )(a)


def _mm(name, a, b, *, tm, tn=None, out_dtype=F32, epilogue=None, extras=(), extra_specs=()):
    m, k = a.shape
    n = b.shape[1]
    tn = tn or n
    tm = min(tm, m)

    def body(a_ref, b_ref, *rest):
        o_ref = rest[-1]
        acc = jnp.dot(a_ref[...], b_ref[...], preferred_element_type=F32)
        if epilogue is None:
            o_ref[...] = acc.astype(o_ref.dtype)
        else:
            epilogue(acc, o_ref, *rest[:-1])

    return pl.pallas_call(
        body, name=name, grid=(m // tm, n // tn),
        in_specs=[pl.BlockSpec((tm, k), lambda i, j: (i, 0)), pl.BlockSpec((k, tn), lambda i, j: (0, j)),
                  *extra_specs],
        out_specs=pl.BlockSpec((tm, tn), lambda i, j: (i, j)),
        out_shape=jax.ShapeDtypeStruct((m, n), out_dtype),
        compiler_params=_cparams("parallel", "parallel"),
    )(a, b, *extras)


def _mm_tn(name, a, b, *, tn=None, tk=1024):
    s, m = a.shape
    n = b.shape[1]
    tn = tn or n
    tk = min(tk, s)

    def body(a_ref, b_ref, o_ref):
        part = lax.dot_general(a_ref[...], b_ref[...], (((0,), (0,)), ((), ())), preferred_element_type=F32)

        @pl.when(pl.program_id(1) == 0)
        def _():
            o_ref[...] = part

        @pl.when(pl.program_id(1) > 0)
        def _():
            o_ref[...] += part

    return pl.pallas_call(
        body, name=name, grid=(n // tn, s // tk),
        in_specs=[pl.BlockSpec((tk, m), lambda j, k: (k, 0)), pl.BlockSpec((tk, tn), lambda j, k: (k, j))],
        out_specs=pl.BlockSpec((m, tn), lambda j, k: (0, j)),
        out_shape=jax.ShapeDtypeStruct((m, n), F32),
        compiler_params=_cparams("parallel", "arbitrary"),
    )(a, b)


def _modulate(name, x, sc, sh):
    s = x.shape[0]
    tr = min(ROW_TILE, s)

    def body(x_ref, sc_ref, sh_ref, h_ref):
        h_ref[...] = (x_ref[...] * (1.0 + sc_ref[...]) + sh_ref[...]).astype(BF16)

    return pl.pallas_call(
        body, name=name, grid=(s // tr,),
        in_specs=[_row_spec(tr, D), _vec_spec(D), _vec_spec(D)],
        out_specs=_row_spec(tr, D),
        out_shape=jax.ShapeDtypeStruct((s, D), BF16),
        compiler_params=_cparams("parallel"),
    )(x, sc, sh)


def _ln_stats(z):
    mu = jnp.mean(z, axis=1, keepdims=True)
    zc = z - mu
    var = jnp.mean(zc * zc, axis=1, keepdims=True)
    r = lax.rsqrt(var + LN_EPS)
    return zc * r, r


def _post_mod(name, x, y, g, gamma, beta, sc, sh):
    s = x.shape[0]
    tr = min(ROW_TILE, s)

    def body(x_ref, y_ref, g_ref, ga_ref, be_ref, sc_ref, sh_ref, xn_ref, h_ref):
        zh, _ = _ln_stats(ALPHA * x_ref[...] + g_ref[...] * y_ref[...])
        xn = zh * ga_ref[...] + be_ref[...]
        xn_ref[...] = xn
        h_ref[...] = (xn * (1.0 + sc_ref[...]) + sh_ref[...]).astype(BF16)

    return pl.pallas_call(
        body, name=name, grid=(s // tr,),
        in_specs=[_row_spec(tr, D), _row_spec(tr, D)] + [_vec_spec(D)] * 5,
        out_specs=[_row_spec(tr, D), _row_spec(tr, D)],
        out_shape=[jax.ShapeDtypeStruct((s, D), F32), jax.ShapeDtypeStruct((s, D), BF16)],
        compiler_params=_cparams("parallel"),
    )(x, y, g, gamma, beta, sc, sh)


def _post_loss(name, x, y, g, gamma, beta, target):
    s = x.shape[0]
    tr = min(ROW_TILE, s)
    nt = s // tr

    def body(x_ref, y_ref, g_ref, ga_ref, be_ref, t_ref, dx_ref, loss_ref, acc):
        i = pl.program_id(0)
        zh, _ = _ln_stats(ALPHA * x_ref[...] + g_ref[...] * y_ref[...])
        e = zh * ga_ref[...] + be_ref[...] - t_ref[...]
        dx_ref[...] = e * (1.0 / D)

        @pl.when(i == 0)
        def _():
            acc[...] = jnp.zeros_like(acc)

        acc[...] += _rowsum8(e * e)

        @pl.when(i == nt - 1)
        def _():
            loss_ref[...] = jnp.full(loss_ref.shape, jnp.sum(acc[...]) * (0.5 / D), F32)

    return pl.pallas_call(
        body, name=name, grid=(nt,),
        in_specs=[_row_spec(tr, D), _row_spec(tr, D)] + [_vec_spec(D)] * 3 + [_row_spec(tr, D)],
        out_specs=[_row_spec(tr, D), pl.BlockSpec((8, LANES), lambda i: (0, 0))],
        out_shape=[jax.ShapeDtypeStruct((s, D), F32), jax.ShapeDtypeStruct((8, LANES), F32)],
        scratch_shapes=[pltpu.VMEM((8, D), F32)],
        compiler_params=_cparams("arbitrary"),
    )(x, y, g, gamma, beta, target)


def _post_bwd(name, dxn, x, y, g, gamma):
    s = x.shape[0]
    tr = min(ROW_TILE, s)
    nt = s // tr

    def body(d_ref, x_ref, y_ref, g_ref, ga_ref, dxp_ref, dy_ref, sums_ref, a0, a1, a2, a3):
        i = pl.program_id(0)
        yv = y_ref[...]
        gv = g_ref[...]
        zh, r = _ln_stats(ALPHA * x_ref[...] + gv * yv)
        dxn_v = d_ref[...]
        dzh = dxn_v * ga_ref[...]
        dz = r * (dzh - jnp.mean(dzh, axis=1, keepdims=True) - zh * jnp.mean(dzh * zh, axis=1, keepdims=True))
        dxp_ref[...] = ALPHA * dz
        dyv = gv * dz
        dy_ref[...] = dyv.astype(BF16)

        @pl.when(i == 0)
        def _():
            for a in (a0, a1, a2, a3):
                a[...] = jnp.zeros_like(a)

        a0[...] += _rowsum8(dxn_v * zh)
        a1[...] += _rowsum8(dxn_v)
        a2[...] += _rowsum8(dz * yv)
        a3[...] += _rowsum8(dyv)

        @pl.when(i == nt - 1)
        def _():
            for k, a in enumerate((a0, a1, a2, a3)):
                sums_ref[k:k + 1, :] = jnp.sum(a[...], axis=0, keepdims=True)

    return pl.pallas_call(
        body, name=name, grid=(nt,),
        in_specs=[_row_spec(tr, D)] * 3 + [_vec_spec(D)] * 2,
        out_specs=[_row_spec(tr, D), _row_spec(tr, D), pl.BlockSpec((4, D), lambda i: (0, 0))],
        out_shape=[jax.ShapeDtypeStruct((s, D), F32), jax.ShapeDtypeStruct((s, D), BF16),
                   jax.ShapeDtypeStruct((4, D), F32)],
        scratch_shapes=[pltpu.VMEM((8, D), F32)] * 4,
        compiler_params=_cparams("arbitrary"),
    )(dxn, x, y, g, gamma)


def _mod_bwd(name, dh, x, dxp, sc):
    s = x.shape[0]
    tr = min(ROW_TILE, s)
    nt = s // tr

    def body(dh_ref, x_ref, dxp_ref, sc_ref, dx_ref, sums_ref, a0, a1):
        i = pl.program_id(0)
        dhv = dh_ref[...]
        dx_ref[...] = dxp_ref[...] + dhv * (1.0 + sc_ref[...])

        @pl.when(i == 0)
        def _():
            a0[...] = jnp.zeros_like(a0)
            a1[...] = jnp.zeros_like(a1)

        a0[...] += _rowsum8(dhv * x_ref[...])
        a1[...] += _rowsum8(dhv)

        @pl.when(i == nt - 1)
        def _():
            sums_ref[0:1, :] = jnp.sum(a0[...], axis=0, keepdims=True)
            sums_ref[1:2, :] = jnp.sum(a1[...], axis=0, keepdims=True)

    return pl.pallas_call(
        body, name=name, grid=(nt,),
        in_specs=[_row_spec(tr, D)] * 3 + [_vec_spec(D)],
        out_specs=[_row_spec(tr, D), pl.BlockSpec((2, D), lambda i: (0, 0))],
        out_shape=[jax.ShapeDtypeStruct((s, D), F32), jax.ShapeDtypeStruct((2, D), F32)],
        scratch_shapes=[pltpu.VMEM((8, D), F32)] * 2,
        compiler_params=_cparams("arbitrary"),
    )(dh, x, dxp, sc)


def _ffn_tiles(s):
    return min(ROW_TILE, s), FF // 2


def _ffn_fwd(name, h, wg, wu, wd):
    s = h.shape[0]
    tm, tf = _ffn_tiles(s)

    def body(h_ref, wg_ref, wu_ref, wd_ref, gate_ref, up_ref, y_ref):
        hv = h_ref[...]
        gt = jnp.dot(hv, wg_ref[...], preferred_element_type=F32)
        up = jnp.dot(hv, wu_ref[...], preferred_element_type=F32)
        gate_ref[...] = gt
        up_ref[...] = up
        act = (gt * jax.nn.sigmoid(gt) * up).astype(BF16)
        part = jnp.dot(act, wd_ref[...], preferred_element_type=F32)

        @pl.when(pl.program_id(1) == 0)
        def _():
            y_ref[...] = part

        @pl.when(pl.program_id(1) > 0)
        def _():
            y_ref[...] += part

    return pl.pallas_call(
        body, name=name, grid=(s // tm, FF // tf),
        in_specs=[pl.BlockSpec((tm, D), lambda i, f: (i, 0)), pl.BlockSpec((D, tf), lambda i, f: (0, f)),
                  pl.BlockSpec((D, tf), lambda i, f: (0, f)), pl.BlockSpec((tf, D), lambda i, f: (f, 0))],
        out_specs=[pl.BlockSpec((tm, tf), lambda i, f: (i, f)), pl.BlockSpec((tm, tf), lambda i, f: (i, f)),
                   pl.BlockSpec((tm, D), lambda i, f: (i, 0))],
        out_shape=[jax.ShapeDtypeStruct((s, FF), F32), jax.ShapeDtypeStruct((s, FF), F32),
                   jax.ShapeDtypeStruct((s, D), F32)],
        compiler_params=_cparams("parallel", "arbitrary"),
    )(h, wg, wu, wd)


def _ffn_bwd(name, dy, gate, up, wd_t, wg_t, wu_t):
    s = dy.shape[0]
    tm, tf = _ffn_tiles(s)

    def body(dy_ref, gate_ref, up_ref, wdt_ref, wgt_ref, wut_ref, dg_ref, du_ref, act_ref, dh_ref):
        dact = jnp.dot(dy_ref[...], wdt_ref[...], preferred_element_type=F32)
        gt = gate_ref[...]
        up = up_ref[...]
        sig = jax.nn.sigmoid(gt)
        silu = gt * sig
        dgt = (dact * up * (sig * (1.0 + gt * (1.0 - sig)))).astype(BF16)
        dup = (dact * silu).astype(BF16)
        dg_ref[...] = dgt
        du_ref[...] = dup
        act_ref[...] = (silu * up).astype(BF16)
        part = (jnp.dot(dgt, wgt_ref[...], preferred_element_type=F32)
                + jnp.dot(dup, wut_ref[...], preferred_element_type=F32))

        @pl.when(pl.program_id(1) == 0)
        def _():
            dh_ref[...] = part

        @pl.when(pl.program_id(1) > 0)
        def _():
            dh_ref[...] += part

    tile = pl.BlockSpec((tm, tf), lambda i, f: (i, f))
    return pl.pallas_call(
        body, name=name, grid=(s // tm, FF // tf),
        in_specs=[pl.BlockSpec((tm, D), lambda i, f: (i, 0)), tile, tile,
                  pl.BlockSpec((D, tf), lambda i, f: (0, f)), pl.BlockSpec((tf, D), lambda i, f: (f, 0)),
                  pl.BlockSpec((tf, D), lambda i, f: (f, 0))],
        out_specs=[tile, tile, tile, pl.BlockSpec((tm, D), lambda i, f: (i, 0))],
        out_shape=[jax.ShapeDtypeStruct((s, FF), BF16)] * 3 + [jax.ShapeDtypeStruct((s, D), F32)],
        compiler_params=_cparams("parallel", "arbitrary"),
    )(dy, gate, up, wd_t, wg_t, wu_t)


def _mla_lat_post(name, lat, qw, kvw, rc, rsa, rsb):
    s = lat.shape[0]
    tr = min(ROW_TILE, s)

    def body(lat_ref, qw_ref, kvw_ref, c_ref, sa_ref, sb_ref, qn_ref, kvn_ref, kr_ref):
        ql = lat_ref[:, 0:MLA_QR]
        kl = lat_ref[:, MLA_QR:MLA_QR + MLA_KVR]
        qn_ref[...] = (ql * lax.rsqrt(jnp.mean(ql * ql, axis=1, keepdims=True) + RMS_EPS) * qw_ref[...]).astype(BF16)
        kvn_ref[...] = (kl * lax.rsqrt(jnp.mean(kl * kl, axis=1, keepdims=True) + RMS_EPS) * kvw_ref[...]).astype(BF16)
        kr_ref[...] = _rope(lat_ref[:, MLA_QR + MLA_KVR:MLA_LAT], c_ref[...], sa_ref[...], sb_ref[...],
                            MLA_ROPE // 2).astype(BF16)

    return pl.pallas_call(
        body, name=name, grid=(s // tr,),
        in_specs=[_row_spec(tr, MLA_LAT), _vec_spec(MLA_QR), _vec_spec(MLA_KVR)] + [_row_spec(tr, LANES)] * 3,
        out_specs=[_row_spec(tr, MLA_QR), _row_spec(tr, MLA_KVR), _row_spec(tr, LANES)],
        out_shape=[jax.ShapeDtypeStruct((s, MLA_QR), BF16), jax.ShapeDtypeStruct((s, MLA_KVR), BF16),
                   jax.ShapeDtypeStruct((s, LANES), BF16)],
        compiler_params=_cparams("parallel"),
    )(lat, qw, kvw, rc, rsa, rsb)


def _mla_lat_bwd(name, lat, dqn, dkvn, dkr_heads, qw, kvw, rc, rsa, rsb):
    s = lat.shape[0]
    tr = min(ROW_TILE, s)
    nt = s // tr

    def rms_bwd(x, w, dy):
        r = lax.rsqrt(jnp.mean(x * x, axis=1, keepdims=True) + RMS_EPS)
        xh = x * r
        gdy = dy * w
        return r * (gdy - xh * jnp.mean(gdy * xh, axis=1, keepdims=True)), dy * xh

    def body(lat_ref, dqn_ref, dkvn_ref, dkr_ref, qw_ref, kvw_ref, c_ref, sa_ref, sb_ref,
             dlat_ref, dqw_ref, dkvw_ref, aq, akv):
        i = pl.program_id(0)
        dq, dqw = rms_bwd(lat_ref[:, 0:MLA_QR], qw_ref[...], dqn_ref[...])
        dk, dkw = rms_bwd(lat_ref[:, MLA_QR:MLA_QR + MLA_KVR], kvw_ref[...], dkvn_ref[...])
        dkr = dkr_ref[0]
        for hh in range(1, MLA_H):
            dkr = dkr + dkr_ref[hh]
        dkr = _rope_t(dkr, c_ref[...], sa_ref[...], sb_ref[...], MLA_ROPE // 2)
        dlat_ref[:, 0:MLA_QR] = dq.astype(BF16)
        dlat_ref[:, MLA_QR:MLA_QR + MLA_KVR] = dk.astype(BF16)
        dlat_ref[:, MLA_QR + MLA_KVR:MLA_LAT] = dkr.astype(BF16)

        @pl.when(i == 0)
        def _():
            aq[...] = jnp.zeros_like(aq)
            akv[...] = jnp.zeros_like(akv)

        aq[...] += _rowsum8(dqw)
        akv[...] += _rowsum8(dkw)

        @pl.when(i == nt - 1)
        def _():
            dqw_ref[...] = jnp.sum(aq[...], axis=0, keepdims=True)
            dkvw_ref[...] = jnp.sum(akv[...], axis=0, keepdims=True)

    return pl.pallas_call(
        body, name=name, grid=(nt,),
        in_specs=[_row_spec(tr, MLA_LAT), _row_spec(tr, MLA_QR), _row_spec(tr, MLA_KVR),
                  pl.BlockSpec((MLA_H, tr, LANES), lambda i: (0, i, 0)), _vec_spec(MLA_QR), _vec_spec(MLA_KVR)]
        + [_row_spec(tr, LANES)] * 3,
        out_specs=[_row_spec(tr, MLA_LAT), _vec_spec(MLA_QR), _vec_spec(MLA_KVR)],
        out_shape=[jax.ShapeDtypeStruct((s, MLA_LAT), BF16), jax.ShapeDtypeStruct((1, MLA_QR), F32),
                   jax.ShapeDtypeStruct((1, MLA_KVR), F32)],
        scratch_shapes=[pltpu.VMEM((8, MLA_QR), F32), pltpu.VMEM((8, MLA_KVR), F32)],
        compiler_params=_cparams("arbitrary"),
    )(lat, dqn, dkvn, dkr_heads, qw, kvw, rc, rsa, rsb)


def _attn_tile(s):
    return min(1024, max(LANES, s // 2))


MLA_SCALE = (MLA_NOPE + MLA_ROPE) ** -0.5
LOG2E = 1.4426950408889634
MLA_C2 = MLA_SCALE * LOG2E
NT_DIMS = (((1,), (1,)), ((), ()))
TN_DIMS = (((0,), (0,)), ((), ()))


def _causal(sc, transposed=False):
    row = lax.broadcasted_iota(jnp.int32, sc.shape, 0)
    col = lax.broadcasted_iota(jnp.int32, sc.shape, 1)
    return jnp.where(row <= col if transposed else col <= row, sc, NEG)


def _mla_attn_fwd(name, qq, kv, kr):
    s = qq.shape[0]
    t = _attn_tile(s)
    n = s // t

    def body(q_ref, kv_ref, kr_ref, o_ref, lse_ref, m_scr, acc_scr, sc_scr):
        qi = pl.program_id(1)
        ki = pl.program_id(2)

        @pl.when(ki == 0)
        def _():
            m_scr[...] = jnp.full(m_scr.shape, NEG, F32)
            acc_scr[...] = jnp.zeros_like(acc_scr)

        @pl.when(ki <= qi)
        def _():
            k = jnp.concatenate([kv_ref[:, 0:LANES], kr_ref[...]], axis=1)
            sc_scr[...] = lax.dot_general(q_ref[...], k, NT_DIMS, preferred_element_type=F32)

        def step(diag):
            sc = sc_scr[...]
            if diag:
                sc = _causal(sc)
            m_prev = m_scr[...]
            m_next = jnp.maximum(m_prev, jnp.max(sc, axis=1, keepdims=True))
            a = jnp.exp2(MLA_C2 * (m_prev - m_next))
            p = jnp.exp2(MLA_C2 * sc - MLA_C2 * m_next[:, 0:1]).astype(BF16)
            v1 = jnp.concatenate([kv_ref[:, LANES:2 * LANES], jnp.ones((t, LANES), BF16)], axis=1)
            pv = jnp.dot(p, v1, preferred_element_type=F32)
            acc_scr[:, 0:LANES] = a * acc_scr[:, 0:LANES] + pv[:, 0:LANES]
            acc_scr[:, LANES:2 * LANES] = a * acc_scr[:, LANES:2 * LANES] + pv[:, LANES:2 * LANES]
            m_scr[...] = m_next

        @pl.when(ki < qi)
        def _():
            step(False)

        @pl.when(ki == qi)
        def _():
            step(True)

        @pl.when(ki == qi)
        def _():
            l = acc_scr[:, LANES:2 * LANES]
            o_ref[...] = (acc_scr[:, 0:LANES] / l).astype(BF16)
            lse_ref[...] = MLA_SCALE * m_scr[...] + jnp.log(l)

    qblk = lambda w: pl.BlockSpec((t, w), lambda h, qi, ki: (qi, h))
    return pl.pallas_call(
        body, name=name, grid=(MLA_H, n, n),
        in_specs=[qblk(2 * LANES), pl.BlockSpec((t, 2 * LANES), lambda h, qi, ki: (jnp.minimum(ki, qi), h)),
                  pl.BlockSpec((t, LANES), lambda h, qi, ki: (jnp.minimum(ki, qi), 0))],
        out_specs=[qblk(LANES), qblk(LANES)],
        out_shape=[jax.ShapeDtypeStruct((s, MLA_H * MLA_V), BF16), jax.ShapeDtypeStruct((s, MLA_H * LANES), F32)],
        scratch_shapes=[pltpu.VMEM((t, LANES), F32), pltpu.VMEM((t, 2 * LANES), F32), pltpu.VMEM((t, t), F32)],
        compiler_params=_cparams("parallel", "parallel", "arbitrary"),
    )(qq, kv, kr)


def _mla_bwd_stats(name, do, o, lse):
    s = do.shape[0]
    tr = min(ROW_TILE, s)

    def body(do_ref, o_ref, lse_ref, stat_ref):
        dl = jnp.sum(do_ref[...].astype(F32) * o_ref[...].astype(F32), axis=1, keepdims=True)
        delta_t = jnp.transpose(jnp.broadcast_to(dl, (tr, LANES)))[0:8]
        lse_t = jnp.transpose(lse_ref[...] * LOG2E)[0:8]
        rows = lax.broadcasted_iota(jnp.int32, (8, tr), 0)
        stat_ref[0] = jnp.where(rows == 0, lse_t, jnp.where(rows == 1, delta_t, 0.0))

    blk = pl.BlockSpec((tr, LANES), lambda h, i: (i, h))
    return pl.pallas_call(
        body, name=name, grid=(MLA_H, s // tr),
        in_specs=[blk, blk, blk],
        out_specs=pl.BlockSpec((1, 8, tr), lambda h, i: (h, 0, i)),
        out_shape=jax.ShapeDtypeStruct((MLA_H, 8, s), F32),
        compiler_params=_cparams("parallel", "parallel"),
    )(do, o, lse)


def _mla_dq_post(name, dq, rc, rsa, rsb):
    s = dq.shape[1]
    tr = min(ROW_TILE, s)

    def body(dq_ref, c_ref, sa_ref, sb_ref, o_ref):
        o_ref[:, 0:LANES] = (MLA_SCALE * dq_ref[0, :, 0:LANES]).astype(BF16)
        o_ref[:, LANES:2 * LANES] = _rope_t(MLA_SCALE * dq_ref[0, :, LANES:2 * LANES], c_ref[...], sa_ref[...],
                                            sb_ref[...], MLA_ROPE // 2).astype(BF16)

    tab = pl.BlockSpec((tr, LANES), lambda h, i: (i, 0))
    return pl.pallas_call(
        body, name=name, grid=(MLA_H, s // tr),
        in_specs=[pl.BlockSpec((1, tr, 2 * LANES), lambda h, i: (h, i, 0)), tab, tab, tab],
        out_specs=pl.BlockSpec((tr, 2 * LANES), lambda h, i: (i, h)),
        out_shape=jax.ShapeDtypeStruct((s, 2 * MLA_H * LANES), BF16),
        compiler_params=_cparams("parallel", "parallel"),
    )(dq, rc, rsa, rsb)


def _mla_attn_bwd(name, qq, kv, kr, do, stats):
    s = qq.shape[0]
    t = _attn_tile(s)
    n = s // t

    def body(q_ref, kv_ref, kr_ref, do_ref, stat_ref, dkv_ref, dkr_ref, dq_hbm, dk_scr, dv_scr, dq_scr, dq_sem):
        h = pl.program_id(0)
        ki = pl.program_id(1)
        qi = pl.program_id(2)

        @pl.when(jnp.logical_and(ki == 0, qi == 0))
        def _():
            dq_scr[...] = jnp.zeros_like(dq_scr)

        @pl.when(qi == 0)
        def _():
            dk_scr[...] = jnp.zeros_like(dk_scr)
            dv_scr[...] = jnp.zeros_like(dv_scr)

        def step(diag):
            q = q_ref[...]
            k = jnp.concatenate([kv_ref[:, 0:LANES], kr_ref[...]], axis=1)
            sc = lax.dot_general(k, q, NT_DIMS, preferred_element_type=F32)
            if diag:
                sc = _causal(sc, transposed=True)
            p = jnp.exp2(MLA_C2 * sc - stat_ref[0, 0:1, :])
            dov = do_ref[...]
            dp = lax.dot_general(kv_ref[:, LANES:2 * LANES], dov, NT_DIMS, preferred_element_type=F32)
            ds = (p * (dp - stat_ref[0, 1:2, :])).astype(BF16)
            dv_scr[...] += jnp.dot(p.astype(BF16), dov, preferred_element_type=F32)
            dk_scr[...] += jnp.dot(ds, q, preferred_element_type=F32)
            rows = pl.ds(pl.multiple_of(qi * t, t), t)
            dq_scr[rows, :] += lax.dot_general(ds, k, TN_DIMS, preferred_element_type=F32)

        @pl.when(qi == ki)
        def _():
            step(True)

        @pl.when(qi > ki)
        def _():
            step(False)

        @pl.when(qi == n - 1)
        def _():
            dkv_ref[:, 0:LANES] = (MLA_SCALE * dk_scr[:, 0:LANES]).astype(BF16)
            dkv_ref[:, LANES:2 * LANES] = dv_scr[...].astype(BF16)
            dkr_ref[0] = MLA_SCALE * dk_scr[:, LANES:2 * LANES]

        @pl.when(jnp.logical_and(ki == n - 1, qi == n - 1))
        def _():
            cp = pltpu.make_async_copy(dq_scr, dq_hbm.at[h], dq_sem)
            cp.start()
            cp.wait()

    qblk = lambda w: pl.BlockSpec((t, w), lambda h, ki, qi: (jnp.maximum(qi, ki), h))
    kblk = pl.BlockSpec((t, 2 * LANES), lambda h, ki, qi: (ki, h))
    return pl.pallas_call(
        body, name=name, grid=(MLA_H, n, n),
        in_specs=[qblk(2 * LANES), kblk, pl.BlockSpec((t, LANES), lambda h, ki, qi: (ki, 0)), qblk(LANES),
                  pl.BlockSpec((1, 8, t), lambda h, ki, qi: (h, 0, jnp.maximum(qi, ki)))],
        out_specs=[kblk, pl.BlockSpec((1, t, LANES), lambda h, ki, qi: (h, ki, 0)),
                   pl.BlockSpec(memory_space=pl.ANY)],
        out_shape=[jax.ShapeDtypeStruct((s, 2 * MLA_H * LANES), BF16),
                   jax.ShapeDtypeStruct((MLA_H, s, LANES), F32),
                   jax.ShapeDtypeStruct((MLA_H, s, 2 * LANES), F32)],
        scratch_shapes=[pltpu.VMEM((t, 2 * LANES), F32), pltpu.VMEM((t, LANES), F32),
                        pltpu.VMEM((s, 2 * LANES), F32), pltpu.SemaphoreType.DMA(())],
        compiler_params=_cparams("arbitrary", "arbitrary", "arbitrary"),
    )(qq, kv, kr, do, stats)


def _rope_groups(acc, o_ref, c, sa, sb, sh, groups):
    for gi in range(acc.shape[1] // LANES):
        blk = acc[:, gi * LANES:(gi + 1) * LANES]
        if gi in groups:
            blk = _rope(blk, c, sa, sb, sh)
        o_ref[:, gi * LANES:(gi + 1) * LANES] = blk.astype(o_ref.dtype)


def _mla_fwd(tag, h, w, tabs):
    s = h.shape[0]
    rc, rsa, rsb = tabs
    lat = _mm(f"{tag}_lat", h, w["w_in"], tm=512)
    qn, kvn, kr = _mla_lat_post(f"{tag}_latpost", lat, w["q_norm"], w["kv_norm"], rc, rsa, rsb)
    tm = min(512, s)

    def q_epi(acc, o_ref, c_ref, sa_ref, sb_ref):
        _rope_groups(acc, o_ref, c_ref[...], sa_ref[...], sb_ref[...], MLA_ROPE // 2, range(1, MLA_H, 2))

    tab = pl.BlockSpec((tm, LANES), lambda i, j: (i, 0))
    qq = _mm(f"{tag}_q", qn, w["w_q"], tm=512, tn=MLA_H * LANES, out_dtype=BF16, epilogue=q_epi,
             extras=(rc, rsa, rsb), extra_specs=(tab, tab, tab))
    kv = _mm(f"{tag}_kv", kvn, w["w_kv"], tm=512, out_dtype=BF16)
    o, lse = _mla_attn_fwd(f"{tag}_attn", qq, kv, kr)
    y = _mm(f"{tag}_o", o, w["w_o"], tm=512)
    return y, dict(h=h, lat=lat, qn=qn, kvn=kvn, kr=kr, qq=qq, kv=kv, o=o, lse=lse)


def _mla_bwd(tag, dy, res, w, tabs):
    rc, rsa, rsb = tabs
    do = _mm(f"{tag}_do", dy, w["w_o_t"], tm=512, out_dtype=BF16)
    g_wo = _mm_tn(f"{tag}_gwo", res["o"], dy)
    stats = _mla_bwd_stats(f"{tag}_stats", do, res["o"], res["lse"])
    dkv, dkr, dq = _mla_attn_bwd(f"{tag}_attnbwd", res["qq"], res["kv"], res["kr"], do, stats)
    dqq = _mla_dq_post(f"{tag}_dqpost", dq, rc, rsa, rsb)
    dqn = _mm(f"{tag}_dqn", dqq, w["w_q_t"], tm=512)
    g_wq = _mm_tn(f"{tag}_gwq", res["qn"], dqq, tn=1024)
    dkvn = _mm(f"{tag}_dkvn", dkv, w["w_kv_t"], tm=512)
    g_wkv = _mm_tn(f"{tag}_gwkv", res["kvn"], dkv, tn=1024)
    dlat, g_qn, g_kvn = _mla_lat_bwd(f"{tag}_latbwd", res["lat"], dqn, dkvn, dkr, w["q_norm"], w["kv_norm"],
                                     rc, rsa, rsb)
    dh = _mm(f"{tag}_dh", dlat, w["w_in_t"], tm=512)
    g_win = _mm_tn(f"{tag}_gwin", res["h"], dlat)
    return dh, dict(w_in=g_win, q_norm=g_qn, w_q=g_wq, kv_norm=g_kvn, w_kv=g_wkv, w_o=g_wo)


SWA_QW = SWA_HQ * SWA_HD
SWA_KW = SWA_HKV * LANES
SWA_NQKV = SWA_QW + 2 * SWA_KW
SWA_SCALE = SWA_HD ** -0.5
SWA_GROUP_ROWS = 4 * SWA_W


def _swa_tile(s):
    return min(512, max(SWA_W, s // 2))


def _swa_masks():
    lane = lax.broadcasted_iota(jnp.int32, (SWA_W, LANES), 1)
    return lane < SWA_HD


def _swa_q4(qa, qb, lo):
    z = jnp.zeros_like(qa)
    return jnp.concatenate([jnp.where(lo, qa, z), jnp.where(lo, z, qa), jnp.where(lo, qb, z), jnp.where(lo, z, qb)],
                           axis=0)


def _swa_probs(q4, kwin, sink_col, first_block):
    sc = lax.dot_general(q4, kwin, NT_DIMS, preferred_element_type=F32) * SWA_SCALE
    row = lax.broadcasted_iota(jnp.int32, sc.shape, 0) % SWA_W
    col = lax.broadcasted_iota(jnp.int32, sc.shape, 1)
    rel = row + SWA_W - col
    ok = (rel >= 0) & (rel < SWA_W) & ((col >= SWA_W) | jnp.logical_not(first_block))
    sc = jnp.where(ok, sc, NEG)
    m = jnp.maximum(jnp.max(sc, axis=1, keepdims=True), sink_col)
    e = jnp.exp(sc - m)
    es = jnp.exp(sink_col - m)
    inv = 1.0 / (jnp.sum(e, axis=1, keepdims=True) + es)
    return e * inv, es * inv


def _sink_col(sinks_ref, grp):
    seg = lax.broadcasted_iota(jnp.int32, (SWA_GROUP_ROWS, 1), 0) // SWA_W
    col = jnp.zeros((SWA_GROUP_ROWS, 1), F32)
    for j in range(4):
        col = jnp.where(seg == j, sinks_ref[0, 4 * grp + j], col)
    return col


def _swa_attn_fwd(name, qkv, sinks):
    s = qkv.shape[0]
    t = _swa_tile(s)
    nb = t // SWA_W

    def body(sinks_ref, q_ref, kv_ref, kvp_ref, o_ref):
        i = pl.program_id(0)
        lo = _swa_masks()
        for grp in range(SWA_HKV):
            sink_col = _sink_col(sinks_ref, grp)
            kcat = jnp.concatenate([kvp_ref[:, grp * LANES:(grp + 1) * LANES],
                                    kv_ref[:, grp * LANES:(grp + 1) * LANES]], axis=0)
            vcat = jnp.concatenate([kvp_ref[:, SWA_KW + grp * LANES:SWA_KW + (grp + 1) * LANES],
                                    kv_ref[:, SWA_KW + grp * LANES:SWA_KW + (grp + 1) * LANES]], axis=0)
            for b in range(nb):
                r0 = b * SWA_W
                qa = q_ref[r0:r0 + SWA_W, grp * 2 * LANES:grp * 2 * LANES + LANES]
                qb = q_ref[r0:r0 + SWA_W, grp * 2 * LANES + LANES:(grp + 1) * 2 * LANES]
                first = jnp.logical_and(i == 0, b == 0)
                p, _ = _swa_probs(_swa_q4(qa, qb, lo), kcat[r0:r0 + 2 * SWA_W], sink_col, first)
                o4 = jnp.dot(p.astype(BF16), vcat[r0:r0 + 2 * SWA_W], preferred_element_type=F32)
                oa = jnp.where(lo, o4[0:SWA_W], o4[SWA_W:2 * SWA_W])
                ob = jnp.where(lo, o4[2 * SWA_W:3 * SWA_W], o4[3 * SWA_W:4 * SWA_W])
                o_ref[r0:r0 + SWA_W, grp * 2 * LANES:grp * 2 * LANES + LANES] = oa.astype(BF16)
                o_ref[r0:r0 + SWA_W, grp * 2 * LANES + LANES:(grp + 1) * 2 * LANES] = ob.astype(BF16)

    return pl.pallas_call(
        body, name=name, grid=(s // t,),
        in_specs=[pl.BlockSpec(memory_space=pltpu.SMEM),
                  pl.BlockSpec((t, SWA_QW), lambda i: (i, 0)),
                  pl.BlockSpec((t, 2 * SWA_KW), lambda i: (i, 1)),
                  pl.BlockSpec((SWA_W, 2 * SWA_KW), lambda i: (jnp.maximum(i * nb - 1, 0), 1))],
        out_specs=pl.BlockSpec((t, SWA_QW), lambda i: (i, 0)),
        out_shape=jax.ShapeDtypeStruct((s, SWA_QW), BF16),
        compiler_params=_cparams("parallel"),
    )(sinks, qkv, qkv, qkv)


def _swa_attn_bwd(name, qkv, sinks, do):
    s = qkv.shape[0]
    t = _swa_tile(s)
    nb = t // SWA_W
    nt = s // t

    def body(sinks_ref, q_ref, kv_ref, kvp_ref, do_ref, dq_ref, dkv_ref, dkvp_ref, dsink_ref, dcat, sink_acc):
        i = pl.program_id(0)
        lo = _swa_masks()

        @pl.when(i == 0)
        def _():
            sink_acc[...] = jnp.zeros_like(sink_acc)

        dcat[...] = jnp.zeros_like(dcat)
        for grp in range(SWA_HKV):
            sink_col = _sink_col(sinks_ref, grp)
            kcat = jnp.concatenate([kvp_ref[:, grp * LANES:(grp + 1) * LANES],
                                    kv_ref[:, grp * LANES:(grp + 1) * LANES]], axis=0)
            vcat = jnp.concatenate([kvp_ref[:, SWA_KW + grp * LANES:SWA_KW + (grp + 1) * LANES],
                                    kv_ref[:, SWA_KW + grp * LANES:SWA_KW + (grp + 1) * LANES]], axis=0)
            for b in range(nb):
                r0 = b * SWA_W
                ca = slice(grp * 2 * LANES, grp * 2 * LANES + LANES)
                cb = slice(grp * 2 * LANES + LANES, (grp + 1) * 2 * LANES)
                q4 = _swa_q4(q_ref[r0:r0 + SWA_W, ca], q_ref[r0:r0 + SWA_W, cb], lo)
                do4 = _swa_q4(do_ref[r0:r0 + SWA_W, ca], do_ref[r0:r0 + SWA_W, cb], lo)
                first = jnp.logical_and(i == 0, b == 0)
                kwin = kcat[r0:r0 + 2 * SWA_W]
                vwin = vcat[r0:r0 + 2 * SWA_W]
                p, ps = _swa_probs(q4, kwin, sink_col, first)
                dp = lax.dot_general(do4, vwin, NT_DIMS, preferred_element_type=F32)
                rowdot = jnp.sum(p * dp, axis=1, keepdims=True)
                ds = (p * (dp - rowdot) * SWA_SCALE).astype(BF16)
                sink_acc[grp] += jnp.broadcast_to(-ps * rowdot, (SWA_GROUP_ROWS, LANES))
                dq4 = jnp.dot(ds, kwin, preferred_element_type=F32)
                dq_ref[r0:r0 + SWA_W, ca] = jnp.where(lo, dq4[0:SWA_W], dq4[SWA_W:2 * SWA_W])
                dq_ref[r0:r0 + SWA_W, cb] = jnp.where(lo, dq4[2 * SWA_W:3 * SWA_W], dq4[3 * SWA_W:4 * SWA_W])
                dk = lax.dot_general(ds, q4, TN_DIMS, preferred_element_type=F32)
                dv = lax.dot_general(p.astype(BF16), do4, TN_DIMS, preferred_element_type=F32)
                dcat[r0:r0 + 2 * SWA_W, grp * LANES:(grp + 1) * LANES] += dk
                dcat[r0:r0 + 2 * SWA_W, SWA_KW + grp * LANES:SWA_KW + (grp + 1) * LANES] += dv
        dkvp_ref[0] = dcat[0:SWA_W]
        dkv_ref[...] = dcat[SWA_W:SWA_W + t]

        @pl.when(i == nt - 1)
        def _():
            for grp in range(SWA_HKV):
                for j in range(4):
                    tot = jnp.sum(sink_acc[grp, j * SWA_W:(j + 1) * SWA_W, 0:1])
                    dsink_ref[4 * grp + j:4 * grp + j + 1, :] = jnp.full((1, LANES), tot, F32)

    return pl.pallas_call(
        body, name=name, grid=(nt,),
        in_specs=[pl.BlockSpec(memory_space=pltpu.SMEM),
                  pl.BlockSpec((t, SWA_QW), lambda i: (i, 0)),
                  pl.BlockSpec((t, 2 * SWA_KW), lambda i: (i, 1)),
                  pl.BlockSpec((SWA_W, 2 * SWA_KW), lambda i: (jnp.maximum(i * nb - 1, 0), 1)),
                  pl.BlockSpec((t, SWA_QW), lambda i: (i, 0))],
        out_specs=[pl.BlockSpec((t, SWA_QW), lambda i: (i, 0)), pl.BlockSpec((t, 2 * SWA_KW), lambda i: (i, 0)),
                   pl.BlockSpec((1, SWA_W, 2 * SWA_KW), lambda i: (i, 0, 0)),
                   pl.BlockSpec((SWA_HQ, LANES), lambda i: (0, 0))],
        out_shape=[jax.ShapeDtypeStruct((s, SWA_QW), F32), jax.ShapeDtypeStruct((s, 2 * SWA_KW), F32),
                   jax.ShapeDtypeStruct((nt, SWA_W, 2 * SWA_KW), F32), jax.ShapeDtypeStruct((SWA_HQ, LANES), F32)],
        scratch_shapes=[pltpu.VMEM((SWA_W + t, 2 * SWA_KW), F32), pltpu.VMEM((SWA_HKV, SWA_GROUP_ROWS, LANES), F32)],
        compiler_params=_cparams("arbitrary"),
    )(sinks, qkv, qkv, qkv, do)


def _swa_dqkv(name, dq, dkv, dkvp, rc, rsa, rsb):
    s = dq.shape[0]
    t = _swa_tile(s)
    nt = s // t
    sh = SWA_ROT // 2

    def body(dq_ref, dkv_ref, dkvn_ref, c_ref, sa_ref, sb_ref, out_ref, bsum_ref, acc):
        i = pl.program_id(0)
        c, sa, sb = c_ref[...], sa_ref[...], sb_ref[...]
        lo = lax.broadcasted_iota(jnp.int32, (t, LANES), 1) < SWA_HD
        rows = lax.broadcasted_iota(jnp.int32, (t, LANES), 0)
        tail = jnp.logical_and(rows >= t - SWA_W, i < nt - 1)

        @pl.when(i == 0)
        def _():
            acc[...] = jnp.zeros_like(acc)

        for gi in range(SWA_QW // LANES):
            blk = _rope_t(dq_ref[:, gi * LANES:(gi + 1) * LANES], c, sa, sb, sh)
            out_ref[:, gi * LANES:(gi + 1) * LANES] = blk.astype(BF16)
            acc[:, gi * LANES:(gi + 1) * LANES] += _rowsum8(blk)
        for gi in range(2 * SWA_KW // LANES):
            cols = slice(gi * LANES, (gi + 1) * LANES)
            nxt = jnp.concatenate([jnp.zeros((t - SWA_W, LANES), F32), dkvn_ref[0, :, cols]], axis=0)
            blk = dkv_ref[:, cols] + jnp.where(tail, nxt, 0.0)
            blk = jnp.where(lo, blk + pltpu.roll(blk, SWA_HD, 1), 0.0)
            if gi < SWA_HKV:
                blk = _rope_t(blk, c, sa, sb, sh)
            out_ref[:, SWA_QW + gi * LANES:SWA_QW + (gi + 1) * LANES] = blk.astype(BF16)
            acc[:, SWA_QW + gi * LANES:SWA_QW + (gi + 1) * LANES] += _rowsum8(blk)

        @pl.when(i == nt - 1)
        def _():
            bsum_ref[...] = jnp.sum(acc[...], axis=0, keepdims=True)

    return pl.pallas_call(
        body, name=name, grid=(nt,),
        in_specs=[pl.BlockSpec((t, SWA_QW), lambda i: (i, 0)), pl.BlockSpec((t, 2 * SWA_KW), lambda i: (i, 0)),
                  pl.BlockSpec((1, SWA_W, 2 * SWA_KW), lambda i: (jnp.minimum(i + 1, nt - 1), 0, 0))]
        + [_row_spec(t, LANES)] * 3,
        out_specs=[pl.BlockSpec((t, SWA_NQKV), lambda i: (i, 0)), pl.BlockSpec((1, SWA_NQKV), lambda i: (0, 0))],
        out_shape=[jax.ShapeDtypeStruct((s, SWA_NQKV), BF16), jax.ShapeDtypeStruct((1, SWA_NQKV), F32)],
        scratch_shapes=[pltpu.VMEM((8, SWA_NQKV), F32)],
        compiler_params=_cparams("arbitrary"),
    )(dq, dkv, dkvp, rc, rsa, rsb)


def _swa_fwd(tag, h, w, tabs):
    s = h.shape[0]
    rc, rsa, rsb = tabs
    tm = min(512, s)
    sh = SWA_ROT // 2

    def qkv_epi(acc, o_ref, b_ref, c_ref, sa_ref, sb_ref):
        acc = acc + b_ref[...]

        @pl.when(pl.program_id(1) == 0)
        def _():
            _rope_groups(acc, o_ref, c_ref[...], sa_ref[...], sb_ref[...], sh, range(SWA_QW // LANES))

        @pl.when(pl.program_id(1) == 1)
        def _():
            _rope_groups(acc, o_ref, c_ref[...], sa_ref[...], sb_ref[...], sh, range(SWA_HKV))

    tab = pl.BlockSpec((tm, LANES), lambda i, j: (i, 0))
    qkv = _mm(f"{tag}_qkv", h, w["w_qkv"], tm=512, tn=SWA_QW, out_dtype=BF16, epilogue=qkv_epi,
              extras=(w["b_qkv"], rc, rsa, rsb),
              extra_specs=(pl.BlockSpec((1, SWA_QW), lambda i, j: (0, j)), tab, tab, tab))
    o = _swa_attn_fwd(f"{tag}_attn", qkv, w["sinks"])

    def o_epi(acc, o_ref, b_ref):
        o_ref[...] = acc + b_ref[...]

    y = _mm(f"{tag}_o", o, w["w_o"], tm=512, epilogue=o_epi, extras=(w["b_o"],),
            extra_specs=(pl.BlockSpec((1, D), lambda i, j: (0, 0)),))
    return y, dict(h=h, qkv=qkv, o=o)


def _swa_bwd(tag, dy, res, w, tabs):
    rc, rsa, rsb = tabs
    do = _mm(f"{tag}_do", dy, w["w_o_t"], tm=512, out_dtype=BF16)
    g_wo = _mm_tn(f"{tag}_gwo", res["o"], dy)
    dq, dkv, dkvp, dsink = _swa_attn_bwd(f"{tag}_attnbwd", res["qkv"], w["sinks"], do)
    dqkv, g_b = _swa_dqkv(f"{tag}_dqkv", dq, dkv, dkvp, rc, rsa, rsb)
    dh = _mm(f"{tag}_dh", dqkv, w["w_qkv_t"], tm=512)
    g_wqkv = _mm_tn(f"{tag}_gwqkv", res["h"], dqkv, tn=1024)
    return dh, dict(w_qkv=g_wqkv, b_qkv=g_b, sinks=dsink, w_o=g_wo)


def _ada_fwd(name, c_all, w_sh, b_sh):
    cols = w_sh.shape[2]
    tn = cols // 3

    def body(c_ref, w_ref, b_ref, o_ref, cond_ref):
        cv = c_ref[...]
        cond = cv * jax.nn.sigmoid(cv)
        cond_ref[...] = cond
        o_ref[0] = jnp.dot(cond, w_ref[0], preferred_element_type=F32, precision=lax.Precision.HIGHEST) + b_ref[0]

    return pl.pallas_call(
        body, name=name, grid=(DEPTH, cols // tn),
        in_specs=[pl.BlockSpec((8, D), lambda l, j: (0, 0)), pl.BlockSpec((1, D, tn), lambda l, j: (l, 0, j)),
                  pl.BlockSpec((1, 1, tn), lambda l, j: (l, 0, j))],
        out_specs=[pl.BlockSpec((1, 8, tn), lambda l, j: (l, 0, j)), pl.BlockSpec((8, D), lambda l, j: (0, 0))],
        out_shape=[jax.ShapeDtypeStruct((DEPTH, 8, cols), F32), jax.ShapeDtypeStruct((8, D), F32)],
        compiler_params=_cparams("arbitrary", "arbitrary"),
    )(c_all, w_sh, b_sh)


def _ada_grad(name, cond_t, dmod_sh):
    cols = dmod_sh.shape[2]
    tn = cols // 3

    def body(ct_ref, dm_ref, o_ref):
        acc = ct_ref[:, 0:1] * dm_ref[0, 0:1, :]
        for b in range(1, 8):
            acc = acc + ct_ref[:, b:b + 1] * dm_ref[0, b:b + 1, :]
        o_ref[0] = acc

    return pl.pallas_call(
        body, name=name, grid=(DEPTH, cols // tn),
        in_specs=[pl.BlockSpec((D, 8), lambda l, j: (0, 0)), pl.BlockSpec((1, 8, tn), lambda l, j: (l, 0, j))],
        out_specs=pl.BlockSpec((1, D, tn), lambda l, j: (l, 0, j)),
        out_shape=jax.ShapeDtypeStruct((DEPTH, D, cols), F32),
        compiler_params=_cparams("parallel", "parallel"),
    )(cond_t, dmod_sh)


def _adamw(name, g, w, m, v):
    r = g.shape[0]
    tr = min(ROW_TILE, r)

    def body(g_ref, w_ref, m_ref, v_ref, d_ref, nm_ref, nv_ref):
        gv = g_ref[...]
        mn = ADAM_B1 * m_ref[...] + (1.0 - ADAM_B1) * gv
        vn = ADAM_B2 * v_ref[...] + (1.0 - ADAM_B2) * (gv * gv)
        m_hat = mn / (1.0 - ADAM_B1 ** ADAM_STEP)
        v_hat = vn / (1.0 - ADAM_B2 ** ADAM_STEP)
        d_ref[...] = -ADAM_LR * (m_hat / (jnp.sqrt(v_hat) + ADAM_EPS) + ADAM_WD * w_ref[...])
        nm_ref[...] = mn
        nv_ref[...] = vn

    spec = _row_spec(tr, PACK_COLS)
    return pl.pallas_call(
        body, name=name, grid=(r // tr,),
        in_specs=[spec] * 4, out_specs=[spec] * 3,
        out_shape=[jax.ShapeDtypeStruct(g.shape, F32)] * 3,
        compiler_params=_cparams("parallel"),
    )(g, w, m, v)


def _to_chips(full, axis):
    shp = full.shape
    a = full.reshape(shp[:axis] + (N_CHIPS, shp[axis] // N_CHIPS) + shp[axis + 1:])
    return jnp.moveaxis(a, axis, 0)


def _from_chips(stacked, axis):
    a = jnp.moveaxis(stacked, 0, axis)
    shp = a.shape
    return a.reshape(shp[:axis] + (shp[axis] * shp[axis + 1],) + shp[axis + 2:])


PIECE_ROW_ALIGN = 16


def _piece_rows(shape):
    n = 1
    for d in shape:
        n *= d
    rows = -(-n // PACK_COLS)
    return -(-rows // PIECE_ROW_ALIGN) * PIECE_ROW_ALIGN


def _as_rows(a, lead):
    head = a.shape[:lead]
    rows = _piece_rows(a.shape[lead:])
    n = 1
    for d in a.shape[lead:]:
        n *= d
    if n == rows * PACK_COLS:
        return a.reshape(head + (rows, PACK_COLS))
    flat = jnp.pad(a.reshape(head + (n,)), [(0, 0)] * lead + [(0, rows * PACK_COLS - n)])
    return flat.reshape(head + (rows, PACK_COLS))


def _pack(parts, lead, rows):
    pieces = [_as_rows(p, lead) for p in parts]
    used = sum(p.shape[lead] for p in pieces)
    head = pieces[0].shape[:lead]
    pieces.append(jnp.zeros(head + (rows - used, PACK_COLS), pieces[0].dtype))
    return jnp.concatenate(pieces, axis=lead)


def _unpack(packed, lead, shapes):
    out, off = [], 0
    head = packed.shape[:lead]
    for shp in shapes:
        rows = _piece_rows(shp)
        n = 1
        for d in shp:
            n *= d
        piece = lax.slice_in_dim(packed, off, off + rows, axis=lead)
        if n != rows * PACK_COLS:
            piece = piece.reshape(head + (rows * PACK_COLS,))[..., :n]
        out.append(piece.reshape(head + tuple(shp)))
        off += rows
    return out


def _pack_rows(shapes):
    rows = sum(_piece_rows(s) for s in shapes)
    return -(-rows // PACK_ROW_ALIGN) * PACK_ROW_ALIGN


def _rope_tables(positions, rot, lanes_per_head):
    half = rot // 2
    inv = ROPE_THETA ** (-jnp.arange(0, rot, 2, dtype=F32) / rot)
    ang = positions.astype(F32)[:, None] * inv
    cos, sin = jnp.cos(ang), jnp.sin(ang)
    s = positions.shape[0]
    rest = lanes_per_head - rot
    fill = 1.0 if lanes_per_head == SWA_HD else 0.0
    c = jnp.concatenate([cos, cos, jnp.full((s, rest), fill, F32)], axis=1)
    sa = jnp.concatenate([-sin, jnp.zeros((s, half + rest), F32)], axis=1)
    sb = jnp.concatenate([jnp.zeros((s, half), F32), sin, jnp.zeros((s, rest), F32)], axis=1)
    reps = LANES // lanes_per_head
    return tuple(jnp.tile(t, (1, reps)) for t in (c, sa, sb))


def _mla_weights(w_in, q_norm, w_q_b, kv_norm, w_kv_b, w_o):
    w_in_p = jnp.pad(w_in, ((0, 0), (0, MLA_LAT - w_in.shape[1])))
    wq = w_q_b.reshape(MLA_QR, MLA_H, MLA_NOPE + MLA_ROPE)
    wq_p = jnp.pad(wq, ((0, 0), (0, 0), (0, 2 * LANES - MLA_NOPE - MLA_ROPE))).reshape(MLA_QR, MLA_H * 2 * LANES)
    return dict(w_in=w_in_p, w_in_t=w_in_p.T, q_norm=q_norm.reshape(1, -1), kv_norm=kv_norm.reshape(1, -1),
                w_q=wq_p, w_q_t=wq_p.T, w_kv=w_kv_b, w_kv_t=w_kv_b.T, w_o=w_o, w_o_t=w_o.T)


def _mla_grads_unpermute(g):
    gq = g["w_q"].reshape(MLA_QR, MLA_H, 2 * LANES)[:, :, :MLA_NOPE + MLA_ROPE]
    return dict(mla_w_in=g["w_in"][:, :MLA_QR + MLA_KVR + MLA_ROPE], mla_q_norm=g["q_norm"][0],
                mla_w_q_b=gq.reshape(MLA_QR, -1), mla_kv_norm=g["kv_norm"][0], mla_w_kv_b=g["w_kv"],
                mla_w_o=g["w_o"])


def _swa_dup(a):
    lead = a.shape[:-1]
    a = a.reshape(lead + (SWA_HKV, SWA_HD))
    return jnp.concatenate([a, a], axis=-1).reshape(lead + (SWA_KW,))


def _swa_undup(a):
    lead = a.shape[:-1]
    return a.reshape(lead + (SWA_HKV, LANES))[..., :SWA_HD].reshape(lead + (SWA_HKV * SWA_HD,))


def _swa_weights(w_qkv, b_qkv, sinks, w_o, b_o):
    nk = SWA_HKV * SWA_HD
    perm = lambda a: jnp.concatenate([a[..., :SWA_QW], _swa_dup(a[..., SWA_QW:SWA_QW + nk]),
                                      _swa_dup(a[..., SWA_QW + nk:])], axis=-1)
    w_p = perm(w_qkv)
    return dict(w_qkv=w_p, w_qkv_t=w_p.T, b_qkv=perm(b_qkv.astype(F32)).reshape(1, -1),
                sinks=sinks.reshape(1, -1), w_o=w_o, w_o_t=w_o.T, b_o=b_o.astype(F32).reshape(1, -1))


def _swa_grads_unpermute(g):
    unperm = lambda a: jnp.concatenate([a[..., :SWA_QW], _swa_undup(a[..., SWA_QW:SWA_QW + SWA_KW]),
                                        _swa_undup(a[..., SWA_QW + SWA_KW:])], axis=-1)
    return dict(swa_w_qkv=unperm(g["w_qkv"]), swa_b_qkv=unperm(g["b_qkv"])[0], swa_sinks=g["sinks"][:, 0],
                swa_w_o=g["w_o"], swa_b_o=g["b_o"])


SMALL_LAYOUT = (("ada_b", 24), ("ln_mix_g", 4), ("ln_mix_b", 4), ("ln_ffn_g", 4), ("ln_ffn_b", 4),
                ("mla_q_norm", 2), ("mla_kv_norm", 2), ("swa_sinks", 1), ("loss", 1))


def _small_pack(vals):
    rows = []
    for name, nrows in SMALL_LAYOUT:
        a = vals[name].reshape(nrows, -1).astype(F32)
        rows.append(jnp.pad(a, ((0, 0), (0, PACK_COLS - a.shape[1]))))
    cat = jnp.concatenate(rows, axis=0)
    return jnp.pad(cat, ((0, SMALL_ROWS - cat.shape[0]), (0, 0)))


def _small_unpack(packed, shapes):
    out, r = {}, 0
    for name, nrows in SMALL_LAYOUT:
        shp = shapes[name]
        n = 1
        for d in shp:
            n *= d
        out[name] = packed[r:r + nrows, :n // nrows].reshape(shp)
        r += nrows
    return out


def kernel(x, c, positions, ada_w, ada_b, ln_mix_g, ln_mix_b, ln_ffn_g, ln_ffn_b, ffn_w_gate, ffn_w_up, ffn_w_down, mla_w_in, mla_q_norm, mla_w_q_b, mla_kv_norm, mla_w_kv_b, mla_w_o, swa_w_qkv, swa_b_qkv, swa_sinks, swa_w_o, swa_b_o, loss_target, m_ada_w, m_ada_b, m_ln_mix_g, m_ln_mix_b, m_ln_ffn_g, m_ln_ffn_b, m_ffn_w_gate, m_ffn_w_up, m_ffn_w_down, m_mla_w_in, m_mla_q_norm, m_mla_w_q_b, m_mla_kv_norm, m_mla_w_kv_b, m_mla_w_o, m_swa_w_qkv, m_swa_b_qkv, m_swa_sinks, m_swa_w_o, m_swa_b_o, v_ada_w, v_ada_b, v_ln_mix_g, v_ln_mix_b, v_ln_ffn_g, v_ln_ffn_b, v_ffn_w_gate, v_ffn_w_up, v_ffn_w_down, v_mla_w_in, v_mla_q_norm, v_mla_w_q_b, v_mla_kv_norm, v_mla_w_kv_b, v_mla_w_o, v_swa_w_qkv, v_swa_b_qkv, v_swa_sinks, v_swa_w_o, v_swa_b_o):
    weights = dict(ada_w=ada_w, ada_b=ada_b, ln_mix_g=ln_mix_g, ln_mix_b=ln_mix_b, ln_ffn_g=ln_ffn_g,
                   ln_ffn_b=ln_ffn_b, ffn_w_gate=ffn_w_gate, ffn_w_up=ffn_w_up, ffn_w_down=ffn_w_down,
                   mla_w_in=mla_w_in, mla_q_norm=mla_q_norm, mla_w_q_b=mla_w_q_b, mla_kv_norm=mla_kv_norm,
                   mla_w_kv_b=mla_w_kv_b, mla_w_o=mla_w_o, swa_w_qkv=swa_w_qkv, swa_b_qkv=swa_b_qkv,
                   swa_sinks=swa_sinks, swa_w_o=swa_w_o, swa_b_o=swa_b_o)
    mom_m = dict(ada_w=m_ada_w, ada_b=m_ada_b, ln_mix_g=m_ln_mix_g, ln_mix_b=m_ln_mix_b, ln_ffn_g=m_ln_ffn_g,
                 ln_ffn_b=m_ln_ffn_b, ffn_w_gate=m_ffn_w_gate, ffn_w_up=m_ffn_w_up, ffn_w_down=m_ffn_w_down,
                 mla_w_in=m_mla_w_in, mla_q_norm=m_mla_q_norm, mla_w_q_b=m_mla_w_q_b, mla_kv_norm=m_mla_kv_norm,
                 mla_w_kv_b=m_mla_w_kv_b, mla_w_o=m_mla_w_o, swa_w_qkv=m_swa_w_qkv, swa_b_qkv=m_swa_b_qkv,
                 swa_sinks=m_swa_sinks, swa_w_o=m_swa_w_o, swa_b_o=m_swa_b_o)
    mom_v = dict(ada_w=v_ada_w, ada_b=v_ada_b, ln_mix_g=v_ln_mix_g, ln_mix_b=v_ln_mix_b, ln_ffn_g=v_ln_ffn_g,
                 ln_ffn_b=v_ln_ffn_b, ffn_w_gate=v_ffn_w_gate, ffn_w_up=v_ffn_w_up, ffn_w_down=v_ffn_w_down,
                 mla_w_in=v_mla_w_in, mla_q_norm=v_mla_q_norm, mla_w_q_b=v_mla_w_q_b, mla_kv_norm=v_mla_kv_norm,
                 mla_w_kv_b=v_mla_w_kv_b, mla_w_o=v_mla_w_o, swa_w_qkv=v_swa_w_qkv, swa_b_qkv=v_swa_b_qkv,
                 swa_sinks=v_swa_sinks, swa_w_o=v_swa_w_o, swa_b_o=v_swa_b_o)
    names = list(weights)
    my_x, my_y, my_c = lax.axis_index("x"), lax.axis_index("y"), lax.axis_index("c")
    chip = 2 * my_x + my_y
    batch_row = 2 * chip + my_c
    xs = x[0]
    target = loss_target[0]
    pos = positions[0]
    s = xs.shape[0]

    shard_shapes = [weights[n].shape for n, _ in SHARDED]
    rows = _pack_rows(shard_shapes)
    half = rows // 2
    wpack = _pack([weights[n].astype(BF16) for n, _ in SHARDED], 0, rows)
    my_half = lax.dynamic_slice_in_dim(wpack, my_c * half, half, axis=0)
    by_chip = _exchange("ag_w_chips", my_half, ("x", "y"), "gather", chunks=8)
    by_core = _exchange("ag_w_cores", by_chip.reshape(N_CHIPS * half, PACK_COLS), ("c",), "gather", chunks=32)
    gathered = jnp.moveaxis(by_core.reshape(2, N_CHIPS, half, PACK_COLS), 0, 1).reshape(N_CHIPS, rows, PACK_COLS)
    full = {}
    for (n, axis), part in zip(SHARDED, _unpack(gathered, 1, shard_shapes)):
        full[n] = _from_chips(part, axis)

    c_rows = jnp.pad(c, ((0, 7), (0, 0)))
    c_all = _exchange("ag_c", c_rows, ("x", "y", "c"), "gather")[:, 0, :]
    ada_cols = ada_w.shape[2]
    ada_b_sh = lax.dynamic_slice_in_dim(ada_b, chip * ada_cols, ada_cols, axis=1).reshape(DEPTH, 1, ada_cols)
    mod_sh, cond_all = _ada_fwd("ada_fwd", c_all, ada_w, ada_b_sh)
    mod_all = _exchange("ag_mod", mod_sh.reshape(DEPTH * 8, ada_cols), ("x", "y"), "gather")
    mod_all = mod_all.reshape(N_CHIPS, DEPTH, 8, ada_cols)
    mod_mine = lax.dynamic_index_in_dim(mod_all, batch_row, axis=2, keepdims=False)
    mod = jnp.moveaxis(mod_mine, 0, 1).reshape(DEPTH, 6, 1, D)

    tabs_a = _rope_tables(pos, MLA_ROPE, LANES)
    tabs_b = _rope_tables(pos, SWA_ROT, SWA_HD)
    vec = lambda a, l: a[l].reshape(1, D)

    mix_w, ffn_w = [], []
    for l in range(DEPTH):
        j = l // 2
        if l % 2 == 0:
            mix_w.append(_mla_weights(full["mla_w_in"][j], mla_q_norm[j], full["mla_w_q_b"][j], mla_kv_norm[j],
                                      full["mla_w_kv_b"][j], full["mla_w_o"][j]))
        else:
            mix_w.append(_swa_weights(full["swa_w_qkv"][j], full["swa_b_qkv"][j], swa_sinks[j], full["swa_w_o"][j],
                                      full["swa_b_o"][j]))
        ffn_w.append(dict(wg=full["ffn_w_gate"][l], wu=full["ffn_w_up"][l], wd=full["ffn_w_down"][l],
                          wg_t=full["ffn_w_gate"][l].T, wu_t=full["ffn_w_up"][l].T, wd_t=full["ffn_w_down"][l].T))

    saved = []
    x_cur = xs
    h = _modulate("mod0", x_cur, mod[0, 1], mod[0, 0])
    for l in range(DEPTH):
        if l % 2 == 0:
            y_mix, res = _mla_fwd(f"mla{l}", h, mix_w[l], tabs_a)
        else:
            y_mix, res = _swa_fwd(f"swa{l}", h, mix_w[l], tabs_b)
        x_mid, h2 = _post_mod(f"post_mix{l}", x_cur, y_mix, mod[l, 2], vec(ln_mix_g, l), vec(ln_mix_b, l),
                              mod[l, 4], mod[l, 3])
        gate, up, y_ffn = _ffn_fwd(f"ffn{l}", h2, ffn_w[l]["wg"], ffn_w[l]["wu"], ffn_w[l]["wd"])
        saved.append(dict(x_in=x_cur, y_mix=y_mix, res=res, x_mid=x_mid, h2=h2, gate=gate, up=up, y_ffn=y_ffn))
        if l < DEPTH - 1:
            x_cur, h = _post_mod(f"post_ffn{l}", x_mid, y_ffn, mod[l, 5], vec(ln_ffn_g, l), vec(ln_ffn_b, l),
                                 mod[l + 1, 1], mod[l + 1, 0])
        else:
            dxn, loss_part = _post_loss("post_loss", x_mid, y_ffn, mod[l, 5], vec(ln_ffn_g, l), vec(ln_ffn_b, l),
                                        target)

    gfull = {n: [None] * weights[n].shape[0] for n, _ in SHARDED}
    gsmall = {n: [None] * weights[n].shape[0] for n in ("ln_mix_g", "ln_mix_b", "ln_ffn_g", "ln_ffn_b",
                                                         "mla_q_norm", "mla_kv_norm", "swa_sinks")}
    dmod = [None] * DEPTH
    for l in reversed(range(DEPTH)):
        sv = saved[l]
        j = l // 2
        dxp, dy, sums_f = _post_bwd(f"post_ffn_bwd{l}", dxn, sv["x_mid"], sv["y_ffn"], mod[l, 5], vec(ln_ffn_g, l))
        dgt, dup, act, dh2 = _ffn_bwd(f"ffn_bwd{l}", dy, sv["gate"], sv["up"], ffn_w[l]["wd_t"], ffn_w[l]["wg_t"],
                                      ffn_w[l]["wu_t"])
        gfull["ffn_w_gate"][l] = _mm_tn(f"ffn_gwg{l}", sv["h2"], dgt, tn=FF // 2)
        gfull["ffn_w_up"][l] = _mm_tn(f"ffn_gwu{l}", sv["h2"], dup, tn=FF // 2)
        gfull["ffn_w_down"][l] = _mm_tn(f"ffn_gwd{l}", act, dy)
        dx_mid, sums_fm = _mod_bwd(f"mod_ffn_bwd{l}", dh2, sv["x_mid"], dxp, mod[l, 4])
        dxp, dy, sums_m = _post_bwd(f"post_mix_bwd{l}", dx_mid, sv["x_in"], sv["y_mix"], mod[l, 2], vec(ln_mix_g, l))
        if l % 2 == 0:
            dh, g = _mla_bwd(f"mla{l}", dy, sv["res"], mix_w[l], tabs_a)
            g = _mla_grads_unpermute(g)
        else:
            dh, g = _swa_bwd(f"swa{l}", dy, sv["res"], mix_w[l], tabs_b)
            g["b_o"] = sums_m[3]
            g = _swa_grads_unpermute(g)
        for n, val in g.items():
            (gfull if n in gfull else gsmall)[n][j] = val
        dxn, sums_mm = _mod_bwd(f"mod_mix_bwd{l}", dh, sv["x_in"], dxp, mod[l, 1])
        gsmall["ln_ffn_g"][l], gsmall["ln_ffn_b"][l] = sums_f[0], sums_f[1]
        gsmall["ln_mix_g"][l], gsmall["ln_mix_b"][l] = sums_m[0], sums_m[1]
        dmod[l] = jnp.stack([sums_mm[1], sums_mm[0], sums_m[2], sums_fm[1], sums_fm[0], sums_f[2]])
    grad_x = dxn[None]

    small_vals = {n: jnp.stack(v) for n, v in gsmall.items()}
    small_vals["ada_b"] = jnp.stack(dmod)
    small_vals["loss"] = loss_part[0, 0:1]
    small_all = _exchange("ag_small", _small_pack(small_vals), ("x", "y", "c"), "gather")
    small_sum = _sum_groups("sum_small", small_all)
    dmod_all = small_all[:, :DEPTH * 6, :].reshape(8, DEPTH, 6 * D)
    dmod_sh = jnp.moveaxis(lax.dynamic_slice_in_dim(dmod_all, chip * ada_cols, ada_cols, axis=2), 0, 1)
    g_ada_w = _ada_grad("ada_grad", cond_all.T, dmod_sh)

    gsend = _pack([_to_chips(jnp.stack(gfull[n]), axis) for n, axis in SHARDED], 1, rows)
    by_half = jnp.moveaxis(gsend.reshape(N_CHIPS, 2, half, PACK_COLS), 1, 0).reshape(2, N_CHIPS * half, PACK_COLS)
    core_parts = _exchange("rs_cores", by_half, ("c",), "a2a", chunks=32)
    core_sum = _sum_groups("rs_sum_cores", core_parts).reshape(N_CHIPS, half, PACK_COLS)
    chip_parts = _exchange("rs_chips", core_sum, ("x", "y"), "a2a", chunks=8)
    chip_sum = _sum_groups("rs_sum_chips", chip_parts)
    g_pack = _exchange("rs_share", chip_sum, ("c",), "gather", chunks=32).reshape(rows, PACK_COLS)

    ada_rows = int(ada_w.size) // PACK_COLS
    small_shapes = {n: weights[n].shape for n, _ in SMALL_LAYOUT if n != "loss"}
    small_shapes["loss"] = (1,)

    def flat_all(src):
        sharded = _pack([src[n] for n, _ in SHARDED], 0, rows)
        small = _small_pack({**{n: src[n] for n in small_shapes if n != "loss"}, "loss": jnp.zeros((1,), F32)})
        return jnp.concatenate([sharded, src["ada_w"].reshape(ada_rows, PACK_COLS), small], axis=0)

    g_flat = jnp.concatenate([g_pack, g_ada_w.reshape(ada_rows, PACK_COLS), small_sum], axis=0)
    delta, new_m, new_v = _adamw("adamw", g_flat, flat_all(weights), flat_all(mom_m), flat_all(mom_v))

    def split_all(flat):
        out = dict(zip([n for n, _ in SHARDED], _unpack(flat[:rows], 0, shard_shapes)))
        out["ada_w"] = flat[rows:rows + ada_rows].reshape(ada_w.shape)
        out.update(_small_unpack(flat[rows + ada_rows:], small_shapes))
        return out

    outs = [split_all(a) for a in (g_flat, delta, new_m, new_v)]
    loss = outs[0]["loss"][0]
    return (loss, grad_x, *[o[n] for o in outs for n in names])
```

```python
import jax
import jax.numpy as jnp
from jax import lax
from jax.experimental import pallas as pl
from jax.experimental.pallas import tpu as pltpu

F32 = jnp.float32
BF16 = jnp.bfloat16

D = 1024
DEPTH = 4
ROPE_THETA = 500000.0
LN_EPS = 1e-5
RMS_EPS = 1e-6
MLA_H = 8
MLA_NOPE = 128
MLA_ROPE = 64
MLA_V = 128
MLA_QR = 384
MLA_KVR = 256
MLA_LAT = 768
SWA_HQ = 16
SWA_HKV = 4
SWA_HD = 64
SWA_W = 128
SWA_ROT = 16
FF = 2816
ALPHA = (2 * DEPTH) ** 0.25
ADAM_LR = 0.001
ADAM_B1 = 0.9
ADAM_B2 = 0.999
ADAM_EPS = 1e-08
ADAM_WD = 0.01
ADAM_STEP = 10
NEG = -1e30
LANES = 128
N_CHIPS = 4
PACK_COLS = 1024
PACK_ROW_ALIGN = 512
SMALL_ROWS = 512
ROW_TILE = 512

SHARDED = (
    ("ffn_w_gate", 2), ("ffn_w_up", 2), ("ffn_w_down", 1), ("mla_w_in", 1), ("mla_w_q_b", 2),
    ("mla_w_kv_b", 2), ("mla_w_o", 1), ("swa_w_qkv", 2), ("swa_b_qkv", 1), ("swa_w_o", 1), ("swa_b_o", 1),
)


def _items(*specs):
    axis = dict(SHARDED)
    return tuple((n, a, b, axis[n]) for names, a, b in specs for n in names)


_FFN = ("ffn_w_gate", "ffn_w_up", "ffn_w_down")
_SWA = ("swa_w_qkv", "swa_b_qkv", "swa_w_o", "swa_b_o")
_MLA_IN = ("mla_w_in", "mla_w_q_b", "mla_w_kv_b")
_MLA_OUT = ("mla_w_o",)
W_EARLY = _items((_MLA_IN, 0, 1))
W_LATE = _items((_FFN, 0, 4), (_MLA_IN, 1, 2), (_MLA_OUT, 0, 2), (_SWA, 0, 2))
G_FIRST = _items((_FFN, 3, 4), (_SWA, 1, 2), (_FFN, 2, 3), (_MLA_OUT, 1, 2))
G_SECOND = _items((_MLA_IN, 1, 2), (_FFN, 1, 2), (_SWA, 0, 1), (_FFN, 0, 1), (_MLA_OUT, 0, 1))
G_LAST = _items((_MLA_IN, 0, 1))


def _cparams(*sem):
    return pltpu.CompilerParams(dimension_semantics=sem)


def _row_spec(tr, cols):
    return pl.BlockSpec((tr, cols), lambda i: (i, 0))


def _vec_spec(cols):
    return pl.BlockSpec((1, cols), lambda i: (0, 0))


def _rope(x, c, sa, sb, sh):
    n = x.shape[1]
    return x * c + pltpu.roll(x, n - sh, 1) * sa + pltpu.roll(x, sh, 1) * sb


def _rope_t(d, c, sa, sb, sh):
    n = d.shape[1]
    return d * c + pltpu.roll(d * sa, sh, 1) + pltpu.roll(d * sb, n - sh, 1)


def _rowsum8(t):
    r, n = t.shape
    return jnp.sum(t.reshape(r // 8, 8, n), axis=0)


class _Exchange:
    def __init__(self, src, axes, mode, chunks=1):
        self.src, self.axes, self.mode, self.chunks = src, axes, mode, chunks
        self.g = 2 ** len(axes)
        self.blk = tuple(src.shape if mode == "gather" else src.shape[1:])
        self.out_shape = jax.ShapeDtypeStruct((self.g,) + self.blk, src.dtype)
        nsem = (self.g - 1) * chunks
        self.scratch = [pltpu.SemaphoreType.DMA((nsem,)), pltpu.SemaphoreType.DMA((nsem,)),
                        pltpu.SemaphoreType.DMA(())]

    def copies(self, src_ref, out_ref, send_sems, recv_sems, loc_sem):
        pos = {a: lax.axis_index(a) for a in ("x", "y", "c")}
        rows = self.blk[0] // self.chunks

        def gidx(p):
            idx = 0
            for a in self.axes:
                idx = idx * 2 + p[a]
            return idx

        def view(p):
            if self.mode == "gather":
                return src_ref
            if self.mode == "to_chip":
                return src_ref.at[2 * p["x"] + p["y"]]
            return src_ref.at[gidx(p)]

        me = gidx(pos)
        out = [pltpu.make_async_copy(view(pos), out_ref.at[me], loc_sem)]
        for k in range(self.chunks):
            piece = pl.ds(k * rows, rows)
            for j in range(1, self.g):
                peer = dict(pos)
                for bit, a in enumerate(reversed(self.axes)):
                    if (j >> bit) & 1:
                        peer[a] = 1 - pos[a]
                sem = (j - 1) * self.chunks + k
                out.append(pltpu.make_async_remote_copy(
                    src_ref=view(peer).at[piece], dst_ref=out_ref.at[me, piece],
                    send_sem=send_sems.at[sem], recv_sem=recv_sems.at[sem],
                    device_id=(peer["x"], peer["y"], peer["c"]), device_id_type=pl.DeviceIdType.MESH))
        return out


def _exchange(name, src, axes, mode, chunks=1):
    ex = _Exchange(src, axes, mode, chunks)

    def body(src_ref, out_ref, send_sems, recv_sems, loc_sem):
        copies = ex.copies(src_ref, out_ref, send_sems, recv_sems, loc_sem)
        for cp in copies:
            cp.start()
        for cp in copies:
            cp.wait()

    return pl.pallas_call(
        body, name=name, out_shape=ex.out_shape,
        in_specs=[pl.BlockSpec(memory_space=pl.ANY)],
        out_specs=pl.BlockSpec(memory_space=pl.ANY),
        scratch_shapes=ex.scratch,
    )(src)


def _call_with_rider(body, ex, first, last, *, name, grid, in_specs, out_specs, out_shape, scratch_shapes, operands):
    n_in, n_out, n_scr = len(in_specs), len(out_specs), len(scratch_shapes)

    def wrapped(*refs):
        ins, src_ref = refs[:n_in], refs[n_in]
        outs, out_ref = refs[n_in + 1:n_in + 1 + n_out], refs[n_in + 1 + n_out]
        scr = refs[n_in + 2 + n_out:n_in + 2 + n_out + n_scr]
        sems = refs[n_in + 2 + n_out + n_scr:]

        @pl.when(first())
        def _():
            for cp in ex.copies(src_ref, out_ref, *sems):
                cp.start()

        body(*ins, *outs, *scr)

        @pl.when(last())
        def _():
            for cp in ex.copies(src_ref, out_ref, *sems):
                cp.wait()

    any_spec = pl.BlockSpec(memory_space=pl.ANY)
    return pl.pallas_call(
        wrapped, name=name, grid=grid,
        in_specs=[*in_specs, any_spec], out_specs=[*out_specs, any_spec],
        out_shape=[*out_shape, ex.out_shape],
        scratch_shapes=[*scratch_shapes, *ex.scratch],
        compiler_params=_cparams(*(["arbitrary"] * len(grid))),
    )(*operands, ex.src)


def _sum_groups(name, a):
    g, r, c = a.shape
    tr = min(ROW_TILE, r)

    def body(a_ref, o_ref):
        acc = a_ref[0].astype(F32)
        for i in range(1, g):
            acc = acc + a_ref[i].astype(F32)
        o_ref[...] = acc

    return pl.pallas_call(
        body, name=name, grid=(r // tr,),
        in_specs=[pl.BlockSpec((g, tr, c), lambda i: (0, i, 0))],
        out_specs=pl.BlockSpec((tr, c), lambda i: (i, 0)),
        out_shape=jax.ShapeDtypeStruct((r, c), F32),
        compiler_params=_cparams("parallel"),
    )(a)


def _mm(name, a, b, *, tm, tn=None, out_dtype=F32, epilogue=None, extras=(), extra_specs=()):
    m, k = a.shape
    n = b.shape[1]
    tn = tn or n
    tm = min(tm, m)

    def body(a_ref, b_ref, *rest):
        o_ref = rest[-1]
        acc = jnp.dot(a_ref[...], b_ref[...], preferred_element_type=F32)
        if epilogue is None:
            o_ref[...] = acc.astype(o_ref.dtype)
        else:
            epilogue(acc, o_ref, *rest[:-1])

    return pl.pallas_call(
        body, name=name, grid=(m // tm, n // tn),
        in_specs=[pl.BlockSpec((tm, k), lambda i, j: (i, 0)), pl.BlockSpec((k, tn), lambda i, j: (0, j)),
                  *extra_specs],
        out_specs=pl.BlockSpec((tm, tn), lambda i, j: (i, j)),
        out_shape=jax.ShapeDtypeStruct((m, n), out_dtype),
        compiler_params=_cparams("parallel", "parallel"),
    )(a, b, *extras)


def _mm_tn(name, a, b, *, tn=None, tk=1024):
    s, m = a.shape
    n = b.shape[1]
    tn = tn or n
    tk = min(tk, s)

    def body(a_ref, b_ref, o_ref):
        part = lax.dot_general(a_ref[...], b_ref[...], (((0,), (0,)), ((), ())), preferred_element_type=F32)

        @pl.when(pl.program_id(1) == 0)
        def _():
            o_ref[...] = part

        @pl.when(pl.program_id(1) > 0)
        def _():
            o_ref[...] += part

    return pl.pallas_call(
        body, name=name, grid=(n // tn, s // tk),
        in_specs=[pl.BlockSpec((tk, m), lambda j, k: (k, 0)), pl.BlockSpec((tk, tn), lambda j, k: (k, j))],
        out_specs=pl.BlockSpec((m, tn), lambda j, k: (0, j)),
        out_shape=jax.ShapeDtypeStruct((m, n), F32),
        compiler_params=_cparams("parallel", "arbitrary"),
    )(a, b)


def _modulate(name, x, sc, sh):
    s = x.shape[0]
    tr = min(ROW_TILE, s)

    def body(x_ref, sc_ref, sh_ref, h_ref):
        h_ref[...] = (x_ref[...] * (1.0 + sc_ref[...]) + sh_ref[...]).astype(BF16)

    return pl.pallas_call(
        body, name=name, grid=(s // tr,),
        in_specs=[_row_spec(tr, D), _vec_spec(D), _vec_spec(D)],
        out_specs=_row_spec(tr, D),
        out_shape=jax.ShapeDtypeStruct((s, D), BF16),
        compiler_params=_cparams("parallel"),
    )(x, sc, sh)


def _ln_stats(z):
    mu = jnp.mean(z, axis=1, keepdims=True)
    zc = z - mu
    var = jnp.mean(zc * zc, axis=1, keepdims=True)
    r = lax.rsqrt(var + LN_EPS)
    return zc * r, r


def _post_mod(name, x, y, g, gamma, beta, sc, sh):
    s = x.shape[0]
    tr = min(ROW_TILE, s)

    def body(x_ref, y_ref, g_ref, ga_ref, be_ref, sc_ref, sh_ref, xn_ref, h_ref):
        zh, _ = _ln_stats(ALPHA * x_ref[...] + g_ref[...] * y_ref[...])
        xn = zh * ga_ref[...] + be_ref[...]
        xn_ref[...] = xn
        h_ref[...] = (xn * (1.0 + sc_ref[...]) + sh_ref[...]).astype(BF16)

    return pl.pallas_call(
        body, name=name, grid=(s // tr,),
        in_specs=[_row_spec(tr, D), _row_spec(tr, D)] + [_vec_spec(D)] * 5,
        out_specs=[_row_spec(tr, D), _row_spec(tr, D)],
        out_shape=[jax.ShapeDtypeStruct((s, D), F32), jax.ShapeDtypeStruct((s, D), BF16)],
        compiler_params=_cparams("parallel"),
    )(x, y, g, gamma, beta, sc, sh)


def _post_loss(name, x, y, g, gamma, beta, target):
    s = x.shape[0]
    tr = min(ROW_TILE, s)
    nt = s // tr

    def body(x_ref, y_ref, g_ref, ga_ref, be_ref, t_ref, dx_ref, loss_ref, acc):
        i = pl.program_id(0)
        zh, _ = _ln_stats(ALPHA * x_ref[...] + g_ref[...] * y_ref[...])
        e = zh * ga_ref[...] + be_ref[...] - t_ref[...]
        dx_ref[...] = e * (1.0 / D)

        @pl.when(i == 0)
        def _():
            acc[...] = jnp.zeros_like(acc)

        acc[...] += _rowsum8(e * e)

        @pl.when(i == nt - 1)
        def _():
            loss_ref[...] = jnp.full(loss_ref.shape, jnp.sum(acc[...]) * (0.5 / D), F32)

    return pl.pallas_call(
        body, name=name, grid=(nt,),
        in_specs=[_row_spec(tr, D), _row_spec(tr, D)] + [_vec_spec(D)] * 3 + [_row_spec(tr, D)],
        out_specs=[_row_spec(tr, D), pl.BlockSpec((8, LANES), lambda i: (0, 0))],
        out_shape=[jax.ShapeDtypeStruct((s, D), F32), jax.ShapeDtypeStruct((8, LANES), F32)],
        scratch_shapes=[pltpu.VMEM((8, D), F32)],
        compiler_params=_cparams("arbitrary"),
    )(x, y, g, gamma, beta, target)


def _post_bwd(name, dxn, x, y, g, gamma):
    s = x.shape[0]
    tr = min(ROW_TILE, s)
    nt = s // tr

    def body(d_ref, x_ref, y_ref, g_ref, ga_ref, dxp_ref, dy_ref, sums_ref, a0, a1, a2, a3):
        i = pl.program_id(0)
        yv = y_ref[...]
        gv = g_ref[...]
        zh, r = _ln_stats(ALPHA * x_ref[...] + gv * yv)
        dxn_v = d_ref[...]
        dzh = dxn_v * ga_ref[...]
        dz = r * (dzh - jnp.mean(dzh, axis=1, keepdims=True) - zh * jnp.mean(dzh * zh, axis=1, keepdims=True))
        dxp_ref[...] = ALPHA * dz
        dyv = gv * dz
        dy_ref[...] = dyv.astype(BF16)

        @pl.when(i == 0)
        def _():
            for a in (a0, a1, a2, a3):
                a[...] = jnp.zeros_like(a)

        a0[...] += _rowsum8(dxn_v * zh)
        a1[...] += _rowsum8(dxn_v)
        a2[...] += _rowsum8(dz * yv)
        a3[...] += _rowsum8(dyv)

        @pl.when(i == nt - 1)
        def _():
            for k, a in enumerate((a0, a1, a2, a3)):
                sums_ref[k:k + 1, :] = jnp.sum(a[...], axis=0, keepdims=True)

    return pl.pallas_call(
        body, name=name, grid=(nt,),
        in_specs=[_row_spec(tr, D)] * 3 + [_vec_spec(D)] * 2,
        out_specs=[_row_spec(tr, D), _row_spec(tr, D), pl.BlockSpec((4, D), lambda i: (0, 0))],
        out_shape=[jax.ShapeDtypeStruct((s, D), F32), jax.ShapeDtypeStruct((s, D), BF16),
                   jax.ShapeDtypeStruct((4, D), F32)],
        scratch_shapes=[pltpu.VMEM((8, D), F32)] * 4,
        compiler_params=_cparams("arbitrary"),
    )(dxn, x, y, g, gamma)


def _mod_bwd(name, dh, x, dxp, sc):
    s = x.shape[0]
    tr = min(ROW_TILE, s)
    nt = s // tr

    def body(dh_ref, x_ref, dxp_ref, sc_ref, dx_ref, sums_ref, a0, a1):
        i = pl.program_id(0)
        dhv = dh_ref[...]
        dx_ref[...] = dxp_ref[...] + dhv * (1.0 + sc_ref[...])

        @pl.when(i == 0)
        def _():
            a0[...] = jnp.zeros_like(a0)
            a1[...] = jnp.zeros_like(a1)

        a0[...] += _rowsum8(dhv * x_ref[...])
        a1[...] += _rowsum8(dhv)

        @pl.when(i == nt - 1)
        def _():
            sums_ref[0:1, :] = jnp.sum(a0[...], axis=0, keepdims=True)
            sums_ref[1:2, :] = jnp.sum(a1[...], axis=0, keepdims=True)

    return pl.pallas_call(
        body, name=name, grid=(nt,),
        in_specs=[_row_spec(tr, D)] * 3 + [_vec_spec(D)],
        out_specs=[_row_spec(tr, D), pl.BlockSpec((2, D), lambda i: (0, 0))],
        out_shape=[jax.ShapeDtypeStruct((s, D), F32), jax.ShapeDtypeStruct((2, D), F32)],
        scratch_shapes=[pltpu.VMEM((8, D), F32)] * 2,
        compiler_params=_cparams("arbitrary"),
    )(dh, x, dxp, sc)


def _ffn_tiles(s):
    return min(ROW_TILE, s), FF // 2


def _ffn_fwd(name, h, wg, wu, wd):
    s = h.shape[0]
    tm, tf = _ffn_tiles(s)

    def body(h_ref, wg_ref, wu_ref, wd_ref, gate_ref, up_ref, y_ref):
        hv = h_ref[...]
        gt = jnp.dot(hv, wg_ref[...], preferred_element_type=F32)
        up = jnp.dot(hv, wu_ref[...], preferred_element_type=F32)
        gate_ref[...] = gt
        up_ref[...] = up
        act = (gt * jax.nn.sigmoid(gt) * up).astype(BF16)
        part = jnp.dot(act, wd_ref[...], preferred_element_type=F32)

        @pl.when(pl.program_id(1) == 0)
        def _():
            y_ref[...] = part

        @pl.when(pl.program_id(1) > 0)
        def _():
            y_ref[...] += part

    return pl.pallas_call(
        body, name=name, grid=(s // tm, FF // tf),
        in_specs=[pl.BlockSpec((tm, D), lambda i, f: (i, 0)), pl.BlockSpec((D, tf), lambda i, f: (0, f)),
                  pl.BlockSpec((D, tf), lambda i, f: (0, f)), pl.BlockSpec((tf, D), lambda i, f: (f, 0))],
        out_specs=[pl.BlockSpec((tm, tf), lambda i, f: (i, f)), pl.BlockSpec((tm, tf), lambda i, f: (i, f)),
                   pl.BlockSpec((tm, D), lambda i, f: (i, 0))],
        out_shape=[jax.ShapeDtypeStruct((s, FF), F32), jax.ShapeDtypeStruct((s, FF), F32),
                   jax.ShapeDtypeStruct((s, D), F32)],
        compiler_params=_cparams("parallel", "arbitrary"),
    )(h, wg, wu, wd)


def _ffn_bwd(name, dy, gate, up, wd_t, wg_t, wu_t):
    s = dy.shape[0]
    tm, tf = _ffn_tiles(s)

    def body(dy_ref, gate_ref, up_ref, wdt_ref, wgt_ref, wut_ref, dg_ref, du_ref, act_ref, dh_ref):
        dact = jnp.dot(dy_ref[...], wdt_ref[...], preferred_element_type=F32)
        gt = gate_ref[...]
        up = up_ref[...]
        sig = jax.nn.sigmoid(gt)
        silu = gt * sig
        dgt = (dact * up * (sig * (1.0 + gt * (1.0 - sig)))).astype(BF16)
        dup = (dact * silu).astype(BF16)
        dg_ref[...] = dgt
        du_ref[...] = dup
        act_ref[...] = (silu * up).astype(BF16)
        part = (jnp.dot(dgt, wgt_ref[...], preferred_element_type=F32)
                + jnp.dot(dup, wut_ref[...], preferred_element_type=F32))

        @pl.when(pl.program_id(1) == 0)
        def _():
            dh_ref[...] = part

        @pl.when(pl.program_id(1) > 0)
        def _():
            dh_ref[...] += part

    tile = pl.BlockSpec((tm, tf), lambda i, f: (i, f))
    return pl.pallas_call(
        body, name=name, grid=(s // tm, FF // tf),
        in_specs=[pl.BlockSpec((tm, D), lambda i, f: (i, 0)), tile, tile,
                  pl.BlockSpec((D, tf), lambda i, f: (0, f)), pl.BlockSpec((tf, D), lambda i, f: (f, 0)),
                  pl.BlockSpec((tf, D), lambda i, f: (f, 0))],
        out_specs=[tile, tile, tile, pl.BlockSpec((tm, D), lambda i, f: (i, 0))],
        out_shape=[jax.ShapeDtypeStruct((s, FF), BF16)] * 3 + [jax.ShapeDtypeStruct((s, D), F32)],
        compiler_params=_cparams("parallel", "arbitrary"),
    )(dy, gate, up, wd_t, wg_t, wu_t)


def _mla_lat_post(name, lat, qw, kvw, rc, rsa, rsb):
    s = lat.shape[0]
    tr = min(ROW_TILE, s)

    def body(lat_ref, qw_ref, kvw_ref, c_ref, sa_ref, sb_ref, qn_ref, kvn_ref, kr_ref):
        ql = lat_ref[:, 0:MLA_QR]
        kl = lat_ref[:, MLA_QR:MLA_QR + MLA_KVR]
        qn_ref[...] = (ql * lax.rsqrt(jnp.mean(ql * ql, axis=1, keepdims=True) + RMS_EPS) * qw_ref[...]).astype(BF16)
        kvn_ref[...] = (kl * lax.rsqrt(jnp.mean(kl * kl, axis=1, keepdims=True) + RMS_EPS) * kvw_ref[...]).astype(BF16)
        kr_ref[...] = _rope(lat_ref[:, MLA_QR + MLA_KVR:MLA_LAT], c_ref[...], sa_ref[...], sb_ref[...],
                            MLA_ROPE // 2).astype(BF16)

    return pl.pallas_call(
        body, name=name, grid=(s // tr,),
        in_specs=[_row_spec(tr, MLA_LAT), _vec_spec(MLA_QR), _vec_spec(MLA_KVR)] + [_row_spec(tr, LANES)] * 3,
        out_specs=[_row_spec(tr, MLA_QR), _row_spec(tr, MLA_KVR), _row_spec(tr, LANES)],
        out_shape=[jax.ShapeDtypeStruct((s, MLA_QR), BF16), jax.ShapeDtypeStruct((s, MLA_KVR), BF16),
                   jax.ShapeDtypeStruct((s, LANES), BF16)],
        compiler_params=_cparams("parallel"),
    )(lat, qw, kvw, rc, rsa, rsb)


def _mla_lat_bwd(name, lat, dqn, dkvn, dkr_heads, qw, kvw, rc, rsa, rsb):
    s = lat.shape[0]
    tr = min(ROW_TILE, s)
    nt = s // tr

    def rms_bwd(x, w, dy):
        r = lax.rsqrt(jnp.mean(x * x, axis=1, keepdims=True) + RMS_EPS)
        xh = x * r
        gdy = dy * w
        return r * (gdy - xh * jnp.mean(gdy * xh, axis=1, keepdims=True)), dy * xh

    def body(lat_ref, dqn_ref, dkvn_ref, dkr_ref, qw_ref, kvw_ref, c_ref, sa_ref, sb_ref,
             dlat_ref, dqw_ref, dkvw_ref, aq, akv):
        i = pl.program_id(0)
        dq, dqw = rms_bwd(lat_ref[:, 0:MLA_QR], qw_ref[...], dqn_ref[...])
        dk, dkw = rms_bwd(lat_ref[:, MLA_QR:MLA_QR + MLA_KVR], kvw_ref[...], dkvn_ref[...])
        dkr = dkr_ref[0]
        for hh in range(1, MLA_H):
            dkr = dkr + dkr_ref[hh]
        dkr = _rope_t(dkr, c_ref[...], sa_ref[...], sb_ref[...], MLA_ROPE // 2)
        dlat_ref[:, 0:MLA_QR] = dq.astype(BF16)
        dlat_ref[:, MLA_QR:MLA_QR + MLA_KVR] = dk.astype(BF16)
        dlat_ref[:, MLA_QR + MLA_KVR:MLA_LAT] = dkr.astype(BF16)

        @pl.when(i == 0)
        def _():
            aq[...] = jnp.zeros_like(aq)
            akv[...] = jnp.zeros_like(akv)

        aq[...] += _rowsum8(dqw)
        akv[...] += _rowsum8(dkw)

        @pl.when(i == nt - 1)
        def _():
            dqw_ref[...] = jnp.sum(aq[...], axis=0, keepdims=True)
            dkvw_ref[...] = jnp.sum(akv[...], axis=0, keepdims=True)

    return pl.pallas_call(
        body, name=name, grid=(nt,),
        in_specs=[_row_spec(tr, MLA_LAT), _row_spec(tr, MLA_QR), _row_spec(tr, MLA_KVR),
                  pl.BlockSpec((MLA_H, tr, LANES), lambda i: (0, i, 0)), _vec_spec(MLA_QR), _vec_spec(MLA_KVR)]
        + [_row_spec(tr, LANES)] * 3,
        out_specs=[_row_spec(tr, MLA_LAT), _vec_spec(MLA_QR), _vec_spec(MLA_KVR)],
        out_shape=[jax.ShapeDtypeStruct((s, MLA_LAT), BF16), jax.ShapeDtypeStruct((1, MLA_QR), F32),
                   jax.ShapeDtypeStruct((1, MLA_KVR), F32)],
        scratch_shapes=[pltpu.VMEM((8, MLA_QR), F32), pltpu.VMEM((8, MLA_KVR), F32)],
        compiler_params=_cparams("arbitrary"),
    )(lat, dqn, dkvn, dkr_heads, qw, kvw, rc, rsa, rsb)


def _attn_tile(s):
    return min(1024, max(LANES, s // 2))


MLA_SCALE = (MLA_NOPE + MLA_ROPE) ** -0.5
LOG2E = 1.4426950408889634
MLA_C2 = MLA_SCALE * LOG2E
NT_DIMS = (((1,), (1,)), ((), ()))
TN_DIMS = (((0,), (0,)), ((), ()))


def _causal(sc, transposed=False):
    row = lax.broadcasted_iota(jnp.int32, sc.shape, 0)
    col = lax.broadcasted_iota(jnp.int32, sc.shape, 1)
    return jnp.where(row <= col if transposed else col <= row, sc, NEG)


def _grid_ends(grid):
    def first():
        ok = pl.program_id(0) == 0
        for d in range(1, len(grid)):
            ok = jnp.logical_and(ok, pl.program_id(d) == 0)
        return ok

    def last():
        ok = pl.program_id(0) == grid[0] - 1
        for d in range(1, len(grid)):
            ok = jnp.logical_and(ok, pl.program_id(d) == grid[d] - 1)
        return ok

    return first, last


def _mla_attn_fwd(name, qq, kv, kr, rider=None):
    s = qq.shape[0]
    t = _attn_tile(s)
    n = s // t

    def body(q_ref, kv_ref, kr_ref, o_ref, lse_ref, m_scr, acc_scr, sc_scr):
        qi = pl.program_id(1)
        ki = pl.program_id(2)

        @pl.when(ki == 0)
        def _():
            m_scr[...] = jnp.full(m_scr.shape, NEG, F32)
            acc_scr[...] = jnp.zeros_like(acc_scr)

        @pl.when(ki <= qi)
        def _():
            k = jnp.concatenate([kv_ref[:, 0:LANES], kr_ref[...]], axis=1)
            sc_scr[...] = lax.dot_general(q_ref[...], k, NT_DIMS, preferred_element_type=F32)

        def step(diag):
            sc = sc_scr[...]
            if diag:
                sc = _causal(sc)
            m_prev = m_scr[...]
            m_next = jnp.maximum(m_prev, jnp.max(sc, axis=1, keepdims=True))
            a = jnp.exp2(MLA_C2 * (m_prev - m_next))
            p = jnp.exp2(MLA_C2 * sc - MLA_C2 * m_next[:, 0:1]).astype(BF16)
            v1 = jnp.concatenate([kv_ref[:, LANES:2 * LANES], jnp.ones((t, LANES), BF16)], axis=1)
            pv = jnp.dot(p, v1, preferred_element_type=F32)
            acc_scr[:, 0:LANES] = a * acc_scr[:, 0:LANES] + pv[:, 0:LANES]
            acc_scr[:, LANES:2 * LANES] = a * acc_scr[:, LANES:2 * LANES] + pv[:, LANES:2 * LANES]
            m_scr[...] = m_next

        @pl.when(ki < qi)
        def _():
            step(False)

        @pl.when(ki == qi)
        def _():
            step(True)

        @pl.when(ki == qi)
        def _():
            l = acc_scr[:, LANES:2 * LANES]
            o_ref[...] = (acc_scr[:, 0:LANES] / l).astype(BF16)
            lse_ref[...] = MLA_SCALE * m_scr[...] + jnp.log(l)

    qblk = lambda w: pl.BlockSpec((t, w), lambda h, qi, ki: (qi, h))
    call = dict(
        name=name, grid=(MLA_H, n, n),
        in_specs=[qblk(2 * LANES), pl.BlockSpec((t, 2 * LANES), lambda h, qi, ki: (jnp.minimum(ki, qi), h)),
                  pl.BlockSpec((t, LANES), lambda h, qi, ki: (jnp.minimum(ki, qi), 0))],
        out_specs=[qblk(LANES), qblk(LANES)],
        out_shape=[jax.ShapeDtypeStruct((s, MLA_H * MLA_V), BF16), jax.ShapeDtypeStruct((s, MLA_H * LANES), F32)],
        scratch_shapes=[pltpu.VMEM((t, LANES), F32), pltpu.VMEM((t, 2 * LANES), F32), pltpu.VMEM((t, t), F32)])
    if rider is None:
        return pl.pallas_call(body, compiler_params=_cparams("parallel", "parallel", "arbitrary"), **call)(qq, kv, kr)
    return _call_with_rider(body, rider, *_grid_ends((MLA_H, n, n)), operands=(qq, kv, kr), **call)


def _mla_bwd_stats(name, do, o, lse):
    s = do.shape[0]
    tr = min(ROW_TILE, s)

    def body(do_ref, o_ref, lse_ref, stat_ref):
        dl = jnp.sum(do_ref[...].astype(F32) * o_ref[...].astype(F32), axis=1, keepdims=True)
        delta_t = jnp.transpose(jnp.broadcast_to(dl, (tr, LANES)))[0:8]
        lse_t = jnp.transpose(lse_ref[...] * LOG2E)[0:8]
        rows = lax.broadcasted_iota(jnp.int32, (8, tr), 0)
        stat_ref[0] = jnp.where(rows == 0, lse_t, jnp.where(rows == 1, delta_t, 0.0))

    blk = pl.BlockSpec((tr, LANES), lambda h, i: (i, h))
    return pl.pallas_call(
        body, name=name, grid=(MLA_H, s // tr),
        in_specs=[blk, blk, blk],
        out_specs=pl.BlockSpec((1, 8, tr), lambda h, i: (h, 0, i)),
        out_shape=jax.ShapeDtypeStruct((MLA_H, 8, s), F32),
        compiler_params=_cparams("parallel", "parallel"),
    )(do, o, lse)


def _mla_dq_post(name, dq, rc, rsa, rsb):
    s = dq.shape[1]
    tr = min(ROW_TILE, s)

    def body(dq_ref, c_ref, sa_ref, sb_ref, o_ref):
        o_ref[:, 0:LANES] = (MLA_SCALE * dq_ref[0, :, 0:LANES]).astype(BF16)
        o_ref[:, LANES:2 * LANES] = _rope_t(MLA_SCALE * dq_ref[0, :, LANES:2 * LANES], c_ref[...], sa_ref[...],
                                            sb_ref[...], MLA_ROPE // 2).astype(BF16)

    tab = pl.BlockSpec((tr, LANES), lambda h, i: (i, 0))
    return pl.pallas_call(
        body, name=name, grid=(MLA_H, s // tr),
        in_specs=[pl.BlockSpec((1, tr, 2 * LANES), lambda h, i: (h, i, 0)), tab, tab, tab],
        out_specs=pl.BlockSpec((tr, 2 * LANES), lambda h, i: (i, h)),
        out_shape=jax.ShapeDtypeStruct((s, 2 * MLA_H * LANES), BF16),
        compiler_params=_cparams("parallel", "parallel"),
    )(dq, rc, rsa, rsb)


def _mla_attn_bwd(name, qq, kv, kr, do, stats, rider=None):
    s = qq.shape[0]
    t = _attn_tile(s)
    n = s // t

    def body(q_ref, kv_ref, kr_ref, do_ref, stat_ref, dkv_ref, dkr_ref, dq_hbm, dk_scr, dv_scr, dq_scr, dq_sem):
        h = pl.program_id(0)
        ki = pl.program_id(1)
        qi = pl.program_id(2)

        @pl.when(jnp.logical_and(ki == 0, qi == 0))
        def _():
            dq_scr[...] = jnp.zeros_like(dq_scr)

        @pl.when(qi == 0)
        def _():
            dk_scr[...] = jnp.zeros_like(dk_scr)
            dv_scr[...] = jnp.zeros_like(dv_scr)

        def step(diag):
            q = q_ref[...]
            k = jnp.concatenate([kv_ref[:, 0:LANES], kr_ref[...]], axis=1)
            sc = lax.dot_general(k, q, NT_DIMS, preferred_element_type=F32)
            if diag:
                sc = _causal(sc, transposed=True)
            p = jnp.exp2(MLA_C2 * sc - stat_ref[0, 0:1, :])
            dov = do_ref[...]
            dp = lax.dot_general(kv_ref[:, LANES:2 * LANES], dov, NT_DIMS, preferred_element_type=F32)
            ds = (p * (dp - stat_ref[0, 1:2, :])).astype(BF16)
            dv_scr[...] += jnp.dot(p.astype(BF16), dov, preferred_element_type=F32)
            dk_scr[...] += jnp.dot(ds, q, preferred_element_type=F32)
            rows = pl.ds(pl.multiple_of(qi * t, t), t)
            dq_scr[rows, :] += lax.dot_general(ds, k, TN_DIMS, preferred_element_type=F32)

        @pl.when(qi == ki)
        def _():
            step(True)

        @pl.when(qi > ki)
        def _():
            step(False)

        @pl.when(qi == n - 1)
        def _():
            dkv_ref[:, 0:LANES] = (MLA_SCALE * dk_scr[:, 0:LANES]).astype(BF16)
            dkv_ref[:, LANES:2 * LANES] = dv_scr[...].astype(BF16)
            dkr_ref[0] = MLA_SCALE * dk_scr[:, LANES:2 * LANES]

        @pl.when(jnp.logical_and(ki == n - 1, qi == n - 1))
        def _():
            cp = pltpu.make_async_copy(dq_scr, dq_hbm.at[h], dq_sem)
            cp.start()
            cp.wait()

    qblk = lambda w: pl.BlockSpec((t, w), lambda h, ki, qi: (jnp.maximum(qi, ki), h))
    kblk = pl.BlockSpec((t, 2 * LANES), lambda h, ki, qi: (ki, h))
    call = dict(
        name=name, grid=(MLA_H, n, n),
        in_specs=[qblk(2 * LANES), kblk, pl.BlockSpec((t, LANES), lambda h, ki, qi: (ki, 0)), qblk(LANES),
                  pl.BlockSpec((1, 8, t), lambda h, ki, qi: (h, 0, jnp.maximum(qi, ki)))],
        out_specs=[kblk, pl.BlockSpec((1, t, LANES), lambda h, ki, qi: (h, ki, 0)),
                   pl.BlockSpec(memory_space=pl.ANY)],
        out_shape=[jax.ShapeDtypeStruct((s, 2 * MLA_H * LANES), BF16),
                   jax.ShapeDtypeStruct((MLA_H, s, LANES), F32),
                   jax.ShapeDtypeStruct((MLA_H, s, 2 * LANES), F32)],
        scratch_shapes=[pltpu.VMEM((t, 2 * LANES), F32), pltpu.VMEM((t, LANES), F32),
                        pltpu.VMEM((s, 2 * LANES), F32), pltpu.SemaphoreType.DMA(())])
    operands = (qq, kv, kr, do, stats)
    if rider is None:
        return pl.pallas_call(body, compiler_params=_cparams("arbitrary", "arbitrary", "arbitrary"), **call)(*operands)
    return _call_with_rider(body, rider, *_grid_ends((MLA_H, n, n)), operands=operands, **call)


def _rope_groups(acc, o_ref, c, sa, sb, sh, groups):
    for gi in range(acc.shape[1] // LANES):
        blk = acc[:, gi * LANES:(gi + 1) * LANES]
        if gi in groups:
            blk = _rope(blk, c, sa, sb, sh)
        o_ref[:, gi * LANES:(gi + 1) * LANES] = blk.astype(o_ref.dtype)


def _mla_fwd(tag, h, w, tabs, rider=None):
    s = h.shape[0]
    rc, rsa, rsb = tabs
    lat = _mm(f"{tag}_lat", h, w["w_in"], tm=512)
    qn, kvn, kr = _mla_lat_post(f"{tag}_latpost", lat, w["q_norm"], w["kv_norm"], rc, rsa, rsb)
    tm = min(512, s)

    def q_epi(acc, o_ref, c_ref, sa_ref, sb_ref):
        _rope_groups(acc, o_ref, c_ref[...], sa_ref[...], sb_ref[...], MLA_ROPE // 2, range(1, MLA_H, 2))

    tab = pl.BlockSpec((tm, LANES), lambda i, j: (i, 0))
    qq = _mm(f"{tag}_q", qn, w["w_q"], tm=512, tn=MLA_H * LANES, out_dtype=BF16, epilogue=q_epi,
             extras=(rc, rsa, rsb), extra_specs=(tab, tab, tab))
    kv = _mm(f"{tag}_kv", kvn, w["w_kv"], tm=512, out_dtype=BF16)
    o, lse, *ridden = _mla_attn_fwd(f"{tag}_attn", qq, kv, kr, rider)
    res = dict(h=h, lat=lat, qn=qn, kvn=kvn, kr=kr, qq=qq, kv=kv, o=o, lse=lse)
    return o, res, (ridden[0] if ridden else None)


def _mla_bwd(tag, dy, res, w, tabs, make_rider=None):
    rc, rsa, rsb = tabs
    do = _mm(f"{tag}_do", dy, w["w_o_t"], tm=512, out_dtype=BF16)
    g_wo = _mm_tn(f"{tag}_gwo", res["o"], dy)
    stats = _mla_bwd_stats(f"{tag}_stats", do, res["o"], res["lse"])
    rider = make_rider(g_wo) if make_rider is not None else None
    dkv, dkr, dq, *ridden = _mla_attn_bwd(f"{tag}_attnbwd", res["qq"], res["kv"], res["kr"], do, stats, rider)
    dqq = _mla_dq_post(f"{tag}_dqpost", dq, rc, rsa, rsb)
    dqn = _mm(f"{tag}_dqn", dqq, w["w_q_t"], tm=512)
    g_wq = _mm_tn(f"{tag}_gwq", res["qn"], dqq, tn=1024)
    dkvn = _mm(f"{tag}_dkvn", dkv, w["w_kv_t"], tm=512)
    g_wkv = _mm_tn(f"{tag}_gwkv", res["kvn"], dkv, tn=1024)
    dlat, g_qn, g_kvn = _mla_lat_bwd(f"{tag}_latbwd", res["lat"], dqn, dkvn, dkr, w["q_norm"], w["kv_norm"],
                                     rc, rsa, rsb)
    dh = _mm(f"{tag}_dh", dlat, w["w_in_t"], tm=512)
    g_win = _mm_tn(f"{tag}_gwin", res["h"], dlat)
    grads = dict(w_in=g_win, q_norm=g_qn, w_q=g_wq, kv_norm=g_kvn, w_kv=g_wkv, w_o=g_wo)
    return dh, grads, (ridden[0] if ridden else None)


SWA_QW = SWA_HQ * SWA_HD
SWA_KW = SWA_HKV * LANES
SWA_NQKV = SWA_QW + 2 * SWA_KW
SWA_SCALE = SWA_HD ** -0.5
SWA_GROUP_ROWS = 4 * SWA_W


def _swa_tile(s):
    return min(512, max(SWA_W, s // 2))


def _swa_masks():
    lane = lax.broadcasted_iota(jnp.int32, (SWA_W, LANES), 1)
    return lane < SWA_HD


def _swa_q4(qa, qb, lo):
    z = jnp.zeros_like(qa)
    return jnp.concatenate([jnp.where(lo, qa, z), jnp.where(lo, z, qa), jnp.where(lo, qb, z), jnp.where(lo, z, qb)],
                           axis=0)


def _swa_probs(q4, kwin, sink_col, first_block):
    sc = lax.dot_general(q4, kwin, NT_DIMS, preferred_element_type=F32) * SWA_SCALE
    row = lax.broadcasted_iota(jnp.int32, sc.shape, 0) % SWA_W
    col = lax.broadcasted_iota(jnp.int32, sc.shape, 1)
    rel = row + SWA_W - col
    ok = (rel >= 0) & (rel < SWA_W) & ((col >= SWA_W) | jnp.logical_not(first_block))
    sc = jnp.where(ok, sc, NEG)
    m = jnp.maximum(jnp.max(sc, axis=1, keepdims=True), sink_col)
    e = jnp.exp(sc - m)
    es = jnp.exp(sink_col - m)
    inv = 1.0 / (jnp.sum(e, axis=1, keepdims=True) + es)
    return e * inv, es * inv


def _sink_col(sinks_ref, grp):
    seg = lax.broadcasted_iota(jnp.int32, (SWA_GROUP_ROWS, 1), 0) // SWA_W
    col = jnp.zeros((SWA_GROUP_ROWS, 1), F32)
    for j in range(4):
        col = jnp.where(seg == j, sinks_ref[0, 4 * grp + j], col)
    return col


def _swa_attn_fwd(name, qkv, sinks):
    s = qkv.shape[0]
    t = _swa_tile(s)
    nb = t // SWA_W

    def body(sinks_ref, q_ref, kv_ref, kvp_ref, o_ref):
        i = pl.program_id(0)
        lo = _swa_masks()
        for grp in range(SWA_HKV):
            sink_col = _sink_col(sinks_ref, grp)
            kcat = jnp.concatenate([kvp_ref[:, grp * LANES:(grp + 1) * LANES],
                                    kv_ref[:, grp * LANES:(grp + 1) * LANES]], axis=0)
            vcat = jnp.concatenate([kvp_ref[:, SWA_KW + grp * LANES:SWA_KW + (grp + 1) * LANES],
                                    kv_ref[:, SWA_KW + grp * LANES:SWA_KW + (grp + 1) * LANES]], axis=0)
            for b in range(nb):
                r0 = b * SWA_W
                qa = q_ref[r0:r0 + SWA_W, grp * 2 * LANES:grp * 2 * LANES + LANES]
                qb = q_ref[r0:r0 + SWA_W, grp * 2 * LANES + LANES:(grp + 1) * 2 * LANES]
                first = jnp.logical_and(i == 0, b == 0)
                p, _ = _swa_probs(_swa_q4(qa, qb, lo), kcat[r0:r0 + 2 * SWA_W], sink_col, first)
                o4 = jnp.dot(p.astype(BF16), vcat[r0:r0 + 2 * SWA_W], preferred_element_type=F32)
                oa = jnp.where(lo, o4[0:SWA_W], o4[SWA_W:2 * SWA_W])
                ob = jnp.where(lo, o4[2 * SWA_W:3 * SWA_W], o4[3 * SWA_W:4 * SWA_W])
                o_ref[r0:r0 + SWA_W, grp * 2 * LANES:grp * 2 * LANES + LANES] = oa.astype(BF16)
                o_ref[r0:r0 + SWA_W, grp * 2 * LANES + LANES:(grp + 1) * 2 * LANES] = ob.astype(BF16)

    return pl.pallas_call(
        body, name=name, grid=(s // t,),
        in_specs=[pl.BlockSpec(memory_space=pltpu.SMEM),
                  pl.BlockSpec((t, SWA_QW), lambda i: (i, 0)),
                  pl.BlockSpec((t, 2 * SWA_KW), lambda i: (i, 1)),
                  pl.BlockSpec((SWA_W, 2 * SWA_KW), lambda i: (jnp.maximum(i * nb - 1, 0), 1))],
        out_specs=pl.BlockSpec((t, SWA_QW), lambda i: (i, 0)),
        out_shape=jax.ShapeDtypeStruct((s, SWA_QW), BF16),
        compiler_params=_cparams("parallel"),
    )(sinks, qkv, qkv, qkv)


def _swa_attn_bwd(name, qkv, sinks, do):
    s = qkv.shape[0]
    t = _swa_tile(s)
    nb = t // SWA_W
    nt = s // t

    def body(sinks_ref, q_ref, kv_ref, kvp_ref, do_ref, dq_ref, dkv_ref, dkvp_ref, dsink_ref, dcat, sink_acc):
        i = pl.program_id(0)
        lo = _swa_masks()

        @pl.when(i == 0)
        def _():
            sink_acc[...] = jnp.zeros_like(sink_acc)

        dcat[...] = jnp.zeros_like(dcat)
        for grp in range(SWA_HKV):
            sink_col = _sink_col(sinks_ref, grp)
            kcat = jnp.concatenate([kvp_ref[:, grp * LANES:(grp + 1) * LANES],
                                    kv_ref[:, grp * LANES:(grp + 1) * LANES]], axis=0)
            vcat = jnp.concatenate([kvp_ref[:, SWA_KW + grp * LANES:SWA_KW + (grp + 1) * LANES],
                                    kv_ref[:, SWA_KW + grp * LANES:SWA_KW + (grp + 1) * LANES]], axis=0)
            for b in range(nb):
                r0 = b * SWA_W
                ca = slice(grp * 2 * LANES, grp * 2 * LANES + LANES)
                cb = slice(grp * 2 * LANES + LANES, (grp + 1) * 2 * LANES)
                q4 = _swa_q4(q_ref[r0:r0 + SWA_W, ca], q_ref[r0:r0 + SWA_W, cb], lo)
                do4 = _swa_q4(do_ref[r0:r0 + SWA_W, ca], do_ref[r0:r0 + SWA_W, cb], lo)
                first = jnp.logical_and(i == 0, b == 0)
                kwin = kcat[r0:r0 + 2 * SWA_W]
                vwin = vcat[r0:r0 + 2 * SWA_W]
                p, ps = _swa_probs(q4, kwin, sink_col, first)
                dp = lax.dot_general(do4, vwin, NT_DIMS, preferred_element_type=F32)
                rowdot = jnp.sum(p * dp, axis=1, keepdims=True)
                ds = (p * (dp - rowdot) * SWA_SCALE).astype(BF16)
                sink_acc[grp] += jnp.broadcast_to(-ps * rowdot, (SWA_GROUP_ROWS, LANES))
                dq4 = jnp.dot(ds, kwin, preferred_element_type=F32)
                dq_ref[r0:r0 + SWA_W, ca] = jnp.where(lo, dq4[0:SWA_W], dq4[SWA_W:2 * SWA_W])
                dq_ref[r0:r0 + SWA_W, cb] = jnp.where(lo, dq4[2 * SWA_W:3 * SWA_W], dq4[3 * SWA_W:4 * SWA_W])
                dk = lax.dot_general(ds, q4, TN_DIMS, preferred_element_type=F32)
                dv = lax.dot_general(p.astype(BF16), do4, TN_DIMS, preferred_element_type=F32)
                dcat[r0:r0 + 2 * SWA_W, grp * LANES:(grp + 1) * LANES] += dk
                dcat[r0:r0 + 2 * SWA_W, SWA_KW + grp * LANES:SWA_KW + (grp + 1) * LANES] += dv
        dkvp_ref[0] = dcat[0:SWA_W]
        dkv_ref[...] = dcat[SWA_W:SWA_W + t]

        @pl.when(i == nt - 1)
        def _():
            for grp in range(SWA_HKV):
                for j in range(4):
                    tot = jnp.sum(sink_acc[grp, j * SWA_W:(j + 1) * SWA_W, 0:1])
                    dsink_ref[4 * grp + j:4 * grp + j + 1, :] = jnp.full((1, LANES), tot, F32)

    return pl.pallas_call(
        body, name=name, grid=(nt,),
        in_specs=[pl.BlockSpec(memory_space=pltpu.SMEM),
                  pl.BlockSpec((t, SWA_QW), lambda i: (i, 0)),
                  pl.BlockSpec((t, 2 * SWA_KW), lambda i: (i, 1)),
                  pl.BlockSpec((SWA_W, 2 * SWA_KW), lambda i: (jnp.maximum(i * nb - 1, 0), 1)),
                  pl.BlockSpec((t, SWA_QW), lambda i: (i, 0))],
        out_specs=[pl.BlockSpec((t, SWA_QW), lambda i: (i, 0)), pl.BlockSpec((t, 2 * SWA_KW), lambda i: (i, 0)),
                   pl.BlockSpec((1, SWA_W, 2 * SWA_KW), lambda i: (i, 0, 0)),
                   pl.BlockSpec((SWA_HQ, LANES), lambda i: (0, 0))],
        out_shape=[jax.ShapeDtypeStruct((s, SWA_QW), F32), jax.ShapeDtypeStruct((s, 2 * SWA_KW), F32),
                   jax.ShapeDtypeStruct((nt, SWA_W, 2 * SWA_KW), F32), jax.ShapeDtypeStruct((SWA_HQ, LANES), F32)],
        scratch_shapes=[pltpu.VMEM((SWA_W + t, 2 * SWA_KW), F32), pltpu.VMEM((SWA_HKV, SWA_GROUP_ROWS, LANES), F32)],
        compiler_params=_cparams("arbitrary"),
    )(sinks, qkv, qkv, qkv, do)


def _swa_dqkv(name, dq, dkv, dkvp, rc, rsa, rsb):
    s = dq.shape[0]
    t = _swa_tile(s)
    nt = s // t
    sh = SWA_ROT // 2

    def body(dq_ref, dkv_ref, dkvn_ref, c_ref, sa_ref, sb_ref, out_ref, bsum_ref, acc):
        i = pl.program_id(0)
        c, sa, sb = c_ref[...], sa_ref[...], sb_ref[...]
        lo = lax.broadcasted_iota(jnp.int32, (t, LANES), 1) < SWA_HD
        rows = lax.broadcasted_iota(jnp.int32, (t, LANES), 0)
        tail = jnp.logical_and(rows >= t - SWA_W, i < nt - 1)

        @pl.when(i == 0)
        def _():
            acc[...] = jnp.zeros_like(acc)

        for gi in range(SWA_QW // LANES):
            blk = _rope_t(dq_ref[:, gi * LANES:(gi + 1) * LANES], c, sa, sb, sh)
            out_ref[:, gi * LANES:(gi + 1) * LANES] = blk.astype(BF16)
            acc[:, gi * LANES:(gi + 1) * LANES] += _rowsum8(blk)
        for gi in range(2 * SWA_KW // LANES):
            cols = slice(gi * LANES, (gi + 1) * LANES)
            nxt = jnp.concatenate([jnp.zeros((t - SWA_W, LANES), F32), dkvn_ref[0, :, cols]], axis=0)
            blk = dkv_ref[:, cols] + jnp.where(tail, nxt, 0.0)
            blk = jnp.where(lo, blk + pltpu.roll(blk, SWA_HD, 1), 0.0)
            if gi < SWA_HKV:
                blk = _rope_t(blk, c, sa, sb, sh)
            out_ref[:, SWA_QW + gi * LANES:SWA_QW + (gi + 1) * LANES] = blk.astype(BF16)
            acc[:, SWA_QW + gi * LANES:SWA_QW + (gi + 1) * LANES] += _rowsum8(blk)

        @pl.when(i == nt - 1)
        def _():
            bsum_ref[...] = jnp.sum(acc[...], axis=0, keepdims=True)

    return pl.pallas_call(
        body, name=name, grid=(nt,),
        in_specs=[pl.BlockSpec((t, SWA_QW), lambda i: (i, 0)), pl.BlockSpec((t, 2 * SWA_KW), lambda i: (i, 0)),
                  pl.BlockSpec((1, SWA_W, 2 * SWA_KW), lambda i: (jnp.minimum(i + 1, nt - 1), 0, 0))]
        + [_row_spec(t, LANES)] * 3,
        out_specs=[pl.BlockSpec((t, SWA_NQKV), lambda i: (i, 0)), pl.BlockSpec((1, SWA_NQKV), lambda i: (0, 0))],
        out_shape=[jax.ShapeDtypeStruct((s, SWA_NQKV), BF16), jax.ShapeDtypeStruct((1, SWA_NQKV), F32)],
        scratch_shapes=[pltpu.VMEM((8, SWA_NQKV), F32)],
        compiler_params=_cparams("arbitrary"),
    )(dq, dkv, dkvp, rc, rsa, rsb)


def _swa_fwd(tag, h, w, tabs):
    s = h.shape[0]
    rc, rsa, rsb = tabs
    tm = min(512, s)
    sh = SWA_ROT // 2

    def qkv_epi(acc, o_ref, b_ref, c_ref, sa_ref, sb_ref):
        acc = acc + b_ref[...]

        @pl.when(pl.program_id(1) == 0)
        def _():
            _rope_groups(acc, o_ref, c_ref[...], sa_ref[...], sb_ref[...], sh, range(SWA_QW // LANES))

        @pl.when(pl.program_id(1) == 1)
        def _():
            _rope_groups(acc, o_ref, c_ref[...], sa_ref[...], sb_ref[...], sh, range(SWA_HKV))

    tab = pl.BlockSpec((tm, LANES), lambda i, j: (i, 0))
    qkv = _mm(f"{tag}_qkv", h, w["w_qkv"], tm=512, tn=SWA_QW, out_dtype=BF16, epilogue=qkv_epi,
              extras=(w["b_qkv"], rc, rsa, rsb),
              extra_specs=(pl.BlockSpec((1, SWA_QW), lambda i, j: (0, j)), tab, tab, tab))
    o = _swa_attn_fwd(f"{tag}_attn", qkv, w["sinks"])

    def o_epi(acc, o_ref, b_ref):
        o_ref[...] = acc + b_ref[...]

    y = _mm(f"{tag}_o", o, w["w_o"], tm=512, epilogue=o_epi, extras=(w["b_o"],),
            extra_specs=(pl.BlockSpec((1, D), lambda i, j: (0, 0)),))
    return y, dict(h=h, qkv=qkv, o=o)


def _swa_bwd(tag, dy, res, w, tabs):
    rc, rsa, rsb = tabs
    do = _mm(f"{tag}_do", dy, w["w_o_t"], tm=512, out_dtype=BF16)
    g_wo = _mm_tn(f"{tag}_gwo", res["o"], dy)
    dq, dkv, dkvp, dsink = _swa_attn_bwd(f"{tag}_attnbwd", res["qkv"], w["sinks"], do)
    dqkv, g_b = _swa_dqkv(f"{tag}_dqkv", dq, dkv, dkvp, rc, rsa, rsb)
    dh = _mm(f"{tag}_dh", dqkv, w["w_qkv_t"], tm=512)
    g_wqkv = _mm_tn(f"{tag}_gwqkv", res["h"], dqkv, tn=1024)
    return dh, dict(w_qkv=g_wqkv, b_qkv=g_b, sinks=dsink, w_o=g_wo)


def _ada_fwd(name, c_all, w_sh, b_sh):
    cols = w_sh.shape[2]
    tn = cols // 3

    def body(c_ref, w_ref, b_ref, o_ref, cond_ref):
        cv = c_ref[...]
        cond = cv * jax.nn.sigmoid(cv)
        cond_ref[...] = cond
        o_ref[0] = jnp.dot(cond, w_ref[0], preferred_element_type=F32, precision=lax.Precision.HIGHEST) + b_ref[0]

    return pl.pallas_call(
        body, name=name, grid=(DEPTH, cols // tn),
        in_specs=[pl.BlockSpec((8, D), lambda l, j: (0, 0)), pl.BlockSpec((1, D, tn), lambda l, j: (l, 0, j)),
                  pl.BlockSpec((1, 1, tn), lambda l, j: (l, 0, j))],
        out_specs=[pl.BlockSpec((1, 8, tn), lambda l, j: (l, 0, j)), pl.BlockSpec((8, D), lambda l, j: (0, 0))],
        out_shape=[jax.ShapeDtypeStruct((DEPTH, 8, cols), F32), jax.ShapeDtypeStruct((8, D), F32)],
        compiler_params=_cparams("arbitrary", "arbitrary"),
    )(c_all, w_sh, b_sh)


def _ada_grad(name, cond_t, dmod_sh):
    cols = dmod_sh.shape[2]
    tn = cols // 3

    def body(ct_ref, dm_ref, o_ref):
        acc = ct_ref[:, 0:1] * dm_ref[0, 0:1, :]
        for b in range(1, 8):
            acc = acc + ct_ref[:, b:b + 1] * dm_ref[0, b:b + 1, :]
        o_ref[0] = acc

    return pl.pallas_call(
        body, name=name, grid=(DEPTH, cols // tn),
        in_specs=[pl.BlockSpec((D, 8), lambda l, j: (0, 0)), pl.BlockSpec((1, 8, tn), lambda l, j: (l, 0, j))],
        out_specs=pl.BlockSpec((1, D, tn), lambda l, j: (l, 0, j)),
        out_shape=jax.ShapeDtypeStruct((DEPTH, D, cols), F32),
        compiler_params=_cparams("parallel", "parallel"),
    )(cond_t, dmod_sh)


def _adamw(name, g, w, m, v):
    r = g.shape[0]
    tr = min(ROW_TILE, r)

    def body(g_ref, w_ref, m_ref, v_ref, d_ref, nm_ref, nv_ref):
        gv = g_ref[...]
        mn = ADAM_B1 * m_ref[...] + (1.0 - ADAM_B1) * gv
        vn = ADAM_B2 * v_ref[...] + (1.0 - ADAM_B2) * (gv * gv)
        m_hat = mn / (1.0 - ADAM_B1 ** ADAM_STEP)
        v_hat = vn / (1.0 - ADAM_B2 ** ADAM_STEP)
        d_ref[...] = -ADAM_LR * (m_hat / (jnp.sqrt(v_hat) + ADAM_EPS) + ADAM_WD * w_ref[...])
        nm_ref[...] = mn
        nv_ref[...] = vn

    spec = _row_spec(tr, PACK_COLS)
    return pl.pallas_call(
        body, name=name, grid=(r // tr,),
        in_specs=[spec] * 4, out_specs=[spec] * 3,
        out_shape=[jax.ShapeDtypeStruct(g.shape, F32)] * 3,
        compiler_params=_cparams("parallel"),
    )(g, w, m, v)


def _to_chips(full, axis):
    shp = full.shape
    a = full.reshape(shp[:axis] + (N_CHIPS, shp[axis] // N_CHIPS) + shp[axis + 1:])
    return jnp.moveaxis(a, axis, 0)


def _from_chips(stacked, axis):
    a = jnp.moveaxis(stacked, 0, axis)
    shp = a.shape
    return a.reshape(shp[:axis] + (shp[axis] * shp[axis + 1],) + shp[axis + 2:])


PIECE_ROW_ALIGN = 16


def _piece_rows(shape):
    n = 1
    for d in shape:
        n *= d
    rows = -(-n // PACK_COLS)
    return -(-rows // PIECE_ROW_ALIGN) * PIECE_ROW_ALIGN


def _as_rows(a, lead):
    head = a.shape[:lead]
    rows = _piece_rows(a.shape[lead:])
    n = 1
    for d in a.shape[lead:]:
        n *= d
    if n == rows * PACK_COLS:
        return a.reshape(head + (rows, PACK_COLS))
    flat = jnp.pad(a.reshape(head + (n,)), [(0, 0)] * lead + [(0, rows * PACK_COLS - n)])
    return flat.reshape(head + (rows, PACK_COLS))


def _pack(parts, lead, rows):
    pieces = [_as_rows(p, lead) for p in parts]
    used = sum(p.shape[lead] for p in pieces)
    head = pieces[0].shape[:lead]
    pieces.append(jnp.zeros(head + (rows - used, PACK_COLS), pieces[0].dtype))
    return jnp.concatenate(pieces, axis=lead)


def _unpack(packed, lead, shapes):
    out, off = [], 0
    head = packed.shape[:lead]
    for shp in shapes:
        rows = _piece_rows(shp)
        n = 1
        for d in shp:
            n *= d
        piece = lax.slice_in_dim(packed, off, off + rows, axis=lead)
        if n != rows * PACK_COLS:
            piece = piece.reshape(head + (rows * PACK_COLS,))[..., :n]
        out.append(piece.reshape(head + tuple(shp)))
        off += rows
    return out


def _pack_rows(shapes):
    rows = sum(_piece_rows(s) for s in shapes)
    return -(-rows // PACK_ROW_ALIGN) * PACK_ROW_ALIGN


def _rope_tables(positions, rot, lanes_per_head):
    half = rot // 2
    inv = ROPE_THETA ** (-jnp.arange(0, rot, 2, dtype=F32) / rot)
    ang = positions.astype(F32)[:, None] * inv
    cos, sin = jnp.cos(ang), jnp.sin(ang)
    s = positions.shape[0]
    rest = lanes_per_head - rot
    fill = 1.0 if lanes_per_head == SWA_HD else 0.0
    c = jnp.concatenate([cos, cos, jnp.full((s, rest), fill, F32)], axis=1)
    sa = jnp.concatenate([-sin, jnp.zeros((s, half + rest), F32)], axis=1)
    sb = jnp.concatenate([jnp.zeros((s, half), F32), sin, jnp.zeros((s, rest), F32)], axis=1)
    reps = LANES // lanes_per_head
    return tuple(jnp.tile(t, (1, reps)) for t in (c, sa, sb))


def _mla_weights(w_in, q_norm, w_q_b, kv_norm, w_kv_b):
    w_in_p = jnp.pad(w_in, ((0, 0), (0, MLA_LAT - w_in.shape[1])))
    wq = w_q_b.reshape(MLA_QR, MLA_H, MLA_NOPE + MLA_ROPE)
    wq_p = jnp.pad(wq, ((0, 0), (0, 0), (0, 2 * LANES - MLA_NOPE - MLA_ROPE))).reshape(MLA_QR, MLA_H * 2 * LANES)
    return dict(w_in=w_in_p, w_in_t=w_in_p.T, q_norm=q_norm.reshape(1, -1), kv_norm=kv_norm.reshape(1, -1),
                w_q=wq_p, w_q_t=wq_p.T, w_kv=w_kv_b, w_kv_t=w_kv_b.T)


def _mla_grads_unpermute(g):
    gq = g["w_q"].reshape(MLA_QR, MLA_H, 2 * LANES)[:, :, :MLA_NOPE + MLA_ROPE]
    return dict(mla_w_in=g["w_in"][:, :MLA_QR + MLA_KVR + MLA_ROPE], mla_q_norm=g["q_norm"][0],
                mla_w_q_b=gq.reshape(MLA_QR, -1), mla_kv_norm=g["kv_norm"][0], mla_w_kv_b=g["w_kv"],
                mla_w_o=g["w_o"])


def _swa_dup(a):
    lead = a.shape[:-1]
    a = a.reshape(lead + (SWA_HKV, SWA_HD))
    return jnp.concatenate([a, a], axis=-1).reshape(lead + (SWA_KW,))


def _swa_undup(a):
    lead = a.shape[:-1]
    return a.reshape(lead + (SWA_HKV, LANES))[..., :SWA_HD].reshape(lead + (SWA_HKV * SWA_HD,))


def _swa_weights(w_qkv, b_qkv, sinks, w_o, b_o):
    nk = SWA_HKV * SWA_HD
    perm = lambda a: jnp.concatenate([a[..., :SWA_QW], _swa_dup(a[..., SWA_QW:SWA_QW + nk]),
                                      _swa_dup(a[..., SWA_QW + nk:])], axis=-1)
    w_p = perm(w_qkv)
    return dict(w_qkv=w_p, w_qkv_t=w_p.T, b_qkv=perm(b_qkv.astype(F32)).reshape(1, -1),
                sinks=sinks.reshape(1, -1), w_o=w_o, w_o_t=w_o.T, b_o=b_o.astype(F32).reshape(1, -1))


def _swa_grads_unpermute(g):
    unperm = lambda a: jnp.concatenate([a[..., :SWA_QW], _swa_undup(a[..., SWA_QW:SWA_QW + SWA_KW]),
                                        _swa_undup(a[..., SWA_QW + SWA_KW:])], axis=-1)
    return dict(swa_w_qkv=unperm(g["w_qkv"]), swa_b_qkv=unperm(g["b_qkv"])[0], swa_sinks=g["sinks"][:, 0],
                swa_w_o=g["w_o"], swa_b_o=g["b_o"])


SMALL_LAYOUT = (("ada_b", 24), ("ln_mix_g", 4), ("ln_mix_b", 4), ("ln_ffn_g", 4), ("ln_ffn_b", 4),
                ("mla_q_norm", 2), ("mla_kv_norm", 2), ("swa_sinks", 1), ("loss", 1))


def _small_pack(vals):
    rows = []
    for name, nrows in SMALL_LAYOUT:
        a = vals[name].reshape(nrows, -1).astype(F32)
        rows.append(jnp.pad(a, ((0, 0), (0, PACK_COLS - a.shape[1]))))
    cat = jnp.concatenate(rows, axis=0)
    return jnp.pad(cat, ((0, SMALL_ROWS - cat.shape[0]), (0, 0)))


def _small_unpack(packed, shapes):
    out, r = {}, 0
    for name, nrows in SMALL_LAYOUT:
        shp = shapes[name]
        n = 1
        for d in shp:
            n *= d
        out[name] = packed[r:r + nrows, :n // nrows].reshape(shp)
        r += nrows
    return out


def kernel(x, c, positions, ada_w, ada_b, ln_mix_g, ln_mix_b, ln_ffn_g, ln_ffn_b, ffn_w_gate, ffn_w_up, ffn_w_down, mla_w_in, mla_q_norm, mla_w_q_b, mla_kv_norm, mla_w_kv_b, mla_w_o, swa_w_qkv, swa_b_qkv, swa_sinks, swa_w_o, swa_b_o, loss_target, m_ada_w, m_ada_b, m_ln_mix_g, m_ln_mix_b, m_ln_ffn_g, m_ln_ffn_b, m_ffn_w_gate, m_ffn_w_up, m_ffn_w_down, m_mla_w_in, m_mla_q_norm, m_mla_w_q_b, m_mla_kv_norm, m_mla_w_kv_b, m_mla_w_o, m_swa_w_qkv, m_swa_b_qkv, m_swa_sinks, m_swa_w_o, m_swa_b_o, v_ada_w, v_ada_b, v_ln_mix_g, v_ln_mix_b, v_ln_ffn_g, v_ln_ffn_b, v_ffn_w_gate, v_ffn_w_up, v_ffn_w_down, v_mla_w_in, v_mla_q_norm, v_mla_w_q_b, v_mla_kv_norm, v_mla_w_kv_b, v_mla_w_o, v_swa_w_qkv, v_swa_b_qkv, v_swa_sinks, v_swa_w_o, v_swa_b_o):
    weights = dict(ada_w=ada_w, ada_b=ada_b, ln_mix_g=ln_mix_g, ln_mix_b=ln_mix_b, ln_ffn_g=ln_ffn_g,
                   ln_ffn_b=ln_ffn_b, ffn_w_gate=ffn_w_gate, ffn_w_up=ffn_w_up, ffn_w_down=ffn_w_down,
                   mla_w_in=mla_w_in, mla_q_norm=mla_q_norm, mla_w_q_b=mla_w_q_b, mla_kv_norm=mla_kv_norm,
                   mla_w_kv_b=mla_w_kv_b, mla_w_o=mla_w_o, swa_w_qkv=swa_w_qkv, swa_b_qkv=swa_b_qkv,
                   swa_sinks=swa_sinks, swa_w_o=swa_w_o, swa_b_o=swa_b_o)
    mom_m = dict(ada_w=m_ada_w, ada_b=m_ada_b, ln_mix_g=m_ln_mix_g, ln_mix_b=m_ln_mix_b, ln_ffn_g=m_ln_ffn_g,
                 ln_ffn_b=m_ln_ffn_b, ffn_w_gate=m_ffn_w_gate, ffn_w_up=m_ffn_w_up, ffn_w_down=m_ffn_w_down,
                 mla_w_in=m_mla_w_in, mla_q_norm=m_mla_q_norm, mla_w_q_b=m_mla_w_q_b, mla_kv_norm=m_mla_kv_norm,
                 mla_w_kv_b=m_mla_w_kv_b, mla_w_o=m_mla_w_o, swa_w_qkv=m_swa_w_qkv, swa_b_qkv=m_swa_b_qkv,
                 swa_sinks=m_swa_sinks, swa_w_o=m_swa_w_o, swa_b_o=m_swa_b_o)
    mom_v = dict(ada_w=v_ada_w, ada_b=v_ada_b, ln_mix_g=v_ln_mix_g, ln_mix_b=v_ln_mix_b, ln_ffn_g=v_ln_ffn_g,
                 ln_ffn_b=v_ln_ffn_b, ffn_w_gate=v_ffn_w_gate, ffn_w_up=v_ffn_w_up, ffn_w_down=v_ffn_w_down,
                 mla_w_in=v_mla_w_in, mla_q_norm=v_mla_q_norm, mla_w_q_b=v_mla_w_q_b, mla_kv_norm=v_mla_kv_norm,
                 mla_w_kv_b=v_mla_w_kv_b, mla_w_o=v_mla_w_o, swa_w_qkv=v_swa_w_qkv, swa_b_qkv=v_swa_b_qkv,
                 swa_sinks=v_swa_sinks, swa_w_o=v_swa_w_o, swa_b_o=v_swa_b_o)
    names = list(weights)
    my_x, my_y, my_c = lax.axis_index("x"), lax.axis_index("y"), lax.axis_index("c")
    chip = 2 * my_x + my_y
    batch_row = 2 * chip + my_c
    xs = x[0]
    target = loss_target[0]
    pos = positions[0]
    s = xs.shape[0]

    def item_shapes(items):
        return [(b - a,) + tuple(weights[n].shape[1:]) for n, a, b, _ in items]

    def pack_items(src, items, dtype):
        return _pack([src[n][a:b].astype(dtype) for n, a, b, _ in items], 0, _pack_rows(item_shapes(items)))

    full = {}

    def unpack_gathered(gathered, items):
        for (n, a, b, axis), part in zip(items, _unpack(gathered, 1, item_shapes(items))):
            whole = _from_chips(part, axis)
            for l in range(a, b):
                full[n, l] = whole[l - a]

    early = _exchange("ag_w_early", pack_items(weights, W_EARLY, BF16), ("x", "y"), "gather")
    unpack_gathered(early, W_EARLY)
    late_ride = _Exchange(pack_items(weights, W_LATE, BF16), ("x", "y"), "gather", chunks=8)

    c_rows = jnp.pad(c, ((0, 7), (0, 0)))
    c_all = _exchange("ag_c", c_rows, ("x", "y", "c"), "gather")[:, 0, :]
    ada_cols = ada_w.shape[2]
    ada_b_sh = lax.dynamic_slice_in_dim(ada_b, chip * ada_cols, ada_cols, axis=1).reshape(DEPTH, 1, ada_cols)
    mod_sh, cond_all = _ada_fwd("ada_fwd", c_all, ada_w, ada_b_sh)
    mod_all = _exchange("ag_mod", mod_sh.reshape(DEPTH * 8, ada_cols), ("x", "y"), "gather")
    mod_all = mod_all.reshape(N_CHIPS, DEPTH, 8, ada_cols)
    mod_mine = lax.dynamic_index_in_dim(mod_all, batch_row, axis=2, keepdims=False)
    mod = jnp.moveaxis(mod_mine, 0, 1).reshape(DEPTH, 6, 1, D)

    tabs_a = _rope_tables(pos, MLA_ROPE, LANES)
    tabs_b = _rope_tables(pos, SWA_ROT, SWA_HD)
    vec = lambda a, l: a[l].reshape(1, D)

    def mla_in_weights(j):
        return _mla_weights(full["mla_w_in", j], mla_q_norm[j], full["mla_w_q_b", j], mla_kv_norm[j],
                            full["mla_w_kv_b", j])

    mix_w, ffn_w = {}, {}
    saved = []
    x_cur = xs
    h = _modulate("mod0", x_cur, mod[0, 1], mod[0, 0])
    for l in range(DEPTH):
        j = l // 2
        if l % 2 == 0:
            mix_w[l] = mla_in_weights(j)
            o, res, ridden = _mla_fwd(f"mla{l}", h, mix_w[l], tabs_a, late_ride if l == 0 else None)
            if l == 0:
                unpack_gathered(ridden, W_LATE)
            mix_w[l].update(w_o=full["mla_w_o", j], w_o_t=full["mla_w_o", j].T)
            y_mix = _mm(f"mla{l}_o", o, mix_w[l]["w_o"], tm=512)
        else:
            mix_w[l] = _swa_weights(full["swa_w_qkv", j], full["swa_b_qkv", j], swa_sinks[j], full["swa_w_o", j],
                                    full["swa_b_o", j])
            y_mix, res = _swa_fwd(f"swa{l}", h, mix_w[l], tabs_b)
        wg, wu, wd = full["ffn_w_gate", l], full["ffn_w_up", l], full["ffn_w_down", l]
        ffn_w[l] = dict(wg=wg, wu=wu, wd=wd, wg_t=wg.T, wu_t=wu.T, wd_t=wd.T)
        x_mid, h2 = _post_mod(f"post_mix{l}", x_cur, y_mix, mod[l, 2], vec(ln_mix_g, l), vec(ln_mix_b, l),
                              mod[l, 4], mod[l, 3])
        gate, up, y_ffn = _ffn_fwd(f"ffn{l}", h2, ffn_w[l]["wg"], ffn_w[l]["wu"], ffn_w[l]["wd"])
        saved.append(dict(x_in=x_cur, y_mix=y_mix, res=res, x_mid=x_mid, h2=h2, gate=gate, up=up, y_ffn=y_ffn))
        if l < DEPTH - 1:
            x_cur, h = _post_mod(f"post_ffn{l}", x_mid, y_ffn, mod[l, 5], vec(ln_ffn_g, l), vec(ln_ffn_b, l),
                                 mod[l + 1, 1], mod[l + 1, 0])
        else:
            dxn, loss_part = _post_loss("post_loss", x_mid, y_ffn, mod[l, 5], vec(ln_ffn_g, l), vec(ln_ffn_b, l),
                                        target)

    gfull = {n: [None] * weights[n].shape[0] for n, _ in SHARDED}
    gsmall = {n: [None] * weights[n].shape[0] for n in ("ln_mix_g", "ln_mix_b", "ln_ffn_g", "ln_ffn_b",
                                                         "mla_q_norm", "mla_kv_norm", "swa_sinks")}
    dmod = [None] * DEPTH

    def grad_ride(items):
        parts = [_to_chips(jnp.stack(gfull[n][a:b]).astype(BF16), axis) for n, a, b, axis in items]
        return _Exchange(_pack(parts, 1, _pack_rows(item_shapes(items))), ("x", "y", "c"), "to_chip", chunks=4)

    def ride_with_wo(items, j):
        def make(g_wo):
            gfull["mla_w_o"][j] = g_wo
            return grad_ride(items)
        return make

    rides = {DEPTH - 2: G_FIRST, 0: G_SECOND}
    g_parts = {}
    for l in reversed(range(DEPTH)):
        sv = saved[l]
        j = l // 2
        dxp, dy, sums_f = _post_bwd(f"post_ffn_bwd{l}", dxn, sv["x_mid"], sv["y_ffn"], mod[l, 5], vec(ln_ffn_g, l))
        dgt, dup, act, dh2 = _ffn_bwd(f"ffn_bwd{l}", dy, sv["gate"], sv["up"], ffn_w[l]["wd_t"], ffn_w[l]["wg_t"],
                                      ffn_w[l]["wu_t"])
        gfull["ffn_w_gate"][l] = _mm_tn(f"ffn_gwg{l}", sv["h2"], dgt, tn=FF // 2)
        gfull["ffn_w_up"][l] = _mm_tn(f"ffn_gwu{l}", sv["h2"], dup, tn=FF // 2)
        gfull["ffn_w_down"][l] = _mm_tn(f"ffn_gwd{l}", act, dy)
        dx_mid, sums_fm = _mod_bwd(f"mod_ffn_bwd{l}", dh2, sv["x_mid"], dxp, mod[l, 4])
        dxp, dy, sums_m = _post_bwd(f"post_mix_bwd{l}", dx_mid, sv["x_in"], sv["y_mix"], mod[l, 2], vec(ln_mix_g, l))
        if l % 2 == 0:
            dh, g, ridden = _mla_bwd(f"mla{l}", dy, sv["res"], mix_w[l], tabs_a, ride_with_wo(rides[l], j))
            g_parts[rides[l]] = _sum_groups(f"rs_sum{l}", ridden)
            g = _mla_grads_unpermute(g)
        else:
            dh, g = _swa_bwd(f"swa{l}", dy, sv["res"], mix_w[l], tabs_b)
            g["b_o"] = sums_m[3]
            g = _swa_grads_unpermute(g)
        for n, val in g.items():
            (gfull if n in gfull else gsmall)[n][j] = val
        dxn, sums_mm = _mod_bwd(f"mod_mix_bwd{l}", dh, sv["x_in"], dxp, mod[l, 1])
        gsmall["ln_ffn_g"][l], gsmall["ln_ffn_b"][l] = sums_f[0], sums_f[1]
        gsmall["ln_mix_g"][l], gsmall["ln_mix_b"][l] = sums_m[0], sums_m[1]
        dmod[l] = jnp.stack([sums_mm[1], sums_mm[0], sums_m[2], sums_fm[1], sums_fm[0], sums_f[2]])
    grad_x = dxn[None]

    small_vals = {n: jnp.stack(v) for n, v in gsmall.items()}
    small_vals["ada_b"] = jnp.stack(dmod)
    small_vals["loss"] = loss_part[0, 0:1]
    small_all = _exchange("ag_small", _small_pack(small_vals), ("x", "y", "c"), "gather")
    small_sum = _sum_groups("sum_small", small_all)
    dmod_all = small_all[:, :DEPTH * 6, :].reshape(8, DEPTH, 6 * D)
    dmod_sh = jnp.moveaxis(lax.dynamic_slice_in_dim(dmod_all, chip * ada_cols, ada_cols, axis=2), 0, 1)
    g_ada_w = _ada_grad("ada_grad", cond_all.T, dmod_sh)

    tail = grad_ride(G_LAST)
    g_parts[G_LAST] = _sum_groups("rs_sum_tail", _exchange("rs_tail", tail.src, tail.axes, tail.mode, tail.chunks))

    groups = (G_FIRST, G_SECOND, G_LAST)
    ada_rows = int(ada_w.size) // PACK_COLS
    small_shapes = {n: weights[n].shape for n, _ in SMALL_LAYOUT if n != "loss"}
    small_shapes["loss"] = (1,)

    def flat_all(src):
        sharded = [pack_items(src, items, F32) for items in groups]
        small = _small_pack({**{n: src[n] for n in small_shapes if n != "loss"}, "loss": jnp.zeros((1,), F32)})
        return jnp.concatenate([*sharded, src["ada_w"].reshape(ada_rows, PACK_COLS), small], axis=0)

    g_flat = jnp.concatenate([*[g_parts[items] for items in groups], g_ada_w.reshape(ada_rows, PACK_COLS),
                              small_sum], axis=0)
    delta, new_m, new_v = _adamw("adamw", g_flat, flat_all(weights), flat_all(mom_m), flat_all(mom_v))

    def split_all(flat):
        pieces, off = {}, 0
        for items in groups:
            rows = _pack_rows(item_shapes(items))
            for (n, a, _, _), part in zip(items, _unpack(flat[off:off + rows], 0, item_shapes(items))):
                pieces.setdefault(n, []).append((a, part))
            off += rows
        out = {n: jnp.concatenate([p for _, p in sorted(ps, key=lambda ap: ap[0])], axis=0)
               for n, ps in pieces.items()}
        out["ada_w"] = flat[off:off + ada_rows].reshape(ada_w.shape)
        out.update(_small_unpack(flat[off + ada_rows:], small_shapes))
        return out

    outs = [split_all(a) for a in (g_flat, delta, new_m, new_v)]
    loss = outs[0]["loss"][0]
    return (loss, grad_x, *[o[n] for o in outs for n in names])
```

```python
import jax
import jax.numpy as jnp
from jax import lax
from jax.experimental import pallas as pl
from jax.experimental.pallas import tpu as pltpu

F32 = jnp.float32
BF16 = jnp.bfloat16

D = 1024
DEPTH = 4
ROPE_THETA = 500000.0
LN_EPS = 1e-5
RMS_EPS = 1e-6
MLA_H = 8
MLA_NOPE = 128
MLA_ROPE = 64
MLA_V = 128
MLA_QR = 384
MLA_KVR = 256
MLA_LAT = 768
SWA_HQ = 16
SWA_HKV = 4
SWA_HD = 64
SWA_W = 128
SWA_ROT = 16
FF = 2816
ALPHA = (2 * DEPTH) ** 0.25
ADAM_LR = 0.001
ADAM_B1 = 0.9
ADAM_B2 = 0.999
ADAM_EPS = 1e-08
ADAM_WD = 0.01
ADAM_STEP = 10
NEG = -1e30
LANES = 128
N_CHIPS = 4
PACK_COLS = 1024
PACK_ROW_ALIGN = 512
SMALL_ROWS = 512
ROW_TILE = 512

SHARDED = (
    ("ffn_w_gate", 2), ("ffn_w_up", 2), ("ffn_w_down", 1), ("mla_w_in", 1), ("mla_w_q_b", 2),
    ("mla_w_kv_b", 2), ("mla_w_o", 1), ("swa_w_qkv", 2), ("swa_b_qkv", 1), ("swa_w_o", 1), ("swa_b_o", 1),
)


def _items(*specs):
    axis = dict(SHARDED)
    return tuple((n, a, b, axis[n]) for names, a, b in specs for n in names)


_FFN = ("ffn_w_gate", "ffn_w_up", "ffn_w_down")
_SWA = ("swa_w_qkv", "swa_b_qkv", "swa_w_o", "swa_b_o")
_MLA_IN = ("mla_w_in", "mla_w_q_b", "mla_w_kv_b")
_MLA_OUT = ("mla_w_o",)
W_EARLY = _items((_MLA_IN, 0, 1))
W_LATE = _items((_FFN, 0, 4), (_MLA_IN, 1, 2), (_MLA_OUT, 0, 2), (_SWA, 0, 2))
G_FIRST = _items((_FFN, 3, 4), (_SWA, 1, 2), (_FFN, 2, 3), (_MLA_OUT, 1, 2))
G_SECOND = _items((_MLA_IN, 1, 2), (_FFN, 1, 2), (_SWA, 0, 1), (_FFN, 0, 1), (_MLA_OUT, 0, 1))
G_LAST = _items((_MLA_IN, 0, 1))


def _cparams(*sem):
    return pltpu.CompilerParams(dimension_semantics=sem)


def _row_spec(tr, cols):
    return pl.BlockSpec((tr, cols), lambda i: (i, 0))


def _vec_spec(cols):
    return pl.BlockSpec((1, cols), lambda i: (0, 0))


def _rope(x, c, sa, sb, sh):
    n = x.shape[1]
    return x * c + pltpu.roll(x, n - sh, 1) * sa + pltpu.roll(x, sh, 1) * sb


def _rope_t(d, c, sa, sb, sh):
    n = d.shape[1]
    return d * c + pltpu.roll(d * sa, sh, 1) + pltpu.roll(d * sb, n - sh, 1)


def _rowsum8(t):
    r, n = t.shape
    return jnp.sum(t.reshape(r // 8, 8, n), axis=0)


class _Exchange:
    def __init__(self, src, axes, mode, chunks=1):
        self.src, self.axes, self.mode, self.chunks = src, axes, mode, chunks
        self.g = 2 ** len(axes)
        self.blk = tuple(src.shape if mode == "gather" else src.shape[1:])
        self.out_shape = jax.ShapeDtypeStruct((self.g,) + self.blk, src.dtype)
        nsem = (self.g - 1) * chunks
        self.scratch = [pltpu.SemaphoreType.DMA((nsem,)), pltpu.SemaphoreType.DMA((nsem,)),
                        pltpu.SemaphoreType.DMA(())]

    def copies(self, src_ref, out_ref, send_sems, recv_sems, loc_sem):
        pos = {a: lax.axis_index(a) for a in ("x", "y", "c")}
        rows = self.blk[0] // self.chunks

        def gidx(p):
            idx = 0
            for a in self.axes:
                idx = idx * 2 + p[a]
            return idx

        def view(p):
            if self.mode == "gather":
                return src_ref
            if self.mode == "to_chip":
                return src_ref.at[2 * p["x"] + p["y"]]
            return src_ref.at[gidx(p)]

        me = gidx(pos)
        out = [pltpu.make_async_copy(view(pos), out_ref.at[me], loc_sem)]
        for k in range(self.chunks):
            piece = pl.ds(k * rows, rows)
            for j in range(1, self.g):
                peer = dict(pos)
                for bit, a in enumerate(reversed(self.axes)):
                    if (j >> bit) & 1:
                        peer[a] = 1 - pos[a]
                sem = (j - 1) * self.chunks + k
                out.append(pltpu.make_async_remote_copy(
                    src_ref=view(peer).at[piece], dst_ref=out_ref.at[me, piece],
                    send_sem=send_sems.at[sem], recv_sem=recv_sems.at[sem],
                    device_id=(peer["x"], peer["y"], peer["c"]), device_id_type=pl.DeviceIdType.MESH))
        return out


def _exchange(name, src, axes, mode, chunks=1):
    ex = _Exchange(src, axes, mode, chunks)

    def body(src_ref, out_ref, send_sems, recv_sems, loc_sem):
        copies = ex.copies(src_ref, out_ref, send_sems, recv_sems, loc_sem)
        for cp in copies:
            cp.start()
        for cp in copies:
            cp.wait()

    return pl.pallas_call(
        body, name=name, out_shape=ex.out_shape,
        in_specs=[pl.BlockSpec(memory_space=pl.ANY)],
        out_specs=pl.BlockSpec(memory_space=pl.ANY),
        scratch_shapes=ex.scratch,
    )(src)


def _tri_call(body, rider, *, name, grid, tables, in_specs, out_specs, out_shape, scratch_shapes, operands):
    n_tab, n_in, n_out, n_scr = len(tables), len(in_specs), len(out_specs), len(scratch_shapes)
    in_specs, out_specs, out_shape = list(in_specs), list(out_specs), list(out_shape)
    scratch_shapes, operands = list(scratch_shapes), list(operands)
    if rider is not None:
        any_spec = pl.BlockSpec(memory_space=pl.ANY)
        in_specs.append(any_spec)
        out_specs.append(any_spec)
        out_shape.append(rider.out_shape)
        scratch_shapes.extend(rider.scratch)
        operands.append(rider.src)
        first, last = _grid_ends(grid)

    def wrapped(*refs):
        tabs, refs = refs[:n_tab], refs[n_tab:]
        if rider is None:
            return body(*tabs, *refs)
        ins, src_ref = refs[:n_in], refs[n_in]
        outs, out_ref = refs[n_in + 1:n_in + 1 + n_out], refs[n_in + 1 + n_out]
        scr = refs[n_in + 2 + n_out:n_in + 2 + n_out + n_scr]
        sems = refs[n_in + 2 + n_out + n_scr:]

        @pl.when(first())
        def _():
            for cp in rider.copies(src_ref, out_ref, *sems):
                cp.start()

        body(*tabs, *ins, *outs, *scr)

        @pl.when(last())
        def _():
            for cp in rider.copies(src_ref, out_ref, *sems):
                cp.wait()

    return pl.pallas_call(
        wrapped, name=name, out_shape=out_shape,
        grid_spec=pltpu.PrefetchScalarGridSpec(num_scalar_prefetch=n_tab, grid=grid, in_specs=in_specs,
                                               out_specs=out_specs, scratch_shapes=scratch_shapes),
        compiler_params=_cparams(*(["arbitrary"] * len(grid))),
    )(*tables, *operands)


def _tri_tables(n, queries_outer):
    if queries_outer:
        pairs = [(qi, ki) for qi in range(n) for ki in range(qi + 1)]
    else:
        pairs = [(qi, ki) for ki in range(n) for qi in range(ki, n)]
    return (jnp.asarray([p[0] for p in pairs], jnp.int32), jnp.asarray([p[1] for p in pairs], jnp.int32))


def _sum_groups(name, a):
    g, r, c = a.shape
    tr = min(ROW_TILE, r)

    def body(a_ref, o_ref):
        acc = a_ref[0].astype(F32)
        for i in range(1, g):
            acc = acc + a_ref[i].astype(F32)
        o_ref[...] = acc

    return pl.pallas_call(
        body, name=name, grid=(r // tr,),
        in_specs=[pl.BlockSpec((g, tr, c), lambda i: (0, i, 0))],
        out_specs=pl.BlockSpec((tr, c), lambda i: (i, 0)),
        out_shape=jax.ShapeDtypeStruct((r, c), F32),
        compiler_params=_cparams("parallel"),
    )(a)


def _mm(name, a, b, *, tm, tn=None, out_dtype=F32, epilogue=None, extras=(), extra_specs=()):
    m, k = a.shape
    n = b.shape[1]
    tn = tn or n
    tm = min(tm, m)

    def body(a_ref, b_ref, *rest):
        o_ref = rest[-1]
        acc = jnp.dot(a_ref[...], b_ref[...], preferred_element_type=F32)
        if epilogue is None:
            o_ref[...] = acc.astype(o_ref.dtype)
        else:
            epilogue(acc, o_ref, *rest[:-1])

    return pl.pallas_call(
        body, name=name, grid=(m // tm, n // tn),
        in_specs=[pl.BlockSpec((tm, k), lambda i, j: (i, 0)), pl.BlockSpec((k, tn), lambda i, j: (0, j)),
                  *extra_specs],
        out_specs=pl.BlockSpec((tm, tn), lambda i, j: (i, j)),
        out_shape=jax.ShapeDtypeStruct((m, n), out_dtype),
        compiler_params=_cparams("parallel", "parallel"),
    )(a, b, *extras)


def _mm_tn(name, a, b, *, tn=None, tk=1024):
    s, m = a.shape
    n = b.shape[1]
    tn = tn or n
    tk = min(tk, s)

    def body(a_ref, b_ref, o_ref):
        part = lax.dot_general(a_ref[...], b_ref[...], (((0,), (0,)), ((), ())), preferred_element_type=F32)

        @pl.when(pl.program_id(1) == 0)
        def _():
            o_ref[...] = part

        @pl.when(pl.program_id(1) > 0)
        def _():
            o_ref[...] += part

    return pl.pallas_call(
        body, name=name, grid=(n // tn, s // tk),
        in_specs=[pl.BlockSpec((tk, m), lambda j, k: (k, 0)), pl.BlockSpec((tk, tn), lambda j, k: (k, j))],
        out_specs=pl.BlockSpec((m, tn), lambda j, k: (0, j)),
        out_shape=jax.ShapeDtypeStruct((m, n), F32),
        compiler_params=_cparams("parallel", "arbitrary"),
    )(a, b)


def _modulate(name, x, sc, sh):
    s = x.shape[0]
    tr = min(ROW_TILE, s)

    def body(x_ref, sc_ref, sh_ref, h_ref):
        h_ref[...] = (x_ref[...] * (1.0 + sc_ref[...]) + sh_ref[...]).astype(BF16)

    return pl.pallas_call(
        body, name=name, grid=(s // tr,),
        in_specs=[_row_spec(tr, D), _vec_spec(D), _vec_spec(D)],
        out_specs=_row_spec(tr, D),
        out_shape=jax.ShapeDtypeStruct((s, D), BF16),
        compiler_params=_cparams("parallel"),
    )(x, sc, sh)


def _ln_stats(z):
    mu = jnp.mean(z, axis=1, keepdims=True)
    zc = z - mu
    var = jnp.mean(zc * zc, axis=1, keepdims=True)
    r = lax.rsqrt(var + LN_EPS)
    return zc * r, r


def _post_mod(name, x, y, g, gamma, beta, sc, sh):
    s = x.shape[0]
    tr = min(ROW_TILE, s)

    def body(x_ref, y_ref, g_ref, ga_ref, be_ref, sc_ref, sh_ref, xn_ref, h_ref):
        zh, _ = _ln_stats(ALPHA * x_ref[...] + g_ref[...] * y_ref[...])
        xn = zh * ga_ref[...] + be_ref[...]
        xn_ref[...] = xn
        h_ref[...] = (xn * (1.0 + sc_ref[...]) + sh_ref[...]).astype(BF16)

    return pl.pallas_call(
        body, name=name, grid=(s // tr,),
        in_specs=[_row_spec(tr, D), _row_spec(tr, D)] + [_vec_spec(D)] * 5,
        out_specs=[_row_spec(tr, D), _row_spec(tr, D)],
        out_shape=[jax.ShapeDtypeStruct((s, D), F32), jax.ShapeDtypeStruct((s, D), BF16)],
        compiler_params=_cparams("parallel"),
    )(x, y, g, gamma, beta, sc, sh)


def _post_loss(name, x, y, g, gamma, beta, target):
    s = x.shape[0]
    tr = min(ROW_TILE, s)
    nt = s // tr

    def body(x_ref, y_ref, g_ref, ga_ref, be_ref, t_ref, dx_ref, loss_ref, acc):
        i = pl.program_id(0)
        zh, _ = _ln_stats(ALPHA * x_ref[...] + g_ref[...] * y_ref[...])
        e = zh * ga_ref[...] + be_ref[...] - t_ref[...]
        dx_ref[...] = e * (1.0 / D)

        @pl.when(i == 0)
        def _():
            acc[...] = jnp.zeros_like(acc)

        acc[...] += _rowsum8(e * e)

        @pl.when(i == nt - 1)
        def _():
            loss_ref[...] = jnp.full(loss_ref.shape, jnp.sum(acc[...]) * (0.5 / D), F32)

    return pl.pallas_call(
        body, name=name, grid=(nt,),
        in_specs=[_row_spec(tr, D), _row_spec(tr, D)] + [_vec_spec(D)] * 3 + [_row_spec(tr, D)],
        out_specs=[_row_spec(tr, D), pl.BlockSpec((8, LANES), lambda i: (0, 0))],
        out_shape=[jax.ShapeDtypeStruct((s, D), F32), jax.ShapeDtypeStruct((8, LANES), F32)],
        scratch_shapes=[pltpu.VMEM((8, D), F32)],
        compiler_params=_cparams("arbitrary"),
    )(x, y, g, gamma, beta, target)


def _post_bwd(name, dxn, x, y, g, gamma):
    s = x.shape[0]
    tr = min(ROW_TILE, s)
    nt = s // tr

    def body(d_ref, x_ref, y_ref, g_ref, ga_ref, dxp_ref, dy_ref, sums_ref, a0, a1, a2, a3):
        i = pl.program_id(0)
        yv = y_ref[...]
        gv = g_ref[...]
        zh, r = _ln_stats(ALPHA * x_ref[...] + gv * yv)
        dxn_v = d_ref[...]
        dzh = dxn_v * ga_ref[...]
        dz = r * (dzh - jnp.mean(dzh, axis=1, keepdims=True) - zh * jnp.mean(dzh * zh, axis=1, keepdims=True))
        dxp_ref[...] = ALPHA * dz
        dyv = gv * dz
        dy_ref[...] = dyv.astype(BF16)

        @pl.when(i == 0)
        def _():
            for a in (a0, a1, a2, a3):
                a[...] = jnp.zeros_like(a)

        a0[...] += _rowsum8(dxn_v * zh)
        a1[...] += _rowsum8(dxn_v)
        a2[...] += _rowsum8(dz * yv)
        a3[...] += _rowsum8(dyv)

        @pl.when(i == nt - 1)
        def _():
            for k, a in enumerate((a0, a1, a2, a3)):
                sums_ref[k:k + 1, :] = jnp.sum(a[...], axis=0, keepdims=True)

    return pl.pallas_call(
        body, name=name, grid=(nt,),
        in_specs=[_row_spec(tr, D)] * 3 + [_vec_spec(D)] * 2,
        out_specs=[_row_spec(tr, D), _row_spec(tr, D), pl.BlockSpec((4, D), lambda i: (0, 0))],
        out_shape=[jax.ShapeDtypeStruct((s, D), F32), jax.ShapeDtypeStruct((s, D), BF16),
                   jax.ShapeDtypeStruct((4, D), F32)],
        scratch_shapes=[pltpu.VMEM((8, D), F32)] * 4,
        compiler_params=_cparams("arbitrary"),
    )(dxn, x, y, g, gamma)


def _mod_bwd(name, dh, x, dxp, sc):
    s = x.shape[0]
    tr = min(ROW_TILE, s)
    nt = s // tr

    def body(dh_ref, x_ref, dxp_ref, sc_ref, dx_ref, sums_ref, a0, a1):
        i = pl.program_id(0)
        dhv = dh_ref[...]
        dx_ref[...] = dxp_ref[...] + dhv * (1.0 + sc_ref[...])

        @pl.when(i == 0)
        def _():
            a0[...] = jnp.zeros_like(a0)
            a1[...] = jnp.zeros_like(a1)

        a0[...] += _rowsum8(dhv * x_ref[...])
        a1[...] += _rowsum8(dhv)

        @pl.when(i == nt - 1)
        def _():
            sums_ref[0:1, :] = jnp.sum(a0[...], axis=0, keepdims=True)
            sums_ref[1:2, :] = jnp.sum(a1[...], axis=0, keepdims=True)

    return pl.pallas_call(
        body, name=name, grid=(nt,),
        in_specs=[_row_spec(tr, D)] * 3 + [_vec_spec(D)],
        out_specs=[_row_spec(tr, D), pl.BlockSpec((2, D), lambda i: (0, 0))],
        out_shape=[jax.ShapeDtypeStruct((s, D), F32), jax.ShapeDtypeStruct((2, D), F32)],
        scratch_shapes=[pltpu.VMEM((8, D), F32)] * 2,
        compiler_params=_cparams("arbitrary"),
    )(dh, x, dxp, sc)


def _ffn_tiles(s):
    return min(ROW_TILE, s), FF // 2


def _ffn_fwd(name, h, wg, wu, wd):
    s = h.shape[0]
    tm, tf = _ffn_tiles(s)

    def body(h_ref, wg_ref, wu_ref, wd_ref, gate_ref, up_ref, y_ref):
        hv = h_ref[...]
        gt = jnp.dot(hv, wg_ref[...], preferred_element_type=F32)
        up = jnp.dot(hv, wu_ref[...], preferred_element_type=F32)
        gate_ref[...] = gt
        up_ref[...] = up
        act = (gt * jax.nn.sigmoid(gt) * up).astype(BF16)
        part = jnp.dot(act, wd_ref[...], preferred_element_type=F32)

        @pl.when(pl.program_id(1) == 0)
        def _():
            y_ref[...] = part

        @pl.when(pl.program_id(1) > 0)
        def _():
            y_ref[...] += part

    return pl.pallas_call(
        body, name=name, grid=(s // tm, FF // tf),
        in_specs=[pl.BlockSpec((tm, D), lambda i, f: (i, 0)), pl.BlockSpec((D, tf), lambda i, f: (0, f)),
                  pl.BlockSpec((D, tf), lambda i, f: (0, f)), pl.BlockSpec((tf, D), lambda i, f: (f, 0))],
        out_specs=[pl.BlockSpec((tm, tf), lambda i, f: (i, f)), pl.BlockSpec((tm, tf), lambda i, f: (i, f)),
                   pl.BlockSpec((tm, D), lambda i, f: (i, 0))],
        out_shape=[jax.ShapeDtypeStruct((s, FF), F32), jax.ShapeDtypeStruct((s, FF), F32),
                   jax.ShapeDtypeStruct((s, D), F32)],
        compiler_params=_cparams("parallel", "arbitrary"),
    )(h, wg, wu, wd)


def _ffn_bwd(name, dy, gate, up, wd_t, wg_t, wu_t):
    s = dy.shape[0]
    tm, tf = _ffn_tiles(s)

    def body(dy_ref, gate_ref, up_ref, wdt_ref, wgt_ref, wut_ref, dg_ref, du_ref, act_ref, dh_ref):
        dact = jnp.dot(dy_ref[...], wdt_ref[...], preferred_element_type=F32)
        gt = gate_ref[...]
        up = up_ref[...]
        sig = jax.nn.sigmoid(gt)
        silu = gt * sig
        dgt = (dact * up * (sig * (1.0 + gt * (1.0 - sig)))).astype(BF16)
        dup = (dact * silu).astype(BF16)
        dg_ref[...] = dgt
        du_ref[...] = dup
        act_ref[...] = (silu * up).astype(BF16)
        part = (jnp.dot(dgt, wgt_ref[...], preferred_element_type=F32)
                + jnp.dot(dup, wut_ref[...], preferred_element_type=F32))

        @pl.when(pl.program_id(1) == 0)
        def _():
            dh_ref[...] = part

        @pl.when(pl.program_id(1) > 0)
        def _():
            dh_ref[...] += part

    tile = pl.BlockSpec((tm, tf), lambda i, f: (i, f))
    return pl.pallas_call(
        body, name=name, grid=(s // tm, FF // tf),
        in_specs=[pl.BlockSpec((tm, D), lambda i, f: (i, 0)), tile, tile,
                  pl.BlockSpec((D, tf), lambda i, f: (0, f)), pl.BlockSpec((tf, D), lambda i, f: (f, 0)),
                  pl.BlockSpec((tf, D), lambda i, f: (f, 0))],
        out_specs=[tile, tile, tile, pl.BlockSpec((tm, D), lambda i, f: (i, 0))],
        out_shape=[jax.ShapeDtypeStruct((s, FF), BF16)] * 3 + [jax.ShapeDtypeStruct((s, D), F32)],
        compiler_params=_cparams("parallel", "arbitrary"),
    )(dy, gate, up, wd_t, wg_t, wu_t)


def _mla_lat_post(name, lat, qw, kvw, rc, rsa, rsb):
    s = lat.shape[0]
    tr = min(ROW_TILE, s)

    def body(lat_ref, qw_ref, kvw_ref, c_ref, sa_ref, sb_ref, qn_ref, kvn_ref, kr_ref):
        ql = lat_ref[:, 0:MLA_QR]
        kl = lat_ref[:, MLA_QR:MLA_QR + MLA_KVR]
        qn_ref[...] = (ql * lax.rsqrt(jnp.mean(ql * ql, axis=1, keepdims=True) + RMS_EPS) * qw_ref[...]).astype(BF16)
        kvn_ref[...] = (kl * lax.rsqrt(jnp.mean(kl * kl, axis=1, keepdims=True) + RMS_EPS) * kvw_ref[...]).astype(BF16)
        kr_ref[...] = _rope(lat_ref[:, MLA_QR + MLA_KVR:MLA_LAT], c_ref[...], sa_ref[...], sb_ref[...],
                            MLA_ROPE // 2).astype(BF16)

    return pl.pallas_call(
        body, name=name, grid=(s // tr,),
        in_specs=[_row_spec(tr, MLA_LAT), _vec_spec(MLA_QR), _vec_spec(MLA_KVR)] + [_row_spec(tr, LANES)] * 3,
        out_specs=[_row_spec(tr, MLA_QR), _row_spec(tr, MLA_KVR), _row_spec(tr, LANES)],
        out_shape=[jax.ShapeDtypeStruct((s, MLA_QR), BF16), jax.ShapeDtypeStruct((s, MLA_KVR), BF16),
                   jax.ShapeDtypeStruct((s, LANES), BF16)],
        compiler_params=_cparams("parallel"),
    )(lat, qw, kvw, rc, rsa, rsb)


def _mla_lat_bwd(name, lat, dqn, dkvn, dkr_heads, qw, kvw, rc, rsa, rsb):
    s = lat.shape[0]
    tr = min(ROW_TILE, s)
    nt = s // tr

    def rms_bwd(x, w, dy):
        r = lax.rsqrt(jnp.mean(x * x, axis=1, keepdims=True) + RMS_EPS)
        xh = x * r
        gdy = dy * w
        return r * (gdy - xh * jnp.mean(gdy * xh, axis=1, keepdims=True)), dy * xh

    def body(lat_ref, dqn_ref, dkvn_ref, dkr_ref, qw_ref, kvw_ref, c_ref, sa_ref, sb_ref,
             dlat_ref, dqw_ref, dkvw_ref, aq, akv):
        i = pl.program_id(0)
        dq, dqw = rms_bwd(lat_ref[:, 0:MLA_QR], qw_ref[...], dqn_ref[...])
        dk, dkw = rms_bwd(lat_ref[:, MLA_QR:MLA_QR + MLA_KVR], kvw_ref[...], dkvn_ref[...])
        dkr = dkr_ref[0]
        for hh in range(1, MLA_H):
            dkr = dkr + dkr_ref[hh]
        dkr = _rope_t(dkr, c_ref[...], sa_ref[...], sb_ref[...], MLA_ROPE // 2)
        dlat_ref[:, 0:MLA_QR] = dq.astype(BF16)
        dlat_ref[:, MLA_QR:MLA_QR + MLA_KVR] = dk.astype(BF16)
        dlat_ref[:, MLA_QR + MLA_KVR:MLA_LAT] = dkr.astype(BF16)

        @pl.when(i == 0)
        def _():
            aq[...] = jnp.zeros_like(aq)
            akv[...] = jnp.zeros_like(akv)

        aq[...] += _rowsum8(dqw)
        akv[...] += _rowsum8(dkw)

        @pl.when(i == nt - 1)
        def _():
            dqw_ref[...] = jnp.sum(aq[...], axis=0, keepdims=True)
            dkvw_ref[...] = jnp.sum(akv[...], axis=0, keepdims=True)

    return pl.pallas_call(
        body, name=name, grid=(nt,),
        in_specs=[_row_spec(tr, MLA_LAT), _row_spec(tr, MLA_QR), _row_spec(tr, MLA_KVR),
                  pl.BlockSpec((MLA_H, tr, LANES), lambda i: (0, i, 0)), _vec_spec(MLA_QR), _vec_spec(MLA_KVR)]
        + [_row_spec(tr, LANES)] * 3,
        out_specs=[_row_spec(tr, MLA_LAT), _vec_spec(MLA_QR), _vec_spec(MLA_KVR)],
        out_shape=[jax.ShapeDtypeStruct((s, MLA_LAT), BF16), jax.ShapeDtypeStruct((1, MLA_QR), F32),
                   jax.ShapeDtypeStruct((1, MLA_KVR), F32)],
        scratch_shapes=[pltpu.VMEM((8, MLA_QR), F32), pltpu.VMEM((8, MLA_KVR), F32)],
        compiler_params=_cparams("arbitrary"),
    )(lat, dqn, dkvn, dkr_heads, qw, kvw, rc, rsa, rsb)


def _attn_tile(s):
    return min(1024, max(LANES, s // 2))


MLA_SCALE = (MLA_NOPE + MLA_ROPE) ** -0.5
LOG2E = 1.4426950408889634
MLA_C2 = MLA_SCALE * LOG2E
NT_DIMS = (((1,), (1,)), ((), ()))
TN_DIMS = (((0,), (0,)), ((), ()))


def _causal(sc, transposed=False):
    row = lax.broadcasted_iota(jnp.int32, sc.shape, 0)
    col = lax.broadcasted_iota(jnp.int32, sc.shape, 1)
    return jnp.where(row <= col if transposed else col <= row, sc, NEG)


def _grid_ends(grid):
    def first():
        ok = pl.program_id(0) == 0
        for d in range(1, len(grid)):
            ok = jnp.logical_and(ok, pl.program_id(d) == 0)
        return ok

    def last():
        ok = pl.program_id(0) == grid[0] - 1
        for d in range(1, len(grid)):
            ok = jnp.logical_and(ok, pl.program_id(d) == grid[d] - 1)
        return ok

    return first, last


def _mla_attn_fwd(name, qq, kv, kr, rider=None):
    s = qq.shape[0]
    t = _attn_tile(s)
    n = s // t

    def body(qi_tab, ki_tab, q_ref, kv_ref, kr_ref, o_ref, lse_ref, m_scr, acc_scr, sc_scr):
        qi = qi_tab[pl.program_id(1)]
        ki = ki_tab[pl.program_id(1)]

        @pl.when(ki == 0)
        def _():
            m_scr[...] = jnp.full(m_scr.shape, NEG, F32)
            acc_scr[...] = jnp.zeros_like(acc_scr)

        @pl.when(ki >= 0)
        def _():
            k = jnp.concatenate([kv_ref[:, 0:LANES], kr_ref[...]], axis=1)
            sc_scr[...] = lax.dot_general(q_ref[...], k, NT_DIMS, preferred_element_type=F32)

        def step(diag):
            sc = sc_scr[...]
            if diag:
                sc = _causal(sc)
            m_prev = m_scr[...]
            m_next = jnp.maximum(m_prev, jnp.max(sc, axis=1, keepdims=True))
            a = jnp.exp2(MLA_C2 * (m_prev - m_next))
            p = jnp.exp2(MLA_C2 * sc - MLA_C2 * m_next[:, 0:1]).astype(BF16)
            v1 = jnp.concatenate([kv_ref[:, LANES:2 * LANES], jnp.ones((t, LANES), BF16)], axis=1)
            pv = jnp.dot(p, v1, preferred_element_type=F32)
            acc_scr[:, 0:LANES] = a * acc_scr[:, 0:LANES] + pv[:, 0:LANES]
            acc_scr[:, LANES:2 * LANES] = a * acc_scr[:, LANES:2 * LANES] + pv[:, LANES:2 * LANES]
            m_scr[...] = m_next

        @pl.when(ki < qi)
        def _():
            step(False)

        @pl.when(ki == qi)
        def _():
            step(True)

        @pl.when(ki == qi)
        def _():
            l = acc_scr[:, LANES:2 * LANES]
            o_ref[...] = (acc_scr[:, 0:LANES] / l).astype(BF16)
            lse_ref[...] = MLA_SCALE * m_scr[...] + jnp.log(l)

    qblk = lambda w: pl.BlockSpec((t, w), lambda h, i, qt, kt: (qt[i], h))
    return _tri_call(
        body, rider, name=name, grid=(MLA_H, n * (n + 1) // 2), tables=_tri_tables(n, queries_outer=True),
        in_specs=[qblk(2 * LANES), pl.BlockSpec((t, 2 * LANES), lambda h, i, qt, kt: (kt[i], h)),
                  pl.BlockSpec((t, LANES), lambda h, i, qt, kt: (kt[i], 0))],
        out_specs=[qblk(LANES), qblk(LANES)],
        out_shape=[jax.ShapeDtypeStruct((s, MLA_H * MLA_V), BF16), jax.ShapeDtypeStruct((s, MLA_H * LANES), F32)],
        scratch_shapes=[pltpu.VMEM((t, LANES), F32), pltpu.VMEM((t, 2 * LANES), F32), pltpu.VMEM((t, t), F32)],
        operands=(qq, kv, kr))


def _mla_bwd_stats(name, do, o, lse):
    s = do.shape[0]
    tr = min(4 * ROW_TILE, s)

    def body(do_ref, o_ref, lse_ref, stat_ref):
        dl = jnp.sum(do_ref[...].astype(F32) * o_ref[...].astype(F32), axis=1, keepdims=True)
        delta_t = jnp.transpose(jnp.broadcast_to(dl, (tr, LANES)))[0:8]
        lse_t = jnp.transpose(lse_ref[...] * LOG2E)[0:8]
        rows = lax.broadcasted_iota(jnp.int32, (8, tr), 0)
        stat_ref[0] = jnp.where(rows == 0, lse_t, jnp.where(rows == 1, delta_t, 0.0))

    blk = pl.BlockSpec((tr, LANES), lambda h, i: (i, h))
    return pl.pallas_call(
        body, name=name, grid=(MLA_H, s // tr),
        in_specs=[blk, blk, blk],
        out_specs=pl.BlockSpec((1, 8, tr), lambda h, i: (h, 0, i)),
        out_shape=jax.ShapeDtypeStruct((MLA_H, 8, s), F32),
        compiler_params=_cparams("parallel", "parallel"),
    )(do, o, lse)


def _mla_dq_post(name, dq, rc, rsa, rsb):
    s = dq.shape[1]
    tr = min(4 * ROW_TILE, s)

    def body(dq_ref, c_ref, sa_ref, sb_ref, o_ref):
        o_ref[:, 0:LANES] = (MLA_SCALE * dq_ref[0, :, 0:LANES]).astype(BF16)
        o_ref[:, LANES:2 * LANES] = _rope_t(MLA_SCALE * dq_ref[0, :, LANES:2 * LANES], c_ref[...], sa_ref[...],
                                            sb_ref[...], MLA_ROPE // 2).astype(BF16)

    tab = pl.BlockSpec((tr, LANES), lambda h, i: (i, 0))
    return pl.pallas_call(
        body, name=name, grid=(MLA_H, s // tr),
        in_specs=[pl.BlockSpec((1, tr, 2 * LANES), lambda h, i: (h, i, 0)), tab, tab, tab],
        out_specs=pl.BlockSpec((tr, 2 * LANES), lambda h, i: (i, h)),
        out_shape=jax.ShapeDtypeStruct((s, 2 * MLA_H * LANES), BF16),
        compiler_params=_cparams("parallel", "parallel"),
    )(dq, rc, rsa, rsb)


def _mla_attn_bwd(name, qq, kv, kr, do, stats, rider=None):
    s = qq.shape[0]
    t = _attn_tile(s)
    n = s // t

    def body(qi_tab, ki_tab, q_ref, kv_ref, kr_ref, do_ref, stat_ref, dkv_ref, dkr_ref, dq_hbm,
             dk_scr, dv_scr, dq_scr, dq_sem):
        h = pl.program_id(0)
        qi = qi_tab[pl.program_id(1)]
        ki = ki_tab[pl.program_id(1)]

        @pl.when(pl.program_id(1) == 0)
        def _():
            dq_scr[...] = jnp.zeros_like(dq_scr)

        @pl.when(qi == ki)
        def _():
            dk_scr[...] = jnp.zeros_like(dk_scr)
            dv_scr[...] = jnp.zeros_like(dv_scr)

        def step(diag):
            q = q_ref[...]
            k = jnp.concatenate([kv_ref[:, 0:LANES], kr_ref[...]], axis=1)
            sc = lax.dot_general(k, q, NT_DIMS, preferred_element_type=F32)
            if diag:
                sc = _causal(sc, transposed=True)
            p = jnp.exp2(MLA_C2 * sc - stat_ref[0, 0:1, :])
            dov = do_ref[...]
            dp = lax.dot_general(kv_ref[:, LANES:2 * LANES], dov, NT_DIMS, preferred_element_type=F32)
            ds = (p * (dp - stat_ref[0, 1:2, :])).astype(BF16)
            dv_scr[...] += jnp.dot(p.astype(BF16), dov, preferred_element_type=F32)
            dk_scr[...] += jnp.dot(ds, q, preferred_element_type=F32)
            rows = pl.ds(pl.multiple_of(qi * t, t), t)
            dq_scr[rows, :] += lax.dot_general(ds, k, TN_DIMS, preferred_element_type=F32)

        @pl.when(qi == ki)
        def _():
            step(True)

        @pl.when(qi > ki)
        def _():
            step(False)

        @pl.when(qi == n - 1)
        def _():
            dkv_ref[:, 0:LANES] = (MLA_SCALE * dk_scr[:, 0:LANES]).astype(BF16)
            dkv_ref[:, LANES:2 * LANES] = dv_scr[...].astype(BF16)
            dkr_ref[0] = MLA_SCALE * dk_scr[:, LANES:2 * LANES]

        @pl.when(pl.program_id(1) == n * (n + 1) // 2 - 1)
        def _():
            cp = pltpu.make_async_copy(dq_scr, dq_hbm.at[h], dq_sem)
            cp.start()
            cp.wait()

    qblk = lambda w: pl.BlockSpec((t, w), lambda h, i, qt, kt: (qt[i], h))
    kblk = pl.BlockSpec((t, 2 * LANES), lambda h, i, qt, kt: (kt[i], h))
    return _tri_call(
        body, rider, name=name, grid=(MLA_H, n * (n + 1) // 2), tables=_tri_tables(n, queries_outer=False),
        in_specs=[qblk(2 * LANES), kblk, pl.BlockSpec((t, LANES), lambda h, i, qt, kt: (kt[i], 0)), qblk(LANES),
                  pl.BlockSpec((1, 8, t), lambda h, i, qt, kt: (h, 0, qt[i]))],
        out_specs=[kblk, pl.BlockSpec((1, t, LANES), lambda h, i, qt, kt: (h, kt[i], 0)),
                   pl.BlockSpec(memory_space=pl.ANY)],
        out_shape=[jax.ShapeDtypeStruct((s, 2 * MLA_H * LANES), BF16),
                   jax.ShapeDtypeStruct((MLA_H, s, LANES), F32),
                   jax.ShapeDtypeStruct((MLA_H, s, 2 * LANES), F32)],
        scratch_shapes=[pltpu.VMEM((t, 2 * LANES), F32), pltpu.VMEM((t, LANES), F32),
                        pltpu.VMEM((s, 2 * LANES), F32), pltpu.SemaphoreType.DMA(())],
        operands=(qq, kv, kr, do, stats))


def _rope_groups(acc, o_ref, c, sa, sb, sh, groups):
    for gi in range(acc.shape[1] // LANES):
        blk = acc[:, gi * LANES:(gi + 1) * LANES]
        if gi in groups:
            blk = _rope(blk, c, sa, sb, sh)
        o_ref[:, gi * LANES:(gi + 1) * LANES] = blk.astype(o_ref.dtype)


def _mla_fwd(tag, h, w, tabs, rider=None):
    s = h.shape[0]
    rc, rsa, rsb = tabs
    lat = _mm(f"{tag}_lat", h, w["w_in"], tm=512)
    qn, kvn, kr = _mla_lat_post(f"{tag}_latpost", lat, w["q_norm"], w["kv_norm"], rc, rsa, rsb)
    tm = min(512, s)

    def q_epi(acc, o_ref, c_ref, sa_ref, sb_ref):
        _rope_groups(acc, o_ref, c_ref[...], sa_ref[...], sb_ref[...], MLA_ROPE // 2, range(1, MLA_H, 2))

    tab = pl.BlockSpec((tm, LANES), lambda i, j: (i, 0))
    qq = _mm(f"{tag}_q", qn, w["w_q"], tm=512, tn=MLA_H * LANES, out_dtype=BF16, epilogue=q_epi,
             extras=(rc, rsa, rsb), extra_specs=(tab, tab, tab))
    kv = _mm(f"{tag}_kv", kvn, w["w_kv"], tm=512, out_dtype=BF16)
    o, lse, *ridden = _mla_attn_fwd(f"{tag}_attn", qq, kv, kr, rider)
    res = dict(h=h, lat=lat, qn=qn, kvn=kvn, kr=kr, qq=qq, kv=kv, o=o, lse=lse)
    return o, res, (ridden[0] if ridden else None)


def _mla_bwd(tag, dy, res, w, tabs, make_rider=None):
    rc, rsa, rsb = tabs
    do = _mm(f"{tag}_do", dy, w["w_o_t"], tm=512, out_dtype=BF16)
    g_wo = _mm_tn(f"{tag}_gwo", res["o"], dy)
    stats = _mla_bwd_stats(f"{tag}_stats", do, res["o"], res["lse"])
    rider = make_rider(g_wo) if make_rider is not None else None
    dkv, dkr, dq, *ridden = _mla_attn_bwd(f"{tag}_attnbwd", res["qq"], res["kv"], res["kr"], do, stats, rider)
    dqq = _mla_dq_post(f"{tag}_dqpost", dq, rc, rsa, rsb)
    dqn = _mm(f"{tag}_dqn", dqq, w["w_q_t"], tm=512)
    g_wq = _mm_tn(f"{tag}_gwq", res["qn"], dqq, tn=1024)
    dkvn = _mm(f"{tag}_dkvn", dkv, w["w_kv_t"], tm=512)
    g_wkv = _mm_tn(f"{tag}_gwkv", res["kvn"], dkv, tn=1024)
    dlat, g_qn, g_kvn = _mla_lat_bwd(f"{tag}_latbwd", res["lat"], dqn, dkvn, dkr, w["q_norm"], w["kv_norm"],
                                     rc, rsa, rsb)
    dh = _mm(f"{tag}_dh", dlat, w["w_in_t"], tm=512)
    g_win = _mm_tn(f"{tag}_gwin", res["h"], dlat)
    grads = dict(w_in=g_win, q_norm=g_qn, w_q=g_wq, kv_norm=g_kvn, w_kv=g_wkv, w_o=g_wo)
    return dh, grads, (ridden[0] if ridden else None)


SWA_QW = SWA_HQ * SWA_HD
SWA_KW = SWA_HKV * LANES
SWA_NQKV = SWA_QW + 2 * SWA_KW
SWA_SCALE = SWA_HD ** -0.5
SWA_GROUP_ROWS = 4 * SWA_W


def _swa_tile(s):
    return min(512, max(SWA_W, s // 2))


def _swa_masks():
    lane = lax.broadcasted_iota(jnp.int32, (SWA_W, LANES), 1)
    return lane < SWA_HD


def _swa_q4(qa, qb, lo):
    z = jnp.zeros_like(qa)
    return jnp.concatenate([jnp.where(lo, qa, z), jnp.where(lo, z, qa), jnp.where(lo, qb, z), jnp.where(lo, z, qb)],
                           axis=0)


def _swa_probs(q4, kwin, sink_col, first_block):
    sc = lax.dot_general(q4, kwin, NT_DIMS, preferred_element_type=F32) * SWA_SCALE
    row = lax.broadcasted_iota(jnp.int32, sc.shape, 0) % SWA_W
    col = lax.broadcasted_iota(jnp.int32, sc.shape, 1)
    rel = row + SWA_W - col
    ok = (rel >= 0) & (rel < SWA_W) & ((col >= SWA_W) | jnp.logical_not(first_block))
    sc = jnp.where(ok, sc, NEG)
    m = jnp.maximum(jnp.max(sc, axis=1, keepdims=True), sink_col)
    e = jnp.exp(sc - m)
    es = jnp.exp(sink_col - m)
    inv = 1.0 / (jnp.sum(e, axis=1, keepdims=True) + es)
    return e * inv, es * inv


def _sink_col(sinks_ref, grp):
    seg = lax.broadcasted_iota(jnp.int32, (SWA_GROUP_ROWS, 1), 0) // SWA_W
    col = jnp.zeros((SWA_GROUP_ROWS, 1), F32)
    for j in range(4):
        col = jnp.where(seg == j, sinks_ref[0, 4 * grp + j], col)
    return col


def _swa_attn_fwd(name, qkv, sinks):
    s = qkv.shape[0]
    t = _swa_tile(s)
    nb = t // SWA_W

    def body(sinks_ref, q_ref, kv_ref, kvp_ref, o_ref):
        i = pl.program_id(0)
        lo = _swa_masks()
        for grp in range(SWA_HKV):
            sink_col = _sink_col(sinks_ref, grp)
            kcat = jnp.concatenate([kvp_ref[:, grp * LANES:(grp + 1) * LANES],
                                    kv_ref[:, grp * LANES:(grp + 1) * LANES]], axis=0)
            vcat = jnp.concatenate([kvp_ref[:, SWA_KW + grp * LANES:SWA_KW + (grp + 1) * LANES],
                                    kv_ref[:, SWA_KW + grp * LANES:SWA_KW + (grp + 1) * LANES]], axis=0)
            for b in range(nb):
                r0 = b * SWA_W
                qa = q_ref[r0:r0 + SWA_W, grp * 2 * LANES:grp * 2 * LANES + LANES]
                qb = q_ref[r0:r0 + SWA_W, grp * 2 * LANES + LANES:(grp + 1) * 2 * LANES]
                first = jnp.logical_and(i == 0, b == 0)
                p, _ = _swa_probs(_swa_q4(qa, qb, lo), kcat[r0:r0 + 2 * SWA_W], sink_col, first)
                o4 = jnp.dot(p.astype(BF16), vcat[r0:r0 + 2 * SWA_W], preferred_element_type=F32)
                oa = jnp.where(lo, o4[0:SWA_W], o4[SWA_W:2 * SWA_W])
                ob = jnp.where(lo, o4[2 * SWA_W:3 * SWA_W], o4[3 * SWA_W:4 * SWA_W])
                o_ref[r0:r0 + SWA_W, grp * 2 * LANES:grp * 2 * LANES + LANES] = oa.astype(BF16)
                o_ref[r0:r0 + SWA_W, grp * 2 * LANES + LANES:(grp + 1) * 2 * LANES] = ob.astype(BF16)

    return pl.pallas_call(
        body, name=name, grid=(s // t,),
        in_specs=[pl.BlockSpec(memory_space=pltpu.SMEM),
                  pl.BlockSpec((t, SWA_QW), lambda i: (i, 0)),
                  pl.BlockSpec((t, 2 * SWA_KW), lambda i: (i, 1)),
                  pl.BlockSpec((SWA_W, 2 * SWA_KW), lambda i: (jnp.maximum(i * nb - 1, 0), 1))],
        out_specs=pl.BlockSpec((t, SWA_QW), lambda i: (i, 0)),
        out_shape=jax.ShapeDtypeStruct((s, SWA_QW), BF16),
        compiler_params=_cparams("parallel"),
    )(sinks, qkv, qkv, qkv)


def _swa_attn_bwd(name, qkv, sinks, do):
    s = qkv.shape[0]
    t = _swa_tile(s)
    nb = t // SWA_W
    nt = s // t

    def body(sinks_ref, q_ref, kv_ref, kvp_ref, do_ref, dq_ref, dkv_ref, dkvp_ref, dsink_ref, dcat, sink_acc):
        i = pl.program_id(0)
        lo = _swa_masks()

        @pl.when(i == 0)
        def _():
            sink_acc[...] = jnp.zeros_like(sink_acc)

        dcat[...] = jnp.zeros_like(dcat)
        for grp in range(SWA_HKV):
            sink_col = _sink_col(sinks_ref, grp)
            kcat = jnp.concatenate([kvp_ref[:, grp * LANES:(grp + 1) * LANES],
                                    kv_ref[:, grp * LANES:(grp + 1) * LANES]], axis=0)
            vcat = jnp.concatenate([kvp_ref[:, SWA_KW + grp * LANES:SWA_KW + (grp + 1) * LANES],
                                    kv_ref[:, SWA_KW + grp * LANES:SWA_KW + (grp + 1) * LANES]], axis=0)
            for b in range(nb):
                r0 = b * SWA_W
                ca = slice(grp * 2 * LANES, grp * 2 * LANES + LANES)
                cb = slice(grp * 2 * LANES + LANES, (grp + 1) * 2 * LANES)
                q4 = _swa_q4(q_ref[r0:r0 + SWA_W, ca], q_ref[r0:r0 + SWA_W, cb], lo)
                do4 = _swa_q4(do_ref[r0:r0 + SWA_W, ca], do_ref[r0:r0 + SWA_W, cb], lo)
                first = jnp.logical_and(i == 0, b == 0)
                kwin = kcat[r0:r0 + 2 * SWA_W]
                vwin = vcat[r0:r0 + 2 * SWA_W]
                p, ps = _swa_probs(q4, kwin, sink_col, first)
                dp = lax.dot_general(do4, vwin, NT_DIMS, preferred_element_type=F32)
                rowdot = jnp.sum(p * dp, axis=1, keepdims=True)
                ds = (p * (dp - rowdot) * SWA_SCALE).astype(BF16)
                sink_acc[grp] += jnp.broadcast_to(-ps * rowdot, (SWA_GROUP_ROWS, LANES))
                dq4 = jnp.dot(ds, kwin, preferred_element_type=F32)
                dq_ref[r0:r0 + SWA_W, ca] = jnp.where(lo, dq4[0:SWA_W], dq4[SWA_W:2 * SWA_W])
                dq_ref[r0:r0 + SWA_W, cb] = jnp.where(lo, dq4[2 * SWA_W:3 * SWA_W], dq4[3 * SWA_W:4 * SWA_W])
                dk = lax.dot_general(ds, q4, TN_DIMS, preferred_element_type=F32)
                dv = lax.dot_general(p.astype(BF16), do4, TN_DIMS, preferred_element_type=F32)
                dcat[r0:r0 + 2 * SWA_W, grp * LANES:(grp + 1) * LANES] += dk
                dcat[r0:r0 + 2 * SWA_W, SWA_KW + grp * LANES:SWA_KW + (grp + 1) * LANES] += dv
        dkvp_ref[0] = dcat[0:SWA_W]
        dkv_ref[...] = dcat[SWA_W:SWA_W + t]

        @pl.when(i == nt - 1)
        def _():
            for grp in range(SWA_HKV):
                for j in range(4):
                    tot = jnp.sum(sink_acc[grp, j * SWA_W:(j + 1) * SWA_W, 0:1])
                    dsink_ref[4 * grp + j:4 * grp + j + 1, :] = jnp.full((1, LANES), tot, F32)

    return pl.pallas_call(
        body, name=name, grid=(nt,),
        in_specs=[pl.BlockSpec(memory_space=pltpu.SMEM),
                  pl.BlockSpec((t, SWA_QW), lambda i: (i, 0)),
                  pl.BlockSpec((t, 2 * SWA_KW), lambda i: (i, 1)),
                  pl.BlockSpec((SWA_W, 2 * SWA_KW), lambda i: (jnp.maximum(i * nb - 1, 0), 1)),
                  pl.BlockSpec((t, SWA_QW), lambda i: (i, 0))],
        out_specs=[pl.BlockSpec((t, SWA_QW), lambda i: (i, 0)), pl.BlockSpec((t, 2 * SWA_KW), lambda i: (i, 0)),
                   pl.BlockSpec((1, SWA_W, 2 * SWA_KW), lambda i: (i, 0, 0)),
                   pl.BlockSpec((SWA_HQ, LANES), lambda i: (0, 0))],
        out_shape=[jax.ShapeDtypeStruct((s, SWA_QW), F32), jax.ShapeDtypeStruct((s, 2 * SWA_KW), F32),
                   jax.ShapeDtypeStruct((nt, SWA_W, 2 * SWA_KW), F32), jax.ShapeDtypeStruct((SWA_HQ, LANES), F32)],
        scratch_shapes=[pltpu.VMEM((SWA_W + t, 2 * SWA_KW), F32), pltpu.VMEM((SWA_HKV, SWA_GROUP_ROWS, LANES), F32)],
        compiler_params=_cparams("arbitrary"),
    )(sinks, qkv, qkv, qkv, do)


def _swa_dqkv(name, dq, dkv, dkvp, rc, rsa, rsb):
    s = dq.shape[0]
    t = _swa_tile(s)
    nt = s // t
    sh = SWA_ROT // 2

    def body(dq_ref, dkv_ref, dkvn_ref, c_ref, sa_ref, sb_ref, out_ref, bsum_ref, acc):
        i = pl.program_id(0)
        c, sa, sb = c_ref[...], sa_ref[...], sb_ref[...]
        lo = lax.broadcasted_iota(jnp.int32, (t, LANES), 1) < SWA_HD
        rows = lax.broadcasted_iota(jnp.int32, (t, LANES), 0)
        tail = jnp.logical_and(rows >= t - SWA_W, i < nt - 1)

        @pl.when(i == 0)
        def _():
            acc[...] = jnp.zeros_like(acc)

        for gi in range(SWA_QW // LANES):
            blk = _rope_t(dq_ref[:, gi * LANES:(gi + 1) * LANES], c, sa, sb, sh)
            out_ref[:, gi * LANES:(gi + 1) * LANES] = blk.astype(BF16)
            acc[:, gi * LANES:(gi + 1) * LANES] += _rowsum8(blk)
        for gi in range(2 * SWA_KW // LANES):
            cols = slice(gi * LANES, (gi + 1) * LANES)
            nxt = jnp.concatenate([jnp.zeros((t - SWA_W, LANES), F32), dkvn_ref[0, :, cols]], axis=0)
            blk = dkv_ref[:, cols] + jnp.where(tail, nxt, 0.0)
            blk = jnp.where(lo, blk + pltpu.roll(blk, SWA_HD, 1), 0.0)
            if gi < SWA_HKV:
                blk = _rope_t(blk, c, sa, sb, sh)
            out_ref[:, SWA_QW + gi * LANES:SWA_QW + (gi + 1) * LANES] = blk.astype(BF16)
            acc[:, SWA_QW + gi * LANES:SWA_QW + (gi + 1) * LANES] += _rowsum8(blk)

        @pl.when(i == nt - 1)
        def _():
            bsum_ref[...] = jnp.sum(acc[...], axis=0, keepdims=True)

    return pl.pallas_call(
        body, name=name, grid=(nt,),
        in_specs=[pl.BlockSpec((t, SWA_QW), lambda i: (i, 0)), pl.BlockSpec((t, 2 * SWA_KW), lambda i: (i, 0)),
                  pl.BlockSpec((1, SWA_W, 2 * SWA_KW), lambda i: (jnp.minimum(i + 1, nt - 1), 0, 0))]
        + [_row_spec(t, LANES)] * 3,
        out_specs=[pl.BlockSpec((t, SWA_NQKV), lambda i: (i, 0)), pl.BlockSpec((1, SWA_NQKV), lambda i: (0, 0))],
        out_shape=[jax.ShapeDtypeStruct((s, SWA_NQKV), BF16), jax.ShapeDtypeStruct((1, SWA_NQKV), F32)],
        scratch_shapes=[pltpu.VMEM((8, SWA_NQKV), F32)],
        compiler_params=_cparams("arbitrary"),
    )(dq, dkv, dkvp, rc, rsa, rsb)


def _swa_fwd(tag, h, w, tabs):
    s = h.shape[0]
    rc, rsa, rsb = tabs
    tm = min(512, s)
    sh = SWA_ROT // 2

    def qkv_epi(acc, o_ref, b_ref, c_ref, sa_ref, sb_ref):
        acc = acc + b_ref[...]

        @pl.when(pl.program_id(1) == 0)
        def _():
            _rope_groups(acc, o_ref, c_ref[...], sa_ref[...], sb_ref[...], sh, range(SWA_QW // LANES))

        @pl.when(pl.program_id(1) == 1)
        def _():
            _rope_groups(acc, o_ref, c_ref[...], sa_ref[...], sb_ref[...], sh, range(SWA_HKV))

    tab = pl.BlockSpec((tm, LANES), lambda i, j: (i, 0))
    qkv = _mm(f"{tag}_qkv", h, w["w_qkv"], tm=512, tn=SWA_QW, out_dtype=BF16, epilogue=qkv_epi,
              extras=(w["b_qkv"], rc, rsa, rsb),
              extra_specs=(pl.BlockSpec((1, SWA_QW), lambda i, j: (0, j)), tab, tab, tab))
    o = _swa_attn_fwd(f"{tag}_attn", qkv, w["sinks"])

    def o_epi(acc, o_ref, b_ref):
        o_ref[...] = acc + b_ref[...]

    y = _mm(f"{tag}_o", o, w["w_o"], tm=512, epilogue=o_epi, extras=(w["b_o"],),
            extra_specs=(pl.BlockSpec((1, D), lambda i, j: (0, 0)),))
    return y, dict(h=h, qkv=qkv, o=o)


def _swa_bwd(tag, dy, res, w, tabs):
    rc, rsa, rsb = tabs
    do = _mm(f"{tag}_do", dy, w["w_o_t"], tm=512, out_dtype=BF16)
    g_wo = _mm_tn(f"{tag}_gwo", res["o"], dy)
    dq, dkv, dkvp, dsink = _swa_attn_bwd(f"{tag}_attnbwd", res["qkv"], w["sinks"], do)
    dqkv, g_b = _swa_dqkv(f"{tag}_dqkv", dq, dkv, dkvp, rc, rsa, rsb)
    dh = _mm(f"{tag}_dh", dqkv, w["w_qkv_t"], tm=512)
    g_wqkv = _mm_tn(f"{tag}_gwqkv", res["h"], dqkv, tn=1024)
    return dh, dict(w_qkv=g_wqkv, b_qkv=g_b, sinks=dsink, w_o=g_wo)


def _ada_fwd(name, c_all, w_sh, b_sh):
    cols = w_sh.shape[2]
    tn = cols // 3

    def body(c_ref, w_ref, b_ref, o_ref, cond_ref):
        cv = c_ref[...]
        cond = cv * jax.nn.sigmoid(cv)
        cond_ref[...] = cond
        o_ref[0] = jnp.dot(cond, w_ref[0], preferred_element_type=F32, precision=lax.Precision.HIGHEST) + b_ref[0]

    return pl.pallas_call(
        body, name=name, grid=(DEPTH, cols // tn),
        in_specs=[pl.BlockSpec((8, D), lambda l, j: (0, 0)), pl.BlockSpec((1, D, tn), lambda l, j: (l, 0, j)),
                  pl.BlockSpec((1, 1, tn), lambda l, j: (l, 0, j))],
        out_specs=[pl.BlockSpec((1, 8, tn), lambda l, j: (l, 0, j)), pl.BlockSpec((8, D), lambda l, j: (0, 0))],
        out_shape=[jax.ShapeDtypeStruct((DEPTH, 8, cols), F32), jax.ShapeDtypeStruct((8, D), F32)],
        compiler_params=_cparams("arbitrary", "arbitrary"),
    )(c_all, w_sh, b_sh)


def _ada_grad(name, cond_t, dmod_sh):
    cols = dmod_sh.shape[2]
    tn = cols // 3

    def body(ct_ref, dm_ref, o_ref):
        acc = ct_ref[:, 0:1] * dm_ref[0, 0:1, :]
        for b in range(1, 8):
            acc = acc + ct_ref[:, b:b + 1] * dm_ref[0, b:b + 1, :]
        o_ref[0] = acc

    return pl.pallas_call(
        body, name=name, grid=(DEPTH, cols // tn),
        in_specs=[pl.BlockSpec((D, 8), lambda l, j: (0, 0)), pl.BlockSpec((1, 8, tn), lambda l, j: (l, 0, j))],
        out_specs=pl.BlockSpec((1, D, tn), lambda l, j: (l, 0, j)),
        out_shape=jax.ShapeDtypeStruct((DEPTH, D, cols), F32),
        compiler_params=_cparams("parallel", "parallel"),
    )(cond_t, dmod_sh)


def _adamw(name, g, w, m, v):
    r = g.shape[0]
    tr = min(ROW_TILE, r)

    def body(g_ref, w_ref, m_ref, v_ref, d_ref, nm_ref, nv_ref):
        gv = g_ref[...]
        mn = ADAM_B1 * m_ref[...] + (1.0 - ADAM_B1) * gv
        vn = ADAM_B2 * v_ref[...] + (1.0 - ADAM_B2) * (gv * gv)
        m_hat = mn / (1.0 - ADAM_B1 ** ADAM_STEP)
        v_hat = vn / (1.0 - ADAM_B2 ** ADAM_STEP)
        d_ref[...] = -ADAM_LR * (m_hat / (jnp.sqrt(v_hat) + ADAM_EPS) + ADAM_WD * w_ref[...])
        nm_ref[...] = mn
        nv_ref[...] = vn

    spec = _row_spec(tr, PACK_COLS)
    return pl.pallas_call(
        body, name=name, grid=(r // tr,),
        in_specs=[spec] * 4, out_specs=[spec] * 3,
        out_shape=[jax.ShapeDtypeStruct(g.shape, F32)] * 3,
        compiler_params=_cparams("parallel"),
    )(g, w, m, v)


def _to_chips(full, axis):
    shp = full.shape
    a = full.reshape(shp[:axis] + (N_CHIPS, shp[axis] // N_CHIPS) + shp[axis + 1:])
    return jnp.moveaxis(a, axis, 0)


def _from_chips(stacked, axis):
    a = jnp.moveaxis(stacked, 0, axis)
    shp = a.shape
    return a.reshape(shp[:axis] + (shp[axis] * shp[axis + 1],) + shp[axis + 2:])


PIECE_ROW_ALIGN = 16


def _piece_rows(shape):
    n = 1
    for d in shape:
        n *= d
    rows = -(-n // PACK_COLS)
    return -(-rows // PIECE_ROW_ALIGN) * PIECE_ROW_ALIGN


def _as_rows(a, lead):
    head = a.shape[:lead]
    rows = _piece_rows(a.shape[lead:])
    n = 1
    for d in a.shape[lead:]:
        n *= d
    if n == rows * PACK_COLS:
        return a.reshape(head + (rows, PACK_COLS))
    flat = jnp.pad(a.reshape(head + (n,)), [(0, 0)] * lead + [(0, rows * PACK_COLS - n)])
    return flat.reshape(head + (rows, PACK_COLS))


def _pack(parts, lead, rows):
    pieces = [_as_rows(p, lead) for p in parts]
    used = sum(p.shape[lead] for p in pieces)
    head = pieces[0].shape[:lead]
    pieces.append(jnp.zeros(head + (rows - used, PACK_COLS), pieces[0].dtype))
    return jnp.concatenate(pieces, axis=lead)


def _unpack(packed, lead, shapes):
    out, off = [], 0
    head = packed.shape[:lead]
    for shp in shapes:
        rows = _piece_rows(shp)
        n = 1
        for d in shp:
            n *= d
        piece = lax.slice_in_dim(packed, off, off + rows, axis=lead)
        if n != rows * PACK_COLS:
            piece = piece.reshape(head + (rows * PACK_COLS,))[..., :n]
        out.append(piece.reshape(head + tuple(shp)))
        off += rows
    return out


def _pack_rows(shapes):
    rows = sum(_piece_rows(s) for s in shapes)
    return -(-rows // PACK_ROW_ALIGN) * PACK_ROW_ALIGN


def _rope_tables(positions, rot, lanes_per_head):
    half = rot // 2
    inv = ROPE_THETA ** (-jnp.arange(0, rot, 2, dtype=F32) / rot)
    ang = positions.astype(F32)[:, None] * inv
    cos, sin = jnp.cos(ang), jnp.sin(ang)
    s = positions.shape[0]
    rest = lanes_per_head - rot
    fill = 1.0 if lanes_per_head == SWA_HD else 0.0
    c = jnp.concatenate([cos, cos, jnp.full((s, rest), fill, F32)], axis=1)
    sa = jnp.concatenate([-sin, jnp.zeros((s, half + rest), F32)], axis=1)
    sb = jnp.concatenate([jnp.zeros((s, half), F32), sin, jnp.zeros((s, rest), F32)], axis=1)
    reps = LANES // lanes_per_head
    return tuple(jnp.tile(t, (1, reps)) for t in (c, sa, sb))


def _mla_weights(w_in, q_norm, w_q_b, kv_norm, w_kv_b):
    w_in_p = jnp.pad(w_in, ((0, 0), (0, MLA_LAT - w_in.shape[1])))
    wq = w_q_b.reshape(MLA_QR, MLA_H, MLA_NOPE + MLA_ROPE)
    wq_p = jnp.pad(wq, ((0, 0), (0, 0), (0, 2 * LANES - MLA_NOPE - MLA_ROPE))).reshape(MLA_QR, MLA_H * 2 * LANES)
    return dict(w_in=w_in_p, w_in_t=w_in_p.T, q_norm=q_norm.reshape(1, -1), kv_norm=kv_norm.reshape(1, -1),
                w_q=wq_p, w_q_t=wq_p.T, w_kv=w_kv_b, w_kv_t=w_kv_b.T)


def _mla_grads_unpermute(g):
    gq = g["w_q"].reshape(MLA_QR, MLA_H, 2 * LANES)[:, :, :MLA_NOPE + MLA_ROPE]
    return dict(mla_w_in=g["w_in"][:, :MLA_QR + MLA_KVR + MLA_ROPE], mla_q_norm=g["q_norm"][0],
                mla_w_q_b=gq.reshape(MLA_QR, -1), mla_kv_norm=g["kv_norm"][0], mla_w_kv_b=g["w_kv"],
                mla_w_o=g["w_o"])


def _swa_dup(a):
    lead = a.shape[:-1]
    a = a.reshape(lead + (SWA_HKV, SWA_HD))
    return jnp.concatenate([a, a], axis=-1).reshape(lead + (SWA_KW,))


def _swa_undup(a):
    lead = a.shape[:-1]
    return a.reshape(lead + (SWA_HKV, LANES))[..., :SWA_HD].reshape(lead + (SWA_HKV * SWA_HD,))


def _swa_weights(w_qkv, b_qkv, sinks, w_o, b_o):
    nk = SWA_HKV * SWA_HD
    perm = lambda a: jnp.concatenate([a[..., :SWA_QW], _swa_dup(a[..., SWA_QW:SWA_QW + nk]),
                                      _swa_dup(a[..., SWA_QW + nk:])], axis=-1)
    w_p = perm(w_qkv)
    return dict(w_qkv=w_p, w_qkv_t=w_p.T, b_qkv=perm(b_qkv.astype(F32)).reshape(1, -1),
                sinks=sinks.reshape(1, -1), w_o=w_o, w_o_t=w_o.T, b_o=b_o.astype(F32).reshape(1, -1))


def _swa_grads_unpermute(g):
    unperm = lambda a: jnp.concatenate([a[..., :SWA_QW], _swa_undup(a[..., SWA_QW:SWA_QW + SWA_KW]),
                                        _swa_undup(a[..., SWA_QW + SWA_KW:])], axis=-1)
    return dict(swa_w_qkv=unperm(g["w_qkv"]), swa_b_qkv=unperm(g["b_qkv"])[0], swa_sinks=g["sinks"][:, 0],
                swa_w_o=g["w_o"], swa_b_o=g["b_o"])


SMALL_LAYOUT = (("ada_b", 24), ("ln_mix_g", 4), ("ln_mix_b", 4), ("ln_ffn_g", 4), ("ln_ffn_b", 4),
                ("mla_q_norm", 2), ("mla_kv_norm", 2), ("swa_sinks", 1), ("loss", 1))


def _small_pack(vals):
    rows = []
    for name, nrows in SMALL_LAYOUT:
        a = vals[name].reshape(nrows, -1).astype(F32)
        rows.append(jnp.pad(a, ((0, 0), (0, PACK_COLS - a.shape[1]))))
    cat = jnp.concatenate(rows, axis=0)
    return jnp.pad(cat, ((0, SMALL_ROWS - cat.shape[0]), (0, 0)))


def _small_unpack(packed, shapes):
    out, r = {}, 0
    for name, nrows in SMALL_LAYOUT:
        shp = shapes[name]
        n = 1
        for d in shp:
            n *= d
        out[name] = packed[r:r + nrows, :n // nrows].reshape(shp)
        r += nrows
    return out


def kernel(x, c, positions, ada_w, ada_b, ln_mix_g, ln_mix_b, ln_ffn_g, ln_ffn_b, ffn_w_gate, ffn_w_up, ffn_w_down, mla_w_in, mla_q_norm, mla_w_q_b, mla_kv_norm, mla_w_kv_b, mla_w_o, swa_w_qkv, swa_b_qkv, swa_sinks, swa_w_o, swa_b_o, loss_target, m_ada_w, m_ada_b, m_ln_mix_g, m_ln_mix_b, m_ln_ffn_g, m_ln_ffn_b, m_ffn_w_gate, m_ffn_w_up, m_ffn_w_down, m_mla_w_in, m_mla_q_norm, m_mla_w_q_b, m_mla_kv_norm, m_mla_w_kv_b, m_mla_w_o, m_swa_w_qkv, m_swa_b_qkv, m_swa_sinks, m_swa_w_o, m_swa_b_o, v_ada_w, v_ada_b, v_ln_mix_g, v_ln_mix_b, v_ln_ffn_g, v_ln_ffn_b, v_ffn_w_gate, v_ffn_w_up, v_ffn_w_down, v_mla_w_in, v_mla_q_norm, v_mla_w_q_b, v_mla_kv_norm, v_mla_w_kv_b, v_mla_w_o, v_swa_w_qkv, v_swa_b_qkv, v_swa_sinks, v_swa_w_o, v_swa_b_o):
    weights = dict(ada_w=ada_w, ada_b=ada_b, ln_mix_g=ln_mix_g, ln_mix_b=ln_mix_b, ln_ffn_g=ln_ffn_g,
                   ln_ffn_b=ln_ffn_b, ffn_w_gate=ffn_w_gate, ffn_w_up=ffn_w_up, ffn_w_down=ffn_w_down,
                   mla_w_in=mla_w_in, mla_q_norm=mla_q_norm, mla_w_q_b=mla_w_q_b, mla_kv_norm=mla_kv_norm,
                   mla_w_kv_b=mla_w_kv_b, mla_w_o=mla_w_o, swa_w_qkv=swa_w_qkv, swa_b_qkv=swa_b_qkv,
                   swa_sinks=swa_sinks, swa_w_o=swa_w_o, swa_b_o=swa_b_o)
    mom_m = dict(ada_w=m_ada_w, ada_b=m_ada_b, ln_mix_g=m_ln_mix_g, ln_mix_b=m_ln_mix_b, ln_ffn_g=m_ln_ffn_g,
                 ln_ffn_b=m_ln_ffn_b, ffn_w_gate=m_ffn_w_gate, ffn_w_up=m_ffn_w_up, ffn_w_down=m_ffn_w_down,
                 mla_w_in=m_mla_w_in, mla_q_norm=m_mla_q_norm, mla_w_q_b=m_mla_w_q_b, mla_kv_norm=m_mla_kv_norm,
                 mla_w_kv_b=m_mla_w_kv_b, mla_w_o=m_mla_w_o, swa_w_qkv=m_swa_w_qkv, swa_b_qkv=m_swa_b_qkv,
                 swa_sinks=m_swa_sinks, swa_w_o=m_swa_w_o, swa_b_o=m_swa_b_o)
    mom_v = dict(ada_w=v_ada_w, ada_b=v_ada_b, ln_mix_g=v_ln_mix_g, ln_mix_b=v_ln_mix_b, ln_ffn_g=v_ln_ffn_g,
                 ln_ffn_b=v_ln_ffn_b, ffn_w_gate=v_ffn_w_gate, ffn_w_up=v_ffn_w_up, ffn_w_down=v_ffn_w_down,
                 mla_w_in=v_mla_w_in, mla_q_norm=v_mla_q_norm, mla_w_q_b=v_mla_w_q_b, mla_kv_norm=v_mla_kv_norm,
                 mla_w_kv_b=v_mla_w_kv_b, mla_w_o=v_mla_w_o, swa_w_qkv=v_swa_w_qkv, swa_b_qkv=v_swa_b_qkv,
                 swa_sinks=v_swa_sinks, swa_w_o=v_swa_w_o, swa_b_o=v_swa_b_o)
    names = list(weights)
    my_x, my_y, my_c = lax.axis_index("x"), lax.axis_index("y"), lax.axis_index("c")
    chip = 2 * my_x + my_y
    batch_row = 2 * chip + my_c
    xs = x[0]
    target = loss_target[0]
    pos = positions[0]
    s = xs.shape[0]

    def item_shapes(items):
        return [(b - a,) + tuple(weights[n].shape[1:]) for n, a, b, _ in items]

    def pack_items(src, items, dtype):
        return _pack([src[n][a:b].astype(dtype) for n, a, b, _ in items], 0, _pack_rows(item_shapes(items)))

    full = {}

    def unpack_gathered(gathered, items):
        for (n, a, b, axis), part in zip(items, _unpack(gathered, 1, item_shapes(items))):
            whole = _from_chips(part, axis)
            for l in range(a, b):
                full[n, l] = whole[l - a]

    early = _exchange("ag_w_early", pack_items(weights, W_EARLY, BF16), ("x", "y"), "gather")
    unpack_gathered(early, W_EARLY)
    late_ride = _Exchange(pack_items(weights, W_LATE, BF16), ("x", "y"), "gather", chunks=8)

    c_rows = jnp.pad(c, ((0, 7), (0, 0)))
    c_all = _exchange("ag_c", c_rows, ("x", "y", "c"), "gather")[:, 0, :]
    ada_cols = ada_w.shape[2]
    ada_b_sh = lax.dynamic_slice_in_dim(ada_b, chip * ada_cols, ada_cols, axis=1).reshape(DEPTH, 1, ada_cols)
    mod_sh, cond_all = _ada_fwd("ada_fwd", c_all, ada_w, ada_b_sh)
    mod_all = _exchange("ag_mod", mod_sh.reshape(DEPTH * 8, ada_cols), ("x", "y"), "gather")
    mod_all = mod_all.reshape(N_CHIPS, DEPTH, 8, ada_cols)
    mod_mine = lax.dynamic_index_in_dim(mod_all, batch_row, axis=2, keepdims=False)
    mod = jnp.moveaxis(mod_mine, 0, 1).reshape(DEPTH, 6, 1, D)

    tabs_a = _rope_tables(pos, MLA_ROPE, LANES)
    tabs_b = _rope_tables(pos, SWA_ROT, SWA_HD)
    vec = lambda a, l: a[l].reshape(1, D)

    def mla_in_weights(j):
        return _mla_weights(full["mla_w_in", j], mla_q_norm[j], full["mla_w_q_b", j], mla_kv_norm[j],
                            full["mla_w_kv_b", j])

    mix_w, ffn_w = {}, {}
    saved = []
    x_cur = xs
    h = _modulate("mod0", x_cur, mod[0, 1], mod[0, 0])
    for l in range(DEPTH):
        j = l // 2
        if l % 2 == 0:
            mix_w[l] = mla_in_weights(j)
            o, res, ridden = _mla_fwd(f"mla{l}", h, mix_w[l], tabs_a, late_ride if l == 0 else None)
            if l == 0:
                unpack_gathered(ridden, W_LATE)
            mix_w[l].update(w_o=full["mla_w_o", j], w_o_t=full["mla_w_o", j].T)
            y_mix = _mm(f"mla{l}_o", o, mix_w[l]["w_o"], tm=512)
        else:
            mix_w[l] = _swa_weights(full["swa_w_qkv", j], full["swa_b_qkv", j], swa_sinks[j], full["swa_w_o", j],
                                    full["swa_b_o", j])
            y_mix, res = _swa_fwd(f"swa{l}", h, mix_w[l], tabs_b)
        wg, wu, wd = full["ffn_w_gate", l], full["ffn_w_up", l], full["ffn_w_down", l]
        ffn_w[l] = dict(wg=wg, wu=wu, wd=wd, wg_t=wg.T, wu_t=wu.T, wd_t=wd.T)
        x_mid, h2 = _post_mod(f"post_mix{l}", x_cur, y_mix, mod[l, 2], vec(ln_mix_g, l), vec(ln_mix_b, l),
                              mod[l, 4], mod[l, 3])
        gate, up, y_ffn = _ffn_fwd(f"ffn{l}", h2, ffn_w[l]["wg"], ffn_w[l]["wu"], ffn_w[l]["wd"])
        saved.append(dict(x_in=x_cur, y_mix=y_mix, res=res, x_mid=x_mid, h2=h2, gate=gate, up=up, y_ffn=y_ffn))
        if l < DEPTH - 1:
            x_cur, h = _post_mod(f"post_ffn{l}", x_mid, y_ffn, mod[l, 5], vec(ln_ffn_g, l), vec(ln_ffn_b, l),
                                 mod[l + 1, 1], mod[l + 1, 0])
        else:
            dxn, loss_part = _post_loss("post_loss", x_mid, y_ffn, mod[l, 5], vec(ln_ffn_g, l), vec(ln_ffn_b, l),
                                        target)

    gfull = {n: [None] * weights[n].shape[0] for n, _ in SHARDED}
    gsmall = {n: [None] * weights[n].shape[0] for n in ("ln_mix_g", "ln_mix_b", "ln_ffn_g", "ln_ffn_b",
                                                         "mla_q_norm", "mla_kv_norm", "swa_sinks")}
    dmod = [None] * DEPTH

    def grad_ride(items):
        parts = [_to_chips(jnp.stack(gfull[n][a:b]).astype(BF16), axis) for n, a, b, axis in items]
        return _Exchange(_pack(parts, 1, _pack_rows(item_shapes(items))), ("x", "y", "c"), "to_chip", chunks=4)

    def ride_with_wo(items, j):
        def make(g_wo):
            gfull["mla_w_o"][j] = g_wo
            return grad_ride(items)
        return make

    rides = {DEPTH - 2: G_FIRST, 0: G_SECOND}
    g_parts = {}
    for l in reversed(range(DEPTH)):
        sv = saved[l]
        j = l // 2
        dxp, dy, sums_f = _post_bwd(f"post_ffn_bwd{l}", dxn, sv["x_mid"], sv["y_ffn"], mod[l, 5], vec(ln_ffn_g, l))
        dgt, dup, act, dh2 = _ffn_bwd(f"ffn_bwd{l}", dy, sv["gate"], sv["up"], ffn_w[l]["wd_t"], ffn_w[l]["wg_t"],
                                      ffn_w[l]["wu_t"])
        gfull["ffn_w_gate"][l] = _mm_tn(f"ffn_gwg{l}", sv["h2"], dgt, tn=FF // 2)
        gfull["ffn_w_up"][l] = _mm_tn(f"ffn_gwu{l}", sv["h2"], dup, tn=FF // 2)
        gfull["ffn_w_down"][l] = _mm_tn(f"ffn_gwd{l}", act, dy)
        dx_mid, sums_fm = _mod_bwd(f"mod_ffn_bwd{l}", dh2, sv["x_mid"], dxp, mod[l, 4])
        dxp, dy, sums_m = _post_bwd(f"post_mix_bwd{l}", dx_mid, sv["x_in"], sv["y_mix"], mod[l, 2], vec(ln_mix_g, l))
        if l % 2 == 0:
            dh, g, ridden = _mla_bwd(f"mla{l}", dy, sv["res"], mix_w[l], tabs_a, ride_with_wo(rides[l], j))
            g_parts[rides[l]] = _sum_groups(f"rs_sum{l}", ridden)
            g = _mla_grads_unpermute(g)
        else:
            dh, g = _swa_bwd(f"swa{l}", dy, sv["res"], mix_w[l], tabs_b)
            g["b_o"] = sums_m[3]
            g = _swa_grads_unpermute(g)
        for n, val in g.items():
            (gfull if n in gfull else gsmall)[n][j] = val
        dxn, sums_mm = _mod_bwd(f"mod_mix_bwd{l}", dh, sv["x_in"], dxp, mod[l, 1])
        gsmall["ln_ffn_g"][l], gsmall["ln_ffn_b"][l] = sums_f[0], sums_f[1]
        gsmall["ln_mix_g"][l], gsmall["ln_mix_b"][l] = sums_m[0], sums_m[1]
        dmod[l] = jnp.stack([sums_mm[1], sums_mm[0], sums_m[2], sums_fm[1], sums_fm[0], sums_f[2]])
    grad_x = dxn[None]

    small_vals = {n: jnp.stack(v) for n, v in gsmall.items()}
    small_vals["ada_b"] = jnp.stack(dmod)
    small_vals["loss"] = loss_part[0, 0:1]
    small_all = _exchange("ag_small", _small_pack(small_vals), ("x", "y", "c"), "gather")
    small_sum = _sum_groups("sum_small", small_all)
    dmod_all = small_all[:, :DEPTH * 6, :].reshape(8, DEPTH, 6 * D)
    dmod_sh = jnp.moveaxis(lax.dynamic_slice_in_dim(dmod_all, chip * ada_cols, ada_cols, axis=2), 0, 1)
    g_ada_w = _ada_grad("ada_grad", cond_all.T, dmod_sh)

    tail = grad_ride(G_LAST)
    g_parts[G_LAST] = _sum_groups("rs_sum_tail", _exchange("rs_tail", tail.src, tail.axes, tail.mode, tail.chunks))

    groups = (G_FIRST, G_SECOND, G_LAST)
    ada_rows = int(ada_w.size) // PACK_COLS
    small_shapes = {n: weights[n].shape for n, _ in SMALL_LAYOUT if n != "loss"}
    small_shapes["loss"] = (1,)

    def flat_all(src):
        sharded = [pack_items(src, items, F32) for items in groups]
        small = _small_pack({**{n: src[n] for n in small_shapes if n != "loss"}, "loss": jnp.zeros((1,), F32)})
        return jnp.concatenate([*sharded, src["ada_w"].reshape(ada_rows, PACK_COLS), small], axis=0)

    g_flat = jnp.concatenate([*[g_parts[items] for items in groups], g_ada_w.reshape(ada_rows, PACK_COLS),
                              small_sum], axis=0)
    delta, new_m, new_v = _adamw("adamw", g_flat, flat_all(weights), flat_all(mom_m), flat_all(mom_v))

    def split_all(flat):
        pieces, off = {}, 0
        for items in groups:
            rows = _pack_rows(item_shapes(items))
            for (n, a, _, _), part in zip(items, _unpack(flat[off:off + rows], 0, item_shapes(items))):
                pieces.setdefault(n, []).append((a, part))
            off += rows
        out = {n: jnp.concatenate([p for _, p in sorted(ps, key=lambda ap: ap[0])], axis=0)
               for n, ps in pieces.items()}
        out["ada_w"] = flat[off:off + ada_rows].reshape(ada_w.shape)
        out.update(_small_unpack(flat[off + ada_rows:], small_shapes))
        return out

    outs = [split_all(a) for a in (g_flat, delta, new_m, new_v)]
    loss = outs[0]["loss"][0]
    return (loss, grad_x, *[o[n] for o in outs for n in names])
```

```python
import jax
import jax.numpy as jnp
from jax import lax
from jax.experimental import pallas as pl
from jax.experimental.pallas import tpu as pltpu

F32 = jnp.float32
BF16 = jnp.bfloat16

D = 1024
DEPTH = 4
ROPE_THETA = 500000.0
LN_EPS = 1e-5
RMS_EPS = 1e-6
MLA_H = 8
MLA_NOPE = 128
MLA_ROPE = 64
MLA_V = 128
MLA_QR = 384
MLA_KVR = 256
MLA_LAT = 768
SWA_HQ = 16
SWA_HKV = 4
SWA_HD = 64
SWA_W = 128
SWA_ROT = 16
FF = 2816
ALPHA = (2 * DEPTH) ** 0.25
ADAM_LR = 0.001
ADAM_B1 = 0.9
ADAM_B2 = 0.999
ADAM_EPS = 1e-08
ADAM_WD = 0.01
ADAM_STEP = 10
NEG = -1e30
LANES = 128
N_CHIPS = 4
PACK_COLS = 1024
PACK_ROW_ALIGN = 512
SMALL_ROWS = 512
ROW_TILE = 512

SHARDED = (
    ("ffn_w_gate", 2), ("ffn_w_up", 2), ("ffn_w_down", 1), ("mla_w_in", 1), ("mla_w_q_b", 2),
    ("mla_w_kv_b", 2), ("mla_w_o", 1), ("swa_w_qkv", 2), ("swa_b_qkv", 1), ("swa_w_o", 1), ("swa_b_o", 1),
)


def _items(*specs):
    axis = dict(SHARDED)
    return tuple((n, a, b, axis[n]) for names, a, b in specs for n in names)


_FFN = ("ffn_w_gate", "ffn_w_up", "ffn_w_down")
_SWA = ("swa_w_qkv", "swa_b_qkv", "swa_w_o", "swa_b_o")
_MLA_IN = ("mla_w_in", "mla_w_q_b", "mla_w_kv_b")
_MLA_OUT = ("mla_w_o",)
W_EARLY = _items((_MLA_IN, 0, 1))
W_LATE = _items((_FFN, 0, 4), (_MLA_IN, 1, 2), (_MLA_OUT, 0, 2), (_SWA, 0, 2))
G_FIRST = _items((_FFN, 3, 4), (_SWA, 1, 2), (_FFN, 2, 3), (_MLA_OUT, 1, 2))
G_SECOND = _items((_MLA_IN, 1, 2), (_FFN, 1, 2), (_SWA, 0, 1), (_FFN, 0, 1), (_MLA_OUT, 0, 1))
G_LAST = _items((_MLA_IN, 0, 1))


def _cparams(*sem):
    return pltpu.CompilerParams(dimension_semantics=sem)


def _row_spec(tr, cols):
    return pl.BlockSpec((tr, cols), lambda i: (i, 0))


def _vec_spec(cols):
    return pl.BlockSpec((1, cols), lambda i: (0, 0))


def _rope(x, c, sa, sb, sh):
    n = x.shape[1]
    return x * c + pltpu.roll(x, n - sh, 1) * sa + pltpu.roll(x, sh, 1) * sb


def _rope_t(d, c, sa, sb, sh):
    n = d.shape[1]
    return d * c + pltpu.roll(d * sa, sh, 1) + pltpu.roll(d * sb, n - sh, 1)


def _rowsum8(t):
    r, n = t.shape
    return jnp.sum(t.reshape(r // 8, 8, n), axis=0)


class _Exchange:
    def __init__(self, src, axes, mode, chunks=1):
        self.src, self.axes, self.mode, self.chunks = src, axes, mode, chunks
        self.g = 2 ** len(axes)
        self.blk = tuple(src.shape if mode == "gather" else src.shape[1:])
        self.out_shape = jax.ShapeDtypeStruct((self.g,) + self.blk, src.dtype)
        nsem = (self.g - 1) * chunks
        self.scratch = [pltpu.SemaphoreType.DMA((nsem,)), pltpu.SemaphoreType.DMA((nsem,)),
                        pltpu.SemaphoreType.DMA(())]

    def copies(self, src_ref, out_ref, send_sems, recv_sems, loc_sem):
        pos = {a: lax.axis_index(a) for a in ("x", "y", "c")}
        rows = self.blk[0] // self.chunks

        def gidx(p):
            idx = 0
            for a in self.axes:
                idx = idx * 2 + p[a]
            return idx

        def view(p):
            if self.mode == "gather":
                return src_ref
            if self.mode == "to_chip":
                return src_ref.at[2 * p["x"] + p["y"]]
            return src_ref.at[gidx(p)]

        me = gidx(pos)
        out = [pltpu.make_async_copy(view(pos), out_ref.at[me], loc_sem)]
        for k in range(self.chunks):
            piece = pl.ds(k * rows, rows)
            for j in range(1, self.g):
                peer = dict(pos)
                for bit, a in enumerate(reversed(self.axes)):
                    if (j >> bit) & 1:
                        peer[a] = 1 - pos[a]
                sem = (j - 1) * self.chunks + k
                out.append(pltpu.make_async_remote_copy(
                    src_ref=view(peer).at[piece], dst_ref=out_ref.at[me, piece],
                    send_sem=send_sems.at[sem], recv_sem=recv_sems.at[sem],
                    device_id=(peer["x"], peer["y"], peer["c"]), device_id_type=pl.DeviceIdType.MESH))
        return out


def _exchange(name, src, axes, mode, chunks=1):
    ex = _Exchange(src, axes, mode, chunks)

    def body(src_ref, out_ref, send_sems, recv_sems, loc_sem):
        copies = ex.copies(src_ref, out_ref, send_sems, recv_sems, loc_sem)
        for cp in copies:
            cp.start()
        for cp in copies:
            cp.wait()

    return pl.pallas_call(
        body, name=name, out_shape=ex.out_shape,
        in_specs=[pl.BlockSpec(memory_space=pl.ANY)],
        out_specs=pl.BlockSpec(memory_space=pl.ANY),
        scratch_shapes=ex.scratch,
    )(src)


def _tri_call(body, rider, *, name, grid, tables, in_specs, out_specs, out_shape, scratch_shapes, operands):
    n_tab, n_in, n_out, n_scr = len(tables), len(in_specs), len(out_specs), len(scratch_shapes)
    in_specs, out_specs, out_shape = list(in_specs), list(out_specs), list(out_shape)
    scratch_shapes, operands = list(scratch_shapes), list(operands)
    if rider is not None:
        any_spec = pl.BlockSpec(memory_space=pl.ANY)
        in_specs.append(any_spec)
        out_specs.append(any_spec)
        out_shape.append(rider.out_shape)
        scratch_shapes.extend(rider.scratch)
        operands.append(rider.src)
        first, last = _grid_ends(grid)

    def wrapped(*refs):
        tabs, refs = refs[:n_tab], refs[n_tab:]
        if rider is None:
            return body(*tabs, *refs)
        ins, src_ref = refs[:n_in], refs[n_in]
        outs, out_ref = refs[n_in + 1:n_in + 1 + n_out], refs[n_in + 1 + n_out]
        scr = refs[n_in + 2 + n_out:n_in + 2 + n_out + n_scr]
        sems = refs[n_in + 2 + n_out + n_scr:]

        @pl.when(first())
        def _():
            for cp in rider.copies(src_ref, out_ref, *sems):
                cp.start()

        body(*tabs, *ins, *outs, *scr)

        @pl.when(last())
        def _():
            for cp in rider.copies(src_ref, out_ref, *sems):
                cp.wait()

    return pl.pallas_call(
        wrapped, name=name, out_shape=out_shape,
        grid_spec=pltpu.PrefetchScalarGridSpec(num_scalar_prefetch=n_tab, grid=grid, in_specs=in_specs,
                                               out_specs=out_specs, scratch_shapes=scratch_shapes),
        compiler_params=_cparams(*(["arbitrary"] * len(grid))),
    )(*tables, *operands)


def _tri_tables(n, queries_outer):
    if queries_outer:
        pairs = [(qi, ki) for qi in range(n) for ki in range(qi + 1)]
    else:
        pairs = [(qi, ki) for ki in range(n) for qi in range(ki, n)]
    return (jnp.asarray([p[0] for p in pairs], jnp.int32), jnp.asarray([p[1] for p in pairs], jnp.int32))


def _sum_groups(name, a):
    g, r, c = a.shape
    tr = min(ROW_TILE, r)

    def body(a_ref, o_ref):
        acc = a_ref[0].astype(F32)
        for i in range(1, g):
            acc = acc + a_ref[i].astype(F32)
        o_ref[...] = acc

    return pl.pallas_call(
        body, name=name, grid=(r // tr,),
        in_specs=[pl.BlockSpec((g, tr, c), lambda i: (0, i, 0))],
        out_specs=pl.BlockSpec((tr, c), lambda i: (i, 0)),
        out_shape=jax.ShapeDtypeStruct((r, c), F32),
        compiler_params=_cparams("parallel"),
    )(a)


def _mm(name, a, b, *, tm, tn=None, out_dtype=F32, epilogue=None, extras=(), extra_specs=()):
    m, k = a.shape
    n = b.shape[1]
    tn = tn or n
    tm = min(tm, m)

    def body(a_ref, b_ref, *rest):
        o_ref = rest[-1]
        acc = jnp.dot(a_ref[...], b_ref[...], preferred_element_type=F32)
        if epilogue is None:
            o_ref[...] = acc.astype(o_ref.dtype)
        else:
            epilogue(acc, o_ref, *rest[:-1])

    return pl.pallas_call(
        body, name=name, grid=(m // tm, n // tn),
        in_specs=[pl.BlockSpec((tm, k), lambda i, j: (i, 0)), pl.BlockSpec((k, tn), lambda i, j: (0, j)),
                  *extra_specs],
        out_specs=pl.BlockSpec((tm, tn), lambda i, j: (i, j)),
        out_shape=jax.ShapeDtypeStruct((m, n), out_dtype),
        compiler_params=_cparams("parallel", "parallel"),
    )(a, b, *extras)


def _mm_tn(name, a, b, *, tn=None, tk=1024):
    s, m = a.shape
    n = b.shape[1]
    tn = tn or n
    tk = min(tk, s)

    def body(a_ref, b_ref, o_ref):
        part = lax.dot_general(a_ref[...], b_ref[...], (((0,), (0,)), ((), ())), preferred_element_type=F32)

        @pl.when(pl.program_id(1) == 0)
        def _():
            o_ref[...] = part

        @pl.when(pl.program_id(1) > 0)
        def _():
            o_ref[...] += part

    return pl.pallas_call(
        body, name=name, grid=(n // tn, s // tk),
        in_specs=[pl.BlockSpec((tk, m), lambda j, k: (k, 0)), pl.BlockSpec((tk, tn), lambda j, k: (k, j))],
        out_specs=pl.BlockSpec((m, tn), lambda j, k: (0, j)),
        out_shape=jax.ShapeDtypeStruct((m, n), F32),
        compiler_params=_cparams("parallel", "arbitrary"),
    )(a, b)


def _modulate(name, x, sc, sh):
    s = x.shape[0]
    tr = min(ROW_TILE, s)

    def body(x_ref, sc_ref, sh_ref, h_ref):
        h_ref[...] = (x_ref[...] * (1.0 + sc_ref[...]) + sh_ref[...]).astype(BF16)

    return pl.pallas_call(
        body, name=name, grid=(s // tr,),
        in_specs=[_row_spec(tr, D), _vec_spec(D), _vec_spec(D)],
        out_specs=_row_spec(tr, D),
        out_shape=jax.ShapeDtypeStruct((s, D), BF16),
        compiler_params=_cparams("parallel"),
    )(x, sc, sh)


def _ln_stats(z):
    mu = jnp.mean(z, axis=1, keepdims=True)
    zc = z - mu
    var = jnp.mean(zc * zc, axis=1, keepdims=True)
    r = lax.rsqrt(var + LN_EPS)
    return zc * r, r


def _post_mod(name, x, y, g, gamma, beta, sc, sh):
    s = x.shape[0]
    tr = min(ROW_TILE, s)

    def body(x_ref, y_ref, g_ref, ga_ref, be_ref, sc_ref, sh_ref, xn_ref, h_ref):
        zh, _ = _ln_stats(ALPHA * x_ref[...] + g_ref[...] * y_ref[...])
        xn = zh * ga_ref[...] + be_ref[...]
        xn_ref[...] = xn
        h_ref[...] = (xn * (1.0 + sc_ref[...]) + sh_ref[...]).astype(BF16)

    return pl.pallas_call(
        body, name=name, grid=(s // tr,),
        in_specs=[_row_spec(tr, D), _row_spec(tr, D)] + [_vec_spec(D)] * 5,
        out_specs=[_row_spec(tr, D), _row_spec(tr, D)],
        out_shape=[jax.ShapeDtypeStruct((s, D), F32), jax.ShapeDtypeStruct((s, D), BF16)],
        compiler_params=_cparams("parallel"),
    )(x, y, g, gamma, beta, sc, sh)


def _post_loss(name, x, y, g, gamma, beta, target):
    s = x.shape[0]
    tr = min(ROW_TILE, s)
    nt = s // tr

    def body(x_ref, y_ref, g_ref, ga_ref, be_ref, t_ref, dx_ref, loss_ref, acc):
        i = pl.program_id(0)
        zh, _ = _ln_stats(ALPHA * x_ref[...] + g_ref[...] * y_ref[...])
        e = zh * ga_ref[...] + be_ref[...] - t_ref[...]
        dx_ref[...] = e * (1.0 / D)

        @pl.when(i == 0)
        def _():
            acc[...] = jnp.zeros_like(acc)

        acc[...] += _rowsum8(e * e)

        @pl.when(i == nt - 1)
        def _():
            loss_ref[...] = jnp.full(loss_ref.shape, jnp.sum(acc[...]) * (0.5 / D), F32)

    return pl.pallas_call(
        body, name=name, grid=(nt,),
        in_specs=[_row_spec(tr, D), _row_spec(tr, D)] + [_vec_spec(D)] * 3 + [_row_spec(tr, D)],
        out_specs=[_row_spec(tr, D), pl.BlockSpec((8, LANES), lambda i: (0, 0))],
        out_shape=[jax.ShapeDtypeStruct((s, D), F32), jax.ShapeDtypeStruct((8, LANES), F32)],
        scratch_shapes=[pltpu.VMEM((8, D), F32)],
        compiler_params=_cparams("arbitrary"),
    )(x, y, g, gamma, beta, target)


def _post_bwd(name, dxn, x, y, g, gamma):
    s = x.shape[0]
    tr = min(ROW_TILE, s)
    nt = s // tr

    def body(d_ref, x_ref, y_ref, g_ref, ga_ref, dxp_ref, dy_ref, sums_ref, a0, a1, a2, a3):
        i = pl.program_id(0)
        yv = y_ref[...]
        gv = g_ref[...]
        zh, r = _ln_stats(ALPHA * x_ref[...] + gv * yv)
        dxn_v = d_ref[...]
        dzh = dxn_v * ga_ref[...]
        dz = r * (dzh - jnp.mean(dzh, axis=1, keepdims=True) - zh * jnp.mean(dzh * zh, axis=1, keepdims=True))
        dxp_ref[...] = ALPHA * dz
        dyv = gv * dz
        dy_ref[...] = dyv.astype(BF16)

        @pl.when(i == 0)
        def _():
            for a in (a0, a1, a2, a3):
                a[...] = jnp.zeros_like(a)

        a0[...] += _rowsum8(dxn_v * zh)
        a1[...] += _rowsum8(dxn_v)
        a2[...] += _rowsum8(dz * yv)
        a3[...] += _rowsum8(dyv)

        @pl.when(i == nt - 1)
        def _():
            for k, a in enumerate((a0, a1, a2, a3)):
                sums_ref[k:k + 1, :] = jnp.sum(a[...], axis=0, keepdims=True)

    return pl.pallas_call(
        body, name=name, grid=(nt,),
        in_specs=[_row_spec(tr, D)] * 3 + [_vec_spec(D)] * 2,
        out_specs=[_row_spec(tr, D), _row_spec(tr, D), pl.BlockSpec((4, D), lambda i: (0, 0))],
        out_shape=[jax.ShapeDtypeStruct((s, D), F32), jax.ShapeDtypeStruct((s, D), BF16),
                   jax.ShapeDtypeStruct((4, D), F32)],
        scratch_shapes=[pltpu.VMEM((8, D), F32)] * 4,
        compiler_params=_cparams("arbitrary"),
    )(dxn, x, y, g, gamma)


def _mod_post_bwd(name, dh, dxp, x, y, g, gamma, beta, sc):
    s = x.shape[0]
    tr = min(ROW_TILE, s)
    nt = s // tr

    def body(dh_ref, dxp_ref, x_ref, y_ref, g_ref, ga_ref, be_ref, sc_ref, dxo_ref, dy_ref, sums_ref, *acc):
        i = pl.program_id(0)
        yv = y_ref[...]
        gv = g_ref[...]
        zh, r = _ln_stats(ALPHA * x_ref[...] + gv * yv)
        xn = zh * ga_ref[...] + be_ref[...]
        dhv = dh_ref[...]
        dxn_v = dxp_ref[...] + dhv * (1.0 + sc_ref[...])
        dzh = dxn_v * ga_ref[...]
        dz = r * (dzh - jnp.mean(dzh, axis=1, keepdims=True) - zh * jnp.mean(dzh * zh, axis=1, keepdims=True))
        dxo_ref[...] = ALPHA * dz
        dyv = gv * dz
        dy_ref[...] = dyv.astype(BF16)

        @pl.when(i == 0)
        def _():
            for a in acc:
                a[...] = jnp.zeros_like(a)

        for a, val in zip(acc, (dhv * xn, dhv, dxn_v * zh, dxn_v, dz * yv, dyv)):
            a[...] += _rowsum8(val)

        @pl.when(i == nt - 1)
        def _():
            for k, a in enumerate(acc):
                sums_ref[k:k + 1, :] = jnp.sum(a[...], axis=0, keepdims=True)

    return pl.pallas_call(
        body, name=name, grid=(nt,),
        in_specs=[_row_spec(tr, D)] * 4 + [_vec_spec(D)] * 4,
        out_specs=[_row_spec(tr, D), _row_spec(tr, D), pl.BlockSpec((6, D), lambda i: (0, 0))],
        out_shape=[jax.ShapeDtypeStruct((s, D), F32), jax.ShapeDtypeStruct((s, D), BF16),
                   jax.ShapeDtypeStruct((6, D), F32)],
        scratch_shapes=[pltpu.VMEM((8, D), F32)] * 6,
        compiler_params=_cparams("arbitrary"),
    )(dh, dxp, x, y, g, gamma, beta, sc)


def _mod_bwd(name, dh, x, dxp, sc):
    s = x.shape[0]
    tr = min(ROW_TILE, s)
    nt = s // tr

    def body(dh_ref, x_ref, dxp_ref, sc_ref, dx_ref, sums_ref, a0, a1):
        i = pl.program_id(0)
        dhv = dh_ref[...]
        dx_ref[...] = dxp_ref[...] + dhv * (1.0 + sc_ref[...])

        @pl.when(i == 0)
        def _():
            a0[...] = jnp.zeros_like(a0)
            a1[...] = jnp.zeros_like(a1)

        a0[...] += _rowsum8(dhv * x_ref[...])
        a1[...] += _rowsum8(dhv)

        @pl.when(i == nt - 1)
        def _():
            sums_ref[0:1, :] = jnp.sum(a0[...], axis=0, keepdims=True)
            sums_ref[1:2, :] = jnp.sum(a1[...], axis=0, keepdims=True)

    return pl.pallas_call(
        body, name=name, grid=(nt,),
        in_specs=[_row_spec(tr, D)] * 3 + [_vec_spec(D)],
        out_specs=[_row_spec(tr, D), pl.BlockSpec((2, D), lambda i: (0, 0))],
        out_shape=[jax.ShapeDtypeStruct((s, D), F32), jax.ShapeDtypeStruct((2, D), F32)],
        scratch_shapes=[pltpu.VMEM((8, D), F32)] * 2,
        compiler_params=_cparams("arbitrary"),
    )(dh, x, dxp, sc)


def _ffn_tiles(s):
    return min(ROW_TILE, s), FF // 2


def _ffn_fwd(name, h, wg, wu, wd):
    s = h.shape[0]
    tm, tf = _ffn_tiles(s)

    def body(h_ref, wg_ref, wu_ref, wd_ref, gate_ref, up_ref, y_ref):
        hv = h_ref[...]
        gt = jnp.dot(hv, wg_ref[...], preferred_element_type=F32)
        up = jnp.dot(hv, wu_ref[...], preferred_element_type=F32)
        gate_ref[...] = gt
        up_ref[...] = up
        act = (gt * jax.nn.sigmoid(gt) * up).astype(BF16)
        part = jnp.dot(act, wd_ref[...], preferred_element_type=F32)

        @pl.when(pl.program_id(1) == 0)
        def _():
            y_ref[...] = part

        @pl.when(pl.program_id(1) > 0)
        def _():
            y_ref[...] += part

    return pl.pallas_call(
        body, name=name, grid=(s // tm, FF // tf),
        in_specs=[pl.BlockSpec((tm, D), lambda i, f: (i, 0)), pl.BlockSpec((D, tf), lambda i, f: (0, f)),
                  pl.BlockSpec((D, tf), lambda i, f: (0, f)), pl.BlockSpec((tf, D), lambda i, f: (f, 0))],
        out_specs=[pl.BlockSpec((tm, tf), lambda i, f: (i, f)), pl.BlockSpec((tm, tf), lambda i, f: (i, f)),
                   pl.BlockSpec((tm, D), lambda i, f: (i, 0))],
        out_shape=[jax.ShapeDtypeStruct((s, FF), F32), jax.ShapeDtypeStruct((s, FF), F32),
                   jax.ShapeDtypeStruct((s, D), F32)],
        compiler_params=_cparams("parallel", "arbitrary"),
    )(h, wg, wu, wd)


def _ffn_bwd(name, dy, gate, up, wd_t, wg_t, wu_t):
    s = dy.shape[0]
    tm, tf = _ffn_tiles(s)

    def body(dy_ref, gate_ref, up_ref, wdt_ref, wgt_ref, wut_ref, dg_ref, du_ref, act_ref, dh_ref):
        dact = jnp.dot(dy_ref[...], wdt_ref[...], preferred_element_type=F32)
        gt = gate_ref[...]
        up = up_ref[...]
        sig = jax.nn.sigmoid(gt)
        silu = gt * sig
        dgt = (dact * up * (sig * (1.0 + gt * (1.0 - sig)))).astype(BF16)
        dup = (dact * silu).astype(BF16)
        dg_ref[...] = dgt
        du_ref[...] = dup
        act_ref[...] = (silu * up).astype(BF16)
        part = (jnp.dot(dgt, wgt_ref[...], preferred_element_type=F32)
                + jnp.dot(dup, wut_ref[...], preferred_element_type=F32))

        @pl.when(pl.program_id(1) == 0)
        def _():
            dh_ref[...] = part

        @pl.when(pl.program_id(1) > 0)
        def _():
            dh_ref[...] += part

    tile = pl.BlockSpec((tm, tf), lambda i, f: (i, f))
    return pl.pallas_call(
        body, name=name, grid=(s // tm, FF // tf),
        in_specs=[pl.BlockSpec((tm, D), lambda i, f: (i, 0)), tile, tile,
                  pl.BlockSpec((D, tf), lambda i, f: (0, f)), pl.BlockSpec((tf, D), lambda i, f: (f, 0)),
                  pl.BlockSpec((tf, D), lambda i, f: (f, 0))],
        out_specs=[tile, tile, tile, pl.BlockSpec((tm, D), lambda i, f: (i, 0))],
        out_shape=[jax.ShapeDtypeStruct((s, FF), BF16)] * 3 + [jax.ShapeDtypeStruct((s, D), F32)],
        compiler_params=_cparams("parallel", "arbitrary"),
    )(dy, gate, up, wd_t, wg_t, wu_t)


def _mla_lat_post(name, lat, qw, kvw, rc, rsa, rsb):
    s = lat.shape[0]
    tr = min(ROW_TILE, s)

    def body(lat_ref, qw_ref, kvw_ref, c_ref, sa_ref, sb_ref, qn_ref, kvn_ref, kr_ref):
        ql = lat_ref[:, 0:MLA_QR]
        kl = lat_ref[:, MLA_QR:MLA_QR + MLA_KVR]
        qn_ref[...] = (ql * lax.rsqrt(jnp.mean(ql * ql, axis=1, keepdims=True) + RMS_EPS) * qw_ref[...]).astype(BF16)
        kvn_ref[...] = (kl * lax.rsqrt(jnp.mean(kl * kl, axis=1, keepdims=True) + RMS_EPS) * kvw_ref[...]).astype(BF16)
        kr_ref[...] = _rope(lat_ref[:, MLA_QR + MLA_KVR:MLA_LAT], c_ref[...], sa_ref[...], sb_ref[...],
                            MLA_ROPE // 2).astype(BF16)

    return pl.pallas_call(
        body, name=name, grid=(s // tr,),
        in_specs=[_row_spec(tr, MLA_LAT), _vec_spec(MLA_QR), _vec_spec(MLA_KVR)] + [_row_spec(tr, LANES)] * 3,
        out_specs=[_row_spec(tr, MLA_QR), _row_spec(tr, MLA_KVR), _row_spec(tr, LANES)],
        out_shape=[jax.ShapeDtypeStruct((s, MLA_QR), BF16), jax.ShapeDtypeStruct((s, MLA_KVR), BF16),
                   jax.ShapeDtypeStruct((s, LANES), BF16)],
        compiler_params=_cparams("parallel"),
    )(lat, qw, kvw, rc, rsa, rsb)


def _mla_lat_bwd(name, lat, dqn, dkvn, dkr_heads, qw, kvw, rc, rsa, rsb):
    s = lat.shape[0]
    tr = min(ROW_TILE, s)
    nt = s // tr

    def rms_bwd(x, w, dy):
        r = lax.rsqrt(jnp.mean(x * x, axis=1, keepdims=True) + RMS_EPS)
        xh = x * r
        gdy = dy * w
        return r * (gdy - xh * jnp.mean(gdy * xh, axis=1, keepdims=True)), dy * xh

    def body(lat_ref, dqn_ref, dkvn_ref, dkr_ref, qw_ref, kvw_ref, c_ref, sa_ref, sb_ref,
             dlat_ref, dqw_ref, dkvw_ref, aq, akv):
        i = pl.program_id(0)
        dq, dqw = rms_bwd(lat_ref[:, 0:MLA_QR], qw_ref[...], dqn_ref[...])
        dk, dkw = rms_bwd(lat_ref[:, MLA_QR:MLA_QR + MLA_KVR], kvw_ref[...], dkvn_ref[...])
        dkr = dkr_ref[0]
        for hh in range(1, MLA_H):
            dkr = dkr + dkr_ref[hh]
        dkr = _rope_t(dkr, c_ref[...], sa_ref[...], sb_ref[...], MLA_ROPE // 2)
        dlat_ref[:, 0:MLA_QR] = dq.astype(BF16)
        dlat_ref[:, MLA_QR:MLA_QR + MLA_KVR] = dk.astype(BF16)
        dlat_ref[:, MLA_QR + MLA_KVR:MLA_LAT] = dkr.astype(BF16)

        @pl.when(i == 0)
        def _():
            aq[...] = jnp.zeros_like(aq)
            akv[...] = jnp.zeros_like(akv)

        aq[...] += _rowsum8(dqw)
        akv[...] += _rowsum8(dkw)

        @pl.when(i == nt - 1)
        def _():
            dqw_ref[...] = jnp.sum(aq[...], axis=0, keepdims=True)
            dkvw_ref[...] = jnp.sum(akv[...], axis=0, keepdims=True)

    return pl.pallas_call(
        body, name=name, grid=(nt,),
        in_specs=[_row_spec(tr, MLA_LAT), _row_spec(tr, MLA_QR), _row_spec(tr, MLA_KVR),
                  pl.BlockSpec((MLA_H, tr, LANES), lambda i: (0, i, 0)), _vec_spec(MLA_QR), _vec_spec(MLA_KVR)]
        + [_row_spec(tr, LANES)] * 3,
        out_specs=[_row_spec(tr, MLA_LAT), _vec_spec(MLA_QR), _vec_spec(MLA_KVR)],
        out_shape=[jax.ShapeDtypeStruct((s, MLA_LAT), BF16), jax.ShapeDtypeStruct((1, MLA_QR), F32),
                   jax.ShapeDtypeStruct((1, MLA_KVR), F32)],
        scratch_shapes=[pltpu.VMEM((8, MLA_QR), F32), pltpu.VMEM((8, MLA_KVR), F32)],
        compiler_params=_cparams("arbitrary"),
    )(lat, dqn, dkvn, dkr_heads, qw, kvw, rc, rsa, rsb)


def _attn_tile(s):
    return min(1024, max(LANES, s // 2))


MLA_SCALE = (MLA_NOPE + MLA_ROPE) ** -0.5
LOG2E = 1.4426950408889634
MLA_C2 = MLA_SCALE * LOG2E
NT_DIMS = (((1,), (1,)), ((), ()))
TN_DIMS = (((0,), (0,)), ((), ()))


def _causal(sc, transposed=False):
    row = lax.broadcasted_iota(jnp.int32, sc.shape, 0)
    col = lax.broadcasted_iota(jnp.int32, sc.shape, 1)
    return jnp.where(row <= col if transposed else col <= row, sc, NEG)


def _grid_ends(grid):
    def first():
        ok = pl.program_id(0) == 0
        for d in range(1, len(grid)):
            ok = jnp.logical_and(ok, pl.program_id(d) == 0)
        return ok

    def last():
        ok = pl.program_id(0) == grid[0] - 1
        for d in range(1, len(grid)):
            ok = jnp.logical_and(ok, pl.program_id(d) == grid[d] - 1)
        return ok

    return first, last


def _mla_attn_fwd(name, qq, kv, kr, rider=None):
    s = qq.shape[0]
    t = _attn_tile(s)
    n = s // t

    def body(qi_tab, ki_tab, q_ref, kv_ref, kr_ref, o_ref, lse_ref, m_scr, acc_scr, sc_scr):
        qi = qi_tab[pl.program_id(1)]
        ki = ki_tab[pl.program_id(1)]

        @pl.when(ki == 0)
        def _():
            m_scr[...] = jnp.full(m_scr.shape, NEG, F32)
            acc_scr[...] = jnp.zeros_like(acc_scr)

        @pl.when(ki >= 0)
        def _():
            k = jnp.concatenate([kv_ref[:, 0:LANES], kr_ref[...]], axis=1)
            sc_scr[...] = lax.dot_general(q_ref[...], k, NT_DIMS, preferred_element_type=F32)

        def step(diag):
            sc = sc_scr[...]
            if diag:
                sc = _causal(sc)
            m_prev = m_scr[...]
            m_next = jnp.maximum(m_prev, jnp.max(sc, axis=1, keepdims=True))
            a = jnp.exp2(MLA_C2 * (m_prev - m_next))
            p = jnp.exp2(MLA_C2 * sc - MLA_C2 * m_next[:, 0:1]).astype(BF16)
            v1 = jnp.concatenate([kv_ref[:, LANES:2 * LANES], jnp.ones((t, LANES), BF16)], axis=1)
            pv = jnp.dot(p, v1, preferred_element_type=F32)
            acc_scr[:, 0:LANES] = a * acc_scr[:, 0:LANES] + pv[:, 0:LANES]
            acc_scr[:, LANES:2 * LANES] = a * acc_scr[:, LANES:2 * LANES] + pv[:, LANES:2 * LANES]
            m_scr[...] = m_next

        @pl.when(ki < qi)
        def _():
            step(False)

        @pl.when(ki == qi)
        def _():
            step(True)

        @pl.when(ki == qi)
        def _():
            l = acc_scr[:, LANES:2 * LANES]
            o_ref[...] = (acc_scr[:, 0:LANES] / l).astype(BF16)
            lse_ref[...] = MLA_SCALE * m_scr[...] + jnp.log(l)

    qblk = lambda w: pl.BlockSpec((t, w), lambda h, i, qt, kt: (qt[i], h))
    return _tri_call(
        body, rider, name=name, grid=(MLA_H, n * (n + 1) // 2), tables=_tri_tables(n, queries_outer=True),
        in_specs=[qblk(2 * LANES), pl.BlockSpec((t, 2 * LANES), lambda h, i, qt, kt: (kt[i], h)),
                  pl.BlockSpec((t, LANES), lambda h, i, qt, kt: (kt[i], 0))],
        out_specs=[qblk(LANES), qblk(LANES)],
        out_shape=[jax.ShapeDtypeStruct((s, MLA_H * MLA_V), BF16), jax.ShapeDtypeStruct((s, MLA_H * LANES), F32)],
        scratch_shapes=[pltpu.VMEM((t, LANES), F32), pltpu.VMEM((t, 2 * LANES), F32), pltpu.VMEM((t, t), F32)],
        operands=(qq, kv, kr))


def _mla_bwd_stats(name, do, o, lse):
    s = do.shape[0]
    tr = min(4 * ROW_TILE, s)

    def body(do_ref, o_ref, lse_ref, stat_ref):
        dl = jnp.sum(do_ref[...].astype(F32) * o_ref[...].astype(F32), axis=1, keepdims=True)
        delta_t = jnp.transpose(jnp.broadcast_to(dl, (tr, LANES)))[0:8]
        lse_t = jnp.transpose(lse_ref[...] * LOG2E)[0:8]
        rows = lax.broadcasted_iota(jnp.int32, (8, tr), 0)
        stat_ref[0] = jnp.where(rows == 0, lse_t, jnp.where(rows == 1, delta_t, 0.0))

    blk = pl.BlockSpec((tr, LANES), lambda h, i: (i, h))
    return pl.pallas_call(
        body, name=name, grid=(MLA_H, s // tr),
        in_specs=[blk, blk, blk],
        out_specs=pl.BlockSpec((1, 8, tr), lambda h, i: (h, 0, i)),
        out_shape=jax.ShapeDtypeStruct((MLA_H, 8, s), F32),
        compiler_params=_cparams("parallel", "parallel"),
    )(do, o, lse)


def _mla_dq_post(name, dq, rc, rsa, rsb):
    s = dq.shape[1]
    tr = min(4 * ROW_TILE, s)

    def body(dq_ref, c_ref, sa_ref, sb_ref, o_ref):
        o_ref[:, 0:LANES] = (MLA_SCALE * dq_ref[0, :, 0:LANES]).astype(BF16)
        o_ref[:, LANES:2 * LANES] = _rope_t(MLA_SCALE * dq_ref[0, :, LANES:2 * LANES], c_ref[...], sa_ref[...],
                                            sb_ref[...], MLA_ROPE // 2).astype(BF16)

    tab = pl.BlockSpec((tr, LANES), lambda h, i: (i, 0))
    return pl.pallas_call(
        body, name=name, grid=(MLA_H, s // tr),
        in_specs=[pl.BlockSpec((1, tr, 2 * LANES), lambda h, i: (h, i, 0)), tab, tab, tab],
        out_specs=pl.BlockSpec((tr, 2 * LANES), lambda h, i: (i, h)),
        out_shape=jax.ShapeDtypeStruct((s, 2 * MLA_H * LANES), BF16),
        compiler_params=_cparams("parallel", "parallel"),
    )(dq, rc, rsa, rsb)


def _mla_attn_bwd(name, qq, kv, kr, do, stats, rider=None):
    s = qq.shape[0]
    t = _attn_tile(s)
    n = s // t

    def body(qi_tab, ki_tab, q_ref, kv_ref, kr_ref, do_ref, stat_ref, dkv_ref, dkr_ref, dq_hbm,
             dk_scr, dv_scr, dq_scr, dq_sem):
        h = pl.program_id(0)
        qi = qi_tab[pl.program_id(1)]
        ki = ki_tab[pl.program_id(1)]

        @pl.when(pl.program_id(1) == 0)
        def _():
            dq_scr[...] = jnp.zeros_like(dq_scr)

        @pl.when(qi == ki)
        def _():
            dk_scr[...] = jnp.zeros_like(dk_scr)
            dv_scr[...] = jnp.zeros_like(dv_scr)

        def step(diag):
            q = q_ref[...]
            k = jnp.concatenate([kv_ref[:, 0:LANES], kr_ref[...]], axis=1)
            sc = lax.dot_general(k, q, NT_DIMS, preferred_element_type=F32)
            if diag:
                sc = _causal(sc, transposed=True)
            p = jnp.exp2(MLA_C2 * sc - stat_ref[0, 0:1, :])
            dov = do_ref[...]
            dp = lax.dot_general(kv_ref[:, LANES:2 * LANES], dov, NT_DIMS, preferred_element_type=F32)
            ds = (p * (dp - stat_ref[0, 1:2, :])).astype(BF16)
            dv_scr[...] += jnp.dot(p.astype(BF16), dov, preferred_element_type=F32)
            dk_scr[...] += jnp.dot(ds, q, preferred_element_type=F32)
            rows = pl.ds(pl.multiple_of(qi * t, t), t)
            dq_scr[rows, :] += lax.dot_general(ds, k, TN_DIMS, preferred_element_type=F32)

        @pl.when(qi == ki)
        def _():
            step(True)

        @pl.when(qi > ki)
        def _():
            step(False)

        @pl.when(qi == n - 1)
        def _():
            dkv_ref[:, 0:LANES] = (MLA_SCALE * dk_scr[:, 0:LANES]).astype(BF16)
            dkv_ref[:, LANES:2 * LANES] = dv_scr[...].astype(BF16)
            dkr_ref[0] = MLA_SCALE * dk_scr[:, LANES:2 * LANES]

        @pl.when(pl.program_id(1) == n * (n + 1) // 2 - 1)
        def _():
            cp = pltpu.make_async_copy(dq_scr, dq_hbm.at[h], dq_sem)
            cp.start()
            cp.wait()

    qblk = lambda w: pl.BlockSpec((t, w), lambda h, i, qt, kt: (qt[i], h))
    kblk = pl.BlockSpec((t, 2 * LANES), lambda h, i, qt, kt: (kt[i], h))
    return _tri_call(
        body, rider, name=name, grid=(MLA_H, n * (n + 1) // 2), tables=_tri_tables(n, queries_outer=False),
        in_specs=[qblk(2 * LANES), kblk, pl.BlockSpec((t, LANES), lambda h, i, qt, kt: (kt[i], 0)), qblk(LANES),
                  pl.BlockSpec((1, 8, t), lambda h, i, qt, kt: (h, 0, qt[i]))],
        out_specs=[kblk, pl.BlockSpec((1, t, LANES), lambda h, i, qt, kt: (h, kt[i], 0)),
                   pl.BlockSpec(memory_space=pl.ANY)],
        out_shape=[jax.ShapeDtypeStruct((s, 2 * MLA_H * LANES), BF16),
                   jax.ShapeDtypeStruct((MLA_H, s, LANES), F32),
                   jax.ShapeDtypeStruct((MLA_H, s, 2 * LANES), F32)],
        scratch_shapes=[pltpu.VMEM((t, 2 * LANES), F32), pltpu.VMEM((t, LANES), F32),
                        pltpu.VMEM((s, 2 * LANES), F32), pltpu.SemaphoreType.DMA(())],
        operands=(qq, kv, kr, do, stats))


def _rope_groups(acc, o_ref, c, sa, sb, sh, groups):
    for gi in range(acc.shape[1] // LANES):
        blk = acc[:, gi * LANES:(gi + 1) * LANES]
        if gi in groups:
            blk = _rope(blk, c, sa, sb, sh)
        o_ref[:, gi * LANES:(gi + 1) * LANES] = blk.astype(o_ref.dtype)


def _mla_fwd(tag, h, w, tabs, rider=None):
    s = h.shape[0]
    rc, rsa, rsb = tabs
    lat = _mm(f"{tag}_lat", h, w["w_in"], tm=512)
    qn, kvn, kr = _mla_lat_post(f"{tag}_latpost", lat, w["q_norm"], w["kv_norm"], rc, rsa, rsb)
    tm = min(512, s)

    def q_epi(acc, o_ref, c_ref, sa_ref, sb_ref):
        _rope_groups(acc, o_ref, c_ref[...], sa_ref[...], sb_ref[...], MLA_ROPE // 2, range(1, MLA_H, 2))

    tab = pl.BlockSpec((tm, LANES), lambda i, j: (i, 0))
    qq = _mm(f"{tag}_q", qn, w["w_q"], tm=512, tn=MLA_H * LANES, out_dtype=BF16, epilogue=q_epi,
             extras=(rc, rsa, rsb), extra_specs=(tab, tab, tab))
    kv = _mm(f"{tag}_kv", kvn, w["w_kv"], tm=512, out_dtype=BF16)
    o, lse, *ridden = _mla_attn_fwd(f"{tag}_attn", qq, kv, kr, rider)
    res = dict(h=h, lat=lat, qn=qn, kvn=kvn, kr=kr, qq=qq, kv=kv, o=o, lse=lse)
    return o, res, (ridden[0] if ridden else None)


def _mla_bwd(tag, dy, res, w, tabs, make_rider=None):
    rc, rsa, rsb = tabs
    do = _mm(f"{tag}_do", dy, w["w_o_t"], tm=512, out_dtype=BF16)
    g_wo = _mm_tn(f"{tag}_gwo", res["o"], dy)
    stats = _mla_bwd_stats(f"{tag}_stats", do, res["o"], res["lse"])
    rider = make_rider(g_wo) if make_rider is not None else None
    dkv, dkr, dq, *ridden = _mla_attn_bwd(f"{tag}_attnbwd", res["qq"], res["kv"], res["kr"], do, stats, rider)
    dqq = _mla_dq_post(f"{tag}_dqpost", dq, rc, rsa, rsb)
    dqn = _mm(f"{tag}_dqn", dqq, w["w_q_t"], tm=512)
    g_wq = _mm_tn(f"{tag}_gwq", res["qn"], dqq, tn=1024)
    dkvn = _mm(f"{tag}_dkvn", dkv, w["w_kv_t"], tm=512)
    g_wkv = _mm_tn(f"{tag}_gwkv", res["kvn"], dkv, tn=1024)
    dlat, g_qn, g_kvn = _mla_lat_bwd(f"{tag}_latbwd", res["lat"], dqn, dkvn, dkr, w["q_norm"], w["kv_norm"],
                                     rc, rsa, rsb)
    dh = _mm(f"{tag}_dh", dlat, w["w_in_t"], tm=512)
    g_win = _mm_tn(f"{tag}_gwin", res["h"], dlat)
    grads = dict(w_in=g_win, q_norm=g_qn, w_q=g_wq, kv_norm=g_kvn, w_kv=g_wkv, w_o=g_wo)
    return dh, grads, (ridden[0] if ridden else None)


SWA_QW = SWA_HQ * SWA_HD
SWA_KW = SWA_HKV * LANES
SWA_NQKV = SWA_QW + 2 * SWA_KW
SWA_SCALE = SWA_HD ** -0.5
SWA_GROUP_ROWS = 4 * SWA_W


def _swa_tile(s):
    return min(512, max(SWA_W, s // 2))


def _swa_masks():
    lane = lax.broadcasted_iota(jnp.int32, (SWA_W, LANES), 1)
    return lane < SWA_HD


def _swa_q4(qa, qb, lo):
    z = jnp.zeros_like(qa)
    return jnp.concatenate([jnp.where(lo, qa, z), jnp.where(lo, z, qa), jnp.where(lo, qb, z), jnp.where(lo, z, qb)],
                           axis=0)


def _swa_probs(q4, kwin, sink_col, first_block):
    sc = lax.dot_general(q4, kwin, NT_DIMS, preferred_element_type=F32) * SWA_SCALE
    row = lax.broadcasted_iota(jnp.int32, sc.shape, 0) % SWA_W
    col = lax.broadcasted_iota(jnp.int32, sc.shape, 1)
    rel = row + SWA_W - col
    ok = (rel >= 0) & (rel < SWA_W) & ((col >= SWA_W) | jnp.logical_not(first_block))
    sc = jnp.where(ok, sc, NEG)
    m = jnp.maximum(jnp.max(sc, axis=1, keepdims=True), sink_col)
    e = jnp.exp(sc - m)
    es = jnp.exp(sink_col - m)
    inv = 1.0 / (jnp.sum(e, axis=1, keepdims=True) + es)
    return e * inv, es * inv


def _sink_col(sinks_ref, grp):
    seg = lax.broadcasted_iota(jnp.int32, (SWA_GROUP_ROWS, 1), 0) // SWA_W
    col = jnp.zeros((SWA_GROUP_ROWS, 1), F32)
    for j in range(4):
        col = jnp.where(seg == j, sinks_ref[0, 4 * grp + j], col)
    return col


def _swa_attn_fwd(name, qkv, sinks):
    s = qkv.shape[0]
    t = _swa_tile(s)
    nb = t // SWA_W

    def body(sinks_ref, q_ref, kv_ref, kvp_ref, o_ref):
        i = pl.program_id(0)
        lo = _swa_masks()
        for grp in range(SWA_HKV):
            sink_col = _sink_col(sinks_ref, grp)
            kcat = jnp.concatenate([kvp_ref[:, grp * LANES:(grp + 1) * LANES],
                                    kv_ref[:, grp * LANES:(grp + 1) * LANES]], axis=0)
            vcat = jnp.concatenate([kvp_ref[:, SWA_KW + grp * LANES:SWA_KW + (grp + 1) * LANES],
                                    kv_ref[:, SWA_KW + grp * LANES:SWA_KW + (grp + 1) * LANES]], axis=0)
            for b in range(nb):
                r0 = b * SWA_W
                qa = q_ref[r0:r0 + SWA_W, grp * 2 * LANES:grp * 2 * LANES + LANES]
                qb = q_ref[r0:r0 + SWA_W, grp * 2 * LANES + LANES:(grp + 1) * 2 * LANES]
                first = jnp.logical_and(i == 0, b == 0)
                p, _ = _swa_probs(_swa_q4(qa, qb, lo), kcat[r0:r0 + 2 * SWA_W], sink_col, first)
                o4 = jnp.dot(p.astype(BF16), vcat[r0:r0 + 2 * SWA_W], preferred_element_type=F32)
                oa = jnp.where(lo, o4[0:SWA_W], o4[SWA_W:2 * SWA_W])
                ob = jnp.where(lo, o4[2 * SWA_W:3 * SWA_W], o4[3 * SWA_W:4 * SWA_W])
                o_ref[r0:r0 + SWA_W, grp * 2 * LANES:grp * 2 * LANES + LANES] = oa.astype(BF16)
                o_ref[r0:r0 + SWA_W, grp * 2 * LANES + LANES:(grp + 1) * 2 * LANES] = ob.astype(BF16)

    return pl.pallas_call(
        body, name=name, grid=(s // t,),
        in_specs=[pl.BlockSpec(memory_space=pltpu.SMEM),
                  pl.BlockSpec((t, SWA_QW), lambda i: (i, 0)),
                  pl.BlockSpec((t, 2 * SWA_KW), lambda i: (i, 1)),
                  pl.BlockSpec((SWA_W, 2 * SWA_KW), lambda i: (jnp.maximum(i * nb - 1, 0), 1))],
        out_specs=pl.BlockSpec((t, SWA_QW), lambda i: (i, 0)),
        out_shape=jax.ShapeDtypeStruct((s, SWA_QW), BF16),
        compiler_params=_cparams("parallel"),
    )(sinks, qkv, qkv, qkv)


def _swa_attn_bwd(name, qkv, sinks, do):
    s = qkv.shape[0]
    t = _swa_tile(s)
    nb = t // SWA_W
    nt = s // t

    def body(sinks_ref, q_ref, kv_ref, kvp_ref, do_ref, dq_ref, dkv_ref, dkvp_ref, dsink_ref, dcat, sink_acc):
        i = pl.program_id(0)
        lo = _swa_masks()

        @pl.when(i == 0)
        def _():
            sink_acc[...] = jnp.zeros_like(sink_acc)

        dcat[...] = jnp.zeros_like(dcat)
        for grp in range(SWA_HKV):
            sink_col = _sink_col(sinks_ref, grp)
            kcat = jnp.concatenate([kvp_ref[:, grp * LANES:(grp + 1) * LANES],
                                    kv_ref[:, grp * LANES:(grp + 1) * LANES]], axis=0)
            vcat = jnp.concatenate([kvp_ref[:, SWA_KW + grp * LANES:SWA_KW + (grp + 1) * LANES],
                                    kv_ref[:, SWA_KW + grp * LANES:SWA_KW + (grp + 1) * LANES]], axis=0)
            for b in range(nb):
                r0 = b * SWA_W
                ca = slice(grp * 2 * LANES, grp * 2 * LANES + LANES)
                cb = slice(grp * 2 * LANES + LANES, (grp + 1) * 2 * LANES)
                q4 = _swa_q4(q_ref[r0:r0 + SWA_W, ca], q_ref[r0:r0 + SWA_W, cb], lo)
                do4 = _swa_q4(do_ref[r0:r0 + SWA_W, ca], do_ref[r0:r0 + SWA_W, cb], lo)
                first = jnp.logical_and(i == 0, b == 0)
                kwin = kcat[r0:r0 + 2 * SWA_W]
                vwin = vcat[r0:r0 + 2 * SWA_W]
                p, ps = _swa_probs(q4, kwin, sink_col, first)
                dp = lax.dot_general(do4, vwin, NT_DIMS, preferred_element_type=F32)
                rowdot = jnp.sum(p * dp, axis=1, keepdims=True)
                ds = (p * (dp - rowdot) * SWA_SCALE).astype(BF16)
                sink_acc[grp] += jnp.broadcast_to(-ps * rowdot, (SWA_GROUP_ROWS, LANES))
                dq4 = jnp.dot(ds, kwin, preferred_element_type=F32)
                dq_ref[r0:r0 + SWA_W, ca] = jnp.where(lo, dq4[0:SWA_W], dq4[SWA_W:2 * SWA_W])
                dq_ref[r0:r0 + SWA_W, cb] = jnp.where(lo, dq4[2 * SWA_W:3 * SWA_W], dq4[3 * SWA_W:4 * SWA_W])
                dk = lax.dot_general(ds, q4, TN_DIMS, preferred_element_type=F32)
                dv = lax.dot_general(p.astype(BF16), do4, TN_DIMS, preferred_element_type=F32)
                dcat[r0:r0 + 2 * SWA_W, grp * LANES:(grp + 1) * LANES] += dk
                dcat[r0:r0 + 2 * SWA_W, SWA_KW + grp * LANES:SWA_KW + (grp + 1) * LANES] += dv
        dkvp_ref[0] = dcat[0:SWA_W]
        dkv_ref[...] = dcat[SWA_W:SWA_W + t]

        @pl.when(i == nt - 1)
        def _():
            for grp in range(SWA_HKV):
                for j in range(4):
                    tot = jnp.sum(sink_acc[grp, j * SWA_W:(j + 1) * SWA_W, 0:1])
                    dsink_ref[4 * grp + j:4 * grp + j + 1, :] = jnp.full((1, LANES), tot, F32)

    return pl.pallas_call(
        body, name=name, grid=(nt,),
        in_specs=[pl.BlockSpec(memory_space=pltpu.SMEM),
                  pl.BlockSpec((t, SWA_QW), lambda i: (i, 0)),
                  pl.BlockSpec((t, 2 * SWA_KW), lambda i: (i, 1)),
                  pl.BlockSpec((SWA_W, 2 * SWA_KW), lambda i: (jnp.maximum(i * nb - 1, 0), 1)),
                  pl.BlockSpec((t, SWA_QW), lambda i: (i, 0))],
        out_specs=[pl.BlockSpec((t, SWA_QW), lambda i: (i, 0)), pl.BlockSpec((t, 2 * SWA_KW), lambda i: (i, 0)),
                   pl.BlockSpec((1, SWA_W, 2 * SWA_KW), lambda i: (i, 0, 0)),
                   pl.BlockSpec((SWA_HQ, LANES), lambda i: (0, 0))],
        out_shape=[jax.ShapeDtypeStruct((s, SWA_QW), F32), jax.ShapeDtypeStruct((s, 2 * SWA_KW), F32),
                   jax.ShapeDtypeStruct((nt, SWA_W, 2 * SWA_KW), F32), jax.ShapeDtypeStruct((SWA_HQ, LANES), F32)],
        scratch_shapes=[pltpu.VMEM((SWA_W + t, 2 * SWA_KW), F32), pltpu.VMEM((SWA_HKV, SWA_GROUP_ROWS, LANES), F32)],
        compiler_params=_cparams("arbitrary"),
    )(sinks, qkv, qkv, qkv, do)


def _swa_dqkv(name, dq, dkv, dkvp, rc, rsa, rsb):
    s = dq.shape[0]
    t = _swa_tile(s)
    nt = s // t
    sh = SWA_ROT // 2

    def body(dq_ref, dkv_ref, dkvn_ref, c_ref, sa_ref, sb_ref, out_ref, bsum_ref, acc):
        i = pl.program_id(0)
        c, sa, sb = c_ref[...], sa_ref[...], sb_ref[...]
        lo = lax.broadcasted_iota(jnp.int32, (t, LANES), 1) < SWA_HD
        rows = lax.broadcasted_iota(jnp.int32, (t, LANES), 0)
        tail = jnp.logical_and(rows >= t - SWA_W, i < nt - 1)

        @pl.when(i == 0)
        def _():
            acc[...] = jnp.zeros_like(acc)

        for gi in range(SWA_QW // LANES):
            blk = _rope_t(dq_ref[:, gi * LANES:(gi + 1) * LANES], c, sa, sb, sh)
            out_ref[:, gi * LANES:(gi + 1) * LANES] = blk.astype(BF16)
            acc[:, gi * LANES:(gi + 1) * LANES] += _rowsum8(blk)
        for gi in range(2 * SWA_KW // LANES):
            cols = slice(gi * LANES, (gi + 1) * LANES)
            nxt = jnp.concatenate([jnp.zeros((t - SWA_W, LANES), F32), dkvn_ref[0, :, cols]], axis=0)
            blk = dkv_ref[:, cols] + jnp.where(tail, nxt, 0.0)
            blk = jnp.where(lo, blk + pltpu.roll(blk, SWA_HD, 1), 0.0)
            if gi < SWA_HKV:
                blk = _rope_t(blk, c, sa, sb, sh)
            out_ref[:, SWA_QW + gi * LANES:SWA_QW + (gi + 1) * LANES] = blk.astype(BF16)
            acc[:, SWA_QW + gi * LANES:SWA_QW + (gi + 1) * LANES] += _rowsum8(blk)

        @pl.when(i == nt - 1)
        def _():
            bsum_ref[...] = jnp.sum(acc[...], axis=0, keepdims=True)

    return pl.pallas_call(
        body, name=name, grid=(nt,),
        in_specs=[pl.BlockSpec((t, SWA_QW), lambda i: (i, 0)), pl.BlockSpec((t, 2 * SWA_KW), lambda i: (i, 0)),
                  pl.BlockSpec((1, SWA_W, 2 * SWA_KW), lambda i: (jnp.minimum(i + 1, nt - 1), 0, 0))]
        + [_row_spec(t, LANES)] * 3,
        out_specs=[pl.BlockSpec((t, SWA_NQKV), lambda i: (i, 0)), pl.BlockSpec((1, SWA_NQKV), lambda i: (0, 0))],
        out_shape=[jax.ShapeDtypeStruct((s, SWA_NQKV), BF16), jax.ShapeDtypeStruct((1, SWA_NQKV), F32)],
        scratch_shapes=[pltpu.VMEM((8, SWA_NQKV), F32)],
        compiler_params=_cparams("arbitrary"),
    )(dq, dkv, dkvp, rc, rsa, rsb)


def _swa_fwd(tag, h, w, tabs):
    s = h.shape[0]
    rc, rsa, rsb = tabs
    tm = min(512, s)
    sh = SWA_ROT // 2

    def qkv_epi(acc, o_ref, b_ref, c_ref, sa_ref, sb_ref):
        acc = acc + b_ref[...]

        @pl.when(pl.program_id(1) == 0)
        def _():
            _rope_groups(acc, o_ref, c_ref[...], sa_ref[...], sb_ref[...], sh, range(SWA_QW // LANES))

        @pl.when(pl.program_id(1) == 1)
        def _():
            _rope_groups(acc, o_ref, c_ref[...], sa_ref[...], sb_ref[...], sh, range(SWA_HKV))

    tab = pl.BlockSpec((tm, LANES), lambda i, j: (i, 0))
    qkv = _mm(f"{tag}_qkv", h, w["w_qkv"], tm=512, tn=SWA_QW, out_dtype=BF16, epilogue=qkv_epi,
              extras=(w["b_qkv"], rc, rsa, rsb),
              extra_specs=(pl.BlockSpec((1, SWA_QW), lambda i, j: (0, j)), tab, tab, tab))
    o = _swa_attn_fwd(f"{tag}_attn", qkv, w["sinks"])

    def o_epi(acc, o_ref, b_ref):
        o_ref[...] = acc + b_ref[...]

    y = _mm(f"{tag}_o", o, w["w_o"], tm=512, epilogue=o_epi, extras=(w["b_o"],),
            extra_specs=(pl.BlockSpec((1, D), lambda i, j: (0, 0)),))
    return y, dict(h=h, qkv=qkv, o=o)


def _swa_bwd(tag, dy, res, w, tabs):
    rc, rsa, rsb = tabs
    do = _mm(f"{tag}_do", dy, w["w_o_t"], tm=512, out_dtype=BF16)
    g_wo = _mm_tn(f"{tag}_gwo", res["o"], dy)
    dq, dkv, dkvp, dsink = _swa_attn_bwd(f"{tag}_attnbwd", res["qkv"], w["sinks"], do)
    dqkv, g_b = _swa_dqkv(f"{tag}_dqkv", dq, dkv, dkvp, rc, rsa, rsb)
    dh = _mm(f"{tag}_dh", dqkv, w["w_qkv_t"], tm=512)
    g_wqkv = _mm_tn(f"{tag}_gwqkv", res["h"], dqkv, tn=1024)
    return dh, dict(w_qkv=g_wqkv, b_qkv=g_b, sinks=dsink, w_o=g_wo)


def _ada_fwd(name, c_all, w_sh, b_sh):
    cols = w_sh.shape[2]
    tn = cols // 3

    def body(c_ref, w_ref, b_ref, o_ref, cond_ref):
        cv = c_ref[...]
        cond = cv * jax.nn.sigmoid(cv)
        cond_ref[...] = cond
        o_ref[0] = jnp.dot(cond, w_ref[0], preferred_element_type=F32, precision=lax.Precision.HIGHEST) + b_ref[0]

    return pl.pallas_call(
        body, name=name, grid=(DEPTH, cols // tn),
        in_specs=[pl.BlockSpec((8, D), lambda l, j: (0, 0)), pl.BlockSpec((1, D, tn), lambda l, j: (l, 0, j)),
                  pl.BlockSpec((1, 1, tn), lambda l, j: (l, 0, j))],
        out_specs=[pl.BlockSpec((1, 8, tn), lambda l, j: (l, 0, j)), pl.BlockSpec((8, D), lambda l, j: (0, 0))],
        out_shape=[jax.ShapeDtypeStruct((DEPTH, 8, cols), F32), jax.ShapeDtypeStruct((8, D), F32)],
        compiler_params=_cparams("arbitrary", "arbitrary"),
    )(c_all, w_sh, b_sh)


def _ada_grad(name, cond_t, dmod_sh):
    cols = dmod_sh.shape[2]
    tn = cols // 3

    def body(ct_ref, dm_ref, o_ref):
        acc = ct_ref[:, 0:1] * dm_ref[0, 0:1, :]
        for b in range(1, 8):
            acc = acc + ct_ref[:, b:b + 1] * dm_ref[0, b:b + 1, :]
        o_ref[0] = acc

    return pl.pallas_call(
        body, name=name, grid=(DEPTH, cols // tn),
        in_specs=[pl.BlockSpec((D, 8), lambda l, j: (0, 0)), pl.BlockSpec((1, 8, tn), lambda l, j: (l, 0, j))],
        out_specs=pl.BlockSpec((1, D, tn), lambda l, j: (l, 0, j)),
        out_shape=jax.ShapeDtypeStruct((DEPTH, D, cols), F32),
        compiler_params=_cparams("parallel", "parallel"),
    )(cond_t, dmod_sh)


def _adamw(name, g, w, m, v):
    r, cols = g.shape
    tile_elems = 512 * 1024
    tr = r if r * cols <= tile_elems else max(d for d in (512, 256, 128, 64, 32, 16, 8)
                                               if r % d == 0 and d * cols <= tile_elems)

    def body(g_ref, w_ref, m_ref, v_ref, d_ref, nm_ref, nv_ref):
        gv = g_ref[...]
        mn = ADAM_B1 * m_ref[...] + (1.0 - ADAM_B1) * gv
        vn = ADAM_B2 * v_ref[...] + (1.0 - ADAM_B2) * (gv * gv)
        m_hat = mn / (1.0 - ADAM_B1 ** ADAM_STEP)
        v_hat = vn / (1.0 - ADAM_B2 ** ADAM_STEP)
        d_ref[...] = -ADAM_LR * (m_hat / (jnp.sqrt(v_hat) + ADAM_EPS) + ADAM_WD * w_ref[...])
        nm_ref[...] = mn
        nv_ref[...] = vn

    spec = _row_spec(tr, cols)
    return pl.pallas_call(
        body, name=name, grid=(r // tr,),
        in_specs=[spec] * 4, out_specs=[spec] * 3,
        out_shape=[jax.ShapeDtypeStruct(g.shape, F32)] * 3,
        compiler_params=_cparams("parallel"),
    )(g, w, m, v)


def _to_chips(full, axis):
    shp = full.shape
    a = full.reshape(shp[:axis] + (N_CHIPS, shp[axis] // N_CHIPS) + shp[axis + 1:])
    return jnp.moveaxis(a, axis, 0)


def _from_chips(stacked, axis):
    a = jnp.moveaxis(stacked, 0, axis)
    shp = a.shape
    return a.reshape(shp[:axis] + (shp[axis] * shp[axis + 1],) + shp[axis + 2:])


PIECE_ROW_ALIGN = 16


def _piece_rows(shape):
    n = 1
    for d in shape:
        n *= d
    rows = -(-n // PACK_COLS)
    return -(-rows // PIECE_ROW_ALIGN) * PIECE_ROW_ALIGN


def _as_rows(a, lead):
    head = a.shape[:lead]
    rows = _piece_rows(a.shape[lead:])
    n = 1
    for d in a.shape[lead:]:
        n *= d
    if n == rows * PACK_COLS:
        return a.reshape(head + (rows, PACK_COLS))
    flat = jnp.pad(a.reshape(head + (n,)), [(0, 0)] * lead + [(0, rows * PACK_COLS - n)])
    return flat.reshape(head + (rows, PACK_COLS))


def _pack(parts, lead, rows):
    pieces = [_as_rows(p, lead) for p in parts]
    used = sum(p.shape[lead] for p in pieces)
    head = pieces[0].shape[:lead]
    pieces.append(jnp.zeros(head + (rows - used, PACK_COLS), pieces[0].dtype))
    return jnp.concatenate(pieces, axis=lead)


def _unpack(packed, lead, shapes):
    out, off = [], 0
    head = packed.shape[:lead]
    for shp in shapes:
        rows = _piece_rows(shp)
        n = 1
        for d in shp:
            n *= d
        piece = lax.slice_in_dim(packed, off, off + rows, axis=lead)
        if n != rows * PACK_COLS:
            piece = piece.reshape(head + (rows * PACK_COLS,))[..., :n]
        out.append(piece.reshape(head + tuple(shp)))
        off += rows
    return out


def _pack_rows(shapes):
    rows = sum(_piece_rows(s) for s in shapes)
    return -(-rows // PACK_ROW_ALIGN) * PACK_ROW_ALIGN


def _rope_tables(positions, rot, lanes_per_head):
    half = rot // 2
    inv = ROPE_THETA ** (-jnp.arange(0, rot, 2, dtype=F32) / rot)
    ang = positions.astype(F32)[:, None] * inv
    cos, sin = jnp.cos(ang), jnp.sin(ang)
    s = positions.shape[0]
    rest = lanes_per_head - rot
    fill = 1.0 if lanes_per_head == SWA_HD else 0.0
    c = jnp.concatenate([cos, cos, jnp.full((s, rest), fill, F32)], axis=1)
    sa = jnp.concatenate([-sin, jnp.zeros((s, half + rest), F32)], axis=1)
    sb = jnp.concatenate([jnp.zeros((s, half), F32), sin, jnp.zeros((s, rest), F32)], axis=1)
    reps = LANES // lanes_per_head
    return tuple(jnp.tile(t, (1, reps)) for t in (c, sa, sb))


def _mla_weights(w_in, q_norm, w_q_b, kv_norm, w_kv_b):
    w_in_p = jnp.pad(w_in, ((0, 0), (0, MLA_LAT - w_in.shape[1])))
    wq = w_q_b.reshape(MLA_QR, MLA_H, MLA_NOPE + MLA_ROPE)
    wq_p = jnp.pad(wq, ((0, 0), (0, 0), (0, 2 * LANES - MLA_NOPE - MLA_ROPE))).reshape(MLA_QR, MLA_H * 2 * LANES)
    return dict(w_in=w_in_p, w_in_t=w_in_p.T, q_norm=q_norm.reshape(1, -1), kv_norm=kv_norm.reshape(1, -1),
                w_q=wq_p, w_q_t=wq_p.T, w_kv=w_kv_b, w_kv_t=w_kv_b.T)


def _mla_grads_unpermute(g):
    gq = g["w_q"].reshape(MLA_QR, MLA_H, 2 * LANES)[:, :, :MLA_NOPE + MLA_ROPE]
    return dict(mla_w_in=g["w_in"][:, :MLA_QR + MLA_KVR + MLA_ROPE], mla_q_norm=g["q_norm"][0],
                mla_w_q_b=gq.reshape(MLA_QR, -1), mla_kv_norm=g["kv_norm"][0], mla_w_kv_b=g["w_kv"],
                mla_w_o=g["w_o"])


def _swa_dup(a):
    lead = a.shape[:-1]
    a = a.reshape(lead + (SWA_HKV, SWA_HD))
    return jnp.concatenate([a, a], axis=-1).reshape(lead + (SWA_KW,))


def _swa_undup(a):
    lead = a.shape[:-1]
    return a.reshape(lead + (SWA_HKV, LANES))[..., :SWA_HD].reshape(lead + (SWA_HKV * SWA_HD,))


def _swa_weights(w_qkv, b_qkv, sinks, w_o, b_o):
    nk = SWA_HKV * SWA_HD
    perm = lambda a: jnp.concatenate([a[..., :SWA_QW], _swa_dup(a[..., SWA_QW:SWA_QW + nk]),
                                      _swa_dup(a[..., SWA_QW + nk:])], axis=-1)
    w_p = perm(w_qkv)
    return dict(w_qkv=w_p, w_qkv_t=w_p.T, b_qkv=perm(b_qkv.astype(F32)).reshape(1, -1),
                sinks=sinks.reshape(1, -1), w_o=w_o, w_o_t=w_o.T, b_o=b_o.astype(F32).reshape(1, -1))


def _swa_grads_unpermute(g):
    unperm = lambda a: jnp.concatenate([a[..., :SWA_QW], _swa_undup(a[..., SWA_QW:SWA_QW + SWA_KW]),
                                        _swa_undup(a[..., SWA_QW + SWA_KW:])], axis=-1)
    return dict(swa_w_qkv=unperm(g["w_qkv"]), swa_b_qkv=unperm(g["b_qkv"])[0], swa_sinks=g["sinks"][:, 0],
                swa_w_o=g["w_o"], swa_b_o=g["b_o"])


SMALL_LAYOUT = (("ada_b", 24), ("ln_mix_g", 4), ("ln_mix_b", 4), ("ln_ffn_g", 4), ("ln_ffn_b", 4),
                ("mla_q_norm", 2), ("mla_kv_norm", 2), ("swa_sinks", 1), ("loss", 1))


def _small_pack(vals):
    rows = []
    for name, nrows in SMALL_LAYOUT:
        a = vals[name].reshape(nrows, -1).astype(F32)
        rows.append(jnp.pad(a, ((0, 0), (0, PACK_COLS - a.shape[1]))))
    cat = jnp.concatenate(rows, axis=0)
    return jnp.pad(cat, ((0, SMALL_ROWS - cat.shape[0]), (0, 0)))


def _small_unpack(packed, shapes):
    out, r = {}, 0
    for name, nrows in SMALL_LAYOUT:
        shp = shapes[name]
        n = 1
        for d in shp:
            n *= d
        out[name] = packed[r:r + nrows, :n // nrows].reshape(shp)
        r += nrows
    return out


def kernel(x, c, positions, ada_w, ada_b, ln_mix_g, ln_mix_b, ln_ffn_g, ln_ffn_b, ffn_w_gate, ffn_w_up, ffn_w_down, mla_w_in, mla_q_norm, mla_w_q_b, mla_kv_norm, mla_w_kv_b, mla_w_o, swa_w_qkv, swa_b_qkv, swa_sinks, swa_w_o, swa_b_o, loss_target, m_ada_w, m_ada_b, m_ln_mix_g, m_ln_mix_b, m_ln_ffn_g, m_ln_ffn_b, m_ffn_w_gate, m_ffn_w_up, m_ffn_w_down, m_mla_w_in, m_mla_q_norm, m_mla_w_q_b, m_mla_kv_norm, m_mla_w_kv_b, m_mla_w_o, m_swa_w_qkv, m_swa_b_qkv, m_swa_sinks, m_swa_w_o, m_swa_b_o, v_ada_w, v_ada_b, v_ln_mix_g, v_ln_mix_b, v_ln_ffn_g, v_ln_ffn_b, v_ffn_w_gate, v_ffn_w_up, v_ffn_w_down, v_mla_w_in, v_mla_q_norm, v_mla_w_q_b, v_mla_kv_norm, v_mla_w_kv_b, v_mla_w_o, v_swa_w_qkv, v_swa_b_qkv, v_swa_sinks, v_swa_w_o, v_swa_b_o):
    weights = dict(ada_w=ada_w, ada_b=ada_b, ln_mix_g=ln_mix_g, ln_mix_b=ln_mix_b, ln_ffn_g=ln_ffn_g,
                   ln_ffn_b=ln_ffn_b, ffn_w_gate=ffn_w_gate, ffn_w_up=ffn_w_up, ffn_w_down=ffn_w_down,
                   mla_w_in=mla_w_in, mla_q_norm=mla_q_norm, mla_w_q_b=mla_w_q_b, mla_kv_norm=mla_kv_norm,
                   mla_w_kv_b=mla_w_kv_b, mla_w_o=mla_w_o, swa_w_qkv=swa_w_qkv, swa_b_qkv=swa_b_qkv,
                   swa_sinks=swa_sinks, swa_w_o=swa_w_o, swa_b_o=swa_b_o)
    mom_m = dict(ada_w=m_ada_w, ada_b=m_ada_b, ln_mix_g=m_ln_mix_g, ln_mix_b=m_ln_mix_b, ln_ffn_g=m_ln_ffn_g,
                 ln_ffn_b=m_ln_ffn_b, ffn_w_gate=m_ffn_w_gate, ffn_w_up=m_ffn_w_up, ffn_w_down=m_ffn_w_down,
                 mla_w_in=m_mla_w_in, mla_q_norm=m_mla_q_norm, mla_w_q_b=m_mla_w_q_b, mla_kv_norm=m_mla_kv_norm,
                 mla_w_kv_b=m_mla_w_kv_b, mla_w_o=m_mla_w_o, swa_w_qkv=m_swa_w_qkv, swa_b_qkv=m_swa_b_qkv,
                 swa_sinks=m_swa_sinks, swa_w_o=m_swa_w_o, swa_b_o=m_swa_b_o)
    mom_v = dict(ada_w=v_ada_w, ada_b=v_ada_b, ln_mix_g=v_ln_mix_g, ln_mix_b=v_ln_mix_b, ln_ffn_g=v_ln_ffn_g,
                 ln_ffn_b=v_ln_ffn_b, ffn_w_gate=v_ffn_w_gate, ffn_w_up=v_ffn_w_up, ffn_w_down=v_ffn_w_down,
                 mla_w_in=v_mla_w_in, mla_q_norm=v_mla_q_norm, mla_w_q_b=v_mla_w_q_b, mla_kv_norm=v_mla_kv_norm,
                 mla_w_kv_b=v_mla_w_kv_b, mla_w_o=v_mla_w_o, swa_w_qkv=v_swa_w_qkv, swa_b_qkv=v_swa_b_qkv,
                 swa_sinks=v_swa_sinks, swa_w_o=v_swa_w_o, swa_b_o=v_swa_b_o)
    names = list(weights)
    my_x, my_y, my_c = lax.axis_index("x"), lax.axis_index("y"), lax.axis_index("c")
    chip = 2 * my_x + my_y
    batch_row = 2 * chip + my_c
    xs = x[0]
    target = loss_target[0]
    pos = positions[0]
    s = xs.shape[0]

    def item_shapes(items):
        return [(b - a,) + tuple(weights[n].shape[1:]) for n, a, b, _ in items]

    def pack_items(src, items, dtype):
        return _pack([src[n][a:b].astype(dtype) for n, a, b, _ in items], 0, _pack_rows(item_shapes(items)))

    full = {}

    def unpack_gathered(gathered, items):
        for (n, a, b, axis), part in zip(items, _unpack(gathered, 1, item_shapes(items))):
            whole = _from_chips(part, axis)
            for l in range(a, b):
                full[n, l] = whole[l - a]

    early = _exchange("ag_w_early", pack_items(weights, W_EARLY, BF16), ("x", "y"), "gather")
    unpack_gathered(early, W_EARLY)
    late_ride = _Exchange(pack_items(weights, W_LATE, BF16), ("x", "y"), "gather", chunks=8)

    c_rows = jnp.pad(c, ((0, 7), (0, 0)))
    c_all = _exchange("ag_c", c_rows, ("x", "y", "c"), "gather")[:, 0, :]
    ada_cols = ada_w.shape[2]
    ada_b_sh = lax.dynamic_slice_in_dim(ada_b, chip * ada_cols, ada_cols, axis=1).reshape(DEPTH, 1, ada_cols)
    mod_sh, cond_all = _ada_fwd("ada_fwd", c_all, ada_w, ada_b_sh)
    mod_all = _exchange("ag_mod", mod_sh.reshape(DEPTH * 8, ada_cols), ("x", "y"), "gather")
    mod_all = mod_all.reshape(N_CHIPS, DEPTH, 8, ada_cols)
    mod_mine = lax.dynamic_index_in_dim(mod_all, batch_row, axis=2, keepdims=False)
    mod = jnp.moveaxis(mod_mine, 0, 1).reshape(DEPTH, 6, 1, D)

    tabs_a = _rope_tables(pos, MLA_ROPE, LANES)
    tabs_b = _rope_tables(pos, SWA_ROT, SWA_HD)
    vec = lambda a, l: a[l].reshape(1, D)

    def mla_in_weights(j):
        return _mla_weights(full["mla_w_in", j], mla_q_norm[j], full["mla_w_q_b", j], mla_kv_norm[j],
                            full["mla_w_kv_b", j])

    mix_w, ffn_w = {}, {}
    saved = []
    x_cur = xs
    h = _modulate("mod0", x_cur, mod[0, 1], mod[0, 0])
    for l in range(DEPTH):
        j = l // 2
        if l % 2 == 0:
            mix_w[l] = mla_in_weights(j)
            o, res, ridden = _mla_fwd(f"mla{l}", h, mix_w[l], tabs_a, late_ride if l == 0 else None)
            if l == 0:
                unpack_gathered(ridden, W_LATE)
            mix_w[l].update(w_o=full["mla_w_o", j], w_o_t=full["mla_w_o", j].T)
            y_mix = _mm(f"mla{l}_o", o, mix_w[l]["w_o"], tm=512)
        else:
            mix_w[l] = _swa_weights(full["swa_w_qkv", j], full["swa_b_qkv", j], swa_sinks[j], full["swa_w_o", j],
                                    full["swa_b_o", j])
            y_mix, res = _swa_fwd(f"swa{l}", h, mix_w[l], tabs_b)
        wg, wu, wd = full["ffn_w_gate", l], full["ffn_w_up", l], full["ffn_w_down", l]
        ffn_w[l] = dict(wg=wg, wu=wu, wd=wd, wg_t=wg.T, wu_t=wu.T, wd_t=wd.T)
        x_mid, h2 = _post_mod(f"post_mix{l}", x_cur, y_mix, mod[l, 2], vec(ln_mix_g, l), vec(ln_mix_b, l),
                              mod[l, 4], mod[l, 3])
        gate, up, y_ffn = _ffn_fwd(f"ffn{l}", h2, ffn_w[l]["wg"], ffn_w[l]["wu"], ffn_w[l]["wd"])
        saved.append(dict(x_in=x_cur, y_mix=y_mix, res=res, x_mid=x_mid, h2=h2, gate=gate, up=up, y_ffn=y_ffn))
        if l < DEPTH - 1:
            x_cur, h = _post_mod(f"post_ffn{l}", x_mid, y_ffn, mod[l, 5], vec(ln_ffn_g, l), vec(ln_ffn_b, l),
                                 mod[l + 1, 1], mod[l + 1, 0])
        else:
            dxn, loss_part = _post_loss("post_loss", x_mid, y_ffn, mod[l, 5], vec(ln_ffn_g, l), vec(ln_ffn_b, l),
                                        target)

    gfull = {n: [None] * weights[n].shape[0] for n, _ in SHARDED}
    gsmall = {n: [None] * weights[n].shape[0] for n in ("ln_mix_g", "ln_mix_b", "ln_ffn_g", "ln_ffn_b",
                                                         "mla_q_norm", "mla_kv_norm", "swa_sinks")}
    dmod = [None] * DEPTH

    def grad_ride(items):
        parts = [_to_chips(jnp.stack(gfull[n][a:b]).astype(BF16), axis) for n, a, b, axis in items]
        return _Exchange(_pack(parts, 1, _pack_rows(item_shapes(items))), ("x", "y", "c"), "to_chip", chunks=4)

    def ride_with_wo(items, j):
        def make(g_wo):
            gfull["mla_w_o"][j] = g_wo
            return grad_ride(items)
        return make

    rides = {DEPTH - 2: G_FIRST, 0: G_SECOND}
    g_parts = {}
    sums_f, sums_m, sums_fm, sums_mm = {}, {}, {}, {}
    top = DEPTH - 1
    dxp, dy, sums_f[top] = _post_bwd(f"post_ffn_bwd{top}", dxn, saved[top]["x_mid"], saved[top]["y_ffn"], mod[top, 5],
                                     vec(ln_ffn_g, top))
    for l in reversed(range(DEPTH)):
        sv = saved[l]
        j = l // 2
        dgt, dup, act, dh2 = _ffn_bwd(f"ffn_bwd{l}", dy, sv["gate"], sv["up"], ffn_w[l]["wd_t"], ffn_w[l]["wg_t"],
                                      ffn_w[l]["wu_t"])
        gfull["ffn_w_gate"][l] = _mm_tn(f"ffn_gwg{l}", sv["h2"], dgt, tn=FF // 2)
        gfull["ffn_w_up"][l] = _mm_tn(f"ffn_gwu{l}", sv["h2"], dup, tn=FF // 2)
        gfull["ffn_w_down"][l] = _mm_tn(f"ffn_gwd{l}", act, dy)
        dxp, dy, six = _mod_post_bwd(f"modpost_mix_bwd{l}", dh2, dxp, sv["x_in"], sv["y_mix"], mod[l, 2],
                                     vec(ln_mix_g, l), vec(ln_mix_b, l), mod[l, 4])
        sums_fm[l], sums_m[l] = six[0:2], six[2:6]
        if l % 2 == 0:
            dh, g, ridden = _mla_bwd(f"mla{l}", dy, sv["res"], mix_w[l], tabs_a, ride_with_wo(rides[l], j))
            g_parts[rides[l]] = _sum_groups(f"rs_sum{l}", ridden)
            g = _mla_grads_unpermute(g)
        else:
            dh, g = _swa_bwd(f"swa{l}", dy, sv["res"], mix_w[l], tabs_b)
            g["b_o"] = sums_m[l][3]
            g = _swa_grads_unpermute(g)
        for n, val in g.items():
            (gfull if n in gfull else gsmall)[n][j] = val
        if l > 0:
            below = saved[l - 1]
            dxp, dy, six = _mod_post_bwd(f"modpost_ffn_bwd{l - 1}", dh, dxp, below["x_mid"], below["y_ffn"],
                                         mod[l - 1, 5], vec(ln_ffn_g, l - 1), vec(ln_ffn_b, l - 1), mod[l, 1])
            sums_mm[l], sums_f[l - 1] = six[0:2], six[2:6]
        else:
            dxn, sums_mm[l] = _mod_bwd("mod_mix_bwd0", dh, sv["x_in"], dxp, mod[l, 1])
    for l in range(DEPTH):
        gsmall["ln_ffn_g"][l], gsmall["ln_ffn_b"][l] = sums_f[l][0], sums_f[l][1]
        gsmall["ln_mix_g"][l], gsmall["ln_mix_b"][l] = sums_m[l][0], sums_m[l][1]
        dmod[l] = jnp.stack([sums_mm[l][1], sums_mm[l][0], sums_m[l][2], sums_fm[l][1], sums_fm[l][0], sums_f[l][2]])
    grad_x = dxn[None]

    small_vals = {n: jnp.stack(v) for n, v in gsmall.items()}
    small_vals["ada_b"] = jnp.stack(dmod)
    small_vals["loss"] = loss_part[0, 0:1]
    small_all = _exchange("ag_small", _small_pack(small_vals), ("x", "y", "c"), "gather")
    small_sum = _sum_groups("sum_small", small_all)
    dmod_all = small_all[:, :DEPTH * 6, :].reshape(8, DEPTH, 6 * D)
    dmod_sh = jnp.moveaxis(lax.dynamic_slice_in_dim(dmod_all, chip * ada_cols, ada_cols, axis=2), 0, 1)
    g_ada_w = _ada_grad("ada_grad", cond_all.T, dmod_sh)

    tail = grad_ride(G_LAST)
    g_parts[G_LAST] = _sum_groups("rs_sum_tail", _exchange("rs_tail", tail.src, tail.axes, tail.mode, tail.chunks))

    pieces = {}
    for items in (G_FIRST, G_SECOND, G_LAST):
        for (n, a, _, _), part in zip(items, _unpack(g_parts[items], 0, item_shapes(items))):
            pieces.setdefault(n, []).append((a, part))
    grads = {n: jnp.concatenate([p for _, p in sorted(ps, key=lambda ap: ap[0])], axis=0) for n, ps in pieces.items()}
    grads["ada_w"] = g_ada_w
    small_shapes = {n: weights[n].shape for n, _ in SMALL_LAYOUT if n != "loss"}
    small_shapes["loss"] = (1,)
    grads.update(_small_unpack(small_sum, small_shapes))

    def as_2d(a):
        return a.reshape(-1, a.shape[-1])

    outs = [grads, {}, {}, {}]
    for n in names:
        if n in small_shapes:
            continue
        res = _adamw(f"adamw_{n}", *[as_2d(src[n]) for src in (grads, weights, mom_m, mom_v)])
        for o, r in zip(outs[1:], res):
            o[n] = r.reshape(weights[n].shape)

    def small_of(src):
        return _small_pack({**{n: src[n] for n in small_shapes if n != "loss"}, "loss": jnp.zeros((1,), F32)})

    res = _adamw("adamw_small", small_sum, small_of(weights), small_of(mom_m), small_of(mom_v))
    for o, r in zip(outs[1:], res):
        o.update(_small_unpack(r, small_shapes))
    loss = grads["loss"][0]
    return (loss, grad_x, *[o[n] for o in outs for n in names])
```

```python
import jax
import jax.numpy as jnp
from jax import lax
from jax.experimental import pallas as pl
from jax.experimental.pallas import tpu as pltpu

F32 = jnp.float32
BF16 = jnp.bfloat16

D = 1024
DEPTH = 4
ROPE_THETA = 500000.0
LN_EPS = 1e-5
RMS_EPS = 1e-6
MLA_H = 8
MLA_NOPE = 128
MLA_ROPE = 64
MLA_V = 128
MLA_QR = 384
MLA_KVR = 256
MLA_LAT = 768
SWA_HQ = 16
SWA_HKV = 4
SWA_HD = 64
SWA_W = 128
SWA_ROT = 16
FF = 2816
ALPHA = (2 * DEPTH) ** 0.25
ADAM_LR = 0.001
ADAM_B1 = 0.9
ADAM_B2 = 0.999
ADAM_EPS = 1e-08
ADAM_WD = 0.01
ADAM_STEP = 10
NEG = -1e30
LANES = 128
N_CHIPS = 4
PACK_COLS = 1024
PACK_ROW_ALIGN = 512
SMALL_ROWS = 48
ROW_TILE = 512

SHARDED = (
    ("ffn_w_gate", 2), ("ffn_w_up", 2), ("ffn_w_down", 1), ("mla_w_in", 1), ("mla_w_q_b", 2),
    ("mla_w_kv_b", 2), ("mla_w_o", 1), ("swa_w_qkv", 2), ("swa_b_qkv", 1), ("swa_w_o", 1), ("swa_b_o", 1),
)


def _items(*specs):
    axis = dict(SHARDED)
    return tuple((n, a, b, axis[n]) for names, a, b in specs for n in names)


_FFN = ("ffn_w_gate", "ffn_w_up", "ffn_w_down")
_SWA = ("swa_w_qkv", "swa_b_qkv", "swa_w_o", "swa_b_o")
_MLA_IN = ("mla_w_in", "mla_w_q_b", "mla_w_kv_b")
_MLA_OUT = ("mla_w_o",)
W_EARLY = _items((_MLA_IN, 0, 1))
W_LATE = _items((_FFN, 0, 4), (_MLA_IN, 1, 2), (_MLA_OUT, 0, 2), (_SWA, 0, 2))
G_FIRST = _items((_FFN, 3, 4), (_SWA, 1, 2), (_FFN, 2, 3), (_MLA_OUT, 1, 2))
G_SECOND = _items((_MLA_IN, 1, 2), (_FFN, 1, 2), (_SWA, 0, 1), (_FFN, 0, 1), (_MLA_OUT, 0, 1))
G_LAST = _items((_MLA_IN, 0, 1))


def _cparams(*sem):
    return pltpu.CompilerParams(dimension_semantics=sem)


def _row_spec(tr, cols):
    return pl.BlockSpec((tr, cols), lambda i: (i, 0))


def _vec_spec(cols):
    return pl.BlockSpec((1, cols), lambda i: (0, 0))


def _rope(x, c, sa, sb, sh):
    n = x.shape[1]
    return x * c + pltpu.roll(x, n - sh, 1) * sa + pltpu.roll(x, sh, 1) * sb


def _rope_t(d, c, sa, sb, sh):
    n = d.shape[1]
    return d * c + pltpu.roll(d * sa, sh, 1) + pltpu.roll(d * sb, n - sh, 1)


def _rowsum8(t):
    r, n = t.shape
    return jnp.sum(t.reshape(r // 8, 8, n), axis=0)


class _Exchange:
    def __init__(self, src, axes, mode, chunks=1):
        self.src, self.axes, self.mode, self.chunks = src, axes, mode, chunks
        self.g = 2 ** len(axes)
        self.blk = tuple(src.shape if mode == "gather" else src.shape[1:])
        self.out_shape = jax.ShapeDtypeStruct((self.g,) + self.blk, src.dtype)
        nsem = (self.g - 1) * chunks
        self.scratch = [pltpu.SemaphoreType.DMA((nsem,)), pltpu.SemaphoreType.DMA((nsem,)),
                        pltpu.SemaphoreType.DMA(())]

    def copies(self, src_ref, out_ref, send_sems, recv_sems, loc_sem):
        pos = {a: lax.axis_index(a) for a in ("x", "y", "c")}
        rows = self.blk[0] // self.chunks

        def gidx(p):
            idx = 0
            for a in self.axes:
                idx = idx * 2 + p[a]
            return idx

        def view(p):
            if self.mode == "gather":
                return src_ref
            if self.mode == "to_chip":
                return src_ref.at[2 * p["x"] + p["y"]]
            return src_ref.at[gidx(p)]

        me = gidx(pos)
        out = [pltpu.make_async_copy(view(pos), out_ref.at[me], loc_sem)]
        for k in range(self.chunks):
            piece = pl.ds(k * rows, rows)
            for j in range(1, self.g):
                peer = dict(pos)
                for bit, a in enumerate(reversed(self.axes)):
                    if (j >> bit) & 1:
                        peer[a] = 1 - pos[a]
                sem = (j - 1) * self.chunks + k
                out.append(pltpu.make_async_remote_copy(
                    src_ref=view(peer).at[piece], dst_ref=out_ref.at[me, piece],
                    send_sem=send_sems.at[sem], recv_sem=recv_sems.at[sem],
                    device_id=(peer["x"], peer["y"], peer["c"]), device_id_type=pl.DeviceIdType.MESH))
        return out


def _exchange(name, src, axes, mode, chunks=1):
    ex = _Exchange(src, axes, mode, chunks)

    def body(src_ref, out_ref, send_sems, recv_sems, loc_sem):
        copies = ex.copies(src_ref, out_ref, send_sems, recv_sems, loc_sem)
        for cp in copies:
            cp.start()
        for cp in copies:
            cp.wait()

    return pl.pallas_call(
        body, name=name, out_shape=ex.out_shape,
        in_specs=[pl.BlockSpec(memory_space=pl.ANY)],
        out_specs=pl.BlockSpec(memory_space=pl.ANY),
        scratch_shapes=ex.scratch,
    )(src)


def _tri_call(body, rider, *, name, grid, tables, in_specs, out_specs, out_shape, scratch_shapes, operands):
    n_tab, n_in, n_out, n_scr = len(tables), len(in_specs), len(out_specs), len(scratch_shapes)
    in_specs, out_specs, out_shape = list(in_specs), list(out_specs), list(out_shape)
    scratch_shapes, operands = list(scratch_shapes), list(operands)
    if rider is not None:
        any_spec = pl.BlockSpec(memory_space=pl.ANY)
        in_specs.append(any_spec)
        out_specs.append(any_spec)
        out_shape.append(rider.out_shape)
        scratch_shapes.extend(rider.scratch)
        operands.append(rider.src)
        first, last = _grid_ends(grid)

    def wrapped(*refs):
        tabs, refs = refs[:n_tab], refs[n_tab:]
        if rider is None:
            return body(*tabs, *refs)
        ins, src_ref = refs[:n_in], refs[n_in]
        outs, out_ref = refs[n_in + 1:n_in + 1 + n_out], refs[n_in + 1 + n_out]
        scr = refs[n_in + 2 + n_out:n_in + 2 + n_out + n_scr]
        sems = refs[n_in + 2 + n_out + n_scr:]

        @pl.when(first())
        def _():
            for cp in rider.copies(src_ref, out_ref, *sems):
                cp.start()

        body(*tabs, *ins, *outs, *scr)

        @pl.when(last())
        def _():
            for cp in rider.copies(src_ref, out_ref, *sems):
                cp.wait()

    return pl.pallas_call(
        wrapped, name=name, out_shape=out_shape,
        grid_spec=pltpu.PrefetchScalarGridSpec(num_scalar_prefetch=n_tab, grid=grid, in_specs=in_specs,
                                               out_specs=out_specs, scratch_shapes=scratch_shapes),
        compiler_params=_cparams(*(["arbitrary"] * len(grid))),
    )(*tables, *operands)


def _tri_tables(n, queries_outer):
    if queries_outer:
        pairs = [(qi, ki) for qi in range(n) for ki in range(qi + 1)]
    else:
        pairs = [(qi, ki) for ki in range(n) for qi in range(ki, n)]
    return (jnp.asarray([p[0] for p in pairs], jnp.int32), jnp.asarray([p[1] for p in pairs], jnp.int32))


def _sum_groups(name, a):
    g, r, c = a.shape
    tr = min(ROW_TILE, r)

    def body(a_ref, o_ref):
        acc = a_ref[0].astype(F32)
        for i in range(1, g):
            acc = acc + a_ref[i].astype(F32)
        o_ref[...] = acc

    return pl.pallas_call(
        body, name=name, grid=(r // tr,),
        in_specs=[pl.BlockSpec((g, tr, c), lambda i: (0, i, 0))],
        out_specs=pl.BlockSpec((tr, c), lambda i: (i, 0)),
        out_shape=jax.ShapeDtypeStruct((r, c), F32),
        compiler_params=_cparams("parallel"),
    )(a)


def _mm(name, a, b, *, tm, tn=None, out_dtype=F32, epilogue=None, extras=(), extra_specs=()):
    m, k = a.shape
    n = b.shape[1]
    tn = tn or n
    tm = min(tm, m)

    def body(a_ref, b_ref, *rest):
        o_ref = rest[-1]
        acc = jnp.dot(a_ref[...], b_ref[...], preferred_element_type=F32)
        if epilogue is None:
            o_ref[...] = acc.astype(o_ref.dtype)
        else:
            epilogue(acc, o_ref, *rest[:-1])

    return pl.pallas_call(
        body, name=name, grid=(m // tm, n // tn),
        in_specs=[pl.BlockSpec((tm, k), lambda i, j: (i, 0)), pl.BlockSpec((k, tn), lambda i, j: (0, j)),
                  *extra_specs],
        out_specs=pl.BlockSpec((tm, tn), lambda i, j: (i, j)),
        out_shape=jax.ShapeDtypeStruct((m, n), out_dtype),
        compiler_params=_cparams("parallel", "parallel"),
    )(a, b, *extras)


def _mm_tn(name, a, b, *, tn=None, tk=1024):
    s, m = a.shape
    n = b.shape[1]
    tn = tn or n
    tk = min(tk, s)

    def body(a_ref, b_ref, o_ref):
        part = lax.dot_general(a_ref[...], b_ref[...], (((0,), (0,)), ((), ())), preferred_element_type=F32)

        @pl.when(pl.program_id(1) == 0)
        def _():
            o_ref[...] = part

        @pl.when(pl.program_id(1) > 0)
        def _():
            o_ref[...] += part

    return pl.pallas_call(
        body, name=name, grid=(n // tn, s // tk),
        in_specs=[pl.BlockSpec((tk, m), lambda j, k: (k, 0)), pl.BlockSpec((tk, tn), lambda j, k: (k, j))],
        out_specs=pl.BlockSpec((m, tn), lambda j, k: (0, j)),
        out_shape=jax.ShapeDtypeStruct((m, n), F32),
        compiler_params=_cparams("parallel", "arbitrary"),
    )(a, b)


def _modulate(name, x, sc, sh):
    s = x.shape[0]
    tr = min(ROW_TILE, s)

    def body(x_ref, sc_ref, sh_ref, h_ref):
        h_ref[...] = (x_ref[...] * (1.0 + sc_ref[...]) + sh_ref[...]).astype(BF16)

    return pl.pallas_call(
        body, name=name, grid=(s // tr,),
        in_specs=[_row_spec(tr, D), _vec_spec(D), _vec_spec(D)],
        out_specs=_row_spec(tr, D),
        out_shape=jax.ShapeDtypeStruct((s, D), BF16),
        compiler_params=_cparams("parallel"),
    )(x, sc, sh)


def _ln_stats(z):
    mu = jnp.mean(z, axis=1, keepdims=True)
    zc = z - mu
    var = jnp.mean(zc * zc, axis=1, keepdims=True)
    r = lax.rsqrt(var + LN_EPS)
    return zc * r, r


def _post_mod(name, x, y, g, gamma, beta, sc, sh):
    s = x.shape[0]
    tr = min(ROW_TILE, s)

    def body(x_ref, y_ref, g_ref, ga_ref, be_ref, sc_ref, sh_ref, xn_ref, h_ref):
        zh, _ = _ln_stats(ALPHA * x_ref[...] + g_ref[...] * y_ref[...])
        xn = zh * ga_ref[...] + be_ref[...]
        xn_ref[...] = xn
        h_ref[...] = (xn * (1.0 + sc_ref[...]) + sh_ref[...]).astype(BF16)

    return pl.pallas_call(
        body, name=name, grid=(s // tr,),
        in_specs=[_row_spec(tr, D), _row_spec(tr, D)] + [_vec_spec(D)] * 5,
        out_specs=[_row_spec(tr, D), _row_spec(tr, D)],
        out_shape=[jax.ShapeDtypeStruct((s, D), F32), jax.ShapeDtypeStruct((s, D), BF16)],
        compiler_params=_cparams("parallel"),
    )(x, y, g, gamma, beta, sc, sh)


def _post_loss(name, x, y, g, gamma, beta, target):
    s = x.shape[0]
    tr = min(ROW_TILE, s)
    nt = s // tr

    def body(x_ref, y_ref, g_ref, ga_ref, be_ref, t_ref, dx_ref, loss_ref, acc):
        i = pl.program_id(0)
        zh, _ = _ln_stats(ALPHA * x_ref[...] + g_ref[...] * y_ref[...])
        e = zh * ga_ref[...] + be_ref[...] - t_ref[...]
        dx_ref[...] = e * (1.0 / D)

        @pl.when(i == 0)
        def _():
            acc[...] = jnp.zeros_like(acc)

        acc[...] += _rowsum8(e * e)

        @pl.when(i == nt - 1)
        def _():
            loss_ref[...] = jnp.full(loss_ref.shape, jnp.sum(acc[...]) * (0.5 / D), F32)

    return pl.pallas_call(
        body, name=name, grid=(nt,),
        in_specs=[_row_spec(tr, D), _row_spec(tr, D)] + [_vec_spec(D)] * 3 + [_row_spec(tr, D)],
        out_specs=[_row_spec(tr, D), pl.BlockSpec((8, LANES), lambda i: (0, 0))],
        out_shape=[jax.ShapeDtypeStruct((s, D), F32), jax.ShapeDtypeStruct((8, LANES), F32)],
        scratch_shapes=[pltpu.VMEM((8, D), F32)],
        compiler_params=_cparams("arbitrary"),
    )(x, y, g, gamma, beta, target)


def _post_bwd(name, dxn, x, y, g, gamma):
    s = x.shape[0]
    tr = min(ROW_TILE, s)
    nt = s // tr

    def body(d_ref, x_ref, y_ref, g_ref, ga_ref, dxp_ref, dy_ref, sums_ref, a0, a1, a2, a3):
        i = pl.program_id(0)
        yv = y_ref[...]
        gv = g_ref[...]
        zh, r = _ln_stats(ALPHA * x_ref[...] + gv * yv)
        dxn_v = d_ref[...]
        dzh = dxn_v * ga_ref[...]
        dz = r * (dzh - jnp.mean(dzh, axis=1, keepdims=True) - zh * jnp.mean(dzh * zh, axis=1, keepdims=True))
        dxp_ref[...] = ALPHA * dz
        dyv = gv * dz
        dy_ref[...] = dyv.astype(BF16)

        @pl.when(i == 0)
        def _():
            for a in (a0, a1, a2, a3):
                a[...] = jnp.zeros_like(a)

        a0[...] += _rowsum8(dxn_v * zh)
        a1[...] += _rowsum8(dxn_v)
        a2[...] += _rowsum8(dz * yv)
        a3[...] += _rowsum8(dyv)

        @pl.when(i == nt - 1)
        def _():
            for k, a in enumerate((a0, a1, a2, a3)):
                sums_ref[k:k + 1, :] = jnp.sum(a[...], axis=0, keepdims=True)

    return pl.pallas_call(
        body, name=name, grid=(nt,),
        in_specs=[_row_spec(tr, D)] * 3 + [_vec_spec(D)] * 2,
        out_specs=[_row_spec(tr, D), _row_spec(tr, D), pl.BlockSpec((4, D), lambda i: (0, 0))],
        out_shape=[jax.ShapeDtypeStruct((s, D), F32), jax.ShapeDtypeStruct((s, D), BF16),
                   jax.ShapeDtypeStruct((4, D), F32)],
        scratch_shapes=[pltpu.VMEM((8, D), F32)] * 4,
        compiler_params=_cparams("arbitrary"),
    )(dxn, x, y, g, gamma)


def _mod_post_bwd(name, dh, dxp, x, y, g, gamma, beta, sc):
    s = x.shape[0]
    tr = min(ROW_TILE, s)
    nt = s // tr

    def body(dh_ref, dxp_ref, x_ref, y_ref, g_ref, ga_ref, be_ref, sc_ref, dxo_ref, dy_ref, sums_ref, *acc):
        i = pl.program_id(0)
        yv = y_ref[...]
        gv = g_ref[...]
        zh, r = _ln_stats(ALPHA * x_ref[...] + gv * yv)
        xn = zh * ga_ref[...] + be_ref[...]
        dhv = dh_ref[...]
        dxn_v = dxp_ref[...] + dhv * (1.0 + sc_ref[...])
        dzh = dxn_v * ga_ref[...]
        dz = r * (dzh - jnp.mean(dzh, axis=1, keepdims=True) - zh * jnp.mean(dzh * zh, axis=1, keepdims=True))
        dxo_ref[...] = ALPHA * dz
        dyv = gv * dz
        dy_ref[...] = dyv.astype(BF16)

        @pl.when(i == 0)
        def _():
            for a in acc:
                a[...] = jnp.zeros_like(a)

        for a, val in zip(acc, (dhv * xn, dhv, dxn_v * zh, dxn_v, dz * yv, dyv)):
            a[...] += _rowsum8(val)

        @pl.when(i == nt - 1)
        def _():
            for k, a in enumerate(acc):
                sums_ref[k:k + 1, :] = jnp.sum(a[...], axis=0, keepdims=True)

    return pl.pallas_call(
        body, name=name, grid=(nt,),
        in_specs=[_row_spec(tr, D)] * 4 + [_vec_spec(D)] * 4,
        out_specs=[_row_spec(tr, D), _row_spec(tr, D), pl.BlockSpec((6, D), lambda i: (0, 0))],
        out_shape=[jax.ShapeDtypeStruct((s, D), F32), jax.ShapeDtypeStruct((s, D), BF16),
                   jax.ShapeDtypeStruct((6, D), F32)],
        scratch_shapes=[pltpu.VMEM((8, D), F32)] * 6,
        compiler_params=_cparams("arbitrary"),
    )(dh, dxp, x, y, g, gamma, beta, sc)


def _mod_bwd(name, dh, x, dxp, sc):
    s = x.shape[0]
    tr = min(ROW_TILE, s)
    nt = s // tr

    def body(dh_ref, x_ref, dxp_ref, sc_ref, dx_ref, sums_ref, a0, a1):
        i = pl.program_id(0)
        dhv = dh_ref[...]
        dx_ref[...] = dxp_ref[...] + dhv * (1.0 + sc_ref[...])

        @pl.when(i == 0)
        def _():
            a0[...] = jnp.zeros_like(a0)
            a1[...] = jnp.zeros_like(a1)

        a0[...] += _rowsum8(dhv * x_ref[...])
        a1[...] += _rowsum8(dhv)

        @pl.when(i == nt - 1)
        def _():
            sums_ref[0:1, :] = jnp.sum(a0[...], axis=0, keepdims=True)
            sums_ref[1:2, :] = jnp.sum(a1[...], axis=0, keepdims=True)

    return pl.pallas_call(
        body, name=name, grid=(nt,),
        in_specs=[_row_spec(tr, D)] * 3 + [_vec_spec(D)],
        out_specs=[_row_spec(tr, D), pl.BlockSpec((2, D), lambda i: (0, 0))],
        out_shape=[jax.ShapeDtypeStruct((s, D), F32), jax.ShapeDtypeStruct((2, D), F32)],
        scratch_shapes=[pltpu.VMEM((8, D), F32)] * 2,
        compiler_params=_cparams("arbitrary"),
    )(dh, x, dxp, sc)


def _ffn_tiles(s):
    return min(ROW_TILE, s), FF // 2


def _ffn_fwd(name, h, wg, wu, wd):
    s = h.shape[0]
    tm, tf = _ffn_tiles(s)

    def body(h_ref, wg_ref, wu_ref, wd_ref, gate_ref, up_ref, y_ref):
        hv = h_ref[...]
        gt = jnp.dot(hv, wg_ref[...], preferred_element_type=F32)
        up = jnp.dot(hv, wu_ref[...], preferred_element_type=F32)
        gate_ref[...] = gt
        up_ref[...] = up
        act = (gt * jax.nn.sigmoid(gt) * up).astype(BF16)
        part = jnp.dot(act, wd_ref[...], preferred_element_type=F32)

        @pl.when(pl.program_id(1) == 0)
        def _():
            y_ref[...] = part

        @pl.when(pl.program_id(1) > 0)
        def _():
            y_ref[...] += part

    return pl.pallas_call(
        body, name=name, grid=(s // tm, FF // tf),
        in_specs=[pl.BlockSpec((tm, D), lambda i, f: (i, 0)), pl.BlockSpec((D, tf), lambda i, f: (0, f)),
                  pl.BlockSpec((D, tf), lambda i, f: (0, f)), pl.BlockSpec((tf, D), lambda i, f: (f, 0))],
        out_specs=[pl.BlockSpec((tm, tf), lambda i, f: (i, f)), pl.BlockSpec((tm, tf), lambda i, f: (i, f)),
                   pl.BlockSpec((tm, D), lambda i, f: (i, 0))],
        out_shape=[jax.ShapeDtypeStruct((s, FF), F32), jax.ShapeDtypeStruct((s, FF), F32),
                   jax.ShapeDtypeStruct((s, D), F32)],
        compiler_params=_cparams("parallel", "arbitrary"),
    )(h, wg, wu, wd)


def _ffn_bwd(name, dy, gate, up, wd_t, wg_t, wu_t):
    s = dy.shape[0]
    tm, tf = _ffn_tiles(s)

    def body(dy_ref, gate_ref, up_ref, wdt_ref, wgt_ref, wut_ref, dg_ref, du_ref, act_ref, dh_ref):
        dact = jnp.dot(dy_ref[...], wdt_ref[...], preferred_element_type=F32)
        gt = gate_ref[...]
        up = up_ref[...]
        sig = jax.nn.sigmoid(gt)
        silu = gt * sig
        dgt = (dact * up * (sig * (1.0 + gt * (1.0 - sig)))).astype(BF16)
        dup = (dact * silu).astype(BF16)
        dg_ref[...] = dgt
        du_ref[...] = dup
        act_ref[...] = (silu * up).astype(BF16)
        part = (jnp.dot(dgt, wgt_ref[...], preferred_element_type=F32)
                + jnp.dot(dup, wut_ref[...], preferred_element_type=F32))

        @pl.when(pl.program_id(1) == 0)
        def _():
            dh_ref[...] = part

        @pl.when(pl.program_id(1) > 0)
        def _():
            dh_ref[...] += part

    tile = pl.BlockSpec((tm, tf), lambda i, f: (i, f))
    return pl.pallas_call(
        body, name=name, grid=(s // tm, FF // tf),
        in_specs=[pl.BlockSpec((tm, D), lambda i, f: (i, 0)), tile, tile,
                  pl.BlockSpec((D, tf), lambda i, f: (0, f)), pl.BlockSpec((tf, D), lambda i, f: (f, 0)),
                  pl.BlockSpec((tf, D), lambda i, f: (f, 0))],
        out_specs=[tile, tile, tile, pl.BlockSpec((tm, D), lambda i, f: (i, 0))],
        out_shape=[jax.ShapeDtypeStruct((s, FF), BF16)] * 3 + [jax.ShapeDtypeStruct((s, D), F32)],
        compiler_params=_cparams("parallel", "arbitrary"),
    )(dy, gate, up, wd_t, wg_t, wu_t)


def _mla_lat_post(name, lat, qw, kvw, rc, rsa, rsb):
    s = lat.shape[0]
    tr = min(ROW_TILE, s)

    def body(lat_ref, qw_ref, kvw_ref, c_ref, sa_ref, sb_ref, qn_ref, kvn_ref, kr_ref):
        ql = lat_ref[:, 0:MLA_QR]
        kl = lat_ref[:, MLA_QR:MLA_QR + MLA_KVR]
        qn_ref[...] = (ql * lax.rsqrt(jnp.mean(ql * ql, axis=1, keepdims=True) + RMS_EPS) * qw_ref[...]).astype(BF16)
        kvn_ref[...] = (kl * lax.rsqrt(jnp.mean(kl * kl, axis=1, keepdims=True) + RMS_EPS) * kvw_ref[...]).astype(BF16)
        kr_ref[...] = _rope(lat_ref[:, MLA_QR + MLA_KVR:MLA_LAT], c_ref[...], sa_ref[...], sb_ref[...],
                            MLA_ROPE // 2).astype(BF16)

    return pl.pallas_call(
        body, name=name, grid=(s // tr,),
        in_specs=[_row_spec(tr, MLA_LAT), _vec_spec(MLA_QR), _vec_spec(MLA_KVR)] + [_row_spec(tr, LANES)] * 3,
        out_specs=[_row_spec(tr, MLA_QR), _row_spec(tr, MLA_KVR), _row_spec(tr, LANES)],
        out_shape=[jax.ShapeDtypeStruct((s, MLA_QR), BF16), jax.ShapeDtypeStruct((s, MLA_KVR), BF16),
                   jax.ShapeDtypeStruct((s, LANES), BF16)],
        compiler_params=_cparams("parallel"),
    )(lat, qw, kvw, rc, rsa, rsb)


def _mla_lat_bwd(name, lat, dqn, dkvn, dkr_heads, qw, kvw, rc, rsa, rsb):
    s = lat.shape[0]
    tr = min(ROW_TILE, s)
    nt = s // tr

    def rms_bwd(x, w, dy):
        r = lax.rsqrt(jnp.mean(x * x, axis=1, keepdims=True) + RMS_EPS)
        xh = x * r
        gdy = dy * w
        return r * (gdy - xh * jnp.mean(gdy * xh, axis=1, keepdims=True)), dy * xh

    def body(lat_ref, dqn_ref, dkvn_ref, dkr_ref, qw_ref, kvw_ref, c_ref, sa_ref, sb_ref,
             dlat_ref, dqw_ref, dkvw_ref, aq, akv):
        i = pl.program_id(0)
        dq, dqw = rms_bwd(lat_ref[:, 0:MLA_QR], qw_ref[...], dqn_ref[...])
        dk, dkw = rms_bwd(lat_ref[:, MLA_QR:MLA_QR + MLA_KVR], kvw_ref[...], dkvn_ref[...])
        dkr = dkr_ref[0]
        for hh in range(1, MLA_H):
            dkr = dkr + dkr_ref[hh]
        dkr = _rope_t(dkr, c_ref[...], sa_ref[...], sb_ref[...], MLA_ROPE // 2)
        dlat_ref[:, 0:MLA_QR] = dq.astype(BF16)
        dlat_ref[:, MLA_QR:MLA_QR + MLA_KVR] = dk.astype(BF16)
        dlat_ref[:, MLA_QR + MLA_KVR:MLA_LAT] = dkr.astype(BF16)

        @pl.when(i == 0)
        def _():
            aq[...] = jnp.zeros_like(aq)
            akv[...] = jnp.zeros_like(akv)

        aq[...] += _rowsum8(dqw)
        akv[...] += _rowsum8(dkw)

        @pl.when(i == nt - 1)
        def _():
            dqw_ref[...] = jnp.sum(aq[...], axis=0, keepdims=True)
            dkvw_ref[...] = jnp.sum(akv[...], axis=0, keepdims=True)

    return pl.pallas_call(
        body, name=name, grid=(nt,),
        in_specs=[_row_spec(tr, MLA_LAT), _row_spec(tr, MLA_QR), _row_spec(tr, MLA_KVR),
                  pl.BlockSpec((MLA_H, tr, LANES), lambda i: (0, i, 0)), _vec_spec(MLA_QR), _vec_spec(MLA_KVR)]
        + [_row_spec(tr, LANES)] * 3,
        out_specs=[_row_spec(tr, MLA_LAT), _vec_spec(MLA_QR), _vec_spec(MLA_KVR)],
        out_shape=[jax.ShapeDtypeStruct((s, MLA_LAT), BF16), jax.ShapeDtypeStruct((1, MLA_QR), F32),
                   jax.ShapeDtypeStruct((1, MLA_KVR), F32)],
        scratch_shapes=[pltpu.VMEM((8, MLA_QR), F32), pltpu.VMEM((8, MLA_KVR), F32)],
        compiler_params=_cparams("arbitrary"),
    )(lat, dqn, dkvn, dkr_heads, qw, kvw, rc, rsa, rsb)


def _attn_tile(s):
    return min(1024, max(LANES, s // 2))


MLA_SCALE = (MLA_NOPE + MLA_ROPE) ** -0.5
LOG2E = 1.4426950408889634
MLA_C2 = MLA_SCALE * LOG2E
NT_DIMS = (((1,), (1,)), ((), ()))
TN_DIMS = (((0,), (0,)), ((), ()))


def _causal(sc, transposed=False):
    row = lax.broadcasted_iota(jnp.int32, sc.shape, 0)
    col = lax.broadcasted_iota(jnp.int32, sc.shape, 1)
    return jnp.where(row <= col if transposed else col <= row, sc, NEG)


def _grid_ends(grid):
    def first():
        ok = pl.program_id(0) == 0
        for d in range(1, len(grid)):
            ok = jnp.logical_and(ok, pl.program_id(d) == 0)
        return ok

    def last():
        ok = pl.program_id(0) == grid[0] - 1
        for d in range(1, len(grid)):
            ok = jnp.logical_and(ok, pl.program_id(d) == grid[d] - 1)
        return ok

    return first, last


def _mla_attn_fwd(name, qq, kv, kr, rider=None):
    s = qq.shape[0]
    t = _attn_tile(s)
    n = s // t

    def body(qi_tab, ki_tab, q_ref, kv_ref, kvn_ref, kr_ref, krn_ref, o_ref, lse_ref, m_scr, acc_scr, sc_a, sc_b):
        qi = qi_tab[pl.program_id(1)]
        ki = ki_tab[pl.program_id(1)]

        def scores(kv_blk, kr_blk):
            k = jnp.concatenate([kv_blk[:, 0:LANES], kr_blk[...]], axis=1)
            return lax.dot_general(q_ref[...], k, NT_DIMS, preferred_element_type=F32)

        @pl.when(ki == 0)
        def _():
            m_scr[...] = jnp.full(m_scr.shape, NEG, F32)
            acc_scr[...] = jnp.zeros_like(acc_scr)
            sc_a[...] = scores(kv_ref, kr_ref)

        def step(cur, nxt, diag):
            if not diag:
                nxt[...] = scores(kvn_ref, krn_ref)
            sc = cur[...]
            if diag:
                sc = _causal(sc)
            m_prev = m_scr[...]
            m_next = jnp.maximum(m_prev, jnp.max(sc, axis=1, keepdims=True))
            a = jnp.exp2(MLA_C2 * (m_prev - m_next))
            p = jnp.exp2(MLA_C2 * sc - MLA_C2 * m_next[:, 0:1]).astype(BF16)
            v1 = jnp.concatenate([kv_ref[:, LANES:2 * LANES], jnp.ones((t, LANES), BF16)], axis=1)
            pv = jnp.dot(p, v1, preferred_element_type=F32)
            acc_scr[:, 0:LANES] = a * acc_scr[:, 0:LANES] + pv[:, 0:LANES]
            acc_scr[:, LANES:2 * LANES] = a * acc_scr[:, LANES:2 * LANES] + pv[:, LANES:2 * LANES]
            m_scr[...] = m_next

        even = ki % 2 == 0
        for is_even, cur, nxt in ((True, sc_a, sc_b), (False, sc_b, sc_a)):
            mine = even if is_even else jnp.logical_not(even)

            @pl.when(jnp.logical_and(mine, ki < qi))
            def _():
                step(cur, nxt, False)

            @pl.when(jnp.logical_and(mine, ki == qi))
            def _():
                step(cur, nxt, True)

        @pl.when(ki == qi)
        def _():
            l = acc_scr[:, LANES:2 * LANES]
            o_ref[...] = (acc_scr[:, 0:LANES] / l).astype(BF16)
            lse_ref[...] = MLA_SCALE * m_scr[...] + jnp.log(l)

    qblk = lambda w: pl.BlockSpec((t, w), lambda h, i, qt, kt: (qt[i], h))
    nxt_blk = lambda i, qt, kt: jnp.where(kt[i] < qt[i], kt[i] + 1, kt[i])
    return _tri_call(
        body, rider, name=name, grid=(MLA_H, n * (n + 1) // 2), tables=_tri_tables(n, queries_outer=True),
        in_specs=[qblk(2 * LANES),
                  pl.BlockSpec((t, 2 * LANES), lambda h, i, qt, kt: (kt[i], h)),
                  pl.BlockSpec((t, 2 * LANES), lambda h, i, qt, kt: (nxt_blk(i, qt, kt), h)),
                  pl.BlockSpec((t, LANES), lambda h, i, qt, kt: (kt[i], 0)),
                  pl.BlockSpec((t, LANES), lambda h, i, qt, kt: (nxt_blk(i, qt, kt), 0))],
        out_specs=[qblk(LANES), qblk(LANES)],
        out_shape=[jax.ShapeDtypeStruct((s, MLA_H * MLA_V), BF16), jax.ShapeDtypeStruct((s, MLA_H * LANES), F32)],
        scratch_shapes=[pltpu.VMEM((t, LANES), F32), pltpu.VMEM((t, 2 * LANES), F32), pltpu.VMEM((t, t), F32),
                        pltpu.VMEM((t, t), F32)],
        operands=(qq, kv, kv, kr, kr))


def _mla_bwd_stats(name, do, o, lse):
    s = do.shape[0]
    tr = min(4 * ROW_TILE, s)

    def body(do_ref, o_ref, lse_ref, stat_ref):
        dl = jnp.sum(do_ref[...].astype(F32) * o_ref[...].astype(F32), axis=1, keepdims=True)
        delta_t = jnp.transpose(jnp.broadcast_to(dl, (tr, LANES)))[0:8]
        lse_t = jnp.transpose(lse_ref[...] * LOG2E)[0:8]
        rows = lax.broadcasted_iota(jnp.int32, (8, tr), 0)
        stat_ref[0] = jnp.where(rows == 0, lse_t, jnp.where(rows == 1, delta_t, 0.0))

    blk = pl.BlockSpec((tr, LANES), lambda h, i: (i, h))
    return pl.pallas_call(
        body, name=name, grid=(MLA_H, s // tr),
        in_specs=[blk, blk, blk],
        out_specs=pl.BlockSpec((1, 8, tr), lambda h, i: (h, 0, i)),
        out_shape=jax.ShapeDtypeStruct((MLA_H, 8, s), F32),
        compiler_params=_cparams("parallel", "parallel"),
    )(do, o, lse)


def _mla_dq_post(name, dq, rc, rsa, rsb):
    s = dq.shape[1]
    tr = min(4 * ROW_TILE, s)

    def body(dq_ref, c_ref, sa_ref, sb_ref, o_ref):
        o_ref[:, 0:LANES] = (MLA_SCALE * dq_ref[0, :, 0:LANES]).astype(BF16)
        o_ref[:, LANES:2 * LANES] = _rope_t(MLA_SCALE * dq_ref[0, :, LANES:2 * LANES], c_ref[...], sa_ref[...],
                                            sb_ref[...], MLA_ROPE // 2).astype(BF16)

    tab = pl.BlockSpec((tr, LANES), lambda h, i: (i, 0))
    return pl.pallas_call(
        body, name=name, grid=(MLA_H, s // tr),
        in_specs=[pl.BlockSpec((1, tr, 2 * LANES), lambda h, i: (h, i, 0)), tab, tab, tab],
        out_specs=pl.BlockSpec((tr, 2 * LANES), lambda h, i: (i, h)),
        out_shape=jax.ShapeDtypeStruct((s, 2 * MLA_H * LANES), BF16),
        compiler_params=_cparams("parallel", "parallel"),
    )(dq, rc, rsa, rsb)


def _mla_attn_bwd(name, qq, kv, kr, do, stats, rider=None):
    s = qq.shape[0]
    t = _attn_tile(s)
    n = s // t

    def body(qi_tab, ki_tab, q_ref, kv_ref, kr_ref, do_ref, stat_ref, dkv_ref, dkr_ref, dq_hbm,
             dk_scr, dv_scr, dq_scr, dq_sem):
        h = pl.program_id(0)
        qi = qi_tab[pl.program_id(1)]
        ki = ki_tab[pl.program_id(1)]

        @pl.when(pl.program_id(1) == 0)
        def _():
            dq_scr[...] = jnp.zeros_like(dq_scr)

        @pl.when(qi == ki)
        def _():
            dk_scr[...] = jnp.zeros_like(dk_scr)
            dv_scr[...] = jnp.zeros_like(dv_scr)

        def step(diag):
            q = q_ref[...]
            k = jnp.concatenate([kv_ref[:, 0:LANES], kr_ref[...]], axis=1)
            sc = lax.dot_general(k, q, NT_DIMS, preferred_element_type=F32)
            if diag:
                sc = _causal(sc, transposed=True)
            p = jnp.exp2(MLA_C2 * sc - stat_ref[0, 0:1, :])
            dov = do_ref[...]
            dp = lax.dot_general(kv_ref[:, LANES:2 * LANES], dov, NT_DIMS, preferred_element_type=F32)
            ds = (p * (dp - stat_ref[0, 1:2, :])).astype(BF16)
            dv_scr[...] += jnp.dot(p.astype(BF16), dov, preferred_element_type=F32)
            dk_scr[...] += jnp.dot(ds, q, preferred_element_type=F32)
            rows = pl.ds(pl.multiple_of(qi * t, t), t)
            dq_scr[rows, :] += lax.dot_general(ds, k, TN_DIMS, preferred_element_type=F32)

        @pl.when(qi == ki)
        def _():
            step(True)

        @pl.when(qi > ki)
        def _():
            step(False)

        @pl.when(qi == n - 1)
        def _():
            dkv_ref[:, 0:LANES] = (MLA_SCALE * dk_scr[:, 0:LANES]).astype(BF16)
            dkv_ref[:, LANES:2 * LANES] = dv_scr[...].astype(BF16)
            dkr_ref[0] = MLA_SCALE * dk_scr[:, LANES:2 * LANES]

        @pl.when(pl.program_id(1) == n * (n + 1) // 2 - 1)
        def _():
            cp = pltpu.make_async_copy(dq_scr, dq_hbm.at[h], dq_sem)
            cp.start()
            cp.wait()

    qblk = lambda w: pl.BlockSpec((t, w), lambda h, i, qt, kt: (qt[i], h))
    kblk = pl.BlockSpec((t, 2 * LANES), lambda h, i, qt, kt: (kt[i], h))
    return _tri_call(
        body, rider, name=name, grid=(MLA_H, n * (n + 1) // 2), tables=_tri_tables(n, queries_outer=False),
        in_specs=[qblk(2 * LANES), kblk, pl.BlockSpec((t, LANES), lambda h, i, qt, kt: (kt[i], 0)), qblk(LANES),
                  pl.BlockSpec((1, 8, t), lambda h, i, qt, kt: (h, 0, qt[i]))],
        out_specs=[kblk, pl.BlockSpec((1, t, LANES), lambda h, i, qt, kt: (h, kt[i], 0)),
                   pl.BlockSpec(memory_space=pl.ANY)],
        out_shape=[jax.ShapeDtypeStruct((s, 2 * MLA_H * LANES), BF16),
                   jax.ShapeDtypeStruct((MLA_H, s, LANES), F32),
                   jax.ShapeDtypeStruct((MLA_H, s, 2 * LANES), F32)],
        scratch_shapes=[pltpu.VMEM((t, 2 * LANES), F32), pltpu.VMEM((t, LANES), F32),
                        pltpu.VMEM((s, 2 * LANES), F32), pltpu.SemaphoreType.DMA(())],
        operands=(qq, kv, kr, do, stats))


def _rope_groups(acc, o_ref, c, sa, sb, sh, groups):
    for gi in range(acc.shape[1] // LANES):
        blk = acc[:, gi * LANES:(gi + 1) * LANES]
        if gi in groups:
            blk = _rope(blk, c, sa, sb, sh)
        o_ref[:, gi * LANES:(gi + 1) * LANES] = blk.astype(o_ref.dtype)


def _mla_fwd(tag, h, w, tabs, rider=None):
    s = h.shape[0]
    rc, rsa, rsb = tabs
    lat = _mm(f"{tag}_lat", h, w["w_in"], tm=512)
    qn, kvn, kr = _mla_lat_post(f"{tag}_latpost", lat, w["q_norm"], w["kv_norm"], rc, rsa, rsb)
    tm = min(512, s)

    def q_epi(acc, o_ref, c_ref, sa_ref, sb_ref):
        _rope_groups(acc, o_ref, c_ref[...], sa_ref[...], sb_ref[...], MLA_ROPE // 2, range(1, MLA_H, 2))

    tab = pl.BlockSpec((tm, LANES), lambda i, j: (i, 0))
    qq = _mm(f"{tag}_q", qn, w["w_q"], tm=512, tn=MLA_H * LANES, out_dtype=BF16, epilogue=q_epi,
             extras=(rc, rsa, rsb), extra_specs=(tab, tab, tab))
    kv = _mm(f"{tag}_kv", kvn, w["w_kv"], tm=512, out_dtype=BF16)
    o, lse, *ridden = _mla_attn_fwd(f"{tag}_attn", qq, kv, kr, rider)
    res = dict(h=h, lat=lat, qn=qn, kvn=kvn, kr=kr, qq=qq, kv=kv, o=o, lse=lse)
    return o, res, (ridden[0] if ridden else None)


def _mla_bwd(tag, dy, res, w, tabs, make_rider=None):
    rc, rsa, rsb = tabs
    do = _mm(f"{tag}_do", dy, w["w_o_t"], tm=512, out_dtype=BF16)
    g_wo = _mm_tn(f"{tag}_gwo", res["o"], dy)
    stats = _mla_bwd_stats(f"{tag}_stats", do, res["o"], res["lse"])
    rider = make_rider(g_wo) if make_rider is not None else None
    dkv, dkr, dq, *ridden = _mla_attn_bwd(f"{tag}_attnbwd", res["qq"], res["kv"], res["kr"], do, stats, rider)
    dqq = _mla_dq_post(f"{tag}_dqpost", dq, rc, rsa, rsb)
    dqn = _mm(f"{tag}_dqn", dqq, w["w_q_t"], tm=512)
    g_wq = _mm_tn(f"{tag}_gwq", res["qn"], dqq, tn=1024)
    dkvn = _mm(f"{tag}_dkvn", dkv, w["w_kv_t"], tm=512)
    g_wkv = _mm_tn(f"{tag}_gwkv", res["kvn"], dkv, tn=1024)
    dlat, g_qn, g_kvn = _mla_lat_bwd(f"{tag}_latbwd", res["lat"], dqn, dkvn, dkr, w["q_norm"], w["kv_norm"],
                                     rc, rsa, rsb)
    dh = _mm(f"{tag}_dh", dlat, w["w_in_t"], tm=512)
    g_win = _mm_tn(f"{tag}_gwin", res["h"], dlat)
    grads = dict(w_in=g_win, q_norm=g_qn, w_q=g_wq, kv_norm=g_kvn, w_kv=g_wkv, w_o=g_wo)
    return dh, grads, (ridden[0] if ridden else None)


SWA_QW = SWA_HQ * SWA_HD
SWA_KW = SWA_HKV * LANES
SWA_NQKV = SWA_QW + 2 * SWA_KW
SWA_SCALE = SWA_HD ** -0.5
SWA_GROUP_ROWS = 4 * SWA_W


def _swa_tile(s):
    return min(512, max(SWA_W, s // 2))


def _swa_masks():
    lane = lax.broadcasted_iota(jnp.int32, (SWA_W, LANES), 1)
    return lane < SWA_HD


def _swa_q4(qa, qb, lo):
    z = jnp.zeros_like(qa)
    return jnp.concatenate([jnp.where(lo, qa, z), jnp.where(lo, z, qa), jnp.where(lo, qb, z), jnp.where(lo, z, qb)],
                           axis=0)


def _swa_probs(q4, kwin, sink_col, first_block):
    sc = lax.dot_general(q4, kwin, NT_DIMS, preferred_element_type=F32) * SWA_SCALE
    row = lax.broadcasted_iota(jnp.int32, sc.shape, 0) % SWA_W
    col = lax.broadcasted_iota(jnp.int32, sc.shape, 1)
    rel = row + SWA_W - col
    ok = (rel >= 0) & (rel < SWA_W) & ((col >= SWA_W) | jnp.logical_not(first_block))
    sc = jnp.where(ok, sc, NEG)
    m = jnp.maximum(jnp.max(sc, axis=1, keepdims=True), sink_col)
    e = jnp.exp(sc - m)
    es = jnp.exp(sink_col - m)
    inv = 1.0 / (jnp.sum(e, axis=1, keepdims=True) + es)
    return e * inv, es * inv


def _sink_col(sinks_ref, grp):
    seg = lax.broadcasted_iota(jnp.int32, (SWA_GROUP_ROWS, 1), 0) // SWA_W
    col = jnp.zeros((SWA_GROUP_ROWS, 1), F32)
    for j in range(4):
        col = jnp.where(seg == j, sinks_ref[0, 4 * grp + j], col)
    return col


def _swa_attn_fwd(name, qkv, sinks):
    s = qkv.shape[0]
    t = _swa_tile(s)
    nb = t // SWA_W

    def body(sinks_ref, q_ref, kv_ref, kvp_ref, o_ref):
        i = pl.program_id(0)
        lo = _swa_masks()
        for grp in range(SWA_HKV):
            sink_col = _sink_col(sinks_ref, grp)
            kcat = jnp.concatenate([kvp_ref[:, grp * LANES:(grp + 1) * LANES],
                                    kv_ref[:, grp * LANES:(grp + 1) * LANES]], axis=0)
            vcat = jnp.concatenate([kvp_ref[:, SWA_KW + grp * LANES:SWA_KW + (grp + 1) * LANES],
                                    kv_ref[:, SWA_KW + grp * LANES:SWA_KW + (grp + 1) * LANES]], axis=0)
            for b in range(nb):
                r0 = b * SWA_W
                qa = q_ref[r0:r0 + SWA_W, grp * 2 * LANES:grp * 2 * LANES + LANES]
                qb = q_ref[r0:r0 + SWA_W, grp * 2 * LANES + LANES:(grp + 1) * 2 * LANES]
                first = jnp.logical_and(i == 0, b == 0)
                p, _ = _swa_probs(_swa_q4(qa, qb, lo), kcat[r0:r0 + 2 * SWA_W], sink_col, first)
                o4 = jnp.dot(p.astype(BF16), vcat[r0:r0 + 2 * SWA_W], preferred_element_type=F32)
                oa = jnp.where(lo, o4[0:SWA_W], o4[SWA_W:2 * SWA_W])
                ob = jnp.where(lo, o4[2 * SWA_W:3 * SWA_W], o4[3 * SWA_W:4 * SWA_W])
                o_ref[r0:r0 + SWA_W, grp * 2 * LANES:grp * 2 * LANES + LANES] = oa.astype(BF16)
                o_ref[r0:r0 + SWA_W, grp * 2 * LANES + LANES:(grp + 1) * 2 * LANES] = ob.astype(BF16)

    return pl.pallas_call(
        body, name=name, grid=(s // t,),
        in_specs=[pl.BlockSpec(memory_space=pltpu.SMEM),
                  pl.BlockSpec((t, SWA_QW), lambda i: (i, 0)),
                  pl.BlockSpec((t, 2 * SWA_KW), lambda i: (i, 1)),
                  pl.BlockSpec((SWA_W, 2 * SWA_KW), lambda i: (jnp.maximum(i * nb - 1, 0), 1))],
        out_specs=pl.BlockSpec((t, SWA_QW), lambda i: (i, 0)),
        out_shape=jax.ShapeDtypeStruct((s, SWA_QW), BF16),
        compiler_params=_cparams("parallel"),
    )(sinks, qkv, qkv, qkv)


def _swa_attn_bwd(name, qkv, sinks, do):
    s = qkv.shape[0]
    t = _swa_tile(s)
    nb = t // SWA_W
    nt = s // t

    def body(sinks_ref, q_ref, kv_ref, kvp_ref, do_ref, dq_ref, dkv_ref, dkvp_ref, dsink_ref, dcat, sink_acc):
        i = pl.program_id(0)
        lo = _swa_masks()

        @pl.when(i == 0)
        def _():
            sink_acc[...] = jnp.zeros_like(sink_acc)

        dcat[...] = jnp.zeros_like(dcat)
        for grp in range(SWA_HKV):
            sink_col = _sink_col(sinks_ref, grp)
            kcat = jnp.concatenate([kvp_ref[:, grp * LANES:(grp + 1) * LANES],
                                    kv_ref[:, grp * LANES:(grp + 1) * LANES]], axis=0)
            vcat = jnp.concatenate([kvp_ref[:, SWA_KW + grp * LANES:SWA_KW + (grp + 1) * LANES],
                                    kv_ref[:, SWA_KW + grp * LANES:SWA_KW + (grp + 1) * LANES]], axis=0)
            for b in range(nb):
                r0 = b * SWA_W
                ca = slice(grp * 2 * LANES, grp * 2 * LANES + LANES)
                cb = slice(grp * 2 * LANES + LANES, (grp + 1) * 2 * LANES)
                q4 = _swa_q4(q_ref[r0:r0 + SWA_W, ca], q_ref[r0:r0 + SWA_W, cb], lo)
                do4 = _swa_q4(do_ref[r0:r0 + SWA_W, ca], do_ref[r0:r0 + SWA_W, cb], lo)
                first = jnp.logical_and(i == 0, b == 0)
                kwin = kcat[r0:r0 + 2 * SWA_W]
                vwin = vcat[r0:r0 + 2 * SWA_W]
                p, ps = _swa_probs(q4, kwin, sink_col, first)
                dp = lax.dot_general(do4, vwin, NT_DIMS, preferred_element_type=F32)
                rowdot = jnp.sum(p * dp, axis=1, keepdims=True)
                ds = (p * (dp - rowdot) * SWA_SCALE).astype(BF16)
                sink_acc[grp] += jnp.broadcast_to(-ps * rowdot, (SWA_GROUP_ROWS, LANES))
                dq4 = jnp.dot(ds, kwin, preferred_element_type=F32)
                dq_ref[r0:r0 + SWA_W, ca] = jnp.where(lo, dq4[0:SWA_W], dq4[SWA_W:2 * SWA_W])
                dq_ref[r0:r0 + SWA_W, cb] = jnp.where(lo, dq4[2 * SWA_W:3 * SWA_W], dq4[3 * SWA_W:4 * SWA_W])
                dk = lax.dot_general(ds, q4, TN_DIMS, preferred_element_type=F32)
                dv = lax.dot_general(p.astype(BF16), do4, TN_DIMS, preferred_element_type=F32)
                dcat[r0:r0 + 2 * SWA_W, grp * LANES:(grp + 1) * LANES] += dk
                dcat[r0:r0 + 2 * SWA_W, SWA_KW + grp * LANES:SWA_KW + (grp + 1) * LANES] += dv
        dkvp_ref[0] = dcat[0:SWA_W]
        dkv_ref[...] = dcat[SWA_W:SWA_W + t]

        @pl.when(i == nt - 1)
        def _():
            for grp in range(SWA_HKV):
                for j in range(4):
                    tot = jnp.sum(sink_acc[grp, j * SWA_W:(j + 1) * SWA_W, 0:1])
                    dsink_ref[4 * grp + j:4 * grp + j + 1, :] = jnp.full((1, LANES), tot, F32)

    return pl.pallas_call(
        body, name=name, grid=(nt,),
        in_specs=[pl.BlockSpec(memory_space=pltpu.SMEM),
                  pl.BlockSpec((t, SWA_QW), lambda i: (i, 0)),
                  pl.BlockSpec((t, 2 * SWA_KW), lambda i: (i, 1)),
                  pl.BlockSpec((SWA_W, 2 * SWA_KW), lambda i: (jnp.maximum(i * nb - 1, 0), 1)),
                  pl.BlockSpec((t, SWA_QW), lambda i: (i, 0))],
        out_specs=[pl.BlockSpec((t, SWA_QW), lambda i: (i, 0)), pl.BlockSpec((t, 2 * SWA_KW), lambda i: (i, 0)),
                   pl.BlockSpec((1, SWA_W, 2 * SWA_KW), lambda i: (i, 0, 0)),
                   pl.BlockSpec((SWA_HQ, LANES), lambda i: (0, 0))],
        out_shape=[jax.ShapeDtypeStruct((s, SWA_QW), F32), jax.ShapeDtypeStruct((s, 2 * SWA_KW), F32),
                   jax.ShapeDtypeStruct((nt, SWA_W, 2 * SWA_KW), F32), jax.ShapeDtypeStruct((SWA_HQ, LANES), F32)],
        scratch_shapes=[pltpu.VMEM((SWA_W + t, 2 * SWA_KW), F32), pltpu.VMEM((SWA_HKV, SWA_GROUP_ROWS, LANES), F32)],
        compiler_params=_cparams("arbitrary"),
    )(sinks, qkv, qkv, qkv, do)


def _swa_dqkv(name, dq, dkv, dkvp, rc, rsa, rsb):
    s = dq.shape[0]
    t = _swa_tile(s)
    nt = s // t
    sh = SWA_ROT // 2

    def body(dq_ref, dkv_ref, dkvn_ref, c_ref, sa_ref, sb_ref, out_ref, bsum_ref, acc):
        i = pl.program_id(0)
        c, sa, sb = c_ref[...], sa_ref[...], sb_ref[...]
        lo = lax.broadcasted_iota(jnp.int32, (t, LANES), 1) < SWA_HD
        rows = lax.broadcasted_iota(jnp.int32, (t, LANES), 0)
        tail = jnp.logical_and(rows >= t - SWA_W, i < nt - 1)

        @pl.when(i == 0)
        def _():
            acc[...] = jnp.zeros_like(acc)

        for gi in range(SWA_QW // LANES):
            blk = _rope_t(dq_ref[:, gi * LANES:(gi + 1) * LANES], c, sa, sb, sh)
            out_ref[:, gi * LANES:(gi + 1) * LANES] = blk.astype(BF16)
            acc[:, gi * LANES:(gi + 1) * LANES] += _rowsum8(blk)
        for gi in range(2 * SWA_KW // LANES):
            cols = slice(gi * LANES, (gi + 1) * LANES)
            nxt = jnp.concatenate([jnp.zeros((t - SWA_W, LANES), F32), dkvn_ref[0, :, cols]], axis=0)
            blk = dkv_ref[:, cols] + jnp.where(tail, nxt, 0.0)
            blk = jnp.where(lo, blk + pltpu.roll(blk, SWA_HD, 1), 0.0)
            if gi < SWA_HKV:
                blk = _rope_t(blk, c, sa, sb, sh)
            out_ref[:, SWA_QW + gi * LANES:SWA_QW + (gi + 1) * LANES] = blk.astype(BF16)
            acc[:, SWA_QW + gi * LANES:SWA_QW + (gi + 1) * LANES] += _rowsum8(blk)

        @pl.when(i == nt - 1)
        def _():
            bsum_ref[...] = jnp.sum(acc[...], axis=0, keepdims=True)

    return pl.pallas_call(
        body, name=name, grid=(nt,),
        in_specs=[pl.BlockSpec((t, SWA_QW), lambda i: (i, 0)), pl.BlockSpec((t, 2 * SWA_KW), lambda i: (i, 0)),
                  pl.BlockSpec((1, SWA_W, 2 * SWA_KW), lambda i: (jnp.minimum(i + 1, nt - 1), 0, 0))]
        + [_row_spec(t, LANES)] * 3,
        out_specs=[pl.BlockSpec((t, SWA_NQKV), lambda i: (i, 0)), pl.BlockSpec((1, SWA_NQKV), lambda i: (0, 0))],
        out_shape=[jax.ShapeDtypeStruct((s, SWA_NQKV), BF16), jax.ShapeDtypeStruct((1, SWA_NQKV), F32)],
        scratch_shapes=[pltpu.VMEM((8, SWA_NQKV), F32)],
        compiler_params=_cparams("arbitrary"),
    )(dq, dkv, dkvp, rc, rsa, rsb)


def _swa_fwd(tag, h, w, tabs):
    s = h.shape[0]
    rc, rsa, rsb = tabs
    tm = min(512, s)
    sh = SWA_ROT // 2

    def qkv_epi(acc, o_ref, b_ref, c_ref, sa_ref, sb_ref):
        acc = acc + b_ref[...]

        @pl.when(pl.program_id(1) == 0)
        def _():
            _rope_groups(acc, o_ref, c_ref[...], sa_ref[...], sb_ref[...], sh, range(SWA_QW // LANES))

        @pl.when(pl.program_id(1) == 1)
        def _():
            _rope_groups(acc, o_ref, c_ref[...], sa_ref[...], sb_ref[...], sh, range(SWA_HKV))

    tab = pl.BlockSpec((tm, LANES), lambda i, j: (i, 0))
    qkv = _mm(f"{tag}_qkv", h, w["w_qkv"], tm=512, tn=SWA_QW, out_dtype=BF16, epilogue=qkv_epi,
              extras=(w["b_qkv"], rc, rsa, rsb),
              extra_specs=(pl.BlockSpec((1, SWA_QW), lambda i, j: (0, j)), tab, tab, tab))
    o = _swa_attn_fwd(f"{tag}_attn", qkv, w["sinks"])

    def o_epi(acc, o_ref, b_ref):
        o_ref[...] = acc + b_ref[...]

    y = _mm(f"{tag}_o", o, w["w_o"], tm=512, epilogue=o_epi, extras=(w["b_o"],),
            extra_specs=(pl.BlockSpec((1, D), lambda i, j: (0, 0)),))
    return y, dict(h=h, qkv=qkv, o=o)


def _swa_bwd(tag, dy, res, w, tabs):
    rc, rsa, rsb = tabs
    do = _mm(f"{tag}_do", dy, w["w_o_t"], tm=512, out_dtype=BF16)
    g_wo = _mm_tn(f"{tag}_gwo", res["o"], dy)
    dq, dkv, dkvp, dsink = _swa_attn_bwd(f"{tag}_attnbwd", res["qkv"], w["sinks"], do)
    dqkv, g_b = _swa_dqkv(f"{tag}_dqkv", dq, dkv, dkvp, rc, rsa, rsb)
    dh = _mm(f"{tag}_dh", dqkv, w["w_qkv_t"], tm=512)
    g_wqkv = _mm_tn(f"{tag}_gwqkv", res["h"], dqkv, tn=1024)
    return dh, dict(w_qkv=g_wqkv, b_qkv=g_b, sinks=dsink, w_o=g_wo)


def _ada_fwd(name, c_all, w_sh, b_sh):
    cols = w_sh.shape[2]
    tn = cols // 3

    def body(c_ref, w_ref, b_ref, o_ref, cond_ref):
        cv = c_ref[...]
        cond = cv * jax.nn.sigmoid(cv)
        cond_ref[...] = cond
        o_ref[0] = jnp.dot(cond, w_ref[0], preferred_element_type=F32, precision=lax.Precision.HIGHEST) + b_ref[0]

    return pl.pallas_call(
        body, name=name, grid=(DEPTH, cols // tn),
        in_specs=[pl.BlockSpec((8, D), lambda l, j: (0, 0)), pl.BlockSpec((1, D, tn), lambda l, j: (l, 0, j)),
                  pl.BlockSpec((1, 1, tn), lambda l, j: (l, 0, j))],
        out_specs=[pl.BlockSpec((1, 8, tn), lambda l, j: (l, 0, j)), pl.BlockSpec((8, D), lambda l, j: (0, 0))],
        out_shape=[jax.ShapeDtypeStruct((DEPTH, 8, cols), F32), jax.ShapeDtypeStruct((8, D), F32)],
        compiler_params=_cparams("arbitrary", "arbitrary"),
    )(c_all, w_sh, b_sh)


def _ada_grad(name, cond_t, dmod_sh):
    cols = dmod_sh.shape[2]
    tn = cols // 3

    def body(ct_ref, dm_ref, o_ref):
        acc = ct_ref[:, 0:1] * dm_ref[0, 0:1, :]
        for b in range(1, 8):
            acc = acc + ct_ref[:, b:b + 1] * dm_ref[0, b:b + 1, :]
        o_ref[0] = acc

    return pl.pallas_call(
        body, name=name, grid=(DEPTH, cols // tn),
        in_specs=[pl.BlockSpec((D, 8), lambda l, j: (0, 0)), pl.BlockSpec((1, 8, tn), lambda l, j: (l, 0, j))],
        out_specs=pl.BlockSpec((1, D, tn), lambda l, j: (l, 0, j)),
        out_shape=jax.ShapeDtypeStruct((DEPTH, D, cols), F32),
        compiler_params=_cparams("parallel", "parallel"),
    )(cond_t, dmod_sh)


def _adamw(name, g, w, m, v):
    r, cols = g.shape
    tile_elems = 512 * 1024
    tr = r if r * cols <= tile_elems else max(d for d in (512, 256, 128, 64, 32, 16, 8)
                                               if r % d == 0 and d * cols <= tile_elems)

    def body(g_ref, w_ref, m_ref, v_ref, d_ref, nm_ref, nv_ref):
        gv = g_ref[...]
        mn = ADAM_B1 * m_ref[...] + (1.0 - ADAM_B1) * gv
        vn = ADAM_B2 * v_ref[...] + (1.0 - ADAM_B2) * (gv * gv)
        m_hat = mn / (1.0 - ADAM_B1 ** ADAM_STEP)
        v_hat = vn / (1.0 - ADAM_B2 ** ADAM_STEP)
        d_ref[...] = -ADAM_LR * (m_hat / (jnp.sqrt(v_hat) + ADAM_EPS) + ADAM_WD * w_ref[...])
        nm_ref[...] = mn
        nv_ref[...] = vn

    spec = _row_spec(tr, cols)
    return pl.pallas_call(
        body, name=name, grid=(r // tr,),
        in_specs=[spec] * 4, out_specs=[spec] * 3,
        out_shape=[jax.ShapeDtypeStruct(g.shape, F32)] * 3,
        compiler_params=_cparams("parallel"),
    )(g, w, m, v)


def _to_chips(full, axis):
    shp = full.shape
    a = full.reshape(shp[:axis] + (N_CHIPS, shp[axis] // N_CHIPS) + shp[axis + 1:])
    return jnp.moveaxis(a, axis, 0)


def _from_chips(stacked, axis):
    a = jnp.moveaxis(stacked, 0, axis)
    shp = a.shape
    return a.reshape(shp[:axis] + (shp[axis] * shp[axis + 1],) + shp[axis + 2:])


PIECE_ROW_ALIGN = 16


def _piece_rows(shape):
    n = 1
    for d in shape:
        n *= d
    rows = -(-n // PACK_COLS)
    return -(-rows // PIECE_ROW_ALIGN) * PIECE_ROW_ALIGN


def _as_rows(a, lead):
    head = a.shape[:lead]
    rows = _piece_rows(a.shape[lead:])
    n = 1
    for d in a.shape[lead:]:
        n *= d
    if n == rows * PACK_COLS:
        return a.reshape(head + (rows, PACK_COLS))
    flat = jnp.pad(a.reshape(head + (n,)), [(0, 0)] * lead + [(0, rows * PACK_COLS - n)])
    return flat.reshape(head + (rows, PACK_COLS))


def _pack(parts, lead, rows):
    pieces = [_as_rows(p, lead) for p in parts]
    used = sum(p.shape[lead] for p in pieces)
    head = pieces[0].shape[:lead]
    pieces.append(jnp.zeros(head + (rows - used, PACK_COLS), pieces[0].dtype))
    return jnp.concatenate(pieces, axis=lead)


def _unpack(packed, lead, shapes):
    out, off = [], 0
    head = packed.shape[:lead]
    for shp in shapes:
        rows = _piece_rows(shp)
        n = 1
        for d in shp:
            n *= d
        piece = lax.slice_in_dim(packed, off, off + rows, axis=lead)
        if n != rows * PACK_COLS:
            piece = piece.reshape(head + (rows * PACK_COLS,))[..., :n]
        out.append(piece.reshape(head + tuple(shp)))
        off += rows
    return out


def _pack_rows(shapes):
    rows = sum(_piece_rows(s) for s in shapes)
    return -(-rows // PACK_ROW_ALIGN) * PACK_ROW_ALIGN


def _rope_tables(positions, rot, lanes_per_head):
    half = rot // 2
    inv = ROPE_THETA ** (-jnp.arange(0, rot, 2, dtype=F32) / rot)
    ang = positions.astype(F32)[:, None] * inv
    cos, sin = jnp.cos(ang), jnp.sin(ang)
    s = positions.shape[0]
    rest = lanes_per_head - rot
    fill = 1.0 if lanes_per_head == SWA_HD else 0.0
    c = jnp.concatenate([cos, cos, jnp.full((s, rest), fill, F32)], axis=1)
    sa = jnp.concatenate([-sin, jnp.zeros((s, half + rest), F32)], axis=1)
    sb = jnp.concatenate([jnp.zeros((s, half), F32), sin, jnp.zeros((s, rest), F32)], axis=1)
    reps = LANES // lanes_per_head
    return tuple(jnp.tile(t, (1, reps)) for t in (c, sa, sb))


def _mla_weights(w_in, q_norm, w_q_b, kv_norm, w_kv_b):
    w_in_p = jnp.pad(w_in, ((0, 0), (0, MLA_LAT - w_in.shape[1])))
    wq = w_q_b.reshape(MLA_QR, MLA_H, MLA_NOPE + MLA_ROPE)
    wq_p = jnp.pad(wq, ((0, 0), (0, 0), (0, 2 * LANES - MLA_NOPE - MLA_ROPE))).reshape(MLA_QR, MLA_H * 2 * LANES)
    return dict(w_in=w_in_p, w_in_t=w_in_p.T, q_norm=q_norm.reshape(1, -1), kv_norm=kv_norm.reshape(1, -1),
                w_q=wq_p, w_q_t=wq_p.T, w_kv=w_kv_b, w_kv_t=w_kv_b.T)


def _mla_grads_unpermute(g):
    gq = g["w_q"].reshape(MLA_QR, MLA_H, 2 * LANES)[:, :, :MLA_NOPE + MLA_ROPE]
    return dict(mla_w_in=g["w_in"][:, :MLA_QR + MLA_KVR + MLA_ROPE], mla_q_norm=g["q_norm"][0],
                mla_w_q_b=gq.reshape(MLA_QR, -1), mla_kv_norm=g["kv_norm"][0], mla_w_kv_b=g["w_kv"],
                mla_w_o=g["w_o"])


def _swa_dup(a):
    lead = a.shape[:-1]
    a = a.reshape(lead + (SWA_HKV, SWA_HD))
    return jnp.concatenate([a, a], axis=-1).reshape(lead + (SWA_KW,))


def _swa_undup(a):
    lead = a.shape[:-1]
    return a.reshape(lead + (SWA_HKV, LANES))[..., :SWA_HD].reshape(lead + (SWA_HKV * SWA_HD,))


def _swa_weights(w_qkv, b_qkv, sinks, w_o, b_o):
    nk = SWA_HKV * SWA_HD
    perm = lambda a: jnp.concatenate([a[..., :SWA_QW], _swa_dup(a[..., SWA_QW:SWA_QW + nk]),
                                      _swa_dup(a[..., SWA_QW + nk:])], axis=-1)
    w_p = perm(w_qkv)
    return dict(w_qkv=w_p, w_qkv_t=w_p.T, b_qkv=perm(b_qkv.astype(F32)).reshape(1, -1),
                sinks=sinks.reshape(1, -1), w_o=w_o, w_o_t=w_o.T, b_o=b_o.astype(F32).reshape(1, -1))


def _swa_grads_unpermute(g):
    unperm = lambda a: jnp.concatenate([a[..., :SWA_QW], _swa_undup(a[..., SWA_QW:SWA_QW + SWA_KW]),
                                        _swa_undup(a[..., SWA_QW + SWA_KW:])], axis=-1)
    return dict(swa_w_qkv=unperm(g["w_qkv"]), swa_b_qkv=unperm(g["b_qkv"])[0], swa_sinks=g["sinks"][:, 0],
                swa_w_o=g["w_o"], swa_b_o=g["b_o"])


SMALL_LAYOUT = (("ada_b", 24), ("ln_mix_g", 4), ("ln_mix_b", 4), ("ln_ffn_g", 4), ("ln_ffn_b", 4),
                ("mla_q_norm", 2), ("mla_kv_norm", 2), ("swa_sinks", 1), ("loss", 1))


def _small_pack(vals):
    rows = []
    for name, nrows in SMALL_LAYOUT:
        a = vals[name].reshape(nrows, -1).astype(F32)
        rows.append(jnp.pad(a, ((0, 0), (0, PACK_COLS - a.shape[1]))))
    cat = jnp.concatenate(rows, axis=0)
    return jnp.pad(cat, ((0, SMALL_ROWS - cat.shape[0]), (0, 0)))


def _small_unpack(packed, shapes):
    out, r = {}, 0
    for name, nrows in SMALL_LAYOUT:
        shp = shapes[name]
        n = 1
        for d in shp:
            n *= d
        out[name] = packed[r:r + nrows, :n // nrows].reshape(shp)
        r += nrows
    return out


def kernel(x, c, positions, ada_w, ada_b, ln_mix_g, ln_mix_b, ln_ffn_g, ln_ffn_b, ffn_w_gate, ffn_w_up, ffn_w_down, mla_w_in, mla_q_norm, mla_w_q_b, mla_kv_norm, mla_w_kv_b, mla_w_o, swa_w_qkv, swa_b_qkv, swa_sinks, swa_w_o, swa_b_o, loss_target, m_ada_w, m_ada_b, m_ln_mix_g, m_ln_mix_b, m_ln_ffn_g, m_ln_ffn_b, m_ffn_w_gate, m_ffn_w_up, m_ffn_w_down, m_mla_w_in, m_mla_q_norm, m_mla_w_q_b, m_mla_kv_norm, m_mla_w_kv_b, m_mla_w_o, m_swa_w_qkv, m_swa_b_qkv, m_swa_sinks, m_swa_w_o, m_swa_b_o, v_ada_w, v_ada_b, v_ln_mix_g, v_ln_mix_b, v_ln_ffn_g, v_ln_ffn_b, v_ffn_w_gate, v_ffn_w_up, v_ffn_w_down, v_mla_w_in, v_mla_q_norm, v_mla_w_q_b, v_mla_kv_norm, v_mla_w_kv_b, v_mla_w_o, v_swa_w_qkv, v_swa_b_qkv, v_swa_sinks, v_swa_w_o, v_swa_b_o):
    weights = dict(ada_w=ada_w, ada_b=ada_b, ln_mix_g=ln_mix_g, ln_mix_b=ln_mix_b, ln_ffn_g=ln_ffn_g,
                   ln_ffn_b=ln_ffn_b, ffn_w_gate=ffn_w_gate, ffn_w_up=ffn_w_up, ffn_w_down=ffn_w_down,
                   mla_w_in=mla_w_in, mla_q_norm=mla_q_norm, mla_w_q_b=mla_w_q_b, mla_kv_norm=mla_kv_norm,
                   mla_w_kv_b=mla_w_kv_b, mla_w_o=mla_w_o, swa_w_qkv=swa_w_qkv, swa_b_qkv=swa_b_qkv,
                   swa_sinks=swa_sinks, swa_w_o=swa_w_o, swa_b_o=swa_b_o)
    mom_m = dict(ada_w=m_ada_w, ada_b=m_ada_b, ln_mix_g=m_ln_mix_g, ln_mix_b=m_ln_mix_b, ln_ffn_g=m_ln_ffn_g,
                 ln_ffn_b=m_ln_ffn_b, ffn_w_gate=m_ffn_w_gate, ffn_w_up=m_ffn_w_up, ffn_w_down=m_ffn_w_down,
                 mla_w_in=m_mla_w_in, mla_q_norm=m_mla_q_norm, mla_w_q_b=m_mla_w_q_b, mla_kv_norm=m_mla_kv_norm,
                 mla_w_kv_b=m_mla_w_kv_b, mla_w_o=m_mla_w_o, swa_w_qkv=m_swa_w_qkv, swa_b_qkv=m_swa_b_qkv,
                 swa_sinks=m_swa_sinks, swa_w_o=m_swa_w_o, swa_b_o=m_swa_b_o)
    mom_v = dict(ada_w=v_ada_w, ada_b=v_ada_b, ln_mix_g=v_ln_mix_g, ln_mix_b=v_ln_mix_b, ln_ffn_g=v_ln_ffn_g,
                 ln_ffn_b=v_ln_ffn_b, ffn_w_gate=v_ffn_w_gate, ffn_w_up=v_ffn_w_up, ffn_w_down=v_ffn_w_down,
                 mla_w_in=v_mla_w_in, mla_q_norm=v_mla_q_norm, mla_w_q_b=v_mla_w_q_b, mla_kv_norm=v_mla_kv_norm,
                 mla_w_kv_b=v_mla_w_kv_b, mla_w_o=v_mla_w_o, swa_w_qkv=v_swa_w_qkv, swa_b_qkv=v_swa_b_qkv,
                 swa_sinks=v_swa_sinks, swa_w_o=v_swa_w_o, swa_b_o=v_swa_b_o)
    names = list(weights)
    my_x, my_y, my_c = lax.axis_index("x"), lax.axis_index("y"), lax.axis_index("c")
    chip = 2 * my_x + my_y
    batch_row = 2 * chip + my_c
    xs = x[0]
    target = loss_target[0]
    pos = positions[0]
    s = xs.shape[0]

    def item_shapes(items):
        return [(b - a,) + tuple(weights[n].shape[1:]) for n, a, b, _ in items]

    def pack_items(src, items, dtype):
        return _pack([src[n][a:b].astype(dtype) for n, a, b, _ in items], 0, _pack_rows(item_shapes(items)))

    full = {}

    def unpack_gathered(gathered, items):
        for (n, a, b, axis), part in zip(items, _unpack(gathered, 1, item_shapes(items))):
            whole = _from_chips(part, axis)
            for l in range(a, b):
                full[n, l] = whole[l - a]

    early = _exchange("ag_w_early", pack_items(weights, W_EARLY, BF16), ("x", "y"), "gather")
    unpack_gathered(early, W_EARLY)
    late_ride = _Exchange(pack_items(weights, W_LATE, BF16), ("x", "y"), "gather", chunks=8)

    c_rows = jnp.pad(c, ((0, 7), (0, 0)))
    c_all = _exchange("ag_c", c_rows, ("x", "y", "c"), "gather")[:, 0, :]
    ada_cols = ada_w.shape[2]
    ada_b_sh = lax.dynamic_slice_in_dim(ada_b, chip * ada_cols, ada_cols, axis=1).reshape(DEPTH, 1, ada_cols)
    mod_sh, cond_all = _ada_fwd("ada_fwd", c_all, ada_w, ada_b_sh)
    mod_all = _exchange("ag_mod", mod_sh.reshape(DEPTH * 8, ada_cols), ("x", "y"), "gather")
    mod_all = mod_all.reshape(N_CHIPS, DEPTH, 8, ada_cols)
    mod_mine = lax.dynamic_index_in_dim(mod_all, batch_row, axis=2, keepdims=False)
    mod = jnp.moveaxis(mod_mine, 0, 1).reshape(DEPTH, 6, 1, D)

    tabs_a = _rope_tables(pos, MLA_ROPE, LANES)
    tabs_b = _rope_tables(pos, SWA_ROT, SWA_HD)
    vec = lambda a, l: a[l].reshape(1, D)

    def mla_in_weights(j):
        return _mla_weights(full["mla_w_in", j], mla_q_norm[j], full["mla_w_q_b", j], mla_kv_norm[j],
                            full["mla_w_kv_b", j])

    mix_w, ffn_w = {}, {}
    saved = []
    x_cur = xs
    h = _modulate("mod0", x_cur, mod[0, 1], mod[0, 0])
    for l in range(DEPTH):
        j = l // 2
        if l % 2 == 0:
            mix_w[l] = mla_in_weights(j)
            o, res, ridden = _mla_fwd(f"mla{l}", h, mix_w[l], tabs_a, late_ride if l == 0 else None)
            if l == 0:
                unpack_gathered(ridden, W_LATE)
            mix_w[l].update(w_o=full["mla_w_o", j], w_o_t=full["mla_w_o", j].T)
            y_mix = _mm(f"mla{l}_o", o, mix_w[l]["w_o"], tm=512)
        else:
            mix_w[l] = _swa_weights(full["swa_w_qkv", j], full["swa_b_qkv", j], swa_sinks[j], full["swa_w_o", j],
                                    full["swa_b_o", j])
            y_mix, res = _swa_fwd(f"swa{l}", h, mix_w[l], tabs_b)
        wg, wu, wd = full["ffn_w_gate", l], full["ffn_w_up", l], full["ffn_w_down", l]
        ffn_w[l] = dict(wg=wg, wu=wu, wd=wd, wg_t=wg.T, wu_t=wu.T, wd_t=wd.T)
        x_mid, h2 = _post_mod(f"post_mix{l}", x_cur, y_mix, mod[l, 2], vec(ln_mix_g, l), vec(ln_mix_b, l),
                              mod[l, 4], mod[l, 3])
        gate, up, y_ffn = _ffn_fwd(f"ffn{l}", h2, ffn_w[l]["wg"], ffn_w[l]["wu"], ffn_w[l]["wd"])
        saved.append(dict(x_in=x_cur, y_mix=y_mix, res=res, x_mid=x_mid, h2=h2, gate=gate, up=up, y_ffn=y_ffn))
        if l < DEPTH - 1:
            x_cur, h = _post_mod(f"post_ffn{l}", x_mid, y_ffn, mod[l, 5], vec(ln_ffn_g, l), vec(ln_ffn_b, l),
                                 mod[l + 1, 1], mod[l + 1, 0])
        else:
            dxn, loss_part = _post_loss("post_loss", x_mid, y_ffn, mod[l, 5], vec(ln_ffn_g, l), vec(ln_ffn_b, l),
                                        target)

    gfull = {n: [None] * weights[n].shape[0] for n, _ in SHARDED}
    gsmall = {n: [None] * weights[n].shape[0] for n in ("ln_mix_g", "ln_mix_b", "ln_ffn_g", "ln_ffn_b",
                                                         "mla_q_norm", "mla_kv_norm", "swa_sinks")}
    dmod = [None] * DEPTH

    def grad_ride(items):
        parts = [_to_chips(jnp.stack(gfull[n][a:b]).astype(BF16), axis) for n, a, b, axis in items]
        return _Exchange(_pack(parts, 1, _pack_rows(item_shapes(items))), ("x", "y", "c"), "to_chip", chunks=4)

    def ride_with_wo(items, j):
        def make(g_wo):
            gfull["mla_w_o"][j] = g_wo
            return grad_ride(items)
        return make

    rides = {DEPTH - 2: G_FIRST, 0: G_SECOND}
    g_parts = {}
    sums_f, sums_m, sums_fm, sums_mm = {}, {}, {}, {}
    top = DEPTH - 1
    dxp, dy, sums_f[top] = _post_bwd(f"post_ffn_bwd{top}", dxn, saved[top]["x_mid"], saved[top]["y_ffn"], mod[top, 5],
                                     vec(ln_ffn_g, top))
    for l in reversed(range(DEPTH)):
        sv = saved[l]
        j = l // 2
        dgt, dup, act, dh2 = _ffn_bwd(f"ffn_bwd{l}", dy, sv["gate"], sv["up"], ffn_w[l]["wd_t"], ffn_w[l]["wg_t"],
                                      ffn_w[l]["wu_t"])
        gfull["ffn_w_gate"][l] = _mm_tn(f"ffn_gwg{l}", sv["h2"], dgt, tn=FF // 2)
        gfull["ffn_w_up"][l] = _mm_tn(f"ffn_gwu{l}", sv["h2"], dup, tn=FF // 2)
        gfull["ffn_w_down"][l] = _mm_tn(f"ffn_gwd{l}", act, dy)
        dxp, dy, six = _mod_post_bwd(f"modpost_mix_bwd{l}", dh2, dxp, sv["x_in"], sv["y_mix"], mod[l, 2],
                                     vec(ln_mix_g, l), vec(ln_mix_b, l), mod[l, 4])
        sums_fm[l], sums_m[l] = six[0:2], six[2:6]
        if l % 2 == 0:
            dh, g, ridden = _mla_bwd(f"mla{l}", dy, sv["res"], mix_w[l], tabs_a, ride_with_wo(rides[l], j))
            g_parts[rides[l]] = _sum_groups(f"rs_sum{l}", ridden)
            g = _mla_grads_unpermute(g)
        else:
            dh, g = _swa_bwd(f"swa{l}", dy, sv["res"], mix_w[l], tabs_b)
            g["b_o"] = sums_m[l][3]
            g = _swa_grads_unpermute(g)
        for n, val in g.items():
            (gfull if n in gfull else gsmall)[n][j] = val
        if l > 0:
            below = saved[l - 1]
            dxp, dy, six = _mod_post_bwd(f"modpost_ffn_bwd{l - 1}", dh, dxp, below["x_mid"], below["y_ffn"],
                                         mod[l - 1, 5], vec(ln_ffn_g, l - 1), vec(ln_ffn_b, l - 1), mod[l, 1])
            sums_mm[l], sums_f[l - 1] = six[0:2], six[2:6]
        else:
            dxn, sums_mm[l] = _mod_bwd("mod_mix_bwd0", dh, sv["x_in"], dxp, mod[l, 1])
    for l in range(DEPTH):
        gsmall["ln_ffn_g"][l], gsmall["ln_ffn_b"][l] = sums_f[l][0], sums_f[l][1]
        gsmall["ln_mix_g"][l], gsmall["ln_mix_b"][l] = sums_m[l][0], sums_m[l][1]
        dmod[l] = jnp.stack([sums_mm[l][1], sums_mm[l][0], sums_m[l][2], sums_fm[l][1], sums_fm[l][0], sums_f[l][2]])
    grad_x = dxn[None]

    small_vals = {n: jnp.stack(v) for n, v in gsmall.items()}
    small_vals["ada_b"] = jnp.stack(dmod)
    small_vals["loss"] = loss_part[0, 0:1]
    small_all = _exchange("ag_small", _small_pack(small_vals), ("x", "y", "c"), "gather")
    small_sum = _sum_groups("sum_small", small_all)
    dmod_all = small_all[:, :DEPTH * 6, :].reshape(8, DEPTH, 6 * D)
    dmod_sh = jnp.moveaxis(lax.dynamic_slice_in_dim(dmod_all, chip * ada_cols, ada_cols, axis=2), 0, 1)
    g_ada_w = _ada_grad("ada_grad", cond_all.T, dmod_sh)

    tail = grad_ride(G_LAST)
    g_parts[G_LAST] = _sum_groups("rs_sum_tail", _exchange("rs_tail", tail.src, tail.axes, tail.mode, tail.chunks))

    pieces = {}
    for items in (G_FIRST, G_SECOND, G_LAST):
        for (n, a, _, _), part in zip(items, _unpack(g_parts[items], 0, item_shapes(items))):
            pieces.setdefault(n, []).append((a, part))
    grads = {n: jnp.concatenate([p for _, p in sorted(ps, key=lambda ap: ap[0])], axis=0) for n, ps in pieces.items()}
    grads["ada_w"] = g_ada_w
    small_shapes = {n: weights[n].shape for n, _ in SMALL_LAYOUT if n != "loss"}
    small_shapes["loss"] = (1,)
    grads.update(_small_unpack(small_sum, small_shapes))

    def as_2d(a):
        return a.reshape(-1, a.shape[-1])

    outs = [grads, {}, {}, {}]
    for n in names:
        if n in small_shapes:
            continue
        res = _adamw(f"adamw_{n}", *[as_2d(src[n]) for src in (grads, weights, mom_m, mom_v)])
        for o, r in zip(outs[1:], res):
            o[n] = r.reshape(weights[n].shape)

    def small_of(src):
        return _small_pack({**{n: src[n] for n in small_shapes if n != "loss"}, "loss": jnp.zeros((1,), F32)})

    res = _adamw("adamw_small", small_sum, small_of(weights), small_of(mom_m), small_of(mom_v))
    for o, r in zip(outs[1:], res):
        o.update(_small_unpack(r, small_shapes))
    loss = grads["loss"][0]
    return (loss, grad_x, *[o[n] for o in outs for n in names])
```

```python
import jax
import jax.numpy as jnp
from jax import lax
from jax.experimental import pallas as pl
from jax.experimental.pallas import tpu as pltpu

F32 = jnp.float32
BF16 = jnp.bfloat16

D = 1024
DEPTH = 4
ROPE_THETA = 500000.0
LN_EPS = 1e-5
RMS_EPS = 1e-6
MLA_H = 8
MLA_NOPE = 128
MLA_ROPE = 64
MLA_V = 128
MLA_QR = 384
MLA_KVR = 256
MLA_LAT = 768
SWA_HQ = 16
SWA_HKV = 4
SWA_HD = 64
SWA_W = 128
SWA_ROT = 16
FF = 2816
ALPHA = (2 * DEPTH) ** 0.25
ADAM_LR = 0.001
ADAM_B1 = 0.9
ADAM_B2 = 0.999
ADAM_EPS = 1e-08
ADAM_WD = 0.01
ADAM_STEP = 10
NEG = -1e30
LANES = 128
N_CHIPS = 4
PACK_COLS = 1024
PACK_ROW_ALIGN = 512
SMALL_ROWS = 48
ROW_TILE = 512

SHARDED = (
    ("ffn_w_gate", 2), ("ffn_w_up", 2), ("ffn_w_down", 1), ("mla_w_in", 1), ("mla_w_q_b", 2),
    ("mla_w_kv_b", 2), ("mla_w_o", 1), ("swa_w_qkv", 2), ("swa_b_qkv", 1), ("swa_w_o", 1), ("swa_b_o", 1),
)


def _items(*specs):
    axis = dict(SHARDED)
    return tuple((n, a, b, axis[n]) for names, a, b in specs for n in names)


_FFN = ("ffn_w_gate", "ffn_w_up", "ffn_w_down")
_SWA = ("swa_w_qkv", "swa_b_qkv", "swa_w_o", "swa_b_o")
_MLA_IN = ("mla_w_in", "mla_w_q_b", "mla_w_kv_b")
_MLA_OUT = ("mla_w_o",)
W_EARLY = _items((_MLA_IN, 0, 1))
W_LATE = _items((_FFN, 0, 4), (_MLA_IN, 1, 2), (_MLA_OUT, 0, 2), (_SWA, 0, 2))
G_FIRST = _items((_FFN, 3, 4), (_SWA, 1, 2), (_FFN, 2, 3), (_MLA_OUT, 1, 2))
G_SECOND = _items((_MLA_IN, 1, 2), (_FFN, 1, 2), (_SWA, 0, 1), (_FFN, 0, 1), (_MLA_OUT, 0, 1))
G_LAST = _items((_MLA_IN, 0, 1))


def _cparams(*sem):
    return pltpu.CompilerParams(dimension_semantics=sem)


def _row_spec(tr, cols):
    return pl.BlockSpec((tr, cols), lambda i: (i, 0))


def _vec_spec(cols):
    return pl.BlockSpec((1, cols), lambda i: (0, 0))


def _rope(x, c, sa, sb, sh):
    n = x.shape[1]
    return x * c + pltpu.roll(x, n - sh, 1) * sa + pltpu.roll(x, sh, 1) * sb


def _rope_t(d, c, sa, sb, sh):
    n = d.shape[1]
    return d * c + pltpu.roll(d * sa, sh, 1) + pltpu.roll(d * sb, n - sh, 1)


def _rowsum8(t):
    r, n = t.shape
    return jnp.sum(t.reshape(r // 8, 8, n), axis=0)


class _Exchange:
    def __init__(self, src, axes, mode, chunks=1):
        self.src, self.axes, self.mode, self.chunks = src, axes, mode, chunks
        self.g = 2 ** len(axes)
        self.blk = tuple(src.shape if mode == "gather" else src.shape[1:])
        self.out_shape = jax.ShapeDtypeStruct((self.g,) + self.blk, src.dtype)
        nsem = (self.g - 1) * chunks
        self.scratch = [pltpu.SemaphoreType.DMA((nsem,)), pltpu.SemaphoreType.DMA((nsem,)),
                        pltpu.SemaphoreType.DMA(())]

    def copies(self, src_ref, out_ref, send_sems, recv_sems, loc_sem):
        pos = {a: lax.axis_index(a) for a in ("x", "y", "c")}
        rows = self.blk[0] // self.chunks

        def gidx(p):
            idx = 0
            for a in self.axes:
                idx = idx * 2 + p[a]
            return idx

        def view(p):
            if self.mode == "gather":
                return src_ref
            if self.mode == "to_chip":
                return src_ref.at[2 * p["x"] + p["y"]]
            return src_ref.at[gidx(p)]

        me = gidx(pos)
        out = [pltpu.make_async_copy(view(pos), out_ref.at[me], loc_sem)]
        for k in range(self.chunks):
            piece = pl.ds(k * rows, rows)
            for j in range(1, self.g):
                peer = dict(pos)
                for bit, a in enumerate(reversed(self.axes)):
                    if (j >> bit) & 1:
                        peer[a] = 1 - pos[a]
                sem = (j - 1) * self.chunks + k
                out.append(pltpu.make_async_remote_copy(
                    src_ref=view(peer).at[piece], dst_ref=out_ref.at[me, piece],
                    send_sem=send_sems.at[sem], recv_sem=recv_sems.at[sem],
                    device_id=(peer["x"], peer["y"], peer["c"]), device_id_type=pl.DeviceIdType.MESH))
        return out


def _exchange(name, src, axes, mode, chunks=1):
    ex = _Exchange(src, axes, mode, chunks)

    def body(src_ref, out_ref, send_sems, recv_sems, loc_sem):
        copies = ex.copies(src_ref, out_ref, send_sems, recv_sems, loc_sem)
        for cp in copies:
            cp.start()
        for cp in copies:
            cp.wait()

    return pl.pallas_call(
        body, name=name, out_shape=ex.out_shape,
        in_specs=[pl.BlockSpec(memory_space=pl.ANY)],
        out_specs=pl.BlockSpec(memory_space=pl.ANY),
        scratch_shapes=ex.scratch,
    )(src)


def _tri_call(body, rider, *, name, grid, tables, in_specs, out_specs, out_shape, scratch_shapes, operands):
    n_tab, n_in, n_out, n_scr = len(tables), len(in_specs), len(out_specs), len(scratch_shapes)
    in_specs, out_specs, out_shape = list(in_specs), list(out_specs), list(out_shape)
    scratch_shapes, operands = list(scratch_shapes), list(operands)
    if rider is not None:
        any_spec = pl.BlockSpec(memory_space=pl.ANY)
        in_specs.append(any_spec)
        out_specs.append(any_spec)
        out_shape.append(rider.out_shape)
        scratch_shapes.extend(rider.scratch)
        operands.append(rider.src)
        first, last = _grid_ends(grid)

    def wrapped(*refs):
        tabs, refs = refs[:n_tab], refs[n_tab:]
        if rider is None:
            return body(*tabs, *refs)
        ins, src_ref = refs[:n_in], refs[n_in]
        outs, out_ref = refs[n_in + 1:n_in + 1 + n_out], refs[n_in + 1 + n_out]
        scr = refs[n_in + 2 + n_out:n_in + 2 + n_out + n_scr]
        sems = refs[n_in + 2 + n_out + n_scr:]

        @pl.when(first())
        def _():
            for cp in rider.copies(src_ref, out_ref, *sems):
                cp.start()

        body(*tabs, *ins, *outs, *scr)

        @pl.when(last())
        def _():
            for cp in rider.copies(src_ref, out_ref, *sems):
                cp.wait()

    return pl.pallas_call(
        wrapped, name=name, out_shape=out_shape,
        grid_spec=pltpu.PrefetchScalarGridSpec(num_scalar_prefetch=n_tab, grid=grid, in_specs=in_specs,
                                               out_specs=out_specs, scratch_shapes=scratch_shapes),
        compiler_params=_cparams(*(["arbitrary"] * len(grid))),
    )(*tables, *operands)


def _tri_tables(n, queries_outer):
    if queries_outer:
        pairs = [(qi, ki) for qi in range(n) for ki in range(qi + 1)]
    else:
        pairs = [(qi, ki) for ki in range(n) for qi in range(ki, n)]
    return (jnp.asarray([p[0] for p in pairs], jnp.int32), jnp.asarray([p[1] for p in pairs], jnp.int32))


def _sum_groups(name, a):
    g, r, c = a.shape
    tr = min(ROW_TILE, r)

    def body(a_ref, o_ref):
        acc = a_ref[0].astype(F32)
        for i in range(1, g):
            acc = acc + a_ref[i].astype(F32)
        o_ref[...] = acc

    return pl.pallas_call(
        body, name=name, grid=(r // tr,),
        in_specs=[pl.BlockSpec((g, tr, c), lambda i: (0, i, 0))],
        out_specs=pl.BlockSpec((tr, c), lambda i: (i, 0)),
        out_shape=jax.ShapeDtypeStruct((r, c), F32),
        compiler_params=_cparams("parallel"),
    )(a)


def _mm(name, a, b, *, tm, tn=None, out_dtype=F32, epilogue=None, extras=(), extra_specs=(), nt=False):
    m, k = a.shape
    n = b.shape[0] if nt else b.shape[1]
    tn = tn or n
    tm = min(tm, m)
    b_spec = pl.BlockSpec((tn, k), lambda i, j: (j, 0)) if nt else pl.BlockSpec((k, tn), lambda i, j: (0, j))

    def body(a_ref, b_ref, *rest):
        o_ref = rest[-1]
        if nt:
            acc = lax.dot_general(a_ref[...], b_ref[...], (((1,), (1,)), ((), ())), preferred_element_type=F32)
        else:
            acc = jnp.dot(a_ref[...], b_ref[...], preferred_element_type=F32)
        if epilogue is None:
            o_ref[...] = acc.astype(o_ref.dtype)
        else:
            epilogue(acc, o_ref, *rest[:-1])

    return pl.pallas_call(
        body, name=name, grid=(m // tm, n // tn),
        in_specs=[pl.BlockSpec((tm, k), lambda i, j: (i, 0)), b_spec, *extra_specs],
        out_specs=pl.BlockSpec((tm, tn), lambda i, j: (i, j)),
        out_shape=jax.ShapeDtypeStruct((m, n), out_dtype),
        compiler_params=_cparams("parallel", "parallel"),
    )(a, b, *extras)


def _mm_tn(name, a, b, *, tn=None, tk=1024):
    s, m = a.shape
    n = b.shape[1]
    tn = tn or n
    tk = min(tk, s)

    def body(a_ref, b_ref, o_ref):
        part = lax.dot_general(a_ref[...], b_ref[...], (((0,), (0,)), ((), ())), preferred_element_type=F32)

        @pl.when(pl.program_id(1) == 0)
        def _():
            o_ref[...] = part

        @pl.when(pl.program_id(1) > 0)
        def _():
            o_ref[...] += part

    return pl.pallas_call(
        body, name=name, grid=(n // tn, s // tk),
        in_specs=[pl.BlockSpec((tk, m), lambda j, k: (k, 0)), pl.BlockSpec((tk, tn), lambda j, k: (k, j))],
        out_specs=pl.BlockSpec((m, tn), lambda j, k: (0, j)),
        out_shape=jax.ShapeDtypeStruct((m, n), F32),
        compiler_params=_cparams("parallel", "arbitrary"),
    )(a, b)


def _modulate(name, x, sc, sh):
    s = x.shape[0]
    tr = min(ROW_TILE, s)

    def body(x_ref, sc_ref, sh_ref, h_ref):
        h_ref[...] = (x_ref[...] * (1.0 + sc_ref[...]) + sh_ref[...]).astype(BF16)

    return pl.pallas_call(
        body, name=name, grid=(s // tr,),
        in_specs=[_row_spec(tr, D), _vec_spec(D), _vec_spec(D)],
        out_specs=_row_spec(tr, D),
        out_shape=jax.ShapeDtypeStruct((s, D), BF16),
        compiler_params=_cparams("parallel"),
    )(x, sc, sh)


def _ln_stats(z):
    mu = jnp.mean(z, axis=1, keepdims=True)
    zc = z - mu
    var = jnp.mean(zc * zc, axis=1, keepdims=True)
    r = lax.rsqrt(var + LN_EPS)
    return zc * r, r


def _post_mod(name, x, y, g, gamma, beta, sc, sh):
    s = x.shape[0]
    tr = min(ROW_TILE, s)

    def body(x_ref, y_ref, g_ref, ga_ref, be_ref, sc_ref, sh_ref, xn_ref, h_ref):
        zh, _ = _ln_stats(ALPHA * x_ref[...] + g_ref[...] * y_ref[...])
        xn = zh * ga_ref[...] + be_ref[...]
        xn_ref[...] = xn
        h_ref[...] = (xn * (1.0 + sc_ref[...]) + sh_ref[...]).astype(BF16)

    return pl.pallas_call(
        body, name=name, grid=(s // tr,),
        in_specs=[_row_spec(tr, D), _row_spec(tr, D)] + [_vec_spec(D)] * 5,
        out_specs=[_row_spec(tr, D), _row_spec(tr, D)],
        out_shape=[jax.ShapeDtypeStruct((s, D), F32), jax.ShapeDtypeStruct((s, D), BF16)],
        compiler_params=_cparams("parallel"),
    )(x, y, g, gamma, beta, sc, sh)


def _post_loss(name, x, y, g, gamma, beta, target):
    s = x.shape[0]
    tr = min(ROW_TILE, s)
    nt = s // tr

    def body(x_ref, y_ref, g_ref, ga_ref, be_ref, t_ref, dx_ref, loss_ref, acc):
        i = pl.program_id(0)
        zh, _ = _ln_stats(ALPHA * x_ref[...] + g_ref[...] * y_ref[...])
        e = zh * ga_ref[...] + be_ref[...] - t_ref[...]
        dx_ref[...] = e * (1.0 / D)

        @pl.when(i == 0)
        def _():
            acc[...] = jnp.zeros_like(acc)

        acc[...] += _rowsum8(e * e)

        @pl.when(i == nt - 1)
        def _():
            loss_ref[...] = jnp.full(loss_ref.shape, jnp.sum(acc[...]) * (0.5 / D), F32)

    return pl.pallas_call(
        body, name=name, grid=(nt,),
        in_specs=[_row_spec(tr, D), _row_spec(tr, D)] + [_vec_spec(D)] * 3 + [_row_spec(tr, D)],
        out_specs=[_row_spec(tr, D), pl.BlockSpec((8, LANES), lambda i: (0, 0))],
        out_shape=[jax.ShapeDtypeStruct((s, D), F32), jax.ShapeDtypeStruct((8, LANES), F32)],
        scratch_shapes=[pltpu.VMEM((8, D), F32)],
        compiler_params=_cparams("arbitrary"),
    )(x, y, g, gamma, beta, target)


def _post_bwd(name, dxn, x, y, g, gamma):
    s = x.shape[0]
    tr = min(ROW_TILE, s)
    nt = s // tr

    def body(d_ref, x_ref, y_ref, g_ref, ga_ref, dxp_ref, dy_ref, sums_ref, a0, a1, a2, a3):
        i = pl.program_id(0)
        yv = y_ref[...]
        gv = g_ref[...]
        zh, r = _ln_stats(ALPHA * x_ref[...] + gv * yv)
        dxn_v = d_ref[...]
        dzh = dxn_v * ga_ref[...]
        dz = r * (dzh - jnp.mean(dzh, axis=1, keepdims=True) - zh * jnp.mean(dzh * zh, axis=1, keepdims=True))
        dxp_ref[...] = ALPHA * dz
        dyv = gv * dz
        dy_ref[...] = dyv.astype(BF16)

        @pl.when(i == 0)
        def _():
            for a in (a0, a1, a2, a3):
                a[...] = jnp.zeros_like(a)

        a0[...] += _rowsum8(dxn_v * zh)
        a1[...] += _rowsum8(dxn_v)
        a2[...] += _rowsum8(dz * yv)
        a3[...] += _rowsum8(dyv)

        @pl.when(i == nt - 1)
        def _():
            for k, a in enumerate((a0, a1, a2, a3)):
                sums_ref[k:k + 1, :] = jnp.sum(a[...], axis=0, keepdims=True)

    return pl.pallas_call(
        body, name=name, grid=(nt,),
        in_specs=[_row_spec(tr, D)] * 3 + [_vec_spec(D)] * 2,
        out_specs=[_row_spec(tr, D), _row_spec(tr, D), pl.BlockSpec((4, D), lambda i: (0, 0))],
        out_shape=[jax.ShapeDtypeStruct((s, D), F32), jax.ShapeDtypeStruct((s, D), BF16),
                   jax.ShapeDtypeStruct((4, D), F32)],
        scratch_shapes=[pltpu.VMEM((8, D), F32)] * 4,
        compiler_params=_cparams("arbitrary"),
    )(dxn, x, y, g, gamma)


def _mod_post_bwd(name, dh, dxp, x, y, g, gamma, beta, sc):
    s = x.shape[0]
    tr = min(ROW_TILE, s)
    nt = s // tr

    def body(dh_ref, dxp_ref, x_ref, y_ref, g_ref, ga_ref, be_ref, sc_ref, dxo_ref, dy_ref, sums_ref, *acc):
        i = pl.program_id(0)
        yv = y_ref[...]
        gv = g_ref[...]
        zh, r = _ln_stats(ALPHA * x_ref[...] + gv * yv)
        xn = zh * ga_ref[...] + be_ref[...]
        dhv = dh_ref[...]
        dxn_v = dxp_ref[...] + dhv * (1.0 + sc_ref[...])
        dzh = dxn_v * ga_ref[...]
        dz = r * (dzh - jnp.mean(dzh, axis=1, keepdims=True) - zh * jnp.mean(dzh * zh, axis=1, keepdims=True))
        dxo_ref[...] = ALPHA * dz
        dyv = gv * dz
        dy_ref[...] = dyv.astype(BF16)

        @pl.when(i == 0)
        def _():
            for a in acc:
                a[...] = jnp.zeros_like(a)

        for a, val in zip(acc, (dhv * xn, dhv, dxn_v * zh, dxn_v, dz * yv, dyv)):
            a[...] += _rowsum8(val)

        @pl.when(i == nt - 1)
        def _():
            for k, a in enumerate(acc):
                sums_ref[k:k + 1, :] = jnp.sum(a[...], axis=0, keepdims=True)

    return pl.pallas_call(
        body, name=name, grid=(nt,),
        in_specs=[_row_spec(tr, D)] * 4 + [_vec_spec(D)] * 4,
        out_specs=[_row_spec(tr, D), _row_spec(tr, D), pl.BlockSpec((6, D), lambda i: (0, 0))],
        out_shape=[jax.ShapeDtypeStruct((s, D), F32), jax.ShapeDtypeStruct((s, D), BF16),
                   jax.ShapeDtypeStruct((6, D), F32)],
        scratch_shapes=[pltpu.VMEM((8, D), F32)] * 6,
        compiler_params=_cparams("arbitrary"),
    )(dh, dxp, x, y, g, gamma, beta, sc)


def _mod_bwd(name, dh, x, dxp, sc):
    s = x.shape[0]
    tr = min(ROW_TILE, s)
    nt = s // tr

    def body(dh_ref, x_ref, dxp_ref, sc_ref, dx_ref, sums_ref, a0, a1):
        i = pl.program_id(0)
        dhv = dh_ref[...]
        dx_ref[...] = dxp_ref[...] + dhv * (1.0 + sc_ref[...])

        @pl.when(i == 0)
        def _():
            a0[...] = jnp.zeros_like(a0)
            a1[...] = jnp.zeros_like(a1)

        a0[...] += _rowsum8(dhv * x_ref[...])
        a1[...] += _rowsum8(dhv)

        @pl.when(i == nt - 1)
        def _():
            sums_ref[0:1, :] = jnp.sum(a0[...], axis=0, keepdims=True)
            sums_ref[1:2, :] = jnp.sum(a1[...], axis=0, keepdims=True)

    return pl.pallas_call(
        body, name=name, grid=(nt,),
        in_specs=[_row_spec(tr, D)] * 3 + [_vec_spec(D)],
        out_specs=[_row_spec(tr, D), pl.BlockSpec((2, D), lambda i: (0, 0))],
        out_shape=[jax.ShapeDtypeStruct((s, D), F32), jax.ShapeDtypeStruct((2, D), F32)],
        scratch_shapes=[pltpu.VMEM((8, D), F32)] * 2,
        compiler_params=_cparams("arbitrary"),
    )(dh, x, dxp, sc)


def _ffn_tiles(s):
    return min(ROW_TILE, s), FF // 2


def _ffn_fwd(name, h, wg, wu, wd):
    s = h.shape[0]
    tm, tf = _ffn_tiles(s)

    def body(h_ref, wg_ref, wu_ref, wd_ref, gate_ref, up_ref, y_ref):
        hv = h_ref[...]
        gt = jnp.dot(hv, wg_ref[...], preferred_element_type=F32)
        up = jnp.dot(hv, wu_ref[...], preferred_element_type=F32)
        gate_ref[...] = gt
        up_ref[...] = up
        act = (gt * jax.nn.sigmoid(gt) * up).astype(BF16)
        part = jnp.dot(act, wd_ref[...], preferred_element_type=F32)

        @pl.when(pl.program_id(1) == 0)
        def _():
            y_ref[...] = part

        @pl.when(pl.program_id(1) > 0)
        def _():
            y_ref[...] += part

    return pl.pallas_call(
        body, name=name, grid=(s // tm, FF // tf),
        in_specs=[pl.BlockSpec((tm, D), lambda i, f: (i, 0)), pl.BlockSpec((D, tf), lambda i, f: (0, f)),
                  pl.BlockSpec((D, tf), lambda i, f: (0, f)), pl.BlockSpec((tf, D), lambda i, f: (f, 0))],
        out_specs=[pl.BlockSpec((tm, tf), lambda i, f: (i, f)), pl.BlockSpec((tm, tf), lambda i, f: (i, f)),
                   pl.BlockSpec((tm, D), lambda i, f: (i, 0))],
        out_shape=[jax.ShapeDtypeStruct((s, FF), F32), jax.ShapeDtypeStruct((s, FF), F32),
                   jax.ShapeDtypeStruct((s, D), F32)],
        compiler_params=_cparams("parallel", "arbitrary"),
    )(h, wg, wu, wd)


def _ffn_bwd(name, dy, gate, up, wd, wg, wu):
    s = dy.shape[0]
    tm, tf = _ffn_tiles(s)
    nt = (((1,), (1,)), ((), ()))

    def body(dy_ref, gate_ref, up_ref, wd_ref, wg_ref, wu_ref, dg_ref, du_ref, act_ref, dh_ref):
        dact = lax.dot_general(dy_ref[...], wd_ref[...], nt, preferred_element_type=F32)
        gt = gate_ref[...]
        up = up_ref[...]
        sig = jax.nn.sigmoid(gt)
        silu = gt * sig
        dgt = (dact * up * (sig * (1.0 + gt * (1.0 - sig)))).astype(BF16)
        dup = (dact * silu).astype(BF16)
        dg_ref[...] = dgt
        du_ref[...] = dup
        act_ref[...] = (silu * up).astype(BF16)
        part = (lax.dot_general(dgt, wg_ref[...], nt, preferred_element_type=F32)
                + lax.dot_general(dup, wu_ref[...], nt, preferred_element_type=F32))

        @pl.when(pl.program_id(1) == 0)
        def _():
            dh_ref[...] = part

        @pl.when(pl.program_id(1) > 0)
        def _():
            dh_ref[...] += part

    tile = pl.BlockSpec((tm, tf), lambda i, f: (i, f))
    return pl.pallas_call(
        body, name=name, grid=(s // tm, FF // tf),
        in_specs=[pl.BlockSpec((tm, D), lambda i, f: (i, 0)), tile, tile,
                  pl.BlockSpec((tf, D), lambda i, f: (f, 0)), pl.BlockSpec((D, tf), lambda i, f: (0, f)),
                  pl.BlockSpec((D, tf), lambda i, f: (0, f))],
        out_specs=[tile, tile, tile, pl.BlockSpec((tm, D), lambda i, f: (i, 0))],
        out_shape=[jax.ShapeDtypeStruct((s, FF), BF16)] * 3 + [jax.ShapeDtypeStruct((s, D), F32)],
        compiler_params=_cparams("parallel", "arbitrary"),
    )(dy, gate, up, wd, wg, wu)


def _mla_lat_post(name, lat, qw, kvw, rc, rsa, rsb):
    s = lat.shape[0]
    tr = min(ROW_TILE, s)

    def body(lat_ref, qw_ref, kvw_ref, c_ref, sa_ref, sb_ref, qn_ref, kvn_ref, kr_ref):
        ql = lat_ref[:, 0:MLA_QR]
        kl = lat_ref[:, MLA_QR:MLA_QR + MLA_KVR]
        qn_ref[...] = (ql * lax.rsqrt(jnp.mean(ql * ql, axis=1, keepdims=True) + RMS_EPS) * qw_ref[...]).astype(BF16)
        kvn_ref[...] = (kl * lax.rsqrt(jnp.mean(kl * kl, axis=1, keepdims=True) + RMS_EPS) * kvw_ref[...]).astype(BF16)
        kr_ref[...] = _rope(lat_ref[:, MLA_QR + MLA_KVR:MLA_LAT], c_ref[...], sa_ref[...], sb_ref[...],
                            MLA_ROPE // 2).astype(BF16)

    return pl.pallas_call(
        body, name=name, grid=(s // tr,),
        in_specs=[_row_spec(tr, MLA_LAT), _vec_spec(MLA_QR), _vec_spec(MLA_KVR)] + [_row_spec(tr, LANES)] * 3,
        out_specs=[_row_spec(tr, MLA_QR), _row_spec(tr, MLA_KVR), _row_spec(tr, LANES)],
        out_shape=[jax.ShapeDtypeStruct((s, MLA_QR), BF16), jax.ShapeDtypeStruct((s, MLA_KVR), BF16),
                   jax.ShapeDtypeStruct((s, LANES), BF16)],
        compiler_params=_cparams("parallel"),
    )(lat, qw, kvw, rc, rsa, rsb)


def _mla_lat_bwd(name, lat, dqn, dkvn, dkr_heads, qw, kvw, rc, rsa, rsb):
    s = lat.shape[0]
    tr = min(ROW_TILE, s)
    nt = s // tr

    def rms_bwd(x, w, dy):
        r = lax.rsqrt(jnp.mean(x * x, axis=1, keepdims=True) + RMS_EPS)
        xh = x * r
        gdy = dy * w
        return r * (gdy - xh * jnp.mean(gdy * xh, axis=1, keepdims=True)), dy * xh

    def body(lat_ref, dqn_ref, dkvn_ref, dkr_ref, qw_ref, kvw_ref, c_ref, sa_ref, sb_ref,
             dlat_ref, dqw_ref, dkvw_ref, aq, akv):
        i = pl.program_id(0)
        dq, dqw = rms_bwd(lat_ref[:, 0:MLA_QR], qw_ref[...], dqn_ref[...])
        dk, dkw = rms_bwd(lat_ref[:, MLA_QR:MLA_QR + MLA_KVR], kvw_ref[...], dkvn_ref[...])
        dkr = dkr_ref[0]
        for hh in range(1, MLA_H):
            dkr = dkr + dkr_ref[hh]
        dkr = _rope_t(dkr, c_ref[...], sa_ref[...], sb_ref[...], MLA_ROPE // 2)
        dlat_ref[:, 0:MLA_QR] = dq.astype(BF16)
        dlat_ref[:, MLA_QR:MLA_QR + MLA_KVR] = dk.astype(BF16)
        dlat_ref[:, MLA_QR + MLA_KVR:MLA_LAT] = dkr.astype(BF16)

        @pl.when(i == 0)
        def _():
            aq[...] = jnp.zeros_like(aq)
            akv[...] = jnp.zeros_like(akv)

        aq[...] += _rowsum8(dqw)
        akv[...] += _rowsum8(dkw)

        @pl.when(i == nt - 1)
        def _():
            dqw_ref[...] = jnp.sum(aq[...], axis=0, keepdims=True)
            dkvw_ref[...] = jnp.sum(akv[...], axis=0, keepdims=True)

    return pl.pallas_call(
        body, name=name, grid=(nt,),
        in_specs=[_row_spec(tr, MLA_LAT), _row_spec(tr, MLA_QR), _row_spec(tr, MLA_KVR),
                  pl.BlockSpec((MLA_H, tr, LANES), lambda i: (0, i, 0)), _vec_spec(MLA_QR), _vec_spec(MLA_KVR)]
        + [_row_spec(tr, LANES)] * 3,
        out_specs=[_row_spec(tr, MLA_LAT), _vec_spec(MLA_QR), _vec_spec(MLA_KVR)],
        out_shape=[jax.ShapeDtypeStruct((s, MLA_LAT), BF16), jax.ShapeDtypeStruct((1, MLA_QR), F32),
                   jax.ShapeDtypeStruct((1, MLA_KVR), F32)],
        scratch_shapes=[pltpu.VMEM((8, MLA_QR), F32), pltpu.VMEM((8, MLA_KVR), F32)],
        compiler_params=_cparams("arbitrary"),
    )(lat, dqn, dkvn, dkr_heads, qw, kvw, rc, rsa, rsb)


def _attn_tile(s):
    return min(1024, max(LANES, s // 2))


MLA_SCALE = (MLA_NOPE + MLA_ROPE) ** -0.5
LOG2E = 1.4426950408889634
MLA_C2 = MLA_SCALE * LOG2E
NT_DIMS = (((1,), (1,)), ((), ()))
TN_DIMS = (((0,), (0,)), ((), ()))


def _causal(sc, transposed=False):
    row = lax.broadcasted_iota(jnp.int32, sc.shape, 0)
    col = lax.broadcasted_iota(jnp.int32, sc.shape, 1)
    return jnp.where(row <= col if transposed else col <= row, sc, NEG)


def _grid_ends(grid):
    def first():
        ok = pl.program_id(0) == 0
        for d in range(1, len(grid)):
            ok = jnp.logical_and(ok, pl.program_id(d) == 0)
        return ok

    def last():
        ok = pl.program_id(0) == grid[0] - 1
        for d in range(1, len(grid)):
            ok = jnp.logical_and(ok, pl.program_id(d) == grid[d] - 1)
        return ok

    return first, last


def _mla_attn_fwd(name, qq, kv, kr, rider=None):
    s = qq.shape[0]
    t = _attn_tile(s)
    n = s // t

    def body(qi_tab, ki_tab, q_ref, kv_ref, kvn_ref, kr_ref, krn_ref, o_ref, lse_ref, m_scr, acc_scr, sc_a, sc_b):
        qi = qi_tab[pl.program_id(1)]
        ki = ki_tab[pl.program_id(1)]

        def scores(kv_blk, kr_blk):
            k = jnp.concatenate([kv_blk[:, 0:LANES], kr_blk[...]], axis=1)
            return lax.dot_general(q_ref[...], k, NT_DIMS, preferred_element_type=F32)

        @pl.when(ki == 0)
        def _():
            m_scr[...] = jnp.full(m_scr.shape, NEG, F32)
            acc_scr[...] = jnp.zeros_like(acc_scr)
            sc_a[...] = scores(kv_ref, kr_ref)

        def step(cur, nxt, diag):
            if not diag:
                nxt[...] = scores(kvn_ref, krn_ref)
            sc = cur[...]
            if diag:
                sc = _causal(sc)
            m_prev = m_scr[...]
            m_next = jnp.maximum(m_prev, jnp.max(sc, axis=1, keepdims=True))
            a = jnp.exp2(MLA_C2 * (m_prev - m_next))
            p = jnp.exp2(MLA_C2 * sc - MLA_C2 * m_next[:, 0:1]).astype(BF16)
            v1 = jnp.concatenate([kv_ref[:, LANES:2 * LANES], jnp.ones((t, LANES), BF16)], axis=1)
            pv = jnp.dot(p, v1, preferred_element_type=F32)
            acc_scr[:, 0:LANES] = a * acc_scr[:, 0:LANES] + pv[:, 0:LANES]
            acc_scr[:, LANES:2 * LANES] = a * acc_scr[:, LANES:2 * LANES] + pv[:, LANES:2 * LANES]
            m_scr[...] = m_next

        even = ki % 2 == 0
        for is_even, cur, nxt in ((True, sc_a, sc_b), (False, sc_b, sc_a)):
            mine = even if is_even else jnp.logical_not(even)

            @pl.when(jnp.logical_and(mine, ki < qi))
            def _():
                step(cur, nxt, False)

            @pl.when(jnp.logical_and(mine, ki == qi))
            def _():
                step(cur, nxt, True)

        @pl.when(ki == qi)
        def _():
            l = acc_scr[:, LANES:2 * LANES]
            o_ref[...] = (acc_scr[:, 0:LANES] / l).astype(BF16)
            lse_ref[...] = MLA_SCALE * m_scr[...] + jnp.log(l)

    qblk = lambda w: pl.BlockSpec((t, w), lambda h, i, qt, kt: (qt[i], h))
    nxt_blk = lambda i, qt, kt: jnp.where(kt[i] < qt[i], kt[i] + 1, kt[i])
    return _tri_call(
        body, rider, name=name, grid=(MLA_H, n * (n + 1) // 2), tables=_tri_tables(n, queries_outer=True),
        in_specs=[qblk(2 * LANES),
                  pl.BlockSpec((t, 2 * LANES), lambda h, i, qt, kt: (kt[i], h)),
                  pl.BlockSpec((t, 2 * LANES), lambda h, i, qt, kt: (nxt_blk(i, qt, kt), h)),
                  pl.BlockSpec((t, LANES), lambda h, i, qt, kt: (kt[i], 0)),
                  pl.BlockSpec((t, LANES), lambda h, i, qt, kt: (nxt_blk(i, qt, kt), 0))],
        out_specs=[qblk(LANES), qblk(LANES)],
        out_shape=[jax.ShapeDtypeStruct((s, MLA_H * MLA_V), BF16), jax.ShapeDtypeStruct((s, MLA_H * LANES), F32)],
        scratch_shapes=[pltpu.VMEM((t, LANES), F32), pltpu.VMEM((t, 2 * LANES), F32), pltpu.VMEM((t, t), F32),
                        pltpu.VMEM((t, t), F32)],
        operands=(qq, kv, kv, kr, kr))


def _mla_bwd_stats(name, do, o, lse):
    s = do.shape[0]
    tr = min(4 * ROW_TILE, s)

    def body(do_ref, o_ref, lse_ref, stat_ref):
        dl = jnp.sum(do_ref[...].astype(F32) * o_ref[...].astype(F32), axis=1, keepdims=True)
        delta_t = jnp.transpose(jnp.broadcast_to(dl, (tr, LANES)))[0:8]
        lse_t = jnp.transpose(lse_ref[...] * LOG2E)[0:8]
        rows = lax.broadcasted_iota(jnp.int32, (8, tr), 0)
        stat_ref[0] = jnp.where(rows == 0, lse_t, jnp.where(rows == 1, delta_t, 0.0))

    blk = pl.BlockSpec((tr, LANES), lambda h, i: (i, h))
    return pl.pallas_call(
        body, name=name, grid=(MLA_H, s // tr),
        in_specs=[blk, blk, blk],
        out_specs=pl.BlockSpec((1, 8, tr), lambda h, i: (h, 0, i)),
        out_shape=jax.ShapeDtypeStruct((MLA_H, 8, s), F32),
        compiler_params=_cparams("parallel", "parallel"),
    )(do, o, lse)


def _mla_dq_post(name, dq, rc, rsa, rsb):
    s = dq.shape[1]
    tr = min(4 * ROW_TILE, s)

    def body(dq_ref, c_ref, sa_ref, sb_ref, o_ref):
        o_ref[:, 0:LANES] = (MLA_SCALE * dq_ref[0, :, 0:LANES]).astype(BF16)
        o_ref[:, LANES:2 * LANES] = _rope_t(MLA_SCALE * dq_ref[0, :, LANES:2 * LANES], c_ref[...], sa_ref[...],
                                            sb_ref[...], MLA_ROPE // 2).astype(BF16)

    tab = pl.BlockSpec((tr, LANES), lambda h, i: (i, 0))
    return pl.pallas_call(
        body, name=name, grid=(MLA_H, s // tr),
        in_specs=[pl.BlockSpec((1, tr, 2 * LANES), lambda h, i: (h, i, 0)), tab, tab, tab],
        out_specs=pl.BlockSpec((tr, 2 * LANES), lambda h, i: (i, h)),
        out_shape=jax.ShapeDtypeStruct((s, 2 * MLA_H * LANES), BF16),
        compiler_params=_cparams("parallel", "parallel"),
    )(dq, rc, rsa, rsb)


def _mla_attn_bwd(name, qq, kv, kr, do, stats, rider=None):
    s = qq.shape[0]
    t = _attn_tile(s)
    n = s // t

    def body(qi_tab, ki_tab, q_ref, kv_ref, kr_ref, do_ref, stat_ref, dkv_ref, dkr_ref, dq_hbm,
             dk_scr, dv_scr, dq_scr, dq_sem):
        h = pl.program_id(0)
        qi = qi_tab[pl.program_id(1)]
        ki = ki_tab[pl.program_id(1)]

        @pl.when(pl.program_id(1) == 0)
        def _():
            dq_scr[...] = jnp.zeros_like(dq_scr)

        @pl.when(qi == ki)
        def _():
            dk_scr[...] = jnp.zeros_like(dk_scr)
            dv_scr[...] = jnp.zeros_like(dv_scr)

        def step(diag):
            q = q_ref[...]
            k = jnp.concatenate([kv_ref[:, 0:LANES], kr_ref[...]], axis=1)
            sc = lax.dot_general(k, q, NT_DIMS, preferred_element_type=F32)
            if diag:
                sc = _causal(sc, transposed=True)
            p = jnp.exp2(MLA_C2 * sc - stat_ref[0, 0:1, :])
            dov = do_ref[...]
            dv_scr[...] += jnp.dot(p.astype(BF16), dov, preferred_element_type=F32)
            dp = lax.dot_general(kv_ref[:, LANES:2 * LANES], dov, NT_DIMS, preferred_element_type=F32)
            ds = (p * (dp - stat_ref[0, 1:2, :])).astype(BF16)
            dk_scr[...] += jnp.dot(ds, q, preferred_element_type=F32)
            rows = pl.ds(pl.multiple_of(qi * t, t), t)
            dq_scr[rows, :] += lax.dot_general(ds, k, TN_DIMS, preferred_element_type=F32)

        @pl.when(qi == ki)
        def _():
            step(True)

        @pl.when(qi > ki)
        def _():
            step(False)

        @pl.when(qi == n - 1)
        def _():
            dkv_ref[:, 0:LANES] = (MLA_SCALE * dk_scr[:, 0:LANES]).astype(BF16)
            dkv_ref[:, LANES:2 * LANES] = dv_scr[...].astype(BF16)
            dkr_ref[0] = MLA_SCALE * dk_scr[:, LANES:2 * LANES]

        @pl.when(pl.program_id(1) == n * (n + 1) // 2 - 1)
        def _():
            cp = pltpu.make_async_copy(dq_scr, dq_hbm.at[h], dq_sem)
            cp.start()
            cp.wait()

    qblk = lambda w: pl.BlockSpec((t, w), lambda h, i, qt, kt: (qt[i], h))
    kblk = pl.BlockSpec((t, 2 * LANES), lambda h, i, qt, kt: (kt[i], h))
    return _tri_call(
        body, rider, name=name, grid=(MLA_H, n * (n + 1) // 2), tables=_tri_tables(n, queries_outer=False),
        in_specs=[qblk(2 * LANES), kblk, pl.BlockSpec((t, LANES), lambda h, i, qt, kt: (kt[i], 0)), qblk(LANES),
                  pl.BlockSpec((1, 8, t), lambda h, i, qt, kt: (h, 0, qt[i]))],
        out_specs=[kblk, pl.BlockSpec((1, t, LANES), lambda h, i, qt, kt: (h, kt[i], 0)),
                   pl.BlockSpec(memory_space=pl.ANY)],
        out_shape=[jax.ShapeDtypeStruct((s, 2 * MLA_H * LANES), BF16),
                   jax.ShapeDtypeStruct((MLA_H, s, LANES), F32),
                   jax.ShapeDtypeStruct((MLA_H, s, 2 * LANES), F32)],
        scratch_shapes=[pltpu.VMEM((t, 2 * LANES), F32), pltpu.VMEM((t, LANES), F32),
                        pltpu.VMEM((s, 2 * LANES), F32), pltpu.SemaphoreType.DMA(())],
        operands=(qq, kv, kr, do, stats))


def _rope_groups(acc, o_ref, c, sa, sb, sh, groups):
    for gi in range(acc.shape[1] // LANES):
        blk = acc[:, gi * LANES:(gi + 1) * LANES]
        if gi in groups:
            blk = _rope(blk, c, sa, sb, sh)
        o_ref[:, gi * LANES:(gi + 1) * LANES] = blk.astype(o_ref.dtype)


def _mla_fwd(tag, h, w, tabs, rider=None):
    s = h.shape[0]
    rc, rsa, rsb = tabs
    lat = _mm(f"{tag}_lat", h, w["w_in"], tm=512)
    qn, kvn, kr = _mla_lat_post(f"{tag}_latpost", lat, w["q_norm"], w["kv_norm"], rc, rsa, rsb)
    tm = min(512, s)

    def q_epi(acc, o_ref, c_ref, sa_ref, sb_ref):
        _rope_groups(acc, o_ref, c_ref[...], sa_ref[...], sb_ref[...], MLA_ROPE // 2, range(1, MLA_H, 2))

    tab = pl.BlockSpec((tm, LANES), lambda i, j: (i, 0))
    qq = _mm(f"{tag}_q", qn, w["w_q"], tm=512, tn=MLA_H * LANES, out_dtype=BF16, epilogue=q_epi,
             extras=(rc, rsa, rsb), extra_specs=(tab, tab, tab))
    kv = _mm(f"{tag}_kv", kvn, w["w_kv"], tm=512, out_dtype=BF16)
    o, lse, *ridden = _mla_attn_fwd(f"{tag}_attn", qq, kv, kr, rider)
    res = dict(h=h, lat=lat, qn=qn, kvn=kvn, kr=kr, qq=qq, kv=kv, o=o, lse=lse)
    return o, res, (ridden[0] if ridden else None)


def _mla_bwd(tag, dy, res, w, tabs, make_rider=None):
    rc, rsa, rsb = tabs
    do = _mm(f"{tag}_do", dy, w["w_o"], tm=512, out_dtype=BF16, nt=True)
    g_wo = _mm_tn(f"{tag}_gwo", res["o"], dy)
    stats = _mla_bwd_stats(f"{tag}_stats", do, res["o"], res["lse"])
    rider = make_rider(g_wo) if make_rider is not None else None
    dkv, dkr, dq, *ridden = _mla_attn_bwd(f"{tag}_attnbwd", res["qq"], res["kv"], res["kr"], do, stats, rider)
    dqq = _mla_dq_post(f"{tag}_dqpost", dq, rc, rsa, rsb)
    dqn = _mm(f"{tag}_dqn", dqq, w["w_q"], tm=512, nt=True)
    g_wq = _mm_tn(f"{tag}_gwq", res["qn"], dqq, tn=1024)
    dkvn = _mm(f"{tag}_dkvn", dkv, w["w_kv"], tm=512, nt=True)
    g_wkv = _mm_tn(f"{tag}_gwkv", res["kvn"], dkv, tn=1024)
    dlat, g_qn, g_kvn = _mla_lat_bwd(f"{tag}_latbwd", res["lat"], dqn, dkvn, dkr, w["q_norm"], w["kv_norm"],
                                     rc, rsa, rsb)
    dh = _mm(f"{tag}_dh", dlat, w["w_in"], tm=512, nt=True)
    g_win = _mm_tn(f"{tag}_gwin", res["h"], dlat)
    grads = dict(w_in=g_win, q_norm=g_qn, w_q=g_wq, kv_norm=g_kvn, w_kv=g_wkv, w_o=g_wo)
    return dh, grads, (ridden[0] if ridden else None)


SWA_QW = SWA_HQ * SWA_HD
SWA_KW = SWA_HKV * LANES
SWA_NQKV = SWA_QW + 2 * SWA_KW
SWA_SCALE = SWA_HD ** -0.5
SWA_GROUP_ROWS = 4 * SWA_W


def _swa_tile(s):
    return min(512, max(SWA_W, s // 2))


def _swa_masks():
    lane = lax.broadcasted_iota(jnp.int32, (SWA_W, LANES), 1)
    return lane < SWA_HD


def _swa_q4(qa, qb, lo):
    z = jnp.zeros_like(qa)
    return jnp.concatenate([jnp.where(lo, qa, z), jnp.where(lo, z, qa), jnp.where(lo, qb, z), jnp.where(lo, z, qb)],
                           axis=0)


def _swa_probs(q4, kwin, sink_col, first_block):
    sc = lax.dot_general(q4, kwin, NT_DIMS, preferred_element_type=F32) * SWA_SCALE
    row = lax.broadcasted_iota(jnp.int32, sc.shape, 0) % SWA_W
    col = lax.broadcasted_iota(jnp.int32, sc.shape, 1)
    rel = row + SWA_W - col
    ok = (rel >= 0) & (rel < SWA_W) & ((col >= SWA_W) | jnp.logical_not(first_block))
    sc = jnp.where(ok, sc, NEG)
    m = jnp.maximum(jnp.max(sc, axis=1, keepdims=True), sink_col)
    e = jnp.exp(sc - m)
    es = jnp.exp(sink_col - m)
    inv = 1.0 / (jnp.sum(e, axis=1, keepdims=True) + es)
    return e * inv, es * inv


def _sink_col(sinks_ref, grp):
    seg = lax.broadcasted_iota(jnp.int32, (SWA_GROUP_ROWS, 1), 0) // SWA_W
    col = jnp.zeros((SWA_GROUP_ROWS, 1), F32)
    for j in range(4):
        col = jnp.where(seg == j, sinks_ref[0, 4 * grp + j], col)
    return col


def _swa_attn_fwd(name, qkv, sinks):
    s = qkv.shape[0]
    t = _swa_tile(s)
    nb = t // SWA_W

    def body(sinks_ref, q_ref, kv_ref, kvp_ref, o_ref):
        i = pl.program_id(0)
        lo = _swa_masks()
        for grp in range(SWA_HKV):
            sink_col = _sink_col(sinks_ref, grp)
            kcat = jnp.concatenate([kvp_ref[:, grp * LANES:(grp + 1) * LANES],
                                    kv_ref[:, grp * LANES:(grp + 1) * LANES]], axis=0)
            vcat = jnp.concatenate([kvp_ref[:, SWA_KW + grp * LANES:SWA_KW + (grp + 1) * LANES],
                                    kv_ref[:, SWA_KW + grp * LANES:SWA_KW + (grp + 1) * LANES]], axis=0)
            for b in range(nb):
                r0 = b * SWA_W
                qa = q_ref[r0:r0 + SWA_W, grp * 2 * LANES:grp * 2 * LANES + LANES]
                qb = q_ref[r0:r0 + SWA_W, grp * 2 * LANES + LANES:(grp + 1) * 2 * LANES]
                first = jnp.logical_and(i == 0, b == 0)
                p, _ = _swa_probs(_swa_q4(qa, qb, lo), kcat[r0:r0 + 2 * SWA_W], sink_col, first)
                o4 = jnp.dot(p.astype(BF16), vcat[r0:r0 + 2 * SWA_W], preferred_element_type=F32)
                oa = jnp.where(lo, o4[0:SWA_W], o4[SWA_W:2 * SWA_W])
                ob = jnp.where(lo, o4[2 * SWA_W:3 * SWA_W], o4[3 * SWA_W:4 * SWA_W])
                o_ref[r0:r0 + SWA_W, grp * 2 * LANES:grp * 2 * LANES + LANES] = oa.astype(BF16)
                o_ref[r0:r0 + SWA_W, grp * 2 * LANES + LANES:(grp + 1) * 2 * LANES] = ob.astype(BF16)

    return pl.pallas_call(
        body, name=name, grid=(s // t,),
        in_specs=[pl.BlockSpec(memory_space=pltpu.SMEM),
                  pl.BlockSpec((t, SWA_QW), lambda i: (i, 0)),
                  pl.BlockSpec((t, 2 * SWA_KW), lambda i: (i, 1)),
                  pl.BlockSpec((SWA_W, 2 * SWA_KW), lambda i: (jnp.maximum(i * nb - 1, 0), 1))],
        out_specs=pl.BlockSpec((t, SWA_QW), lambda i: (i, 0)),
        out_shape=jax.ShapeDtypeStruct((s, SWA_QW), BF16),
        compiler_params=_cparams("parallel"),
    )(sinks, qkv, qkv, qkv)


def _swa_attn_bwd(name, qkv, sinks, do):
    s = qkv.shape[0]
    t = _swa_tile(s)
    nb = t // SWA_W
    nt = s // t

    def body(sinks_ref, q_ref, kv_ref, kvp_ref, do_ref, dq_ref, dkv_ref, dkvp_ref, dsink_ref, dcat, sink_acc):
        i = pl.program_id(0)
        lo = _swa_masks()

        @pl.when(i == 0)
        def _():
            sink_acc[...] = jnp.zeros_like(sink_acc)

        dcat[...] = jnp.zeros_like(dcat)
        for grp in range(SWA_HKV):
            sink_col = _sink_col(sinks_ref, grp)
            kcat = jnp.concatenate([kvp_ref[:, grp * LANES:(grp + 1) * LANES],
                                    kv_ref[:, grp * LANES:(grp + 1) * LANES]], axis=0)
            vcat = jnp.concatenate([kvp_ref[:, SWA_KW + grp * LANES:SWA_KW + (grp + 1) * LANES],
                                    kv_ref[:, SWA_KW + grp * LANES:SWA_KW + (grp + 1) * LANES]], axis=0)
            for b in range(nb):
                r0 = b * SWA_W
                ca = slice(grp * 2 * LANES, grp * 2 * LANES + LANES)
                cb = slice(grp * 2 * LANES + LANES, (grp + 1) * 2 * LANES)
                q4 = _swa_q4(q_ref[r0:r0 + SWA_W, ca], q_ref[r0:r0 + SWA_W, cb], lo)
                do4 = _swa_q4(do_ref[r0:r0 + SWA_W, ca], do_ref[r0:r0 + SWA_W, cb], lo)
                first = jnp.logical_and(i == 0, b == 0)
                kwin = kcat[r0:r0 + 2 * SWA_W]
                vwin = vcat[r0:r0 + 2 * SWA_W]
                p, ps = _swa_probs(q4, kwin, sink_col, first)
                dp = lax.dot_general(do4, vwin, NT_DIMS, preferred_element_type=F32)
                rowdot = jnp.sum(p * dp, axis=1, keepdims=True)
                ds = (p * (dp - rowdot) * SWA_SCALE).astype(BF16)
                sink_acc[grp] += jnp.broadcast_to(-ps * rowdot, (SWA_GROUP_ROWS, LANES))
                dq4 = jnp.dot(ds, kwin, preferred_element_type=F32)
                dq_ref[r0:r0 + SWA_W, ca] = jnp.where(lo, dq4[0:SWA_W], dq4[SWA_W:2 * SWA_W])
                dq_ref[r0:r0 + SWA_W, cb] = jnp.where(lo, dq4[2 * SWA_W:3 * SWA_W], dq4[3 * SWA_W:4 * SWA_W])
                dk = lax.dot_general(ds, q4, TN_DIMS, preferred_element_type=F32)
                dv = lax.dot_general(p.astype(BF16), do4, TN_DIMS, preferred_element_type=F32)
                dcat[r0:r0 + 2 * SWA_W, grp * LANES:(grp + 1) * LANES] += dk
                dcat[r0:r0 + 2 * SWA_W, SWA_KW + grp * LANES:SWA_KW + (grp + 1) * LANES] += dv
        dkvp_ref[0] = dcat[0:SWA_W]
        dkv_ref[...] = dcat[SWA_W:SWA_W + t]

        @pl.when(i == nt - 1)
        def _():
            for grp in range(SWA_HKV):
                for j in range(4):
                    tot = jnp.sum(sink_acc[grp, j * SWA_W:(j + 1) * SWA_W, 0:1])
                    dsink_ref[4 * grp + j:4 * grp + j + 1, :] = jnp.full((1, LANES), tot, F32)

    return pl.pallas_call(
        body, name=name, grid=(nt,),
        in_specs=[pl.BlockSpec(memory_space=pltpu.SMEM),
                  pl.BlockSpec((t, SWA_QW), lambda i: (i, 0)),
                  pl.BlockSpec((t, 2 * SWA_KW), lambda i: (i, 1)),
                  pl.BlockSpec((SWA_W, 2 * SWA_KW), lambda i: (jnp.maximum(i * nb - 1, 0), 1)),
                  pl.BlockSpec((t, SWA_QW), lambda i: (i, 0))],
        out_specs=[pl.BlockSpec((t, SWA_QW), lambda i: (i, 0)), pl.BlockSpec((t, 2 * SWA_KW), lambda i: (i, 0)),
                   pl.BlockSpec((1, SWA_W, 2 * SWA_KW), lambda i: (i, 0, 0)),
                   pl.BlockSpec((SWA_HQ, LANES), lambda i: (0, 0))],
        out_shape=[jax.ShapeDtypeStruct((s, SWA_QW), F32), jax.ShapeDtypeStruct((s, 2 * SWA_KW), F32),
                   jax.ShapeDtypeStruct((nt, SWA_W, 2 * SWA_KW), F32), jax.ShapeDtypeStruct((SWA_HQ, LANES), F32)],
        scratch_shapes=[pltpu.VMEM((SWA_W + t, 2 * SWA_KW), F32), pltpu.VMEM((SWA_HKV, SWA_GROUP_ROWS, LANES), F32)],
        compiler_params=_cparams("arbitrary"),
    )(sinks, qkv, qkv, qkv, do)


def _swa_dqkv(name, dq, dkv, dkvp, rc, rsa, rsb):
    s = dq.shape[0]
    t = _swa_tile(s)
    nt = s // t
    sh = SWA_ROT // 2

    def body(dq_ref, dkv_ref, dkvn_ref, c_ref, sa_ref, sb_ref, out_ref, bsum_ref, acc):
        i = pl.program_id(0)
        c, sa, sb = c_ref[...], sa_ref[...], sb_ref[...]
        lo = lax.broadcasted_iota(jnp.int32, (t, LANES), 1) < SWA_HD
        rows = lax.broadcasted_iota(jnp.int32, (t, LANES), 0)
        tail = jnp.logical_and(rows >= t - SWA_W, i < nt - 1)

        @pl.when(i == 0)
        def _():
            acc[...] = jnp.zeros_like(acc)

        for gi in range(SWA_QW // LANES):
            blk = _rope_t(dq_ref[:, gi * LANES:(gi + 1) * LANES], c, sa, sb, sh)
            out_ref[:, gi * LANES:(gi + 1) * LANES] = blk.astype(BF16)
            acc[:, gi * LANES:(gi + 1) * LANES] += _rowsum8(blk)
        for gi in range(2 * SWA_KW // LANES):
            cols = slice(gi * LANES, (gi + 1) * LANES)
            nxt = jnp.concatenate([jnp.zeros((t - SWA_W, LANES), F32), dkvn_ref[0, :, cols]], axis=0)
            blk = dkv_ref[:, cols] + jnp.where(tail, nxt, 0.0)
            blk = jnp.where(lo, blk + pltpu.roll(blk, SWA_HD, 1), 0.0)
            if gi < SWA_HKV:
                blk = _rope_t(blk, c, sa, sb, sh)
            out_ref[:, SWA_QW + gi * LANES:SWA_QW + (gi + 1) * LANES] = blk.astype(BF16)
            acc[:, SWA_QW + gi * LANES:SWA_QW + (gi + 1) * LANES] += _rowsum8(blk)

        @pl.when(i == nt - 1)
        def _():
            bsum_ref[...] = jnp.sum(acc[...], axis=0, keepdims=True)

    return pl.pallas_call(
        body, name=name, grid=(nt,),
        in_specs=[pl.BlockSpec((t, SWA_QW), lambda i: (i, 0)), pl.BlockSpec((t, 2 * SWA_KW), lambda i: (i, 0)),
                  pl.BlockSpec((1, SWA_W, 2 * SWA_KW), lambda i: (jnp.minimum(i + 1, nt - 1), 0, 0))]
        + [_row_spec(t, LANES)] * 3,
        out_specs=[pl.BlockSpec((t, SWA_NQKV), lambda i: (i, 0)), pl.BlockSpec((1, SWA_NQKV), lambda i: (0, 0))],
        out_shape=[jax.ShapeDtypeStruct((s, SWA_NQKV), BF16), jax.ShapeDtypeStruct((1, SWA_NQKV), F32)],
        scratch_shapes=[pltpu.VMEM((8, SWA_NQKV), F32)],
        compiler_params=_cparams("arbitrary"),
    )(dq, dkv, dkvp, rc, rsa, rsb)


def _swa_fwd(tag, h, w, tabs):
    s = h.shape[0]
    rc, rsa, rsb = tabs
    tm = min(512, s)
    sh = SWA_ROT // 2

    def qkv_epi(acc, o_ref, b_ref, c_ref, sa_ref, sb_ref):
        acc = acc + b_ref[...]

        @pl.when(pl.program_id(1) == 0)
        def _():
            _rope_groups(acc, o_ref, c_ref[...], sa_ref[...], sb_ref[...], sh, range(SWA_QW // LANES))

        @pl.when(pl.program_id(1) == 1)
        def _():
            _rope_groups(acc, o_ref, c_ref[...], sa_ref[...], sb_ref[...], sh, range(SWA_HKV))

    tab = pl.BlockSpec((tm, LANES), lambda i, j: (i, 0))
    qkv = _mm(f"{tag}_qkv", h, w["w_qkv"], tm=512, tn=SWA_QW, out_dtype=BF16, epilogue=qkv_epi,
              extras=(w["b_qkv"], rc, rsa, rsb),
              extra_specs=(pl.BlockSpec((1, SWA_QW), lambda i, j: (0, j)), tab, tab, tab))
    o = _swa_attn_fwd(f"{tag}_attn", qkv, w["sinks"])

    def o_epi(acc, o_ref, b_ref):
        o_ref[...] = acc + b_ref[...]

    y = _mm(f"{tag}_o", o, w["w_o"], tm=512, epilogue=o_epi, extras=(w["b_o"],),
            extra_specs=(pl.BlockSpec((1, D), lambda i, j: (0, 0)),))
    return y, dict(h=h, qkv=qkv, o=o)


def _swa_bwd(tag, dy, res, w, tabs):
    rc, rsa, rsb = tabs
    do = _mm(f"{tag}_do", dy, w["w_o"], tm=512, out_dtype=BF16, nt=True)
    g_wo = _mm_tn(f"{tag}_gwo", res["o"], dy)
    dq, dkv, dkvp, dsink = _swa_attn_bwd(f"{tag}_attnbwd", res["qkv"], w["sinks"], do)
    dqkv, g_b = _swa_dqkv(f"{tag}_dqkv", dq, dkv, dkvp, rc, rsa, rsb)
    dh = _mm(f"{tag}_dh", dqkv, w["w_qkv"], tm=512, nt=True)
    g_wqkv = _mm_tn(f"{tag}_gwqkv", res["h"], dqkv, tn=1024)
    return dh, dict(w_qkv=g_wqkv, b_qkv=g_b, sinks=dsink, w_o=g_wo)


def _ada_fwd(name, c_all, w_sh, b_sh):
    cols = w_sh.shape[2]
    tn = cols // 3

    def body(c_ref, w_ref, b_ref, o_ref, cond_ref):
        cv = c_ref[...]
        cond = cv * jax.nn.sigmoid(cv)
        cond_ref[...] = cond
        o_ref[0] = jnp.dot(cond, w_ref[0], preferred_element_type=F32, precision=lax.Precision.HIGHEST) + b_ref[0]

    return pl.pallas_call(
        body, name=name, grid=(DEPTH, cols // tn),
        in_specs=[pl.BlockSpec((8, D), lambda l, j: (0, 0)), pl.BlockSpec((1, D, tn), lambda l, j: (l, 0, j)),
                  pl.BlockSpec((1, 1, tn), lambda l, j: (l, 0, j))],
        out_specs=[pl.BlockSpec((1, 8, tn), lambda l, j: (l, 0, j)), pl.BlockSpec((8, D), lambda l, j: (0, 0))],
        out_shape=[jax.ShapeDtypeStruct((DEPTH, 8, cols), F32), jax.ShapeDtypeStruct((8, D), F32)],
        compiler_params=_cparams("arbitrary", "arbitrary"),
    )(c_all, w_sh, b_sh)


def _ada_grad(name, cond_t, dmod_sh):
    cols = dmod_sh.shape[2]
    tn = cols // 3

    def body(ct_ref, dm_ref, o_ref):
        acc = ct_ref[:, 0:1] * dm_ref[0, 0:1, :]
        for b in range(1, 8):
            acc = acc + ct_ref[:, b:b + 1] * dm_ref[0, b:b + 1, :]
        o_ref[0] = acc

    return pl.pallas_call(
        body, name=name, grid=(DEPTH, cols // tn),
        in_specs=[pl.BlockSpec((D, 8), lambda l, j: (0, 0)), pl.BlockSpec((1, 8, tn), lambda l, j: (l, 0, j))],
        out_specs=pl.BlockSpec((1, D, tn), lambda l, j: (l, 0, j)),
        out_shape=jax.ShapeDtypeStruct((DEPTH, D, cols), F32),
        compiler_params=_cparams("parallel", "parallel"),
    )(cond_t, dmod_sh)


def _adamw(name, g, w, m, v):
    r, cols = g.shape
    tile_elems = 512 * 1024
    tr = r if r * cols <= tile_elems else max(d for d in (512, 256, 128, 64, 32, 16, 8)
                                               if r % d == 0 and d * cols <= tile_elems)

    def body(g_ref, w_ref, m_ref, v_ref, d_ref, nm_ref, nv_ref):
        gv = g_ref[...]
        mn = ADAM_B1 * m_ref[...] + (1.0 - ADAM_B1) * gv
        vn = ADAM_B2 * v_ref[...] + (1.0 - ADAM_B2) * (gv * gv)
        m_hat = mn / (1.0 - ADAM_B1 ** ADAM_STEP)
        v_hat = vn / (1.0 - ADAM_B2 ** ADAM_STEP)
        d_ref[...] = -ADAM_LR * (m_hat / (jnp.sqrt(v_hat) + ADAM_EPS) + ADAM_WD * w_ref[...])
        nm_ref[...] = mn
        nv_ref[...] = vn

    spec = _row_spec(tr, cols)
    return pl.pallas_call(
        body, name=name, grid=(r // tr,),
        in_specs=[spec] * 4, out_specs=[spec] * 3,
        out_shape=[jax.ShapeDtypeStruct(g.shape, F32)] * 3,
        compiler_params=_cparams("parallel"),
    )(g, w, m, v)


def _to_chips(full, axis):
    shp = full.shape
    a = full.reshape(shp[:axis] + (N_CHIPS, shp[axis] // N_CHIPS) + shp[axis + 1:])
    return jnp.moveaxis(a, axis, 0)


def _from_chips(stacked, axis):
    a = jnp.moveaxis(stacked, 0, axis)
    shp = a.shape
    return a.reshape(shp[:axis] + (shp[axis] * shp[axis + 1],) + shp[axis + 2:])


PIECE_ROW_ALIGN = 16


def _piece_rows(shape):
    n = 1
    for d in shape:
        n *= d
    rows = -(-n // PACK_COLS)
    return -(-rows // PIECE_ROW_ALIGN) * PIECE_ROW_ALIGN


def _as_rows(a, lead):
    head = a.shape[:lead]
    rows = _piece_rows(a.shape[lead:])
    n = 1
    for d in a.shape[lead:]:
        n *= d
    if n == rows * PACK_COLS:
        return a.reshape(head + (rows, PACK_COLS))
    flat = jnp.pad(a.reshape(head + (n,)), [(0, 0)] * lead + [(0, rows * PACK_COLS - n)])
    return flat.reshape(head + (rows, PACK_COLS))


def _pack(parts, lead, rows):
    pieces = [_as_rows(p, lead) for p in parts]
    used = sum(p.shape[lead] for p in pieces)
    head = pieces[0].shape[:lead]
    pieces.append(jnp.zeros(head + (rows - used, PACK_COLS), pieces[0].dtype))
    return jnp.concatenate(pieces, axis=lead)


def _unpack(packed, lead, shapes):
    out, off = [], 0
    head = packed.shape[:lead]
    for shp in shapes:
        rows = _piece_rows(shp)
        n = 1
        for d in shp:
            n *= d
        piece = lax.slice_in_dim(packed, off, off + rows, axis=lead)
        if n != rows * PACK_COLS:
            piece = piece.reshape(head + (rows * PACK_COLS,))[..., :n]
        out.append(piece.reshape(head + tuple(shp)))
        off += rows
    return out


def _pack_rows(shapes):
    rows = sum(_piece_rows(s) for s in shapes)
    return -(-rows // PACK_ROW_ALIGN) * PACK_ROW_ALIGN


def _rope_tables(positions, rot, lanes_per_head):
    half = rot // 2
    inv = ROPE_THETA ** (-jnp.arange(0, rot, 2, dtype=F32) / rot)
    ang = positions.astype(F32)[:, None] * inv
    cos, sin = jnp.cos(ang), jnp.sin(ang)
    s = positions.shape[0]
    rest = lanes_per_head - rot
    fill = 1.0 if lanes_per_head == SWA_HD else 0.0
    c = jnp.concatenate([cos, cos, jnp.full((s, rest), fill, F32)], axis=1)
    sa = jnp.concatenate([-sin, jnp.zeros((s, half + rest), F32)], axis=1)
    sb = jnp.concatenate([jnp.zeros((s, half), F32), sin, jnp.zeros((s, rest), F32)], axis=1)
    reps = LANES // lanes_per_head
    return tuple(jnp.tile(t, (1, reps)) for t in (c, sa, sb))


def _mla_weights(w_in, q_norm, w_q_b, kv_norm, w_kv_b):
    w_in_p = jnp.pad(w_in, ((0, 0), (0, MLA_LAT - w_in.shape[1])))
    wq = w_q_b.reshape(MLA_QR, MLA_H, MLA_NOPE + MLA_ROPE)
    wq_p = jnp.pad(wq, ((0, 0), (0, 0), (0, 2 * LANES - MLA_NOPE - MLA_ROPE))).reshape(MLA_QR, MLA_H * 2 * LANES)
    return dict(w_in=w_in_p, q_norm=q_norm.reshape(1, -1), kv_norm=kv_norm.reshape(1, -1), w_q=wq_p, w_kv=w_kv_b)


def _mla_grads_unpermute(g):
    gq = g["w_q"].reshape(MLA_QR, MLA_H, 2 * LANES)[:, :, :MLA_NOPE + MLA_ROPE]
    return dict(mla_w_in=g["w_in"][:, :MLA_QR + MLA_KVR + MLA_ROPE], mla_q_norm=g["q_norm"][0],
                mla_w_q_b=gq.reshape(MLA_QR, -1), mla_kv_norm=g["kv_norm"][0], mla_w_kv_b=g["w_kv"],
                mla_w_o=g["w_o"])


def _swa_dup(a):
    lead = a.shape[:-1]
    a = a.reshape(lead + (SWA_HKV, SWA_HD))
    return jnp.concatenate([a, a], axis=-1).reshape(lead + (SWA_KW,))


def _swa_undup(a):
    lead = a.shape[:-1]
    return a.reshape(lead + (SWA_HKV, LANES))[..., :SWA_HD].reshape(lead + (SWA_HKV * SWA_HD,))


def _swa_weights(w_qkv, b_qkv, sinks, w_o, b_o):
    nk = SWA_HKV * SWA_HD
    perm = lambda a: jnp.concatenate([a[..., :SWA_QW], _swa_dup(a[..., SWA_QW:SWA_QW + nk]),
                                      _swa_dup(a[..., SWA_QW + nk:])], axis=-1)
    w_p = perm(w_qkv)
    return dict(w_qkv=w_p, b_qkv=perm(b_qkv.astype(F32)).reshape(1, -1),
                sinks=sinks.reshape(1, -1), w_o=w_o, b_o=b_o.astype(F32).reshape(1, -1))


def _swa_grads_unpermute(g):
    unperm = lambda a: jnp.concatenate([a[..., :SWA_QW], _swa_undup(a[..., SWA_QW:SWA_QW + SWA_KW]),
                                        _swa_undup(a[..., SWA_QW + SWA_KW:])], axis=-1)
    return dict(swa_w_qkv=unperm(g["w_qkv"]), swa_b_qkv=unperm(g["b_qkv"])[0], swa_sinks=g["sinks"][:, 0],
                swa_w_o=g["w_o"], swa_b_o=g["b_o"])


SMALL_LAYOUT = (("ada_b", 24), ("ln_mix_g", 4), ("ln_mix_b", 4), ("ln_ffn_g", 4), ("ln_ffn_b", 4),
                ("mla_q_norm", 2), ("mla_kv_norm", 2), ("swa_sinks", 1), ("loss", 1))


def _small_pack(vals):
    rows = []
    for name, nrows in SMALL_LAYOUT:
        a = vals[name].reshape(nrows, -1).astype(F32)
        rows.append(jnp.pad(a, ((0, 0), (0, PACK_COLS - a.shape[1]))))
    cat = jnp.concatenate(rows, axis=0)
    return jnp.pad(cat, ((0, SMALL_ROWS - cat.shape[0]), (0, 0)))


def _small_unpack(packed, shapes):
    out, r = {}, 0
    for name, nrows in SMALL_LAYOUT:
        shp = shapes[name]
        n = 1
        for d in shp:
            n *= d
        out[name] = packed[r:r + nrows, :n // nrows].reshape(shp)
        r += nrows
    return out


def kernel(x, c, positions, ada_w, ada_b, ln_mix_g, ln_mix_b, ln_ffn_g, ln_ffn_b, ffn_w_gate, ffn_w_up, ffn_w_down, mla_w_in, mla_q_norm, mla_w_q_b, mla_kv_norm, mla_w_kv_b, mla_w_o, swa_w_qkv, swa_b_qkv, swa_sinks, swa_w_o, swa_b_o, loss_target, m_ada_w, m_ada_b, m_ln_mix_g, m_ln_mix_b, m_ln_ffn_g, m_ln_ffn_b, m_ffn_w_gate, m_ffn_w_up, m_ffn_w_down, m_mla_w_in, m_mla_q_norm, m_mla_w_q_b, m_mla_kv_norm, m_mla_w_kv_b, m_mla_w_o, m_swa_w_qkv, m_swa_b_qkv, m_swa_sinks, m_swa_w_o, m_swa_b_o, v_ada_w, v_ada_b, v_ln_mix_g, v_ln_mix_b, v_ln_ffn_g, v_ln_ffn_b, v_ffn_w_gate, v_ffn_w_up, v_ffn_w_down, v_mla_w_in, v_mla_q_norm, v_mla_w_q_b, v_mla_kv_norm, v_mla_w_kv_b, v_mla_w_o, v_swa_w_qkv, v_swa_b_qkv, v_swa_sinks, v_swa_w_o, v_swa_b_o):
    weights = dict(ada_w=ada_w, ada_b=ada_b, ln_mix_g=ln_mix_g, ln_mix_b=ln_mix_b, ln_ffn_g=ln_ffn_g,
                   ln_ffn_b=ln_ffn_b, ffn_w_gate=ffn_w_gate, ffn_w_up=ffn_w_up, ffn_w_down=ffn_w_down,
                   mla_w_in=mla_w_in, mla_q_norm=mla_q_norm, mla_w_q_b=mla_w_q_b, mla_kv_norm=mla_kv_norm,
                   mla_w_kv_b=mla_w_kv_b, mla_w_o=mla_w_o, swa_w_qkv=swa_w_qkv, swa_b_qkv=swa_b_qkv,
                   swa_sinks=swa_sinks, swa_w_o=swa_w_o, swa_b_o=swa_b_o)
    mom_m = dict(ada_w=m_ada_w, ada_b=m_ada_b, ln_mix_g=m_ln_mix_g, ln_mix_b=m_ln_mix_b, ln_ffn_g=m_ln_ffn_g,
                 ln_ffn_b=m_ln_ffn_b, ffn_w_gate=m_ffn_w_gate, ffn_w_up=m_ffn_w_up, ffn_w_down=m_ffn_w_down,
                 mla_w_in=m_mla_w_in, mla_q_norm=m_mla_q_norm, mla_w_q_b=m_mla_w_q_b, mla_kv_norm=m_mla_kv_norm,
                 mla_w_kv_b=m_mla_w_kv_b, mla_w_o=m_mla_w_o, swa_w_qkv=m_swa_w_qkv, swa_b_qkv=m_swa_b_qkv,
                 swa_sinks=m_swa_sinks, swa_w_o=m_swa_w_o, swa_b_o=m_swa_b_o)
    mom_v = dict(ada_w=v_ada_w, ada_b=v_ada_b, ln_mix_g=v_ln_mix_g, ln_mix_b=v_ln_mix_b, ln_ffn_g=v_ln_ffn_g,
                 ln_ffn_b=v_ln_ffn_b, ffn_w_gate=v_ffn_w_gate, ffn_w_up=v_ffn_w_up, ffn_w_down=v_ffn_w_down,
                 mla_w_in=v_mla_w_in, mla_q_norm=v_mla_q_norm, mla_w_q_b=v_mla_w_q_b, mla_kv_norm=v_mla_kv_norm,
                 mla_w_kv_b=v_mla_w_kv_b, mla_w_o=v_mla_w_o, swa_w_qkv=v_swa_w_qkv, swa_b_qkv=v_swa_b_qkv,
                 swa_sinks=v_swa_sinks, swa_w_o=v_swa_w_o, swa_b_o=v_swa_b_o)
    names = list(weights)
    my_x, my_y, my_c = lax.axis_index("x"), lax.axis_index("y"), lax.axis_index("c")
    chip = 2 * my_x + my_y
    batch_row = 2 * chip + my_c
    xs = x[0]
    target = loss_target[0]
    pos = positions[0]
    s = xs.shape[0]

    def item_shapes(items):
        return [(b - a,) + tuple(weights[n].shape[1:]) for n, a, b, _ in items]

    def pack_items(src, items, dtype):
        return _pack([src[n][a:b].astype(dtype) for n, a, b, _ in items], 0, _pack_rows(item_shapes(items)))

    full = {}

    def unpack_gathered(gathered, items):
        for (n, a, b, axis), part in zip(items, _unpack(gathered, 1, item_shapes(items))):
            whole = _from_chips(part, axis)
            for l in range(a, b):
                full[n, l] = whole[l - a]

    early = _exchange("ag_w_early", pack_items(weights, W_EARLY, BF16), ("x", "y"), "gather")
    unpack_gathered(early, W_EARLY)
    late_ride = _Exchange(pack_items(weights, W_LATE, BF16), ("x", "y"), "gather", chunks=8)

    c_rows = jnp.pad(c, ((0, 7), (0, 0)))
    c_all = _exchange("ag_c", c_rows, ("x", "y", "c"), "gather")[:, 0, :]
    ada_cols = ada_w.shape[2]
    ada_b_sh = lax.dynamic_slice_in_dim(ada_b, chip * ada_cols, ada_cols, axis=1).reshape(DEPTH, 1, ada_cols)
    mod_sh, cond_all = _ada_fwd("ada_fwd", c_all, ada_w, ada_b_sh)
    mod_all = _exchange("ag_mod", mod_sh.reshape(DEPTH * 8, ada_cols), ("x", "y"), "gather")
    mod_all = mod_all.reshape(N_CHIPS, DEPTH, 8, ada_cols)
    mod_mine = lax.dynamic_index_in_dim(mod_all, batch_row, axis=2, keepdims=False)
    mod = jnp.moveaxis(mod_mine, 0, 1).reshape(DEPTH, 6, 1, D)

    tabs_a = _rope_tables(pos, MLA_ROPE, LANES)
    tabs_b = _rope_tables(pos, SWA_ROT, SWA_HD)
    vec = lambda a, l: a[l].reshape(1, D)

    def mla_in_weights(j):
        return _mla_weights(full["mla_w_in", j], mla_q_norm[j], full["mla_w_q_b", j], mla_kv_norm[j],
                            full["mla_w_kv_b", j])

    mix_w, ffn_w = {}, {}
    saved = []
    x_cur = xs
    h = _modulate("mod0", x_cur, mod[0, 1], mod[0, 0])
    for l in range(DEPTH):
        j = l // 2
        if l % 2 == 0:
            mix_w[l] = mla_in_weights(j)
            o, res, ridden = _mla_fwd(f"mla{l}", h, mix_w[l], tabs_a, late_ride if l == 0 else None)
            if l == 0:
                unpack_gathered(ridden, W_LATE)
            mix_w[l]["w_o"] = full["mla_w_o", j]
            y_mix = _mm(f"mla{l}_o", o, mix_w[l]["w_o"], tm=512)
        else:
            mix_w[l] = _swa_weights(full["swa_w_qkv", j], full["swa_b_qkv", j], swa_sinks[j], full["swa_w_o", j],
                                    full["swa_b_o", j])
            y_mix, res = _swa_fwd(f"swa{l}", h, mix_w[l], tabs_b)
        wg, wu, wd = full["ffn_w_gate", l], full["ffn_w_up", l], full["ffn_w_down", l]
        ffn_w[l] = dict(wg=wg, wu=wu, wd=wd)
        x_mid, h2 = _post_mod(f"post_mix{l}", x_cur, y_mix, mod[l, 2], vec(ln_mix_g, l), vec(ln_mix_b, l),
                              mod[l, 4], mod[l, 3])
        gate, up, y_ffn = _ffn_fwd(f"ffn{l}", h2, ffn_w[l]["wg"], ffn_w[l]["wu"], ffn_w[l]["wd"])
        saved.append(dict(x_in=x_cur, y_mix=y_mix, res=res, x_mid=x_mid, h2=h2, gate=gate, up=up, y_ffn=y_ffn))
        if l < DEPTH - 1:
            x_cur, h = _post_mod(f"post_ffn{l}", x_mid, y_ffn, mod[l, 5], vec(ln_ffn_g, l), vec(ln_ffn_b, l),
                                 mod[l + 1, 1], mod[l + 1, 0])
        else:
            dxn, loss_part = _post_loss("post_loss", x_mid, y_ffn, mod[l, 5], vec(ln_ffn_g, l), vec(ln_ffn_b, l),
                                        target)

    gfull = {n: [None] * weights[n].shape[0] for n, _ in SHARDED}
    gsmall = {n: [None] * weights[n].shape[0] for n in ("ln_mix_g", "ln_mix_b", "ln_ffn_g", "ln_ffn_b",
                                                         "mla_q_norm", "mla_kv_norm", "swa_sinks")}
    dmod = [None] * DEPTH

    def grad_ride(items):
        parts = [_to_chips(jnp.stack(gfull[n][a:b]).astype(BF16), axis) for n, a, b, axis in items]
        return _Exchange(_pack(parts, 1, _pack_rows(item_shapes(items))), ("x", "y", "c"), "to_chip", chunks=4)

    def ride_with_wo(items, j):
        def make(g_wo):
            gfull["mla_w_o"][j] = g_wo
            return grad_ride(items)
        return make

    rides = {DEPTH - 2: G_FIRST, 0: G_SECOND}
    g_parts = {}
    sums_f, sums_m, sums_fm, sums_mm = {}, {}, {}, {}
    top = DEPTH - 1
    dxp, dy, sums_f[top] = _post_bwd(f"post_ffn_bwd{top}", dxn, saved[top]["x_mid"], saved[top]["y_ffn"], mod[top, 5],
                                     vec(ln_ffn_g, top))
    for l in reversed(range(DEPTH)):
        sv = saved[l]
        j = l // 2
        dgt, dup, act, dh2 = _ffn_bwd(f"ffn_bwd{l}", dy, sv["gate"], sv["up"], ffn_w[l]["wd"], ffn_w[l]["wg"],
                                      ffn_w[l]["wu"])
        gfull["ffn_w_gate"][l] = _mm_tn(f"ffn_gwg{l}", sv["h2"], dgt, tn=FF // 2)
        gfull["ffn_w_up"][l] = _mm_tn(f"ffn_gwu{l}", sv["h2"], dup, tn=FF // 2)
        gfull["ffn_w_down"][l] = _mm_tn(f"ffn_gwd{l}", act, dy)
        dxp, dy, six = _mod_post_bwd(f"modpost_mix_bwd{l}", dh2, dxp, sv["x_in"], sv["y_mix"], mod[l, 2],
                                     vec(ln_mix_g, l), vec(ln_mix_b, l), mod[l, 4])
        sums_fm[l], sums_m[l] = six[0:2], six[2:6]
        if l % 2 == 0:
            dh, g, ridden = _mla_bwd(f"mla{l}", dy, sv["res"], mix_w[l], tabs_a, ride_with_wo(rides[l], j))
            g_parts[rides[l]] = _sum_groups(f"rs_sum{l}", ridden)
            g = _mla_grads_unpermute(g)
        else:
            dh, g = _swa_bwd(f"swa{l}", dy, sv["res"], mix_w[l], tabs_b)
            g["b_o"] = sums_m[l][3]
            g = _swa_grads_unpermute(g)
        for n, val in g.items():
            (gfull if n in gfull else gsmall)[n][j] = val
        if l > 0:
            below = saved[l - 1]
            dxp, dy, six = _mod_post_bwd(f"modpost_ffn_bwd{l - 1}", dh, dxp, below["x_mid"], below["y_ffn"],
                                         mod[l - 1, 5], vec(ln_ffn_g, l - 1), vec(ln_ffn_b, l - 1), mod[l, 1])
            sums_mm[l], sums_f[l - 1] = six[0:2], six[2:6]
        else:
            dxn, sums_mm[l] = _mod_bwd("mod_mix_bwd0", dh, sv["x_in"], dxp, mod[l, 1])
    for l in range(DEPTH):
        gsmall["ln_ffn_g"][l], gsmall["ln_ffn_b"][l] = sums_f[l][0], sums_f[l][1]
        gsmall["ln_mix_g"][l], gsmall["ln_mix_b"][l] = sums_m[l][0], sums_m[l][1]
        dmod[l] = jnp.stack([sums_mm[l][1], sums_mm[l][0], sums_m[l][2], sums_fm[l][1], sums_fm[l][0], sums_f[l][2]])
    grad_x = dxn[None]

    small_vals = {n: jnp.stack(v) for n, v in gsmall.items()}
    small_vals["ada_b"] = jnp.stack(dmod)
    small_vals["loss"] = loss_part[0, 0:1]
    small_all = _exchange("ag_small", _small_pack(small_vals), ("x", "y", "c"), "gather")
    small_sum = _sum_groups("sum_small", small_all)
    dmod_all = small_all[:, :DEPTH * 6, :].reshape(8, DEPTH, 6 * D)
    dmod_sh = jnp.moveaxis(lax.dynamic_slice_in_dim(dmod_all, chip * ada_cols, ada_cols, axis=2), 0, 1)
    g_ada_w = _ada_grad("ada_grad", cond_all.T, dmod_sh)

    tail = grad_ride(G_LAST)
    g_parts[G_LAST] = _sum_groups("rs_sum_tail", _exchange("rs_tail", tail.src, tail.axes, tail.mode, tail.chunks))

    pieces = {}
    for items in (G_FIRST, G_SECOND, G_LAST):
        for (n, a, _, _), part in zip(items, _unpack(g_parts[items], 0, item_shapes(items))):
            pieces.setdefault(n, []).append((a, part))
    grads = {n: jnp.concatenate([p for _, p in sorted(ps, key=lambda ap: ap[0])], axis=0) for n, ps in pieces.items()}
    grads["ada_w"] = g_ada_w
    small_shapes = {n: weights[n].shape for n, _ in SMALL_LAYOUT if n != "loss"}
    small_shapes["loss"] = (1,)
    grads.update(_small_unpack(small_sum, small_shapes))

    def as_2d(a):
        return a.reshape(-1, a.shape[-1])

    outs = [grads, {}, {}, {}]
    for n in names:
        if n in small_shapes:
            continue
        res = _adamw(f"adamw_{n}", *[as_2d(src[n]) for src in (grads, weights, mom_m, mom_v)])
        for o, r in zip(outs[1:], res):
            o[n] = r.reshape(weights[n].shape)

    def small_of(src):
        return _small_pack({**{n: src[n] for n in small_shapes if n != "loss"}, "loss": jnp.zeros((1,), F32)})

    res = _adamw("adamw_small", small_sum, small_of(weights), small_of(mom_m), small_of(mom_v))
    for o, r in zip(outs[1:], res):
        o.update(_small_unpack(r, small_shapes))
    loss = grads["loss"][0]
    return (loss, grad_x, *[o[n] for o in outs for n in names])
```

```python
import jax
import jax.numpy as jnp
from jax import lax
from jax.experimental import pallas as pl
from jax.experimental.pallas import tpu as pltpu

F32 = jnp.float32
BF16 = jnp.bfloat16

D = 1024
DEPTH = 4
ROPE_THETA = 500000.0
LN_EPS = 1e-5
RMS_EPS = 1e-6
MLA_H = 8
MLA_NOPE = 128
MLA_ROPE = 64
MLA_V = 128
MLA_QR = 384
MLA_KVR = 256
MLA_LAT = 768
SWA_HQ = 16
SWA_HKV = 4
SWA_HD = 64
SWA_W = 128
SWA_ROT = 16
FF = 2816
ALPHA = (2 * DEPTH) ** 0.25
ADAM_LR = 0.001
ADAM_B1 = 0.9
ADAM_B2 = 0.999
ADAM_EPS = 1e-08
ADAM_WD = 0.01
ADAM_STEP = 10
NEG = -1e30
LANES = 128
N_CHIPS = 4
PACK_COLS = 1024
PACK_ROW_ALIGN = 512
SMALL_ROWS = 48
ROW_TILE = 512

SHARDED = (
    ("ffn_w_gate", 2), ("ffn_w_up", 2), ("ffn_w_down", 1), ("mla_w_in", 1), ("mla_w_q_b", 2),
    ("mla_w_kv_b", 2), ("mla_w_o", 1), ("swa_w_qkv", 2), ("swa_b_qkv", 1), ("swa_w_o", 1), ("swa_b_o", 1),
)


def _items(*specs):
    axis = dict(SHARDED)
    return tuple((n, a, b, axis[n]) for names, a, b in specs for n in names)


_FFN = ("ffn_w_gate", "ffn_w_up", "ffn_w_down")
_SWA = ("swa_w_qkv", "swa_b_qkv", "swa_w_o", "swa_b_o")
_MLA_IN = ("mla_w_in", "mla_w_q_b", "mla_w_kv_b")
_MLA_OUT = ("mla_w_o",)
W_EARLY = _items((_MLA_IN, 0, 1))
W_LATE = _items((_FFN, 0, 4), (_MLA_IN, 1, 2), (_MLA_OUT, 0, 2), (_SWA, 0, 2))
G_FIRST = _items((_FFN, 3, 4), (_SWA, 1, 2), (_FFN, 2, 3), (_MLA_OUT, 1, 2))
G_SECOND = _items((_MLA_IN, 1, 2), (_FFN, 1, 2), (_SWA, 0, 1), (_FFN, 0, 1), (_MLA_OUT, 0, 1))
G_LAST = _items((_MLA_IN, 0, 1))


def _cparams(*sem):
    return pltpu.CompilerParams(dimension_semantics=sem)


def _row_spec(tr, cols):
    return pl.BlockSpec((tr, cols), lambda i: (i, 0))


def _vec_spec(cols):
    return pl.BlockSpec((1, cols), lambda i: (0, 0))


def _rope(x, c, sa, sb, sh):
    n = x.shape[1]
    return x * c + pltpu.roll(x, n - sh, 1) * sa + pltpu.roll(x, sh, 1) * sb


def _rope_t(d, c, sa, sb, sh):
    n = d.shape[1]
    return d * c + pltpu.roll(d * sa, sh, 1) + pltpu.roll(d * sb, n - sh, 1)


def _rowsum8(t):
    r, n = t.shape
    return jnp.sum(t.reshape(r // 8, 8, n), axis=0)


class _Exchange:
    def __init__(self, src, axes, mode, chunks=1):
        self.src, self.axes, self.mode, self.chunks = src, axes, mode, chunks
        self.g = 2 ** len(axes)
        self.blk = tuple(src.shape if mode == "gather" else src.shape[1:])
        self.out_shape = jax.ShapeDtypeStruct((self.g,) + self.blk, src.dtype)
        nsem = (self.g - 1) * chunks
        self.scratch = [pltpu.SemaphoreType.DMA((nsem,)), pltpu.SemaphoreType.DMA((nsem,)),
                        pltpu.SemaphoreType.DMA(())]

    def copies(self, src_ref, out_ref, send_sems, recv_sems, loc_sem):
        pos = {a: lax.axis_index(a) for a in ("x", "y", "c")}
        rows = self.blk[0] // self.chunks

        def gidx(p):
            idx = 0
            for a in self.axes:
                idx = idx * 2 + p[a]
            return idx

        def view(p):
            if self.mode == "gather":
                return src_ref
            if self.mode == "to_chip":
                return src_ref.at[2 * p["x"] + p["y"]]
            return src_ref.at[gidx(p)]

        me = gidx(pos)
        out = [pltpu.make_async_copy(view(pos), out_ref.at[me], loc_sem)]
        for k in range(self.chunks):
            piece = pl.ds(k * rows, rows)
            for j in range(1, self.g):
                peer = dict(pos)
                for bit, a in enumerate(reversed(self.axes)):
                    if (j >> bit) & 1:
                        peer[a] = 1 - pos[a]
                sem = (j - 1) * self.chunks + k
                out.append(pltpu.make_async_remote_copy(
                    src_ref=view(peer).at[piece], dst_ref=out_ref.at[me, piece],
                    send_sem=send_sems.at[sem], recv_sem=recv_sems.at[sem],
                    device_id=(peer["x"], peer["y"], peer["c"]), device_id_type=pl.DeviceIdType.MESH))
        return out


def _exchange(name, src, axes, mode, chunks=1):
    ex = _Exchange(src, axes, mode, chunks)

    def body(src_ref, out_ref, send_sems, recv_sems, loc_sem):
        copies = ex.copies(src_ref, out_ref, send_sems, recv_sems, loc_sem)
        for cp in copies:
            cp.start()
        for cp in copies:
            cp.wait()

    return pl.pallas_call(
        body, name=name, out_shape=ex.out_shape,
        in_specs=[pl.BlockSpec(memory_space=pl.ANY)],
        out_specs=pl.BlockSpec(memory_space=pl.ANY),
        scratch_shapes=ex.scratch,
    )(src)


def _tri_call(body, rider, *, name, grid, tables, in_specs, out_specs, out_shape, scratch_shapes, operands):
    n_tab, n_in, n_out, n_scr = len(tables), len(in_specs), len(out_specs), len(scratch_shapes)
    in_specs, out_specs, out_shape = list(in_specs), list(out_specs), list(out_shape)
    scratch_shapes, operands = list(scratch_shapes), list(operands)
    if rider is not None:
        any_spec = pl.BlockSpec(memory_space=pl.ANY)
        in_specs.append(any_spec)
        out_specs.append(any_spec)
        out_shape.append(rider.out_shape)
        scratch_shapes.extend(rider.scratch)
        operands.append(rider.src)
        first, last = _grid_ends(grid)

    def wrapped(*refs):
        tabs, refs = refs[:n_tab], refs[n_tab:]
        if rider is None:
            return body(*tabs, *refs)
        ins, src_ref = refs[:n_in], refs[n_in]
        outs, out_ref = refs[n_in + 1:n_in + 1 + n_out], refs[n_in + 1 + n_out]
        scr = refs[n_in + 2 + n_out:n_in + 2 + n_out + n_scr]
        sems = refs[n_in + 2 + n_out + n_scr:]

        @pl.when(first())
        def _():
            for cp in rider.copies(src_ref, out_ref, *sems):
                cp.start()

        body(*tabs, *ins, *outs, *scr)

        @pl.when(last())
        def _():
            for cp in rider.copies(src_ref, out_ref, *sems):
                cp.wait()

    return pl.pallas_call(
        wrapped, name=name, out_shape=out_shape,
        grid_spec=pltpu.PrefetchScalarGridSpec(num_scalar_prefetch=n_tab, grid=grid, in_specs=in_specs,
                                               out_specs=out_specs, scratch_shapes=scratch_shapes),
        compiler_params=_cparams(*(["arbitrary"] * len(grid))),
    )(*tables, *operands)


def _tri_tables(n, queries_outer):
    if queries_outer:
        pairs = [(qi, ki) for qi in range(n) for ki in range(qi + 1)]
    else:
        pairs = [(qi, ki) for ki in range(n) for qi in range(ki, n)]
    return (jnp.asarray([p[0] for p in pairs], jnp.int32), jnp.asarray([p[1] for p in pairs], jnp.int32))


def _sum_groups(name, a):
    g, r, c = a.shape
    tr = min(ROW_TILE, r)

    def body(a_ref, o_ref):
        acc = a_ref[0].astype(F32)
        for i in range(1, g):
            acc = acc + a_ref[i].astype(F32)
        o_ref[...] = acc

    return pl.pallas_call(
        body, name=name, grid=(r // tr,),
        in_specs=[pl.BlockSpec((g, tr, c), lambda i: (0, i, 0))],
        out_specs=pl.BlockSpec((tr, c), lambda i: (i, 0)),
        out_shape=jax.ShapeDtypeStruct((r, c), F32),
        compiler_params=_cparams("parallel"),
    )(a)


def _mm(name, a, b, *, tm, tn=None, out_dtype=F32, epilogue=None, extras=(), extra_specs=(), nt=False):
    m, k = a.shape
    n = b.shape[0] if nt else b.shape[1]
    tn = tn or n
    tm = min(tm, m)
    b_spec = pl.BlockSpec((tn, k), lambda i, j: (j, 0)) if nt else pl.BlockSpec((k, tn), lambda i, j: (0, j))

    def body(a_ref, b_ref, *rest):
        o_ref = rest[-1]
        if nt:
            acc = lax.dot_general(a_ref[...], b_ref[...], (((1,), (1,)), ((), ())), preferred_element_type=F32)
        else:
            acc = jnp.dot(a_ref[...], b_ref[...], preferred_element_type=F32)
        if epilogue is None:
            o_ref[...] = acc.astype(o_ref.dtype)
        else:
            epilogue(acc, o_ref, *rest[:-1])

    return pl.pallas_call(
        body, name=name, grid=(m // tm, n // tn),
        in_specs=[pl.BlockSpec((tm, k), lambda i, j: (i, 0)), b_spec, *extra_specs],
        out_specs=pl.BlockSpec((tm, tn), lambda i, j: (i, j)),
        out_shape=jax.ShapeDtypeStruct((m, n), out_dtype),
        compiler_params=_cparams("parallel", "parallel"),
    )(a, b, *extras)


def _mm_tn(name, a, b, *, tn=None, tk=1024):
    s, m = a.shape
    n = b.shape[1]
    tn = tn or n
    tk = min(tk, s)

    def body(a_ref, b_ref, o_ref):
        part = lax.dot_general(a_ref[...], b_ref[...], (((0,), (0,)), ((), ())), preferred_element_type=F32)

        @pl.when(pl.program_id(1) == 0)
        def _():
            o_ref[...] = part

        @pl.when(pl.program_id(1) > 0)
        def _():
            o_ref[...] += part

    return pl.pallas_call(
        body, name=name, grid=(n // tn, s // tk),
        in_specs=[pl.BlockSpec((tk, m), lambda j, k: (k, 0)), pl.BlockSpec((tk, tn), lambda j, k: (k, j))],
        out_specs=pl.BlockSpec((m, tn), lambda j, k: (0, j)),
        out_shape=jax.ShapeDtypeStruct((m, n), F32),
        compiler_params=_cparams("parallel", "arbitrary"),
    )(a, b)


def _modulate(name, x, sc, sh):
    s = x.shape[0]
    tr = min(ROW_TILE, s)

    def body(x_ref, sc_ref, sh_ref, h_ref):
        h_ref[...] = (x_ref[...] * (1.0 + sc_ref[...]) + sh_ref[...]).astype(BF16)

    return pl.pallas_call(
        body, name=name, grid=(s // tr,),
        in_specs=[_row_spec(tr, D), _vec_spec(D), _vec_spec(D)],
        out_specs=_row_spec(tr, D),
        out_shape=jax.ShapeDtypeStruct((s, D), BF16),
        compiler_params=_cparams("parallel"),
    )(x, sc, sh)


def _ln_stats(z):
    mu = jnp.mean(z, axis=1, keepdims=True)
    zc = z - mu
    var = jnp.mean(zc * zc, axis=1, keepdims=True)
    r = lax.rsqrt(var + LN_EPS)
    return zc * r, r


def _post_mod(name, x, y, g, gamma, beta, sc, sh):
    s = x.shape[0]
    tr = min(ROW_TILE, s)

    def body(x_ref, y_ref, g_ref, ga_ref, be_ref, sc_ref, sh_ref, xn_ref, h_ref):
        zh, _ = _ln_stats(ALPHA * x_ref[...] + g_ref[...] * y_ref[...])
        xn = zh * ga_ref[...] + be_ref[...]
        xn_ref[...] = xn
        h_ref[...] = (xn * (1.0 + sc_ref[...]) + sh_ref[...]).astype(BF16)

    return pl.pallas_call(
        body, name=name, grid=(s // tr,),
        in_specs=[_row_spec(tr, D), _row_spec(tr, D)] + [_vec_spec(D)] * 5,
        out_specs=[_row_spec(tr, D), _row_spec(tr, D)],
        out_shape=[jax.ShapeDtypeStruct((s, D), F32), jax.ShapeDtypeStruct((s, D), BF16)],
        compiler_params=_cparams("parallel"),
    )(x, y, g, gamma, beta, sc, sh)


def _post_loss(name, x, y, g, gamma, beta, target):
    s = x.shape[0]
    tr = min(ROW_TILE, s)
    nt = s // tr

    def body(x_ref, y_ref, g_ref, ga_ref, be_ref, t_ref, dx_ref, loss_ref, acc):
        i = pl.program_id(0)
        zh, _ = _ln_stats(ALPHA * x_ref[...] + g_ref[...] * y_ref[...])
        e = zh * ga_ref[...] + be_ref[...] - t_ref[...]
        dx_ref[...] = e * (1.0 / D)

        @pl.when(i == 0)
        def _():
            acc[...] = jnp.zeros_like(acc)

        acc[...] += _rowsum8(e * e)

        @pl.when(i == nt - 1)
        def _():
            loss_ref[...] = jnp.full(loss_ref.shape, jnp.sum(acc[...]) * (0.5 / D), F32)

    return pl.pallas_call(
        body, name=name, grid=(nt,),
        in_specs=[_row_spec(tr, D), _row_spec(tr, D)] + [_vec_spec(D)] * 3 + [_row_spec(tr, D)],
        out_specs=[_row_spec(tr, D), pl.BlockSpec((8, LANES), lambda i: (0, 0))],
        out_shape=[jax.ShapeDtypeStruct((s, D), F32), jax.ShapeDtypeStruct((8, LANES), F32)],
        scratch_shapes=[pltpu.VMEM((8, D), F32)],
        compiler_params=_cparams("arbitrary"),
    )(x, y, g, gamma, beta, target)


def _post_bwd(name, dxn, x, y, g, gamma):
    s = x.shape[0]
    tr = min(ROW_TILE, s)
    nt = s // tr

    def body(d_ref, x_ref, y_ref, g_ref, ga_ref, dxp_ref, dy_ref, sums_ref, a0, a1, a2, a3):
        i = pl.program_id(0)
        yv = y_ref[...]
        gv = g_ref[...]
        zh, r = _ln_stats(ALPHA * x_ref[...] + gv * yv)
        dxn_v = d_ref[...]
        dzh = dxn_v * ga_ref[...]
        dz = r * (dzh - jnp.mean(dzh, axis=1, keepdims=True) - zh * jnp.mean(dzh * zh, axis=1, keepdims=True))
        dxp_ref[...] = ALPHA * dz
        dyv = gv * dz
        dy_ref[...] = dyv.astype(BF16)

        @pl.when(i == 0)
        def _():
            for a in (a0, a1, a2, a3):
                a[...] = jnp.zeros_like(a)

        a0[...] += _rowsum8(dxn_v * zh)
        a1[...] += _rowsum8(dxn_v)
        a2[...] += _rowsum8(dz * yv)
        a3[...] += _rowsum8(dyv)

        @pl.when(i == nt - 1)
        def _():
            for k, a in enumerate((a0, a1, a2, a3)):
                sums_ref[k:k + 1, :] = jnp.sum(a[...], axis=0, keepdims=True)

    return pl.pallas_call(
        body, name=name, grid=(nt,),
        in_specs=[_row_spec(tr, D)] * 3 + [_vec_spec(D)] * 2,
        out_specs=[_row_spec(tr, D), _row_spec(tr, D), pl.BlockSpec((4, D), lambda i: (0, 0))],
        out_shape=[jax.ShapeDtypeStruct((s, D), F32), jax.ShapeDtypeStruct((s, D), BF16),
                   jax.ShapeDtypeStruct((4, D), F32)],
        scratch_shapes=[pltpu.VMEM((8, D), F32)] * 4,
        compiler_params=_cparams("arbitrary"),
    )(dxn, x, y, g, gamma)


def _mod_post_bwd(name, dh, dxp, x, y, g, gamma, beta, sc):
    s = x.shape[0]
    tr = min(ROW_TILE, s)
    nt = s // tr

    def body(dh_ref, dxp_ref, x_ref, y_ref, g_ref, ga_ref, be_ref, sc_ref, dxo_ref, dy_ref, sums_ref, *acc):
        i = pl.program_id(0)
        yv = y_ref[...]
        gv = g_ref[...]
        zh, r = _ln_stats(ALPHA * x_ref[...] + gv * yv)
        xn = zh * ga_ref[...] + be_ref[...]
        dhv = dh_ref[...]
        dxn_v = dxp_ref[...] + dhv * (1.0 + sc_ref[...])
        dzh = dxn_v * ga_ref[...]
        dz = r * (dzh - jnp.mean(dzh, axis=1, keepdims=True) - zh * jnp.mean(dzh * zh, axis=1, keepdims=True))
        dxo_ref[...] = ALPHA * dz
        dyv = gv * dz
        dy_ref[...] = dyv.astype(BF16)

        @pl.when(i == 0)
        def _():
            for a in acc:
                a[...] = jnp.zeros_like(a)

        for a, val in zip(acc, (dhv * xn, dhv, dxn_v * zh, dxn_v, dz * yv, dyv)):
            a[...] += _rowsum8(val)

        @pl.when(i == nt - 1)
        def _():
            for k, a in enumerate(acc):
                sums_ref[k:k + 1, :] = jnp.sum(a[...], axis=0, keepdims=True)

    return pl.pallas_call(
        body, name=name, grid=(nt,),
        in_specs=[_row_spec(tr, D)] * 4 + [_vec_spec(D)] * 4,
        out_specs=[_row_spec(tr, D), _row_spec(tr, D), pl.BlockSpec((6, D), lambda i: (0, 0))],
        out_shape=[jax.ShapeDtypeStruct((s, D), F32), jax.ShapeDtypeStruct((s, D), BF16),
                   jax.ShapeDtypeStruct((6, D), F32)],
        scratch_shapes=[pltpu.VMEM((8, D), F32)] * 6,
        compiler_params=_cparams("arbitrary"),
    )(dh, dxp, x, y, g, gamma, beta, sc)


def _mod_bwd(name, dh, x, dxp, sc):
    s = x.shape[0]
    tr = min(ROW_TILE, s)
    nt = s // tr

    def body(dh_ref, x_ref, dxp_ref, sc_ref, dx_ref, sums_ref, a0, a1):
        i = pl.program_id(0)
        dhv = dh_ref[...]
        dx_ref[...] = dxp_ref[...] + dhv * (1.0 + sc_ref[...])

        @pl.when(i == 0)
        def _():
            a0[...] = jnp.zeros_like(a0)
            a1[...] = jnp.zeros_like(a1)

        a0[...] += _rowsum8(dhv * x_ref[...])
        a1[...] += _rowsum8(dhv)

        @pl.when(i == nt - 1)
        def _():
            sums_ref[0:1, :] = jnp.sum(a0[...], axis=0, keepdims=True)
            sums_ref[1:2, :] = jnp.sum(a1[...], axis=0, keepdims=True)

    return pl.pallas_call(
        body, name=name, grid=(nt,),
        in_specs=[_row_spec(tr, D)] * 3 + [_vec_spec(D)],
        out_specs=[_row_spec(tr, D), pl.BlockSpec((2, D), lambda i: (0, 0))],
        out_shape=[jax.ShapeDtypeStruct((s, D), F32), jax.ShapeDtypeStruct((2, D), F32)],
        scratch_shapes=[pltpu.VMEM((8, D), F32)] * 2,
        compiler_params=_cparams("arbitrary"),
    )(dh, x, dxp, sc)


def _ffn_tiles(s):
    return min(ROW_TILE, s), FF // 2


def _ffn_fwd(name, h, wg, wu, wd):
    s = h.shape[0]
    tm, tf = _ffn_tiles(s)

    def body(h_ref, wg_ref, wu_ref, wd_ref, gate_ref, up_ref, y_ref):
        hv = h_ref[...]
        gt = jnp.dot(hv, wg_ref[...], preferred_element_type=F32)
        up = jnp.dot(hv, wu_ref[...], preferred_element_type=F32)
        gate_ref[...] = gt
        up_ref[...] = up
        act = (gt * jax.nn.sigmoid(gt) * up).astype(BF16)
        part = jnp.dot(act, wd_ref[...], preferred_element_type=F32)

        @pl.when(pl.program_id(1) == 0)
        def _():
            y_ref[...] = part

        @pl.when(pl.program_id(1) > 0)
        def _():
            y_ref[...] += part

    return pl.pallas_call(
        body, name=name, grid=(s // tm, FF // tf),
        in_specs=[pl.BlockSpec((tm, D), lambda i, f: (i, 0)), pl.BlockSpec((D, tf), lambda i, f: (0, f)),
                  pl.BlockSpec((D, tf), lambda i, f: (0, f)), pl.BlockSpec((tf, D), lambda i, f: (f, 0))],
        out_specs=[pl.BlockSpec((tm, tf), lambda i, f: (i, f)), pl.BlockSpec((tm, tf), lambda i, f: (i, f)),
                   pl.BlockSpec((tm, D), lambda i, f: (i, 0))],
        out_shape=[jax.ShapeDtypeStruct((s, FF), F32), jax.ShapeDtypeStruct((s, FF), F32),
                   jax.ShapeDtypeStruct((s, D), F32)],
        compiler_params=_cparams("parallel", "arbitrary"),
    )(h, wg, wu, wd)


def _ffn_bwd(name, dy, gate, up, wd, wg, wu):
    s = dy.shape[0]
    tm, tf = _ffn_tiles(s)
    nt = (((1,), (1,)), ((), ()))

    def body(dy_ref, gate_ref, up_ref, wd_ref, wg_ref, wu_ref, dg_ref, du_ref, act_ref, dh_ref):
        dact = lax.dot_general(dy_ref[...], wd_ref[...], nt, preferred_element_type=F32)
        gt = gate_ref[...]
        up = up_ref[...]
        sig = jax.nn.sigmoid(gt)
        silu = gt * sig
        dgt = (dact * up * (sig * (1.0 + gt * (1.0 - sig)))).astype(BF16)
        dup = (dact * silu).astype(BF16)
        dg_ref[...] = dgt
        du_ref[...] = dup
        act_ref[...] = (silu * up).astype(BF16)
        part = (lax.dot_general(dgt, wg_ref[...], nt, preferred_element_type=F32)
                + lax.dot_general(dup, wu_ref[...], nt, preferred_element_type=F32))

        @pl.when(pl.program_id(1) == 0)
        def _():
            dh_ref[...] = part

        @pl.when(pl.program_id(1) > 0)
        def _():
            dh_ref[...] += part

    tile = pl.BlockSpec((tm, tf), lambda i, f: (i, f))
    return pl.pallas_call(
        body, name=name, grid=(s // tm, FF // tf),
        in_specs=[pl.BlockSpec((tm, D), lambda i, f: (i, 0)), tile, tile,
                  pl.BlockSpec((tf, D), lambda i, f: (f, 0)), pl.BlockSpec((D, tf), lambda i, f: (0, f)),
                  pl.BlockSpec((D, tf), lambda i, f: (0, f))],
        out_specs=[tile, tile, tile, pl.BlockSpec((tm, D), lambda i, f: (i, 0))],
        out_shape=[jax.ShapeDtypeStruct((s, FF), BF16)] * 3 + [jax.ShapeDtypeStruct((s, D), F32)],
        compiler_params=_cparams("parallel", "arbitrary"),
    )(dy, gate, up, wd, wg, wu)


def _mla_lat_post(name, lat, qw, kvw, rc, rsa, rsb):
    s = lat.shape[0]
    tr = min(ROW_TILE, s)

    def body(lat_ref, qw_ref, kvw_ref, c_ref, sa_ref, sb_ref, qn_ref, kvn_ref, kr_ref):
        ql = lat_ref[:, 0:MLA_QR]
        kl = lat_ref[:, MLA_QR:MLA_QR + MLA_KVR]
        qn_ref[...] = (ql * lax.rsqrt(jnp.mean(ql * ql, axis=1, keepdims=True) + RMS_EPS) * qw_ref[...]).astype(BF16)
        kvn_ref[...] = (kl * lax.rsqrt(jnp.mean(kl * kl, axis=1, keepdims=True) + RMS_EPS) * kvw_ref[...]).astype(BF16)
        kr_ref[...] = _rope(lat_ref[:, MLA_QR + MLA_KVR:MLA_LAT], c_ref[...], sa_ref[...], sb_ref[...],
                            MLA_ROPE // 2).astype(BF16)

    return pl.pallas_call(
        body, name=name, grid=(s // tr,),
        in_specs=[_row_spec(tr, MLA_LAT), _vec_spec(MLA_QR), _vec_spec(MLA_KVR)] + [_row_spec(tr, LANES)] * 3,
        out_specs=[_row_spec(tr, MLA_QR), _row_spec(tr, MLA_KVR), _row_spec(tr, LANES)],
        out_shape=[jax.ShapeDtypeStruct((s, MLA_QR), BF16), jax.ShapeDtypeStruct((s, MLA_KVR), BF16),
                   jax.ShapeDtypeStruct((s, LANES), BF16)],
        compiler_params=_cparams("parallel"),
    )(lat, qw, kvw, rc, rsa, rsb)


def _mla_lat_bwd(name, lat, dqn, dkvn, dkr_heads, qw, kvw, rc, rsa, rsb):
    s = lat.shape[0]
    tr = min(ROW_TILE, s)
    nt = s // tr

    def rms_bwd(x, w, dy):
        r = lax.rsqrt(jnp.mean(x * x, axis=1, keepdims=True) + RMS_EPS)
        xh = x * r
        gdy = dy * w
        return r * (gdy - xh * jnp.mean(gdy * xh, axis=1, keepdims=True)), dy * xh

    def body(lat_ref, dqn_ref, dkvn_ref, dkr_ref, qw_ref, kvw_ref, c_ref, sa_ref, sb_ref,
             dlat_ref, dqw_ref, dkvw_ref, aq, akv):
        i = pl.program_id(0)
        dq, dqw = rms_bwd(lat_ref[:, 0:MLA_QR], qw_ref[...], dqn_ref[...])
        dk, dkw = rms_bwd(lat_ref[:, MLA_QR:MLA_QR + MLA_KVR], kvw_ref[...], dkvn_ref[...])
        dkr = dkr_ref[0]
        for hh in range(1, MLA_H):
            dkr = dkr + dkr_ref[hh]
        dkr = _rope_t(dkr, c_ref[...], sa_ref[...], sb_ref[...], MLA_ROPE // 2)
        dlat_ref[:, 0:MLA_QR] = dq.astype(BF16)
        dlat_ref[:, MLA_QR:MLA_QR + MLA_KVR] = dk.astype(BF16)
        dlat_ref[:, MLA_QR + MLA_KVR:MLA_LAT] = dkr.astype(BF16)

        @pl.when(i == 0)
        def _():
            aq[...] = jnp.zeros_like(aq)
            akv[...] = jnp.zeros_like(akv)

        aq[...] += _rowsum8(dqw)
        akv[...] += _rowsum8(dkw)

        @pl.when(i == nt - 1)
        def _():
            dqw_ref[...] = jnp.sum(aq[...], axis=0, keepdims=True)
            dkvw_ref[...] = jnp.sum(akv[...], axis=0, keepdims=True)

    return pl.pallas_call(
        body, name=name, grid=(nt,),
        in_specs=[_row_spec(tr, MLA_LAT), _row_spec(tr, MLA_QR), _row_spec(tr, MLA_KVR),
                  pl.BlockSpec((MLA_H, tr, LANES), lambda i: (0, i, 0)), _vec_spec(MLA_QR), _vec_spec(MLA_KVR)]
        + [_row_spec(tr, LANES)] * 3,
        out_specs=[_row_spec(tr, MLA_LAT), _vec_spec(MLA_QR), _vec_spec(MLA_KVR)],
        out_shape=[jax.ShapeDtypeStruct((s, MLA_LAT), BF16), jax.ShapeDtypeStruct((1, MLA_QR), F32),
                   jax.ShapeDtypeStruct((1, MLA_KVR), F32)],
        scratch_shapes=[pltpu.VMEM((8, MLA_QR), F32), pltpu.VMEM((8, MLA_KVR), F32)],
        compiler_params=_cparams("arbitrary"),
    )(lat, dqn, dkvn, dkr_heads, qw, kvw, rc, rsa, rsb)


def _attn_tile(s):
    return min(1024, max(LANES, s // 2))


MLA_SCALE = (MLA_NOPE + MLA_ROPE) ** -0.5
LOG2E = 1.4426950408889634
LN2 = 1.0 / LOG2E
MLA_C2 = MLA_SCALE * LOG2E
NT_DIMS = (((1,), (1,)), ((), ()))
TN_DIMS = (((0,), (0,)), ((), ()))


def _causal(sc, transposed=False):
    row = lax.broadcasted_iota(jnp.int32, sc.shape, 0)
    col = lax.broadcasted_iota(jnp.int32, sc.shape, 1)
    return jnp.where(row <= col if transposed else col <= row, sc, NEG)


def _grid_ends(grid):
    def first():
        ok = pl.program_id(0) == 0
        for d in range(1, len(grid)):
            ok = jnp.logical_and(ok, pl.program_id(d) == 0)
        return ok

    def last():
        ok = pl.program_id(0) == grid[0] - 1
        for d in range(1, len(grid)):
            ok = jnp.logical_and(ok, pl.program_id(d) == grid[d] - 1)
        return ok

    return first, last


def _mla_attn_fwd(name, qq, kv, kr, rider=None):
    s = qq.shape[0]
    t = _attn_tile(s)
    n = s // t

    def body(qi_tab, ki_tab, q_ref, kv_ref, kvn_ref, kr_ref, krn_ref, o_ref, lse_ref, m_scr, acc_scr, sc_a, sc_b):
        qi = qi_tab[pl.program_id(1)]
        ki = ki_tab[pl.program_id(1)]

        def scores(kv_blk, kr_blk):
            k = jnp.concatenate([kv_blk[:, 0:LANES], kr_blk[...]], axis=1)
            return lax.dot_general(q_ref[...], k, NT_DIMS, preferred_element_type=F32)

        @pl.when(ki == 0)
        def _():
            m_scr[...] = jnp.full(m_scr.shape, NEG, F32)
            acc_scr[...] = jnp.zeros_like(acc_scr)
            sc_a[...] = scores(kv_ref, kr_ref)

        def step(cur, nxt, diag):
            if not diag:
                nxt[...] = scores(kvn_ref, krn_ref)
            sc = cur[...]
            if diag:
                sc = _causal(sc)
            m_prev = m_scr[...]
            m_next = jnp.maximum(m_prev, jnp.max(sc, axis=1, keepdims=True))
            a = jnp.exp2(m_prev - m_next)
            p = jnp.exp2(sc - m_next[:, 0:1]).astype(BF16)
            v1 = jnp.concatenate([kv_ref[:, LANES:2 * LANES], jnp.ones((t, LANES), BF16)], axis=1)
            pv = jnp.dot(p, v1, preferred_element_type=F32)
            acc_scr[:, 0:LANES] = a * acc_scr[:, 0:LANES] + pv[:, 0:LANES]
            acc_scr[:, LANES:2 * LANES] = a * acc_scr[:, LANES:2 * LANES] + pv[:, LANES:2 * LANES]
            m_scr[...] = m_next

        even = ki % 2 == 0
        for is_even, cur, nxt in ((True, sc_a, sc_b), (False, sc_b, sc_a)):
            mine = even if is_even else jnp.logical_not(even)

            @pl.when(jnp.logical_and(mine, ki < qi))
            def _():
                step(cur, nxt, False)

            @pl.when(jnp.logical_and(mine, ki == qi))
            def _():
                step(cur, nxt, True)

        @pl.when(ki == qi)
        def _():
            l = acc_scr[:, LANES:2 * LANES]
            o_ref[...] = (acc_scr[:, 0:LANES] / l).astype(BF16)
            lse_ref[...] = LN2 * m_scr[...] + jnp.log(l)

    qblk = lambda w: pl.BlockSpec((t, w), lambda h, i, qt, kt: (qt[i], h))
    nxt_blk = lambda i, qt, kt: jnp.where(kt[i] < qt[i], kt[i] + 1, kt[i])
    return _tri_call(
        body, rider, name=name, grid=(MLA_H, n * (n + 1) // 2), tables=_tri_tables(n, queries_outer=True),
        in_specs=[qblk(2 * LANES),
                  pl.BlockSpec((t, 2 * LANES), lambda h, i, qt, kt: (kt[i], h)),
                  pl.BlockSpec((t, 2 * LANES), lambda h, i, qt, kt: (nxt_blk(i, qt, kt), h)),
                  pl.BlockSpec((t, LANES), lambda h, i, qt, kt: (kt[i], 0)),
                  pl.BlockSpec((t, LANES), lambda h, i, qt, kt: (nxt_blk(i, qt, kt), 0))],
        out_specs=[qblk(LANES), qblk(LANES)],
        out_shape=[jax.ShapeDtypeStruct((s, MLA_H * MLA_V), BF16), jax.ShapeDtypeStruct((s, MLA_H * LANES), F32)],
        scratch_shapes=[pltpu.VMEM((t, LANES), F32), pltpu.VMEM((t, 2 * LANES), F32), pltpu.VMEM((t, t), F32),
                        pltpu.VMEM((t, t), F32)],
        operands=(qq, kv, kv, kr, kr))


def _mla_bwd_stats(name, do, o, lse):
    s = do.shape[0]
    tr = min(4 * ROW_TILE, s)

    def body(do_ref, o_ref, lse_ref, stat_ref):
        dl = jnp.sum(do_ref[...].astype(F32) * o_ref[...].astype(F32), axis=1, keepdims=True)
        delta_t = jnp.transpose(jnp.broadcast_to(dl, (tr, LANES)))[0:8]
        lse_t = jnp.transpose(lse_ref[...] * LOG2E)[0:8]
        rows = lax.broadcasted_iota(jnp.int32, (8, tr), 0)
        stat_ref[0] = jnp.where(rows == 0, lse_t, jnp.where(rows == 1, delta_t, 0.0))

    blk = pl.BlockSpec((tr, LANES), lambda h, i: (i, h))
    return pl.pallas_call(
        body, name=name, grid=(MLA_H, s // tr),
        in_specs=[blk, blk, blk],
        out_specs=pl.BlockSpec((1, 8, tr), lambda h, i: (h, 0, i)),
        out_shape=jax.ShapeDtypeStruct((MLA_H, 8, s), F32),
        compiler_params=_cparams("parallel", "parallel"),
    )(do, o, lse)


def _mla_dq_post(name, dq, rc, rsa, rsb):
    s = dq.shape[1]
    tr = min(4 * ROW_TILE, s)

    def body(dq_ref, c_ref, sa_ref, sb_ref, o_ref):
        o_ref[:, 0:LANES] = (MLA_SCALE * dq_ref[0, :, 0:LANES]).astype(BF16)
        o_ref[:, LANES:2 * LANES] = _rope_t(MLA_SCALE * dq_ref[0, :, LANES:2 * LANES], c_ref[...], sa_ref[...],
                                            sb_ref[...], MLA_ROPE // 2).astype(BF16)

    tab = pl.BlockSpec((tr, LANES), lambda h, i: (i, 0))
    return pl.pallas_call(
        body, name=name, grid=(MLA_H, s // tr),
        in_specs=[pl.BlockSpec((1, tr, 2 * LANES), lambda h, i: (h, i, 0)), tab, tab, tab],
        out_specs=pl.BlockSpec((tr, 2 * LANES), lambda h, i: (i, h)),
        out_shape=jax.ShapeDtypeStruct((s, 2 * MLA_H * LANES), BF16),
        compiler_params=_cparams("parallel", "parallel"),
    )(dq, rc, rsa, rsb)


def _mla_attn_bwd(name, qq, kv, kr, do, stats, rider=None):
    s = qq.shape[0]
    t = _attn_tile(s)
    n = s // t

    def body(qi_tab, ki_tab, q_ref, kv_ref, kr_ref, do_ref, stat_ref, dkv_ref, dkr_ref, dq_hbm,
             dk_scr, dv_scr, dq_scr, dq_sem):
        h = pl.program_id(0)
        qi = qi_tab[pl.program_id(1)]
        ki = ki_tab[pl.program_id(1)]

        @pl.when(pl.program_id(1) == 0)
        def _():
            dq_scr[...] = jnp.zeros_like(dq_scr)

        @pl.when(qi == ki)
        def _():
            dk_scr[...] = jnp.zeros_like(dk_scr)
            dv_scr[...] = jnp.zeros_like(dv_scr)

        def step(diag):
            q = q_ref[...]
            k = jnp.concatenate([kv_ref[:, 0:LANES], kr_ref[...]], axis=1)
            sc = lax.dot_general(k, q, NT_DIMS, preferred_element_type=F32)
            if diag:
                sc = _causal(sc, transposed=True)
            p = jnp.exp2(sc - stat_ref[0, 0:1, :])
            dov = do_ref[...]
            dv_scr[...] += jnp.dot(p.astype(BF16), dov, preferred_element_type=F32)
            dp = lax.dot_general(kv_ref[:, LANES:2 * LANES], dov, NT_DIMS, preferred_element_type=F32)
            ds = (p * (dp - stat_ref[0, 1:2, :])).astype(BF16)
            dk_scr[...] += jnp.dot(ds, q, preferred_element_type=F32)
            rows = pl.ds(pl.multiple_of(qi * t, t), t)
            dq_scr[rows, :] += lax.dot_general(ds, k, TN_DIMS, preferred_element_type=F32)

        @pl.when(qi == ki)
        def _():
            step(True)

        @pl.when(qi > ki)
        def _():
            step(False)

        @pl.when(qi == n - 1)
        def _():
            dkv_ref[:, 0:LANES] = (LN2 * dk_scr[:, 0:LANES]).astype(BF16)
            dkv_ref[:, LANES:2 * LANES] = dv_scr[...].astype(BF16)
            dkr_ref[0] = LN2 * dk_scr[:, LANES:2 * LANES]

        @pl.when(pl.program_id(1) == n * (n + 1) // 2 - 1)
        def _():
            cp = pltpu.make_async_copy(dq_scr, dq_hbm.at[h], dq_sem)
            cp.start()
            cp.wait()

    qblk = lambda w: pl.BlockSpec((t, w), lambda h, i, qt, kt: (qt[i], h))
    kblk = pl.BlockSpec((t, 2 * LANES), lambda h, i, qt, kt: (kt[i], h))
    return _tri_call(
        body, rider, name=name, grid=(MLA_H, n * (n + 1) // 2), tables=_tri_tables(n, queries_outer=False),
        in_specs=[qblk(2 * LANES), kblk, pl.BlockSpec((t, LANES), lambda h, i, qt, kt: (kt[i], 0)), qblk(LANES),
                  pl.BlockSpec((1, 8, t), lambda h, i, qt, kt: (h, 0, qt[i]))],
        out_specs=[kblk, pl.BlockSpec((1, t, LANES), lambda h, i, qt, kt: (h, kt[i], 0)),
                   pl.BlockSpec(memory_space=pl.ANY)],
        out_shape=[jax.ShapeDtypeStruct((s, 2 * MLA_H * LANES), BF16),
                   jax.ShapeDtypeStruct((MLA_H, s, LANES), F32),
                   jax.ShapeDtypeStruct((MLA_H, s, 2 * LANES), F32)],
        scratch_shapes=[pltpu.VMEM((t, 2 * LANES), F32), pltpu.VMEM((t, LANES), F32),
                        pltpu.VMEM((s, 2 * LANES), F32), pltpu.SemaphoreType.DMA(())],
        operands=(qq, kv, kr, do, stats))


def _rope_groups(acc, o_ref, c, sa, sb, sh, groups):
    for gi in range(acc.shape[1] // LANES):
        blk = acc[:, gi * LANES:(gi + 1) * LANES]
        if gi in groups:
            blk = _rope(blk, c, sa, sb, sh)
        o_ref[:, gi * LANES:(gi + 1) * LANES] = blk.astype(o_ref.dtype)


def _mla_fwd(tag, h, w, tabs, rider=None):
    s = h.shape[0]
    rc, rsa, rsb = tabs
    lat = _mm(f"{tag}_lat", h, w["w_in"], tm=512)
    qn, kvn, kr = _mla_lat_post(f"{tag}_latpost", lat, w["q_norm"], w["kv_norm"], rc, rsa, rsb)
    tm = min(512, s)

    def q_epi(acc, o_ref, c_ref, sa_ref, sb_ref):
        _rope_groups(MLA_C2 * acc, o_ref, c_ref[...], sa_ref[...], sb_ref[...], MLA_ROPE // 2, range(1, MLA_H, 2))

    tab = pl.BlockSpec((tm, LANES), lambda i, j: (i, 0))
    qq = _mm(f"{tag}_q", qn, w["w_q"], tm=512, tn=MLA_H * LANES, out_dtype=BF16, epilogue=q_epi,
             extras=(rc, rsa, rsb), extra_specs=(tab, tab, tab))
    kv = _mm(f"{tag}_kv", kvn, w["w_kv"], tm=512, out_dtype=BF16)
    o, lse, *ridden = _mla_attn_fwd(f"{tag}_attn", qq, kv, kr, rider)
    res = dict(h=h, lat=lat, qn=qn, kvn=kvn, kr=kr, qq=qq, kv=kv, o=o, lse=lse)
    return o, res, (ridden[0] if ridden else None)


def _mla_bwd(tag, dy, res, w, tabs, make_rider=None):
    rc, rsa, rsb = tabs
    do = _mm(f"{tag}_do", dy, w["w_o"], tm=512, out_dtype=BF16, nt=True)
    g_wo = _mm_tn(f"{tag}_gwo", res["o"], dy)
    stats = _mla_bwd_stats(f"{tag}_stats", do, res["o"], res["lse"])
    rider = make_rider(g_wo) if make_rider is not None else None
    dkv, dkr, dq, *ridden = _mla_attn_bwd(f"{tag}_attnbwd", res["qq"], res["kv"], res["kr"], do, stats, rider)
    dqq = _mla_dq_post(f"{tag}_dqpost", dq, rc, rsa, rsb)
    dqn = _mm(f"{tag}_dqn", dqq, w["w_q"], tm=512, nt=True)
    g_wq = _mm_tn(f"{tag}_gwq", res["qn"], dqq, tn=1024)
    dkvn = _mm(f"{tag}_dkvn", dkv, w["w_kv"], tm=512, nt=True)
    g_wkv = _mm_tn(f"{tag}_gwkv", res["kvn"], dkv, tn=1024)
    dlat, g_qn, g_kvn = _mla_lat_bwd(f"{tag}_latbwd", res["lat"], dqn, dkvn, dkr, w["q_norm"], w["kv_norm"],
                                     rc, rsa, rsb)
    dh = _mm(f"{tag}_dh", dlat, w["w_in"], tm=512, nt=True)
    g_win = _mm_tn(f"{tag}_gwin", res["h"], dlat)
    grads = dict(w_in=g_win, q_norm=g_qn, w_q=g_wq, kv_norm=g_kvn, w_kv=g_wkv, w_o=g_wo)
    return dh, grads, (ridden[0] if ridden else None)


SWA_QW = SWA_HQ * SWA_HD
SWA_KW = SWA_HKV * LANES
SWA_NQKV = SWA_QW + 2 * SWA_KW
SWA_SCALE = SWA_HD ** -0.5
SWA_GROUP_ROWS = 4 * SWA_W


def _swa_tile(s):
    return min(512, max(SWA_W, s // 2))


def _swa_masks():
    lane = lax.broadcasted_iota(jnp.int32, (SWA_W, LANES), 1)
    return lane < SWA_HD


def _swa_q4(qa, qb, lo):
    z = jnp.zeros_like(qa)
    return jnp.concatenate([jnp.where(lo, qa, z), jnp.where(lo, z, qa), jnp.where(lo, qb, z), jnp.where(lo, z, qb)],
                           axis=0)


def _swa_probs(q4, kwin, sink_col, first_block):
    sc = lax.dot_general(q4, kwin, NT_DIMS, preferred_element_type=F32) * SWA_SCALE
    row = lax.broadcasted_iota(jnp.int32, sc.shape, 0) % SWA_W
    col = lax.broadcasted_iota(jnp.int32, sc.shape, 1)
    rel = row + SWA_W - col
    ok = (rel >= 0) & (rel < SWA_W) & ((col >= SWA_W) | jnp.logical_not(first_block))
    sc = jnp.where(ok, sc, NEG)
    m = jnp.maximum(jnp.max(sc, axis=1, keepdims=True), sink_col)
    e = jnp.exp(sc - m)
    es = jnp.exp(sink_col - m)
    inv = 1.0 / (jnp.sum(e, axis=1, keepdims=True) + es)
    return e * inv, es * inv


def _sink_col(sinks_ref, grp):
    seg = lax.broadcasted_iota(jnp.int32, (SWA_GROUP_ROWS, 1), 0) // SWA_W
    col = jnp.zeros((SWA_GROUP_ROWS, 1), F32)
    for j in range(4):
        col = jnp.where(seg == j, sinks_ref[0, 4 * grp + j], col)
    return col


def _swa_attn_fwd(name, qkv, sinks):
    s = qkv.shape[0]
    t = _swa_tile(s)
    nb = t // SWA_W

    def body(sinks_ref, q_ref, kv_ref, kvp_ref, o_ref):
        i = pl.program_id(0)
        lo = _swa_masks()
        for grp in range(SWA_HKV):
            sink_col = _sink_col(sinks_ref, grp)
            kcat = jnp.concatenate([kvp_ref[:, grp * LANES:(grp + 1) * LANES],
                                    kv_ref[:, grp * LANES:(grp + 1) * LANES]], axis=0)
            vcat = jnp.concatenate([kvp_ref[:, SWA_KW + grp * LANES:SWA_KW + (grp + 1) * LANES],
                                    kv_ref[:, SWA_KW + grp * LANES:SWA_KW + (grp + 1) * LANES]], axis=0)
            for b in range(nb):
                r0 = b * SWA_W
                qa = q_ref[r0:r0 + SWA_W, grp * 2 * LANES:grp * 2 * LANES + LANES]
                qb = q_ref[r0:r0 + SWA_W, grp * 2 * LANES + LANES:(grp + 1) * 2 * LANES]
                first = jnp.logical_and(i == 0, b == 0)
                p, _ = _swa_probs(_swa_q4(qa, qb, lo), kcat[r0:r0 + 2 * SWA_W], sink_col, first)
                o4 = jnp.dot(p.astype(BF16), vcat[r0:r0 + 2 * SWA_W], preferred_element_type=F32)
                oa = jnp.where(lo, o4[0:SWA_W], o4[SWA_W:2 * SWA_W])
                ob = jnp.where(lo, o4[2 * SWA_W:3 * SWA_W], o4[3 * SWA_W:4 * SWA_W])
                o_ref[r0:r0 + SWA_W, grp * 2 * LANES:grp * 2 * LANES + LANES] = oa.astype(BF16)
                o_ref[r0:r0 + SWA_W, grp * 2 * LANES + LANES:(grp + 1) * 2 * LANES] = ob.astype(BF16)

    return pl.pallas_call(
        body, name=name, grid=(s // t,),
        in_specs=[pl.BlockSpec(memory_space=pltpu.SMEM),
                  pl.BlockSpec((t, SWA_QW), lambda i: (i, 0)),
                  pl.BlockSpec((t, 2 * SWA_KW), lambda i: (i, 1)),
                  pl.BlockSpec((SWA_W, 2 * SWA_KW), lambda i: (jnp.maximum(i * nb - 1, 0), 1))],
        out_specs=pl.BlockSpec((t, SWA_QW), lambda i: (i, 0)),
        out_shape=jax.ShapeDtypeStruct((s, SWA_QW), BF16),
        compiler_params=_cparams("parallel"),
    )(sinks, qkv, qkv, qkv)


def _swa_attn_bwd(name, qkv, sinks, do):
    s = qkv.shape[0]
    t = _swa_tile(s)
    nb = t // SWA_W
    nt = s // t

    def body(sinks_ref, q_ref, kv_ref, kvp_ref, do_ref, dq_ref, dkv_ref, dkvp_ref, dsink_ref, dcat, sink_acc):
        i = pl.program_id(0)
        lo = _swa_masks()

        @pl.when(i == 0)
        def _():
            sink_acc[...] = jnp.zeros_like(sink_acc)

        dcat[...] = jnp.zeros_like(dcat)
        for grp in range(SWA_HKV):
            sink_col = _sink_col(sinks_ref, grp)
            kcat = jnp.concatenate([kvp_ref[:, grp * LANES:(grp + 1) * LANES],
                                    kv_ref[:, grp * LANES:(grp + 1) * LANES]], axis=0)
            vcat = jnp.concatenate([kvp_ref[:, SWA_KW + grp * LANES:SWA_KW + (grp + 1) * LANES],
                                    kv_ref[:, SWA_KW + grp * LANES:SWA_KW + (grp + 1) * LANES]], axis=0)
            for b in range(nb):
                r0 = b * SWA_W
                ca = slice(grp * 2 * LANES, grp * 2 * LANES + LANES)
                cb = slice(grp * 2 * LANES + LANES, (grp + 1) * 2 * LANES)
                q4 = _swa_q4(q_ref[r0:r0 + SWA_W, ca], q_ref[r0:r0 + SWA_W, cb], lo)
                do4 = _swa_q4(do_ref[r0:r0 + SWA_W, ca], do_ref[r0:r0 + SWA_W, cb], lo)
                first = jnp.logical_and(i == 0, b == 0)
                kwin = kcat[r0:r0 + 2 * SWA_W]
                vwin = vcat[r0:r0 + 2 * SWA_W]
                p, ps = _swa_probs(q4, kwin, sink_col, first)
                dp = lax.dot_general(do4, vwin, NT_DIMS, preferred_element_type=F32)
                rowdot = jnp.sum(p * dp, axis=1, keepdims=True)
                ds = (p * (dp - rowdot) * SWA_SCALE).astype(BF16)
                sink_acc[grp] += jnp.broadcast_to(-ps * rowdot, (SWA_GROUP_ROWS, LANES))
                dq4 = jnp.dot(ds, kwin, preferred_element_type=F32)
                dq_ref[r0:r0 + SWA_W, ca] = jnp.where(lo, dq4[0:SWA_W], dq4[SWA_W:2 * SWA_W])
                dq_ref[r0:r0 + SWA_W, cb] = jnp.where(lo, dq4[2 * SWA_W:3 * SWA_W], dq4[3 * SWA_W:4 * SWA_W])
                dk = lax.dot_general(ds, q4, TN_DIMS, preferred_element_type=F32)
                dv = lax.dot_general(p.astype(BF16), do4, TN_DIMS, preferred_element_type=F32)
                dcat[r0:r0 + 2 * SWA_W, grp * LANES:(grp + 1) * LANES] += dk
                dcat[r0:r0 + 2 * SWA_W, SWA_KW + grp * LANES:SWA_KW + (grp + 1) * LANES] += dv
        dkvp_ref[0] = dcat[0:SWA_W]
        dkv_ref[...] = dcat[SWA_W:SWA_W + t]

        @pl.when(i == nt - 1)
        def _():
            for grp in range(SWA_HKV):
                for j in range(4):
                    tot = jnp.sum(sink_acc[grp, j * SWA_W:(j + 1) * SWA_W, 0:1])
                    dsink_ref[4 * grp + j:4 * grp + j + 1, :] = jnp.full((1, LANES), tot, F32)

    return pl.pallas_call(
        body, name=name, grid=(nt,),
        in_specs=[pl.BlockSpec(memory_space=pltpu.SMEM),
                  pl.BlockSpec((t, SWA_QW), lambda i: (i, 0)),
                  pl.BlockSpec((t, 2 * SWA_KW), lambda i: (i, 1)),
                  pl.BlockSpec((SWA_W, 2 * SWA_KW), lambda i: (jnp.maximum(i * nb - 1, 0), 1)),
                  pl.BlockSpec((t, SWA_QW), lambda i: (i, 0))],
        out_specs=[pl.BlockSpec((t, SWA_QW), lambda i: (i, 0)), pl.BlockSpec((t, 2 * SWA_KW), lambda i: (i, 0)),
                   pl.BlockSpec((1, SWA_W, 2 * SWA_KW), lambda i: (i, 0, 0)),
                   pl.BlockSpec((SWA_HQ, LANES), lambda i: (0, 0))],
        out_shape=[jax.ShapeDtypeStruct((s, SWA_QW), F32), jax.ShapeDtypeStruct((s, 2 * SWA_KW), F32),
                   jax.ShapeDtypeStruct((nt, SWA_W, 2 * SWA_KW), F32), jax.ShapeDtypeStruct((SWA_HQ, LANES), F32)],
        scratch_shapes=[pltpu.VMEM((SWA_W + t, 2 * SWA_KW), F32), pltpu.VMEM((SWA_HKV, SWA_GROUP_ROWS, LANES), F32)],
        compiler_params=_cparams("arbitrary"),
    )(sinks, qkv, qkv, qkv, do)


def _swa_dqkv(name, dq, dkv, dkvp, rc, rsa, rsb):
    s = dq.shape[0]
    t = _swa_tile(s)
    nt = s // t
    sh = SWA_ROT // 2

    def body(dq_ref, dkv_ref, dkvn_ref, c_ref, sa_ref, sb_ref, out_ref, bsum_ref, acc):
        i = pl.program_id(0)
        c, sa, sb = c_ref[...], sa_ref[...], sb_ref[...]
        lo = lax.broadcasted_iota(jnp.int32, (t, LANES), 1) < SWA_HD
        rows = lax.broadcasted_iota(jnp.int32, (t, LANES), 0)
        tail = jnp.logical_and(rows >= t - SWA_W, i < nt - 1)

        @pl.when(i == 0)
        def _():
            acc[...] = jnp.zeros_like(acc)

        for gi in range(SWA_QW // LANES):
            blk = _rope_t(dq_ref[:, gi * LANES:(gi + 1) * LANES], c, sa, sb, sh)
            out_ref[:, gi * LANES:(gi + 1) * LANES] = blk.astype(BF16)
            acc[:, gi * LANES:(gi + 1) * LANES] += _rowsum8(blk)
        for gi in range(2 * SWA_KW // LANES):
            cols = slice(gi * LANES, (gi + 1) * LANES)
            nxt = jnp.concatenate([jnp.zeros((t - SWA_W, LANES), F32), dkvn_ref[0, :, cols]], axis=0)
            blk = dkv_ref[:, cols] + jnp.where(tail, nxt, 0.0)
            blk = jnp.where(lo, blk + pltpu.roll(blk, SWA_HD, 1), 0.0)
            if gi < SWA_HKV:
                blk = _rope_t(blk, c, sa, sb, sh)
            out_ref[:, SWA_QW + gi * LANES:SWA_QW + (gi + 1) * LANES] = blk.astype(BF16)
            acc[:, SWA_QW + gi * LANES:SWA_QW + (gi + 1) * LANES] += _rowsum8(blk)

        @pl.when(i == nt - 1)
        def _():
            bsum_ref[...] = jnp.sum(acc[...], axis=0, keepdims=True)

    return pl.pallas_call(
        body, name=name, grid=(nt,),
        in_specs=[pl.BlockSpec((t, SWA_QW), lambda i: (i, 0)), pl.BlockSpec((t, 2 * SWA_KW), lambda i: (i, 0)),
                  pl.BlockSpec((1, SWA_W, 2 * SWA_KW), lambda i: (jnp.minimum(i + 1, nt - 1), 0, 0))]
        + [_row_spec(t, LANES)] * 3,
        out_specs=[pl.BlockSpec((t, SWA_NQKV), lambda i: (i, 0)), pl.BlockSpec((1, SWA_NQKV), lambda i: (0, 0))],
        out_shape=[jax.ShapeDtypeStruct((s, SWA_NQKV), BF16), jax.ShapeDtypeStruct((1, SWA_NQKV), F32)],
        scratch_shapes=[pltpu.VMEM((8, SWA_NQKV), F32)],
        compiler_params=_cparams("arbitrary"),
    )(dq, dkv, dkvp, rc, rsa, rsb)


def _swa_fwd(tag, h, w, tabs):
    s = h.shape[0]
    rc, rsa, rsb = tabs
    tm = min(512, s)
    sh = SWA_ROT // 2

    def qkv_epi(acc, o_ref, b_ref, c_ref, sa_ref, sb_ref):
        acc = acc + b_ref[...]

        @pl.when(pl.program_id(1) == 0)
        def _():
            _rope_groups(acc, o_ref, c_ref[...], sa_ref[...], sb_ref[...], sh, range(SWA_QW // LANES))

        @pl.when(pl.program_id(1) == 1)
        def _():
            _rope_groups(acc, o_ref, c_ref[...], sa_ref[...], sb_ref[...], sh, range(SWA_HKV))

    tab = pl.BlockSpec((tm, LANES), lambda i, j: (i, 0))
    qkv = _mm(f"{tag}_qkv", h, w["w_qkv"], tm=512, tn=SWA_QW, out_dtype=BF16, epilogue=qkv_epi,
              extras=(w["b_qkv"], rc, rsa, rsb),
              extra_specs=(pl.BlockSpec((1, SWA_QW), lambda i, j: (0, j)), tab, tab, tab))
    o = _swa_attn_fwd(f"{tag}_attn", qkv, w["sinks"])

    def o_epi(acc, o_ref, b_ref):
        o_ref[...] = acc + b_ref[...]

    y = _mm(f"{tag}_o", o, w["w_o"], tm=512, epilogue=o_epi, extras=(w["b_o"],),
            extra_specs=(pl.BlockSpec((1, D), lambda i, j: (0, 0)),))
    return y, dict(h=h, qkv=qkv, o=o)


def _swa_bwd(tag, dy, res, w, tabs):
    rc, rsa, rsb = tabs
    do = _mm(f"{tag}_do", dy, w["w_o"], tm=512, out_dtype=BF16, nt=True)
    g_wo = _mm_tn(f"{tag}_gwo", res["o"], dy)
    dq, dkv, dkvp, dsink = _swa_attn_bwd(f"{tag}_attnbwd", res["qkv"], w["sinks"], do)
    dqkv, g_b = _swa_dqkv(f"{tag}_dqkv", dq, dkv, dkvp, rc, rsa, rsb)
    dh = _mm(f"{tag}_dh", dqkv, w["w_qkv"], tm=512, nt=True)
    g_wqkv = _mm_tn(f"{tag}_gwqkv", res["h"], dqkv, tn=1024)
    return dh, dict(w_qkv=g_wqkv, b_qkv=g_b, sinks=dsink, w_o=g_wo)


def _ada_fwd(name, c_all, w_sh, b_sh):
    cols = w_sh.shape[2]
    tn = cols // 3

    def body(c_ref, w_ref, b_ref, o_ref, cond_ref):
        cv = c_ref[...]
        cond = cv * jax.nn.sigmoid(cv)
        cond_ref[...] = cond
        o_ref[0] = jnp.dot(cond, w_ref[0], preferred_element_type=F32, precision=lax.Precision.HIGHEST) + b_ref[0]

    return pl.pallas_call(
        body, name=name, grid=(DEPTH, cols // tn),
        in_specs=[pl.BlockSpec((8, D), lambda l, j: (0, 0)), pl.BlockSpec((1, D, tn), lambda l, j: (l, 0, j)),
                  pl.BlockSpec((1, 1, tn), lambda l, j: (l, 0, j))],
        out_specs=[pl.BlockSpec((1, 8, tn), lambda l, j: (l, 0, j)), pl.BlockSpec((8, D), lambda l, j: (0, 0))],
        out_shape=[jax.ShapeDtypeStruct((DEPTH, 8, cols), F32), jax.ShapeDtypeStruct((8, D), F32)],
        compiler_params=_cparams("arbitrary", "arbitrary"),
    )(c_all, w_sh, b_sh)


def _ada_grad(name, cond_t, dmod_sh):
    cols = dmod_sh.shape[2]
    tn = cols // 3

    def body(ct_ref, dm_ref, o_ref):
        acc = ct_ref[:, 0:1] * dm_ref[0, 0:1, :]
        for b in range(1, 8):
            acc = acc + ct_ref[:, b:b + 1] * dm_ref[0, b:b + 1, :]
        o_ref[0] = acc

    return pl.pallas_call(
        body, name=name, grid=(DEPTH, cols // tn),
        in_specs=[pl.BlockSpec((D, 8), lambda l, j: (0, 0)), pl.BlockSpec((1, 8, tn), lambda l, j: (l, 0, j))],
        out_specs=pl.BlockSpec((1, D, tn), lambda l, j: (l, 0, j)),
        out_shape=jax.ShapeDtypeStruct((DEPTH, D, cols), F32),
        compiler_params=_cparams("parallel", "parallel"),
    )(cond_t, dmod_sh)


def _adamw(name, g, w, m, v):
    r, cols = g.shape
    tile_elems = 512 * 1024
    tr = r if r * cols <= tile_elems else max(d for d in (512, 256, 128, 64, 32, 16, 8)
                                               if r % d == 0 and d * cols <= tile_elems)

    def body(g_ref, w_ref, m_ref, v_ref, d_ref, nm_ref, nv_ref):
        gv = g_ref[...]
        mn = ADAM_B1 * m_ref[...] + (1.0 - ADAM_B1) * gv
        vn = ADAM_B2 * v_ref[...] + (1.0 - ADAM_B2) * (gv * gv)
        m_hat = mn / (1.0 - ADAM_B1 ** ADAM_STEP)
        v_hat = vn / (1.0 - ADAM_B2 ** ADAM_STEP)
        d_ref[...] = -ADAM_LR * (m_hat / (jnp.sqrt(v_hat) + ADAM_EPS) + ADAM_WD * w_ref[...])
        nm_ref[...] = mn
        nv_ref[...] = vn

    spec = _row_spec(tr, cols)
    return pl.pallas_call(
        body, name=name, grid=(r // tr,),
        in_specs=[spec] * 4, out_specs=[spec] * 3,
        out_shape=[jax.ShapeDtypeStruct(g.shape, F32)] * 3,
        compiler_params=_cparams("parallel"),
    )(g, w, m, v)


def _to_chips(full, axis):
    shp = full.shape
    a = full.reshape(shp[:axis] + (N_CHIPS, shp[axis] // N_CHIPS) + shp[axis + 1:])
    return jnp.moveaxis(a, axis, 0)


def _from_chips(stacked, axis):
    a = jnp.moveaxis(stacked, 0, axis)
    shp = a.shape
    return a.reshape(shp[:axis] + (shp[axis] * shp[axis + 1],) + shp[axis + 2:])


PIECE_ROW_ALIGN = 16


def _piece_rows(shape):
    n = 1
    for d in shape:
        n *= d
    rows = -(-n // PACK_COLS)
    return -(-rows // PIECE_ROW_ALIGN) * PIECE_ROW_ALIGN


def _as_rows(a, lead):
    head = a.shape[:lead]
    rows = _piece_rows(a.shape[lead:])
    n = 1
    for d in a.shape[lead:]:
        n *= d
    if n == rows * PACK_COLS:
        return a.reshape(head + (rows, PACK_COLS))
    flat = jnp.pad(a.reshape(head + (n,)), [(0, 0)] * lead + [(0, rows * PACK_COLS - n)])
    return flat.reshape(head + (rows, PACK_COLS))


def _pack(parts, lead, rows):
    pieces = [_as_rows(p, lead) for p in parts]
    used = sum(p.shape[lead] for p in pieces)
    head = pieces[0].shape[:lead]
    pieces.append(jnp.zeros(head + (rows - used, PACK_COLS), pieces[0].dtype))
    return jnp.concatenate(pieces, axis=lead)


def _unpack(packed, lead, shapes):
    out, off = [], 0
    head = packed.shape[:lead]
    for shp in shapes:
        rows = _piece_rows(shp)
        n = 1
        for d in shp:
            n *= d
        piece = lax.slice_in_dim(packed, off, off + rows, axis=lead)
        if n != rows * PACK_COLS:
            piece = piece.reshape(head + (rows * PACK_COLS,))[..., :n]
        out.append(piece.reshape(head + tuple(shp)))
        off += rows
    return out


def _pack_rows(shapes):
    rows = sum(_piece_rows(s) for s in shapes)
    return -(-rows // PACK_ROW_ALIGN) * PACK_ROW_ALIGN


def _rope_tables(positions, rot, lanes_per_head):
    half = rot // 2
    inv = ROPE_THETA ** (-jnp.arange(0, rot, 2, dtype=F32) / rot)
    ang = positions.astype(F32)[:, None] * inv
    cos, sin = jnp.cos(ang), jnp.sin(ang)
    s = positions.shape[0]
    rest = lanes_per_head - rot
    fill = 1.0 if lanes_per_head == SWA_HD else 0.0
    c = jnp.concatenate([cos, cos, jnp.full((s, rest), fill, F32)], axis=1)
    sa = jnp.concatenate([-sin, jnp.zeros((s, half + rest), F32)], axis=1)
    sb = jnp.concatenate([jnp.zeros((s, half), F32), sin, jnp.zeros((s, rest), F32)], axis=1)
    reps = LANES // lanes_per_head
    return tuple(jnp.tile(t, (1, reps)) for t in (c, sa, sb))


def _mla_weights(w_in, q_norm, w_q_b, kv_norm, w_kv_b):
    w_in_p = jnp.pad(w_in, ((0, 0), (0, MLA_LAT - w_in.shape[1])))
    wq = w_q_b.reshape(MLA_QR, MLA_H, MLA_NOPE + MLA_ROPE)
    wq_p = jnp.pad(wq, ((0, 0), (0, 0), (0, 2 * LANES - MLA_NOPE - MLA_ROPE))).reshape(MLA_QR, MLA_H * 2 * LANES)
    return dict(w_in=w_in_p, q_norm=q_norm.reshape(1, -1), kv_norm=kv_norm.reshape(1, -1), w_q=wq_p, w_kv=w_kv_b)


def _mla_grads_unpermute(g):
    gq = g["w_q"].reshape(MLA_QR, MLA_H, 2 * LANES)[:, :, :MLA_NOPE + MLA_ROPE]
    return dict(mla_w_in=g["w_in"][:, :MLA_QR + MLA_KVR + MLA_ROPE], mla_q_norm=g["q_norm"][0],
                mla_w_q_b=gq.reshape(MLA_QR, -1), mla_kv_norm=g["kv_norm"][0], mla_w_kv_b=g["w_kv"],
                mla_w_o=g["w_o"])


def _swa_dup(a):
    lead = a.shape[:-1]
    a = a.reshape(lead + (SWA_HKV, SWA_HD))
    return jnp.concatenate([a, a], axis=-1).reshape(lead + (SWA_KW,))


def _swa_undup(a):
    lead = a.shape[:-1]
    return a.reshape(lead + (SWA_HKV, LANES))[..., :SWA_HD].reshape(lead + (SWA_HKV * SWA_HD,))


def _swa_weights(w_qkv, b_qkv, sinks, w_o, b_o):
    nk = SWA_HKV * SWA_HD
    perm = lambda a: jnp.concatenate([a[..., :SWA_QW], _swa_dup(a[..., SWA_QW:SWA_QW + nk]),
                                      _swa_dup(a[..., SWA_QW + nk:])], axis=-1)
    w_p = perm(w_qkv)
    return dict(w_qkv=w_p, b_qkv=perm(b_qkv.astype(F32)).reshape(1, -1),
                sinks=sinks.reshape(1, -1), w_o=w_o, b_o=b_o.astype(F32).reshape(1, -1))


def _swa_grads_unpermute(g):
    unperm = lambda a: jnp.concatenate([a[..., :SWA_QW], _swa_undup(a[..., SWA_QW:SWA_QW + SWA_KW]),
                                        _swa_undup(a[..., SWA_QW + SWA_KW:])], axis=-1)
    return dict(swa_w_qkv=unperm(g["w_qkv"]), swa_b_qkv=unperm(g["b_qkv"])[0], swa_sinks=g["sinks"][:, 0],
                swa_w_o=g["w_o"], swa_b_o=g["b_o"])


SMALL_LAYOUT = (("ada_b", 24), ("ln_mix_g", 4), ("ln_mix_b", 4), ("ln_ffn_g", 4), ("ln_ffn_b", 4),
                ("mla_q_norm", 2), ("mla_kv_norm", 2), ("swa_sinks", 1), ("loss", 1))


def _small_pack(vals):
    rows = []
    for name, nrows in SMALL_LAYOUT:
        a = vals[name].reshape(nrows, -1).astype(F32)
        rows.append(jnp.pad(a, ((0, 0), (0, PACK_COLS - a.shape[1]))))
    cat = jnp.concatenate(rows, axis=0)
    return jnp.pad(cat, ((0, SMALL_ROWS - cat.shape[0]), (0, 0)))


def _small_unpack(packed, shapes):
    out, r = {}, 0
    for name, nrows in SMALL_LAYOUT:
        shp = shapes[name]
        n = 1
        for d in shp:
            n *= d
        out[name] = packed[r:r + nrows, :n // nrows].reshape(shp)
        r += nrows
    return out


def kernel(x, c, positions, ada_w, ada_b, ln_mix_g, ln_mix_b, ln_ffn_g, ln_ffn_b, ffn_w_gate, ffn_w_up, ffn_w_down, mla_w_in, mla_q_norm, mla_w_q_b, mla_kv_norm, mla_w_kv_b, mla_w_o, swa_w_qkv, swa_b_qkv, swa_sinks, swa_w_o, swa_b_o, loss_target, m_ada_w, m_ada_b, m_ln_mix_g, m_ln_mix_b, m_ln_ffn_g, m_ln_ffn_b, m_ffn_w_gate, m_ffn_w_up, m_ffn_w_down, m_mla_w_in, m_mla_q_norm, m_mla_w_q_b, m_mla_kv_norm, m_mla_w_kv_b, m_mla_w_o, m_swa_w_qkv, m_swa_b_qkv, m_swa_sinks, m_swa_w_o, m_swa_b_o, v_ada_w, v_ada_b, v_ln_mix_g, v_ln_mix_b, v_ln_ffn_g, v_ln_ffn_b, v_ffn_w_gate, v_ffn_w_up, v_ffn_w_down, v_mla_w_in, v_mla_q_norm, v_mla_w_q_b, v_mla_kv_norm, v_mla_w_kv_b, v_mla_w_o, v_swa_w_qkv, v_swa_b_qkv, v_swa_sinks, v_swa_w_o, v_swa_b_o):
    weights = dict(ada_w=ada_w, ada_b=ada_b, ln_mix_g=ln_mix_g, ln_mix_b=ln_mix_b, ln_ffn_g=ln_ffn_g,
                   ln_ffn_b=ln_ffn_b, ffn_w_gate=ffn_w_gate, ffn_w_up=ffn_w_up, ffn_w_down=ffn_w_down,
                   mla_w_in=mla_w_in, mla_q_norm=mla_q_norm, mla_w_q_b=mla_w_q_b, mla_kv_norm=mla_kv_norm,
                   mla_w_kv_b=mla_w_kv_b, mla_w_o=mla_w_o, swa_w_qkv=swa_w_qkv, swa_b_qkv=swa_b_qkv,
                   swa_sinks=swa_sinks, swa_w_o=swa_w_o, swa_b_o=swa_b_o)
    mom_m = dict(ada_w=m_ada_w, ada_b=m_ada_b, ln_mix_g=m_ln_mix_g, ln_mix_b=m_ln_mix_b, ln_ffn_g=m_ln_ffn_g,
                 ln_ffn_b=m_ln_ffn_b, ffn_w_gate=m_ffn_w_gate, ffn_w_up=m_ffn_w_up, ffn_w_down=m_ffn_w_down,
                 mla_w_in=m_mla_w_in, mla_q_norm=m_mla_q_norm, mla_w_q_b=m_mla_w_q_b, mla_kv_norm=m_mla_kv_norm,
                 mla_w_kv_b=m_mla_w_kv_b, mla_w_o=m_mla_w_o, swa_w_qkv=m_swa_w_qkv, swa_b_qkv=m_swa_b_qkv,
                 swa_sinks=m_swa_sinks, swa_w_o=m_swa_w_o, swa_b_o=m_swa_b_o)
    mom_v = dict(ada_w=v_ada_w, ada_b=v_ada_b, ln_mix_g=v_ln_mix_g, ln_mix_b=v_ln_mix_b, ln_ffn_g=v_ln_ffn_g,
                 ln_ffn_b=v_ln_ffn_b, ffn_w_gate=v_ffn_w_gate, ffn_w_up=v_ffn_w_up, ffn_w_down=v_ffn_w_down,
                 mla_w_in=v_mla_w_in, mla_q_norm=v_mla_q_norm, mla_w_q_b=v_mla_w_q_b, mla_kv_norm=v_mla_kv_norm,
                 mla_w_kv_b=v_mla_w_kv_b, mla_w_o=v_mla_w_o, swa_w_qkv=v_swa_w_qkv, swa_b_qkv=v_swa_b_qkv,
                 swa_sinks=v_swa_sinks, swa_w_o=v_swa_w_o, swa_b_o=v_swa_b_o)
    names = list(weights)
    my_x, my_y, my_c = lax.axis_index("x"), lax.axis_index("y"), lax.axis_index("c")
    chip = 2 * my_x + my_y
    batch_row = 2 * chip + my_c
    xs = x[0]
    target = loss_target[0]
    pos = positions[0]
    s = xs.shape[0]

    def item_shapes(items):
        return [(b - a,) + tuple(weights[n].shape[1:]) for n, a, b, _ in items]

    def pack_items(src, items, dtype):
        return _pack([src[n][a:b].astype(dtype) for n, a, b, _ in items], 0, _pack_rows(item_shapes(items)))

    full = {}

    def unpack_gathered(gathered, items):
        for (n, a, b, axis), part in zip(items, _unpack(gathered, 1, item_shapes(items))):
            whole = _from_chips(part, axis)
            for l in range(a, b):
                full[n, l] = whole[l - a]

    early = _exchange("ag_w_early", pack_items(weights, W_EARLY, BF16), ("x", "y"), "gather")
    unpack_gathered(early, W_EARLY)
    late_ride = _Exchange(pack_items(weights, W_LATE, BF16), ("x", "y"), "gather", chunks=8)

    c_rows = jnp.pad(c, ((0, 7), (0, 0)))
    c_all = _exchange("ag_c", c_rows, ("x", "y", "c"), "gather")[:, 0, :]
    ada_cols = ada_w.shape[2]
    ada_b_sh = lax.dynamic_slice_in_dim(ada_b, chip * ada_cols, ada_cols, axis=1).reshape(DEPTH, 1, ada_cols)
    mod_sh, cond_all = _ada_fwd("ada_fwd", c_all, ada_w, ada_b_sh)
    mod_all = _exchange("ag_mod", mod_sh.reshape(DEPTH * 8, ada_cols), ("x", "y"), "gather")
    mod_all = mod_all.reshape(N_CHIPS, DEPTH, 8, ada_cols)
    mod_mine = lax.dynamic_index_in_dim(mod_all, batch_row, axis=2, keepdims=False)
    mod = jnp.moveaxis(mod_mine, 0, 1).reshape(DEPTH, 6, 1, D)

    tabs_a = _rope_tables(pos, MLA_ROPE, LANES)
    tabs_b = _rope_tables(pos, SWA_ROT, SWA_HD)
    vec = lambda a, l: a[l].reshape(1, D)

    def mla_in_weights(j):
        return _mla_weights(full["mla_w_in", j], mla_q_norm[j], full["mla_w_q_b", j], mla_kv_norm[j],
                            full["mla_w_kv_b", j])

    mix_w, ffn_w = {}, {}
    saved = []
    x_cur = xs
    h = _modulate("mod0", x_cur, mod[0, 1], mod[0, 0])
    for l in range(DEPTH):
        j = l // 2
        if l % 2 == 0:
            mix_w[l] = mla_in_weights(j)
            o, res, ridden = _mla_fwd(f"mla{l}", h, mix_w[l], tabs_a, late_ride if l == 0 else None)
            if l == 0:
                unpack_gathered(ridden, W_LATE)
            mix_w[l]["w_o"] = full["mla_w_o", j]
            y_mix = _mm(f"mla{l}_o", o, mix_w[l]["w_o"], tm=512)
        else:
            mix_w[l] = _swa_weights(full["swa_w_qkv", j], full["swa_b_qkv", j], swa_sinks[j], full["swa_w_o", j],
                                    full["swa_b_o", j])
            y_mix, res = _swa_fwd(f"swa{l}", h, mix_w[l], tabs_b)
        wg, wu, wd = full["ffn_w_gate", l], full["ffn_w_up", l], full["ffn_w_down", l]
        ffn_w[l] = dict(wg=wg, wu=wu, wd=wd)
        x_mid, h2 = _post_mod(f"post_mix{l}", x_cur, y_mix, mod[l, 2], vec(ln_mix_g, l), vec(ln_mix_b, l),
                              mod[l, 4], mod[l, 3])
        gate, up, y_ffn = _ffn_fwd(f"ffn{l}", h2, ffn_w[l]["wg"], ffn_w[l]["wu"], ffn_w[l]["wd"])
        saved.append(dict(x_in=x_cur, y_mix=y_mix, res=res, x_mid=x_mid, h2=h2, gate=gate, up=up, y_ffn=y_ffn))
        if l < DEPTH - 1:
            x_cur, h = _post_mod(f"post_ffn{l}", x_mid, y_ffn, mod[l, 5], vec(ln_ffn_g, l), vec(ln_ffn_b, l),
                                 mod[l + 1, 1], mod[l + 1, 0])
        else:
            dxn, loss_part = _post_loss("post_loss", x_mid, y_ffn, mod[l, 5], vec(ln_ffn_g, l), vec(ln_ffn_b, l),
                                        target)

    gfull = {n: [None] * weights[n].shape[0] for n, _ in SHARDED}
    gsmall = {n: [None] * weights[n].shape[0] for n in ("ln_mix_g", "ln_mix_b", "ln_ffn_g", "ln_ffn_b",
                                                         "mla_q_norm", "mla_kv_norm", "swa_sinks")}
    dmod = [None] * DEPTH

    def grad_ride(items):
        parts = [_to_chips(jnp.stack(gfull[n][a:b]).astype(BF16), axis) for n, a, b, axis in items]
        return _Exchange(_pack(parts, 1, _pack_rows(item_shapes(items))), ("x", "y", "c"), "to_chip", chunks=4)

    def ride_with_wo(items, j):
        def make(g_wo):
            gfull["mla_w_o"][j] = g_wo
            return grad_ride(items)
        return make

    rides = {DEPTH - 2: G_FIRST, 0: G_SECOND}
    g_parts = {}
    sums_f, sums_m, sums_fm, sums_mm = {}, {}, {}, {}
    top = DEPTH - 1
    dxp, dy, sums_f[top] = _post_bwd(f"post_ffn_bwd{top}", dxn, saved[top]["x_mid"], saved[top]["y_ffn"], mod[top, 5],
                                     vec(ln_ffn_g, top))
    for l in reversed(range(DEPTH)):
        sv = saved[l]
        j = l // 2
        dgt, dup, act, dh2 = _ffn_bwd(f"ffn_bwd{l}", dy, sv["gate"], sv["up"], ffn_w[l]["wd"], ffn_w[l]["wg"],
                                      ffn_w[l]["wu"])
        gfull["ffn_w_gate"][l] = _mm_tn(f"ffn_gwg{l}", sv["h2"], dgt, tn=FF // 2)
        gfull["ffn_w_up"][l] = _mm_tn(f"ffn_gwu{l}", sv["h2"], dup, tn=FF // 2)
        gfull["ffn_w_down"][l] = _mm_tn(f"ffn_gwd{l}", act, dy)
        dxp, dy, six = _mod_post_bwd(f"modpost_mix_bwd{l}", dh2, dxp, sv["x_in"], sv["y_mix"], mod[l, 2],
                                     vec(ln_mix_g, l), vec(ln_mix_b, l), mod[l, 4])
        sums_fm[l], sums_m[l] = six[0:2], six[2:6]
        if l % 2 == 0:
            dh, g, ridden = _mla_bwd(f"mla{l}", dy, sv["res"], mix_w[l], tabs_a, ride_with_wo(rides[l], j))
            g_parts[rides[l]] = _sum_groups(f"rs_sum{l}", ridden)
            g = _mla_grads_unpermute(g)
        else:
            dh, g = _swa_bwd(f"swa{l}", dy, sv["res"], mix_w[l], tabs_b)
            g["b_o"] = sums_m[l][3]
            g = _swa_grads_unpermute(g)
        for n, val in g.items():
            (gfull if n in gfull else gsmall)[n][j] = val
        if l > 0:
            below = saved[l - 1]
            dxp, dy, six = _mod_post_bwd(f"modpost_ffn_bwd{l - 1}", dh, dxp, below["x_mid"], below["y_ffn"],
                                         mod[l - 1, 5], vec(ln_ffn_g, l - 1), vec(ln_ffn_b, l - 1), mod[l, 1])
            sums_mm[l], sums_f[l - 1] = six[0:2], six[2:6]
        else:
            dxn, sums_mm[l] = _mod_bwd("mod_mix_bwd0", dh, sv["x_in"], dxp, mod[l, 1])
    for l in range(DEPTH):
        gsmall["ln_ffn_g"][l], gsmall["ln_ffn_b"][l] = sums_f[l][0], sums_f[l][1]
        gsmall["ln_mix_g"][l], gsmall["ln_mix_b"][l] = sums_m[l][0], sums_m[l][1]
        dmod[l] = jnp.stack([sums_mm[l][1], sums_mm[l][0], sums_m[l][2], sums_fm[l][1], sums_fm[l][0], sums_f[l][2]])
    grad_x = dxn[None]

    small_vals = {n: jnp.stack(v) for n, v in gsmall.items()}
    small_vals["ada_b"] = jnp.stack(dmod)
    small_vals["loss"] = loss_part[0, 0:1]
    small_all = _exchange("ag_small", _small_pack(small_vals), ("x", "y", "c"), "gather")
    small_sum = _sum_groups("sum_small", small_all)
    dmod_all = small_all[:, :DEPTH * 6, :].reshape(8, DEPTH, 6 * D)
    dmod_sh = jnp.moveaxis(lax.dynamic_slice_in_dim(dmod_all, chip * ada_cols, ada_cols, axis=2), 0, 1)
    g_ada_w = _ada_grad("ada_grad", cond_all.T, dmod_sh)

    tail = grad_ride(G_LAST)
    g_parts[G_LAST] = _sum_groups("rs_sum_tail", _exchange("rs_tail", tail.src, tail.axes, tail.mode, tail.chunks))

    pieces = {}
    for items in (G_FIRST, G_SECOND, G_LAST):
        for (n, a, _, _), part in zip(items, _unpack(g_parts[items], 0, item_shapes(items))):
            pieces.setdefault(n, []).append((a, part))
    grads = {n: jnp.concatenate([p for _, p in sorted(ps, key=lambda ap: ap[0])], axis=0) for n, ps in pieces.items()}
    grads["ada_w"] = g_ada_w
    small_shapes = {n: weights[n].shape for n, _ in SMALL_LAYOUT if n != "loss"}
    small_shapes["loss"] = (1,)
    grads.update(_small_unpack(small_sum, small_shapes))

    def as_2d(a):
        return a.reshape(-1, a.shape[-1])

    outs = [grads, {}, {}, {}]
    for n in names:
        if n in small_shapes:
            continue
        res = _adamw(f"adamw_{n}", *[as_2d(src[n]) for src in (grads, weights, mom_m, mom_v)])
        for o, r in zip(outs[1:], res):
            o[n] = r.reshape(weights[n].shape)

    def small_of(src):
        return _small_pack({**{n: src[n] for n in small_shapes if n != "loss"}, "loss": jnp.zeros((1,), F32)})

    res = _adamw("adamw_small", small_sum, small_of(weights), small_of(mom_m), small_of(mom_v))
    for o, r in zip(outs[1:], res):
        o.update(_small_unpack(r, small_shapes))
    loss = grads["loss"][0]
    return (loss, grad_x, *[o[n] for o in outs for n in names])
```
